```python
import jax, jax.numpy as jnp
from jax import lax
import numpy as np

D_MODEL = 2048
BATCH = 2
SEQ = 4096
DEPTH = 1

N_META = 16
MIX_WIDTH = D_MODEL
HY_WIDTH = MIX_WIDTH // 2
HY_GROUPS = 8
HY_SHORT = 3
FILT_EMB = 33
FILT_BANDS = (FILT_EMB - 1) // 2
FILT_HIDDEN = 64
FILT_FAST_DECAY = 0.3
FILT_SLOW_DECAY = 1.5
FILT_TARGET = 1e-2
GLA_WIDTH = MIX_WIDTH - HY_WIDTH
GLA_HEADS = 4
GLA_KEY_WIDTH = GLA_WIDTH // 2
GLA_DK = GLA_KEY_WIDTH // GLA_HEADS
GLA_DV = GLA_WIDTH // GLA_HEADS
GATE_RANK = 16
GATE_NORMALIZER = 16.0
CHUNK = 64
D_FF = ((8 * D_MODEL // 3 + 255) // 256) * 256
IN_SPLITS = (3 * HY_WIDTH, GLA_KEY_WIDTH, GLA_KEY_WIDTH, GLA_WIDTH, GLA_WIDTH, 2 * GATE_RANK)
IN_COLS = 3 * HY_WIDTH + 2 * GLA_KEY_WIDTH + 2 * GLA_WIDTH + 2 * GATE_RANK
EPS = 1e-6

kernel_name = 'hybrid_hyena_gla_macaron_encoder'


def rms_norm(x, gain):
    xf = x.astype(jnp.float32)
    y = xf * lax.rsqrt(jnp.mean(xf * xf, axis=-1, keepdims=True) + EPS)
    return y.astype(x.dtype) * gain


def swiglu(h, w_gate, w_up, w_down):
    return (jax.nn.silu(h @ w_gate) * (h @ w_up)) @ w_down


def centred_short_conv(u, w, b):
    L = u.shape[1]
    half = (w.shape[0] - 1) // 2
    up = jnp.pad(u, ((0, 0), (half, half), (0, 0)))
    y = b
    for j in range(w.shape[0]):
        y = y + up[:, j:j + L] * w[j]
    return y


def hyena_filters(L, w1, b1, w2, b2, w3, freq):
    f32 = jnp.float32
    t = jnp.linspace(0.0, 1.0, L, dtype=f32)[:, None]
    w = (2.0 * np.pi / L) * jnp.arange(L, dtype=f32)
    bands = jnp.linspace(1e-4, FILT_BANDS - 1, FILT_BANDS, dtype=f32)
    ang = w[:, None] * bands[None, :]
    feats = jnp.concatenate([t, jnp.cos(ang), -jnp.sin(ang)], axis=-1)
    freq = freq.astype(f32)
    z = jnp.sin(freq * (feats @ w1.astype(f32) + b1.astype(f32)))
    z = jnp.sin(freq * (z @ w2.astype(f32) + b2.astype(f32)))
    h = z @ w3.astype(f32)
    deltas = jnp.abs(jnp.linspace(np.log(FILT_TARGET) / FILT_SLOW_DECAY,
                                  np.log(FILT_TARGET) / FILT_FAST_DECAY, HY_WIDTH, dtype=f32))
    window = jnp.exp(-t * deltas[None, :])
    h = h.reshape(L, 2, HY_WIDTH) * window[:, None, :]
    return h[:, 0], h[:, 1]


def bidirectional_long_conv(u, h_fwd, h_bwd):
    B, L, C = u.shape
    g = jnp.concatenate([h_fwd[:1] + h_bwd[:1], h_fwd[1:], jnp.zeros((1, C), h_fwd.dtype), h_bwd[:0:-1]], axis=0)
    G = jnp.fft.rfft(g, axis=0)
    U = jnp.fft.rfft(u.astype(jnp.float32), n=2 * L, axis=1)
    y = jnp.fft.irfft(U * G[None], n=2 * L, axis=1)[:, :L]
    return y.astype(u.dtype)


def gla_chunk_scan(q, k, v, g):
    B, H, T, dk = q.shape
    dv = v.shape[-1]
    n = T // CHUNK

    def to_chunks(a):
        return jnp.moveaxis(a.reshape(B, H, n, CHUNK, a.shape[-1]), 2, 0)

    qc, kc, vc, gc = to_chunks(q), to_chunks(k), to_chunks(v), to_chunks(g)
    bc = jnp.cumsum(gc, axis=-2)
    mask = jnp.tril(jnp.ones((CHUNK, CHUNK), dtype=bool))

    def step(S, xs):
        qi, ki, vi, bi = xs
        o_inter = jnp.einsum('bhcd,bhde->bhce', qi * jnp.exp(bi), S)
        diff = bi[:, :, :, None, :] - bi[:, :, None, :, :]
        dec = jnp.exp(jnp.where(mask[:, :, None], diff, -jnp.inf))
        A = jnp.einsum('bhid,bhjd,bhijd->bhij', qi, ki, dec)
        o_intra = jnp.einsum('bhij,bhje->bhie', A, vi)
        b_last = bi[:, :, -1:, :]
        S = S * jnp.exp(b_last)[:, :, 0, :, None] + jnp.einsum('bhcd,bhce->bhde', ki * jnp.exp(b_last - bi), vi)
        return S, o_inter + o_intra

    S0 = jnp.zeros((B, H, dk, dv), jnp.float32)
    _, o = lax.scan(step, S0, (qc, kc, vc, bc))
    return jnp.moveaxis(o, 0, 2).reshape(B, H, T, dv)


def gla_bidirectional(q, k, v, lg_f, lg_b):
    L = q.shape[1]
    pad_front = (-N_META) % CHUNK
    pad_back = (-(pad_front + L)) % CHUNK

    def prep(a):
        a = jnp.pad(a, ((0, 0), (pad_front, pad_back), (0, 0), (0, 0)))
        return jnp.transpose(a, (0, 2, 1, 3))

    qp, kp, vp, gf, gb = prep(q), prep(k), prep(v), prep(lg_f), prep(lg_b)
    flip = lambda a: a[:, :, ::-1]
    o_f = gla_chunk_scan(qp, kp, vp, gf)
    o_b = flip(gla_chunk_scan(flip(qp), flip(kp), flip(vp), flip(gb)))
    o = (o_f + o_b)[:, :, pad_front:pad_front + L]
    return jnp.transpose(o, (0, 2, 1, 3))


def token_mixer(h, w_in, conv_w, conv_b, filt_w1, filt_b1, filt_w2, filt_b2, filt_w3, filt_freq,
                hyena_d, hyena_norm, gk_w2, gk_b2, gla_norm, w_out):
    B, L, _ = h.shape
    p = h @ w_in
    idx = np.cumsum(IN_SPLITS)[:-1].tolist()
    hy, q, k, v, og, lr = jnp.split(p, idx, axis=-1)

    hy = centred_short_conv(hy, conv_w, conv_b)
    x0, x1, vh = jnp.split(hy, 3, axis=-1)
    h_f, h_b = hyena_filters(L, filt_w1, filt_b1, filt_w2, filt_b2, filt_w3, filt_freq)
    u = vh * x1
    u = bidirectional_long_conv(u, h_f, h_b) + hyena_d * u
    y_h = u * x0
    y_h = rms_norm(y_h.reshape(B, L, HY_GROUPS, -1), hyena_norm.reshape(HY_GROUPS, -1)).reshape(B, L, HY_WIDTH)

    f32 = jnp.float32
    qh = q.reshape(B, L, GLA_HEADS, GLA_DK).astype(f32) * (GLA_DK ** -0.5)
    kh = k.reshape(B, L, GLA_HEADS, GLA_DK).astype(f32)
    vh_ = v.reshape(B, L, GLA_HEADS, GLA_DV).astype(f32)
    lr_f, lr_b = jnp.split(lr, 2, axis=-1)
    lg_f = jax.nn.log_sigmoid((lr_f @ gk_w2[0] + gk_b2[0]).astype(f32)) / GATE_NORMALIZER
    lg_b = jax.nn.log_sigmoid((lr_b @ gk_w2[1] + gk_b2[1]).astype(f32)) / GATE_NORMALIZER
    lg_f = lg_f.reshape(B, L, GLA_HEADS, GLA_DK)
    lg_b = lg_b.reshape(B, L, GLA_HEADS, GLA_DK)
    o = gla_bidirectional(qh, kh, vh_, lg_f, lg_b)
    o = rms_norm(o, gla_norm).reshape(B, L, GLA_WIDTH).astype(h.dtype)
    y_g = o * jax.nn.silu(og)

    return jnp.concatenate([y_h, y_g], axis=-1) @ w_out


def setup_inputs(seed: int = 0) -> dict:
    key = jax.random.key(seed)
    ks = iter(jax.random.split(key, 32))
    f32 = jnp.float32

    def nrm(shape, scale):
        return scale * jax.random.normal(next(ks), shape, f32)

    def gain(shape):
        return 1.0 + nrm(shape, 0.01)

    D, L_ = D_MODEL, DEPTH
    return {
        'x': nrm((BATCH, SEQ, D), 1.0),
        'meta_tokens': nrm((N_META, D), 1.0),
        'ffn1_norm': gain((L_, D)),
        'ffn1_w_gate': nrm((L_, D, D_FF), D ** -0.5),
        'ffn1_w_up': nrm((L_, D, D_FF), D ** -0.5),
        'ffn1_w_down': nrm((L_, D_FF, D), D_FF ** -0.5),
        'mix_norm': gain((L_, D)),
        'w_in': nrm((L_, D, IN_COLS), D ** -0.5),
        'conv_w': nrm((L_, HY_SHORT, 3 * HY_WIDTH), HY_SHORT ** -0.5),
        'conv_b': nrm((L_, 3 * HY_WIDTH), 0.01),
        'filt_w1': nrm((L_, FILT_EMB, FILT_HIDDEN), FILT_EMB ** -0.5),
        'filt_b1': nrm((L_, FILT_HIDDEN), 0.1),
        'filt_w2': nrm((L_, FILT_HIDDEN, FILT_HIDDEN), FILT_HIDDEN ** -0.5),
        'filt_b2': nrm((L_, FILT_HIDDEN), 0.1),
        'filt_w3': nrm((L_, FILT_HIDDEN, 2 * HY_WIDTH), FILT_HIDDEN ** -0.5),
        'filt_freq': gain((L_, FILT_HIDDEN)),
        'hyena_d': nrm((L_, HY_WIDTH), 1.0),
        'hyena_norm': gain((L_, HY_WIDTH)),
        'gk_w2': nrm((L_, 2, GATE_RANK, GLA_KEY_WIDTH), GATE_RANK ** -0.5),
        'gk_b2': nrm((L_, 2, GLA_KEY_WIDTH), 0.1),
        'gla_norm': gain((L_, GLA_DV)),
        'w_out': nrm((L_, MIX_WIDTH, D), MIX_WIDTH ** -0.5),
        'ffn2_norm': gain((L_, D)),
        'ffn2_w_gate': nrm((L_, D, D_FF), D ** -0.5),
        'ffn2_w_up': nrm((L_, D, D_FF), D ** -0.5),
        'ffn2_w_down': nrm((L_, D_FF, D), D_FF ** -0.5),
        'final_norm': gain((D,)),
    }


def reference(x, meta_tokens, ffn1_norm, ffn1_w_gate, ffn1_w_up, ffn1_w_down, mix_norm, w_in,
              conv_w, conv_b, filt_w1, filt_b1, filt_w2, filt_b2, filt_w3, filt_freq, hyena_d,
              hyena_norm, gk_w2, gk_b2, gla_norm, w_out, ffn2_norm, ffn2_w_gate, ffn2_w_up,
              ffn2_w_down, final_norm):
    B = x.shape[0]
    meta = jnp.broadcast_to(meta_tokens[None].astype(x.dtype), (B, N_META, D_MODEL))
    h = jnp.concatenate([meta, x], axis=1)
    for l in range(DEPTH):
        h = h + 0.5 * swiglu(rms_norm(h, ffn1_norm[l]), ffn1_w_gate[l], ffn1_w_up[l], ffn1_w_down[l])
        h = h + token_mixer(rms_norm(h, mix_norm[l]), w_in[l], conv_w[l], conv_b[l],
                            filt_w1[l], filt_b1[l], filt_w2[l], filt_b2[l], filt_w3[l], filt_freq[l],
                            hyena_d[l], hyena_norm[l], gk_w2[l], gk_b2[l], gla_norm[l], w_out[l])
        h = h + 0.5 * swiglu(rms_norm(h, ffn2_norm[l]), ffn2_w_gate[l], ffn2_w_up[l], ffn2_w_down[l])
    h = rms_norm(h, final_norm)
    return h[:, N_META:]
```

```python
import functools

import numpy as np
import jax
import jax.numpy as jnp
from jax import lax
from jax.experimental import pallas as pl
from jax.experimental.pallas import tpu as pltpu

F32 = jnp.float32
BF16 = jnp.bfloat16

D_MODEL = 2048
BATCH = 2
SEQ = 4096
N_META = 16
L_TOK = SEQ + N_META
PAD = 112
TL = PAD + L_TOK
ROWS = BATCH * TL
HY_WIDTH = 1024
HY_GROUPS = 8
HY_GROUP = HY_WIDTH // HY_GROUPS
FILT_EMB = 33
FILT_BANDS = 16
FILT_HIDDEN = 64
GLA_WIDTH = 1024
GLA_HEADS = 4
GLA_KEY_WIDTH = 512
GLA_DK = 128
GLA_DV = 256
GATE_RANK = 16
GATE_NORMALIZER = 16.0
CHUNK = 64
SUB = 16
D_FF = 5632
P_MAIN = 3 * HY_WIDTH + 2 * GLA_KEY_WIDTH + 2 * GLA_WIDTH
EPS = 1e-6

FFT_N1 = 72
FFT_N2 = 128
NFFT = FFT_N1 * FFT_N2
FFT_R1 = 40
TFFT = FFT_R1 * FFT_N2
FFT_COLS = FFT_N2 * HY_WIDTH

VMEM_LIMIT = 60 * 1024 * 1024


def _params(*sem):
    return pltpu.CompilerParams(dimension_semantics=sem, vmem_limit_bytes=VMEM_LIMIT)


def _rms(x):
    return x * lax.rsqrt(jnp.mean(x * x, axis=-1, keepdims=True) + EPS)


def _silu(x):
    return x * jax.nn.sigmoid(x)


FFN_TM = 768
FFN_TF = 512


def _ffn_body(x_ref, gain_ref, wg_ref, wu_ref, wd_ref, fgain_ref, o_ref, xn_ref, acc_ref, *, final):
    j = pl.program_id(1)

    @pl.when(j == 0)
    def _():
        xn_ref[...] = (_rms(x_ref[...]) * gain_ref[...]).astype(BF16)
        acc_ref[...] = jnp.zeros_like(acc_ref)

    xn = xn_ref[...]
    g = jnp.dot(xn, wg_ref[...], preferred_element_type=F32)
    u = jnp.dot(xn, wu_ref[...], preferred_element_type=F32)
    a = (_silu(g) * u).astype(BF16)
    acc_ref[...] += jnp.dot(a, wd_ref[...], preferred_element_type=F32)

    @pl.when(j == pl.num_programs(1) - 1)
    def _():
        h = x_ref[...] + 0.5 * acc_ref[...]
        if final:
            h = _rms(h) * fgain_ref[...]
        o_ref[...] = h


def _ffn(h, gain, wg, wu, wd, fgain, final):
    rows = h.shape[0]
    return pl.pallas_call(
        functools.partial(_ffn_body, final=final),
        out_shape=jax.ShapeDtypeStruct((rows, D_MODEL), F32),
        grid=(rows // FFN_TM, D_FF // FFN_TF),
        in_specs=[
            pl.BlockSpec((FFN_TM, D_MODEL), lambda i, j: (i, 0)),
            pl.BlockSpec((1, D_MODEL), lambda i, j: (0, 0)),
            pl.BlockSpec((D_MODEL, FFN_TF), lambda i, j: (0, j)),
            pl.BlockSpec((D_MODEL, FFN_TF), lambda i, j: (0, j)),
            pl.BlockSpec((FFN_TF, D_MODEL), lambda i, j: (j, 0)),
            pl.BlockSpec((1, D_MODEL), lambda i, j: (0, 0)),
        ],
        out_specs=pl.BlockSpec((FFN_TM, D_MODEL), lambda i, j: (i, 0)),
        scratch_shapes=[pltpu.VMEM((FFN_TM, D_MODEL), BF16), pltpu.VMEM((FFN_TM, D_MODEL), F32)],
        compiler_params=_params("parallel", "arbitrary"),
        name="ffn_final" if final else "ffn",
    )(h, gain.reshape(1, -1), wg.astype(BF16), wu.astype(BF16), wd.astype(BF16), fgain.reshape(1, -1))


INP_TM = 768
INP_TN = 512
LR_PAD = 128


def _inproj_body(x_ref, gain_ref, w_ref, wlr_ref, w2_ref, b2_ref, p_ref, lg_ref, xn_ref):
    j = pl.program_id(1)

    @pl.when(j == 0)
    def _():
        xn = (_rms(x_ref[...]) * gain_ref[...]).astype(BF16)
        xn_ref[...] = xn
        lr = jnp.dot(xn, wlr_ref[...], preferred_element_type=F32).astype(BF16)
        z = jnp.dot(lr, w2_ref[...], preferred_element_type=F32) + b2_ref[...]
        lg_ref[...] = (jnp.minimum(z, 0.0) - jnp.log1p(jnp.exp(-jnp.abs(z)))) * (1.0 / GATE_NORMALIZER)

    p_ref[...] = jnp.dot(xn_ref[...], w_ref[...], preferred_element_type=F32)


def _inproj(h, gain, w_in, gk_w2, gk_b2):
    w_main = w_in[:, :P_MAIN].astype(BF16)
    w_lr = jnp.pad(w_in[:, P_MAIN:], ((0, 0), (0, LR_PAD - 2 * GATE_RANK))).astype(BF16)
    w2 = jnp.zeros((LR_PAD, 2 * GLA_KEY_WIDTH), F32)
    w2 = w2.at[:GATE_RANK, :GLA_KEY_WIDTH].set(gk_w2[0])
    w2 = w2.at[GATE_RANK:2 * GATE_RANK, GLA_KEY_WIDTH:].set(gk_w2[1]).astype(BF16)
    b2 = gk_b2.reshape(1, 2 * GLA_KEY_WIDTH)
    return pl.pallas_call(
        _inproj_body,
        out_shape=(jax.ShapeDtypeStruct((ROWS, P_MAIN), F32),
                   jax.ShapeDtypeStruct((ROWS, 2 * GLA_KEY_WIDTH), F32)),
        grid=(ROWS // INP_TM, P_MAIN // INP_TN),
        in_specs=[
            pl.BlockSpec((INP_TM, D_MODEL), lambda i, j: (i, 0)),
            pl.BlockSpec((1, D_MODEL), lambda i, j: (0, 0)),
            pl.BlockSpec((D_MODEL, INP_TN), lambda i, j: (0, j)),
            pl.BlockSpec((D_MODEL, LR_PAD), lambda i, j: (0, 0)),
            pl.BlockSpec((LR_PAD, 2 * GLA_KEY_WIDTH), lambda i, j: (0, 0)),
            pl.BlockSpec((1, 2 * GLA_KEY_WIDTH), lambda i, j: (0, 0)),
        ],
        out_specs=(pl.BlockSpec((INP_TM, INP_TN), lambda i, j: (i, j)),
                   pl.BlockSpec((INP_TM, 2 * GLA_KEY_WIDTH), lambda i, j: (i, 0))),
        scratch_shapes=[pltpu.VMEM((INP_TM, D_MODEL), BF16)],
        compiler_params=_params("parallel", "arbitrary"),
        name="inproj",
    )(h, gain.reshape(1, -1), w_main, w_lr, w2, b2)


FILT_TR = 528
FEAT_PAD = 128


def _filt_tables():
    pos = np.arange(TL, dtype=np.float64)
    t = pos / (L_TOK - 1)
    w = (2.0 * np.pi / L_TOK) * pos
    bands = 1e-4 + np.arange(FILT_BANDS, dtype=np.float64) * ((FILT_BANDS - 1 - 1e-4) / (FILT_BANDS - 1))
    ang = w[:, None] * bands[None, :]
    feats = np.zeros((TL, FEAT_PAD), np.float64)
    feats[:, 0] = t
    feats[:, 1:1 + FILT_BANDS] = np.cos(ang)
    feats[:, 1 + FILT_BANDS:FILT_EMB] = -np.sin(ang)
    lo, hi = np.log(1e-2) / 1.5, np.log(1e-2) / 0.3
    deltas = np.abs(lo + np.arange(HY_WIDTH, dtype=np.float64) * ((hi - lo) / (HY_WIDTH - 1)))
    return feats.astype(np.float32), deltas.astype(np.float32).reshape(1, HY_WIDTH)


def _filt_body(feat_ref, w1_ref, b1_ref, w2_ref, b2_ref, w3_ref, fr_ref, dl_ref, hf_ref, hb_ref):
    hp = lax.Precision.HIGHEST
    feats = feat_ref[...]
    fr = fr_ref[...]
    z = jnp.sin(fr * (jnp.dot(feats, w1_ref[...], precision=hp, preferred_element_type=F32) + b1_ref[...]))
    z = jnp.sin(fr * (jnp.dot(z, w2_ref[...], precision=hp, preferred_element_type=F32) + b2_ref[...]))
    hh = jnp.dot(z, w3_ref[...], precision=hp, preferred_element_type=F32)
    pos = pl.program_id(0) * FILT_TR + lax.broadcasted_iota(jnp.int32, (FILT_TR, 1), 0)
    win = jnp.exp(-feats[:, 0:1] * dl_ref[...])
    win = jnp.where(pos < L_TOK, win, 0.0)
    hf = hh[:, :HY_WIDTH] * win
    hb = hh[:, HY_WIDTH:] * win
    hf_ref[...] = jnp.where(pos == 0, hf + hb, hf).astype(BF16)
    hb_ref[...] = jnp.where(pos == 0, 0.0, hb).astype(BF16)


def _filters(w1, b1, w2, b2, w3, freq):
    feats, deltas = _filt_tables()
    w1p = jnp.pad(w1, ((0, FEAT_PAD - FILT_EMB), (0, 0)))
    full = lambda shape: pl.BlockSpec(shape, lambda i: (0, 0))
    return pl.pallas_call(
        _filt_body,
        out_shape=(jax.ShapeDtypeStruct((TL, HY_WIDTH), BF16), jax.ShapeDtypeStruct((TL, HY_WIDTH), BF16)),
        grid=(TL // FILT_TR,),
        in_specs=[
            pl.BlockSpec((FILT_TR, FEAT_PAD), lambda i: (i, 0)),
            full((FEAT_PAD, FILT_HIDDEN)), full((1, FILT_HIDDEN)),
            full((FILT_HIDDEN, FILT_HIDDEN)), full((1, FILT_HIDDEN)),
            full((FILT_HIDDEN, 2 * HY_WIDTH)), full((1, FILT_HIDDEN)), full((1, HY_WIDTH)),
        ],
        out_specs=(pl.BlockSpec((FILT_TR, HY_WIDTH), lambda i: (i, 0)),
                   pl.BlockSpec((FILT_TR, HY_WIDTH), lambda i: (i, 0))),
        compiler_params=_params("parallel"),
        name="filt",
    )(jnp.asarray(feats), w1p, b1.reshape(1, -1), w2, b2.reshape(1, -1), w3, freq.reshape(1, -1),
      jnp.asarray(deltas))


def _dft_tables():
    n1 = np.arange(FFT_N1)
    ang1 = 2.0 * np.pi * ((n1[:, None] * n1[None, :]) % FFT_N1) / FFT_N1
    c1, s1 = np.cos(ang1), np.sin(ang1)
    f1 = np.block([[c1[:, :FFT_R1], s1[:, :FFT_R1]], [-s1[:, :FFT_R1], c1[:, :FFT_R1]]])
    fg = np.zeros((2 * FFT_N1, 2 * FFT_R1))
    fg[:FFT_N1, :FFT_N1] = c1
    fg[FFT_N1:, :FFT_N1] = -s1
    ci, si = c1.T[:FFT_R1], s1.T[:FFT_R1]
    f3 = np.block([[ci, -si], [si, ci]])
    k1 = np.arange(FFT_N1)[:, None, None]
    k2 = np.arange(FFT_N2)[None, :, None]
    n2 = np.arange(FFT_N2)[None, None, :]
    ang2 = 2.0 * np.pi * ((n2 * (k1 + FFT_N1 * k2)) % NFFT) / NFFT
    c2, s2 = np.cos(ang2), np.sin(ang2)
    mf = np.concatenate([np.concatenate([c2, s2], axis=2), np.concatenate([-s2, c2], axis=2)], axis=1)
    c2t, s2t = np.swapaxes(c2, 1, 2), np.swapaxes(s2, 1, 2)
    mi = np.concatenate([np.concatenate([c2t, -s2t], axis=2), np.concatenate([s2t, c2t], axis=2)], axis=1)
    as32 = lambda a: jnp.asarray(a.astype(np.float32)).astype(BF16)
    return as32(f1), as32(fg), as32(f3), as32(mf), as32(mi)


LMM_TC = 4096


def _lmm_body(f_ref, x_ref, o_ref, *, scale):
    r = jnp.dot(f_ref[...], x_ref[...], preferred_element_type=F32)
    if scale != 1.0:
        r = r * scale
    o_ref[...] = r.astype(o_ref.dtype)


def _lmm(f, x, out_dtype, scale=1.0, name="lmm"):
    m, k = f.shape
    cols = x.shape[1]
    return pl.pallas_call(
        functools.partial(_lmm_body, scale=scale),
        out_shape=jax.ShapeDtypeStruct((m, cols), out_dtype),
        grid=(cols // LMM_TC,),
        in_specs=[pl.BlockSpec((m, k), lambda j: (0, 0)), pl.BlockSpec((k, LMM_TC), lambda j: (0, j))],
        out_specs=pl.BlockSpec((m, LMM_TC), lambda j: (0, j)),
        compiler_params=_params("parallel"),
        name=name,
    )(f, x)


def _gspec_body(mf_ref, a_ref, g_ref):
    a = a_ref[...].reshape(2 * FFT_N2, HY_WIDTH)
    g_ref[...] = jnp.dot(mf_ref[...], a, preferred_element_type=F32).reshape(2, 1, FFT_N2, HY_WIDTH)


def _gspec(mf, ag):
    blk = pl.BlockSpec((2, 1, FFT_N2, HY_WIDTH), lambda i: (0, i, 0, 0))
    return pl.pallas_call(
        _gspec_body,
        out_shape=jax.ShapeDtypeStruct((2, FFT_N1, FFT_N2, HY_WIDTH), F32),
        grid=(FFT_N1,),
        in_specs=[pl.BlockSpec((None, 2 * FFT_N2, 2 * FFT_N2), lambda i: (i, 0, 0)), blk],
        out_specs=blk,
        compiler_params=_params("parallel"),
        name="gspec",
    )(mf, ag)


def _spec_body(mf_ref, mi_ref, a_ref, g_ref, o_ref):
    a = a_ref[...].reshape(2 * FFT_N2, HY_WIDTH)
    x = jnp.dot(mf_ref[...], a, preferred_element_type=F32)
    xr, xi = x[:FFT_N2], x[FFT_N2:]
    gr, gi = g_ref[0, 0], g_ref[1, 0]
    y = jnp.concatenate([xr * gr - xi * gi, xr * gi + xi * gr], axis=0).astype(BF16)
    o_ref[...] = jnp.dot(mi_ref[...], y, preferred_element_type=F32).astype(BF16).reshape(
        2, 1, FFT_N2, HY_WIDTH)


def _spec(mf, mi, a, g):
    blk = pl.BlockSpec((2, 1, FFT_N2, HY_WIDTH), lambda i: (0, i, 0, 0))
    mat = pl.BlockSpec((None, 2 * FFT_N2, 2 * FFT_N2), lambda i: (i, 0, 0))
    return pl.pallas_call(
        _spec_body,
        out_shape=jax.ShapeDtypeStruct((2, FFT_N1, FFT_N2, HY_WIDTH), BF16),
        grid=(FFT_N1,),
        in_specs=[mat, mat, blk, blk],
        out_specs=blk,
        compiler_params=_params("parallel"),
        name="spec",
    )(mf, mi, a, g)


HY_CB = 256


def _short_conv(p_ref, w_ref, b_ref):
    p = p_ref[...]
    w = w_ref[...]
    prev = pltpu.roll(p, 1, 0)
    nxt = pltpu.roll(p, TL - 1, 0)
    return b_ref[...] + prev * w[0:1] + p * w[1:2] + nxt * w[2:3]


def _uconv_body(x1_ref, vh_ref, w1_ref, wv_ref, b1_ref, bv_ref, u_ref):
    u = _short_conv(vh_ref, wv_ref, bv_ref) * _short_conv(x1_ref, w1_ref, b1_ref)
    row = lax.broadcasted_iota(jnp.int32, (TL, 1), 0)
    u_ref[pl.ds(0, TL), :] = jnp.where(row >= PAD, u, 0.0).astype(BF16)
    u_ref[pl.ds(TL, TFFT - TL), :] = jnp.zeros((TFFT - TL, HY_CB), BF16)


def _hy_specs(first_block):
    nb = HY_WIDTH // HY_CB
    return (pl.BlockSpec((None, TL, HY_CB), lambda b, j: (b, 0, first_block * nb + j)),
            pl.BlockSpec((3, HY_CB), lambda b, j: (0, first_block * nb + j)),
            pl.BlockSpec((1, HY_CB), lambda b, j: (0, first_block * nb + j)))


def _uconv(p3, conv_w, conv_b):
    x1, w1, b1 = _hy_specs(1)
    vh, wv, bv = _hy_specs(2)
    return pl.pallas_call(
        _uconv_body,
        out_shape=jax.ShapeDtypeStruct((BATCH, TFFT, HY_WIDTH), BF16),
        grid=(BATCH, HY_WIDTH // HY_CB),
        in_specs=[x1, vh, w1, wv, b1, bv],
        out_specs=pl.BlockSpec((None, TFFT, HY_CB), lambda b, j: (b, 0, j)),
        compiler_params=_params("parallel", "parallel"),
        name="uconv",
    )(p3, p3, conv_w, conv_w, conv_b, conv_b)


def _ymix_body(y_ref, x0_ref, x1_ref, vh_ref, w0_ref, w1_ref, wv_ref, b0_ref, b1_ref, bv_ref,
               d_ref, gain_ref, o_ref):
    u = _short_conv(vh_ref, wv_ref, bv_ref) * _short_conv(x1_ref, w1_ref, b1_ref)
    yy = (y_ref[...] + d_ref[...] * u) * _short_conv(x0_ref, w0_ref, b0_ref)
    gain = gain_ref[...]
    for s in range(0, HY_CB, HY_GROUP):
        o_ref[:, s:s + HY_GROUP] = (_rms(yy[:, s:s + HY_GROUP]) * gain[:, s:s + HY_GROUP]).astype(BF16)


def _ymix(y3, p3, conv_w, conv_b, hyena_d, hyena_norm):
    x0, w0, b0 = _hy_specs(0)
    x1, w1, b1 = _hy_specs(1)
    vh, wv, bv = _hy_specs(2)
    vec = pl.BlockSpec((1, HY_CB), lambda b, j: (0, j))
    return pl.pallas_call(
        _ymix_body,
        out_shape=jax.ShapeDtypeStruct((BATCH, TL, HY_WIDTH), BF16),
        grid=(BATCH, HY_WIDTH // HY_CB),
        in_specs=[pl.BlockSpec((None, TL, HY_CB), lambda b, j: (b, 0, j)),
                  x0, x1, vh, w0, w1, wv, b0, b1, bv, vec, vec],
        out_specs=pl.BlockSpec((None, TL, HY_CB), lambda b, j: (b, 0, j)),
        compiler_params=_params("parallel", "parallel"),
        name="ymix",
    )(y3, p3, p3, p3, conv_w, conv_w, conv_w, conv_b, conv_b, conv_b,
      hyena_d.reshape(1, -1), hyena_norm.reshape(1, -1))


N_CHUNK = TL // CHUNK
N_SUB = CHUNK // SUB


def _gla_chunk(c, rev, q_ref, k_ref, v_ref, g_ref, st_ref, o_ref, tri, ones, row, col):
    rows = pl.ds(pl.multiple_of(c * CHUNK, CHUNK), CHUNK)
    q = q_ref[rows, :] * (GLA_DK ** -0.5)
    k = k_ref[rows, :]
    v = v_ref[rows, :].astype(BF16)
    g = g_ref[rows, :]
    b = jnp.dot(tri, g, precision=lax.Precision.HIGHEST, preferred_element_type=F32)

    rsub = row % SUB
    terms = []
    for d in range(SUB):
        if d == 0:
            kr, br = k, b
        else:
            sh = CHUNK - d if rev else d
            kr, br = pltpu.roll(k, sh, 0), pltpu.roll(b, sh, 0)
        valid = (rsub + d < SUB) if rev else (rsub >= d)
        t = q * kr * jnp.exp(jnp.minimum(b - br, 0.0))
        terms.append(jnp.where(valid, t, 0.0).astype(BF16))
    sums = jnp.dot(jnp.concatenate(terms, axis=0), ones, preferred_element_type=F32)
    a = jnp.zeros((CHUNK, CHUNK), F32)
    for d in range(SUB):
        tgt = row + d if rev else row - d
        a = jnp.where(col == tgt, sums[d * CHUNK:(d + 1) * CHUNK, :CHUNK], a)

    rblk = row // SUB
    cblk = col // SUB
    for jb in (range(1, N_SUB) if rev else range(N_SUB - 1)):
        e = jb * SUB if rev else jb * SUB + SUB - 1
        ref = b[e:e + 1, :]
        qh = (q * jnp.exp(jnp.minimum(b - ref, 0.0))).astype(BF16)
        kh = (k * jnp.exp(jnp.minimum(ref - b, 0.0))).astype(BF16)
        pm = lax.dot_general(qh, kh, (((1,), (1,)), ((), ())), preferred_element_type=F32)
        side = jnp.where(cblk == jb, rblk, jb)
        a = jnp.where((side < jb) if rev else (side > jb), pm, a)

    st = st_ref[...]
    qt = (q * jnp.exp(b)).astype(BF16)
    o = lax.dot_general(qt, st.astype(BF16), (((1,), (1,)), ((), ())), preferred_element_type=F32)
    o = o + jnp.dot(a.astype(BF16), v, preferred_element_type=F32)
    o_ref[rows, :] = o

    bend = b[0:1, :] if rev else b[CHUNK - 1:CHUNK, :]
    kt = (k * jnp.exp(bend - b)).astype(BF16)
    st_ref[...] = st * jnp.exp(bend) + lax.dot_general(
        v, kt, (((0,), (0,)), ((), ())), preferred_element_type=F32)


def _gla_body(q_ref, k_ref, v_ref, og_ref, gf_ref, gb_ref, gain_ref, o_ref, of_ref, ob_ref, sf_ref, sb_ref):
    sf_ref[...] = jnp.zeros_like(sf_ref)
    sb_ref[...] = jnp.zeros_like(sb_ref)
    row = lax.broadcasted_iota(jnp.int32, (CHUNK, 1), 0)
    col = lax.broadcasted_iota(jnp.int32, (CHUNK, CHUNK), 1)
    rr = lax.broadcasted_iota(jnp.int32, (CHUNK, CHUNK), 0)
    tril = (col <= rr).astype(F32)
    triu = (col >= rr).astype(F32)
    ones = jnp.ones((GLA_DK, GLA_DK), BF16)

    def step(c, carry):
        _gla_chunk(c, False, q_ref, k_ref, v_ref, gf_ref, sf_ref, of_ref, tril, ones, row, col)
        _gla_chunk(N_CHUNK - 1 - c, True, q_ref, k_ref, v_ref, gb_ref, sb_ref, ob_ref, triu, ones, row, col)
        return carry

    lax.fori_loop(0, N_CHUNK, step, 0)

    def fin(c, carry):
        rows = pl.ds(pl.multiple_of(c * CHUNK, CHUNK), CHUNK)
        o = _rms(of_ref[rows, :] + ob_ref[rows, :]) * gain_ref[...]
        o_ref[rows, :] = (o * _silu(og_ref[rows, :])).astype(BF16)
        return carry

    lax.fori_loop(0, N_CHUNK, fin, 0)


def _gla(p3, lg3, gla_norm):
    qb = (3 * HY_WIDTH) // GLA_DK
    kb = qb + GLA_HEADS
    vb = (3 * HY_WIDTH + 2 * GLA_KEY_WIDTH) // GLA_DV
    gb = vb + GLA_HEADS
    sdk = lambda first: pl.BlockSpec((None, TL, GLA_DK), lambda b, h: (b, 0, first + h))
    sdv = lambda first: pl.BlockSpec((None, TL, GLA_DV), lambda b, h: (b, 0, first + h))
    return pl.pallas_call(
        _gla_body,
        out_shape=jax.ShapeDtypeStruct((BATCH, TL, GLA_WIDTH), BF16),
        grid=(BATCH, GLA_HEADS),
        in_specs=[sdk(qb), sdk(kb), sdv(vb), sdv(gb), sdk(0), sdk(GLA_HEADS),
                  pl.BlockSpec((1, GLA_DV), lambda b, h: (0, 0))],
        out_specs=sdv(0),
        scratch_shapes=[pltpu.VMEM((TL, GLA_DV), F32), pltpu.VMEM((TL, GLA_DV), F32),
                        pltpu.VMEM((GLA_DV, GLA_DK), F32), pltpu.VMEM((GLA_DV, GLA_DK), F32)],
        compiler_params=_params("parallel", "parallel"),
        name="gla",
    )(p3, p3, p3, p3, lg3, lg3, gla_norm.reshape(1, -1))


OUT_TM = 528


def _outproj_body(h_ref, yh_ref, yg_ref, wh_ref, wg_ref, o_ref):
    o_ref[...] = (h_ref[...] + jnp.dot(yh_ref[...], wh_ref[...], preferred_element_type=F32)
                  + jnp.dot(yg_ref[...], wg_ref[...], preferred_element_type=F32))


def _outproj(h, yh, yg, w_out):
    wo = w_out.astype(BF16)
    half = lambda i: pl.BlockSpec((HY_WIDTH, D_MODEL), lambda r: (i, 0))
    return pl.pallas_call(
        _outproj_body,
        out_shape=jax.ShapeDtypeStruct((ROWS, D_MODEL), F32),
        grid=(ROWS // OUT_TM,),
        in_specs=[pl.BlockSpec((OUT_TM, D_MODEL), lambda r: (r, 0)),
                  pl.BlockSpec((OUT_TM, HY_WIDTH), lambda r: (r, 0)),
                  pl.BlockSpec((OUT_TM, GLA_WIDTH), lambda r: (r, 0)),
                  half(0), half(1)],
        out_specs=pl.BlockSpec((OUT_TM, D_MODEL), lambda r: (r, 0)),
        compiler_params=_params("parallel"),
        name="outproj",
    )(h, yh, yg, wo, wo)


def _long_conv(u_fft, hf, hb):
    f1, fg, f3, mf, mi = _dft_tables()
    g = jnp.concatenate([hf[:L_TOK], jnp.zeros((NFFT - 2 * L_TOK + 1, HY_WIDTH), BF16), hb[1:L_TOK][::-1],
                         jnp.zeros((2 * TFFT - NFFT, HY_WIDTH), BF16)], axis=0)
    ag = _lmm(fg, g.reshape(2 * FFT_R1, FFT_COLS), BF16, name="lmm_g")
    gs = _gspec(mf, ag.reshape(2, FFT_N1, FFT_N2, HY_WIDTH))
    a = _lmm(f1, u_fft.reshape(2 * FFT_R1, FFT_COLS), BF16, name="lmm_fwd")
    bm = _spec(mf, mi, a.reshape(2, FFT_N1, FFT_N2, HY_WIDTH), gs)
    y = _lmm(f3, bm.reshape(2 * FFT_N1, FFT_COLS), F32, scale=1.0 / NFFT, name="lmm_inv")
    return y.reshape(BATCH, TFFT, HY_WIDTH)


def kernel(x, meta_tokens, ffn1_norm, ffn1_w_gate, ffn1_w_up, ffn1_w_down, mix_norm, w_in, conv_w, conv_b,
           filt_w1, filt_b1, filt_w2, filt_b2, filt_w3, filt_freq, hyena_d, hyena_norm, gk_w2, gk_b2,
           gla_norm, w_out, ffn2_norm, ffn2_w_gate, ffn2_w_up, ffn2_w_down, final_norm):
    assert x.shape == (BATCH, SEQ, D_MODEL) and ffn1_norm.shape[0] == 1
    meta = jnp.broadcast_to(meta_tokens[None].astype(x.dtype), (BATCH, N_META, D_MODEL))
    h = jnp.concatenate([jnp.zeros((BATCH, PAD, D_MODEL), x.dtype), meta, x], axis=1).reshape(ROWS, D_MODEL)

    h = _ffn(h, ffn1_norm[0], ffn1_w_gate[0], ffn1_w_up[0], ffn1_w_down[0], final_norm, final=False)

    p, lg = _inproj(h, mix_norm[0], w_in[0], gk_w2[0], gk_b2[0])
    p3 = p.reshape(BATCH, TL, P_MAIN)
    lg3 = lg.reshape(BATCH, TL, 2 * GLA_KEY_WIDTH)

    hf, hb = _filters(filt_w1[0], filt_b1[0], filt_w2[0], filt_b2[0], filt_w3[0], filt_freq[0])
    cw, cb = conv_w[0], conv_b[0].reshape(1, -1)
    y3 = _long_conv(_uconv(p3, cw, cb), hf, hb)
    yh = _ymix(y3, p3, cw, cb, hyena_d[0], hyena_norm[0])
    yg = _gla(p3, lg3, gla_norm[0])

    h = _outproj(h, yh.reshape(ROWS, HY_WIDTH), yg.reshape(ROWS, GLA_WIDTH), w_out[0])
    h = _ffn(h, ffn2_norm[0], ffn2_w_gate[0], ffn2_w_up[0], ffn2_w_down[0], final_norm, final=True)
    return h.reshape(BATCH, TL, D_MODEL)[:, PAD + N_META:]
```

```python
import functools

import numpy as np
import jax
import jax.numpy as jnp
from jax import lax
from jax.experimental import pallas as pl
from jax.experimental.pallas import tpu as pltpu

F32 = jnp.float32
BF16 = jnp.bfloat16

D_MODEL = 2048
BATCH = 2
SEQ = 4096
N_META = 16
L_TOK = SEQ + N_META
PAD = 112
X0 = PAD + N_META
TL = PAD + L_TOK
ROWS = BATCH * TL
X_ROWS = BATCH * SEQ
HY_WIDTH = 1024
HY_GROUPS = 8
HY_GROUP = HY_WIDTH // HY_GROUPS
FILT_EMB = 33
FILT_BANDS = 16
FILT_HIDDEN = 64
GLA_WIDTH = 1024
GLA_HEADS = 4
GLA_KEY_WIDTH = 512
GLA_DK = 128
GLA_DV = 256
GATE_RANK = 16
GATE_NORMALIZER = 16.0
CHUNK = 64
SUB = 16
D_FF = 5632
P_MAIN = 3 * HY_WIDTH + 2 * GLA_KEY_WIDTH + 2 * GLA_WIDTH
EPS = 1e-6

FFT_N1 = 72
FFT_N2 = 128
NFFT = FFT_N1 * FFT_N2
FFT_R1 = 40
TFFT = FFT_R1 * FFT_N2

MIB = 1024 * 1024
SUBLANE = 8


def _params(sem, vmem_mib):
    return pltpu.CompilerParams(dimension_semantics=sem, vmem_limit_bytes=vmem_mib * MIB)


def _rms(x):
    return x * lax.rsqrt(jnp.mean(x * x, axis=-1, keepdims=True) + EPS)


def _silu(x):
    return x * jax.nn.sigmoid(x)


def _x_row(i, tm):
    per_batch = SEQ // tm
    r8 = (i // per_batch) * (TL // SUBLANE) + X0 // SUBLANE + (i % per_batch) * (tm // SUBLANE)
    return pl.multiple_of(r8 * SUBLANE, SUBLANE)


def _meta_rows(meta_tokens):
    return jnp.concatenate([jnp.zeros((PAD, D_MODEL), F32), meta_tokens.astype(F32)], axis=0)


FFN_TM = 512
FFN_TF = 512


def _ffn_body(x_ref, gain_ref, wg_ref, wu_ref, wd_ref, fgain_ref, *rest, final, j_axis, shared):
    o_ref, xn_ref, acc_ref = rest[-3:]
    j = pl.program_id(j_axis)

    @pl.when(j == 0)
    def _():
        xn_ref[...] = (_rms(x_ref[...]) * gain_ref[...]).astype(BF16)
        acc_ref[...] = jnp.zeros_like(acc_ref)

    xn = xn_ref[...]
    g = jnp.dot(xn, wg_ref[...], preferred_element_type=F32)
    u = jnp.dot(xn, wu_ref[...], preferred_element_type=F32)
    a = (_silu(g) * u).astype(BF16)
    acc_ref[...] += jnp.dot(a, wd_ref[...], preferred_element_type=F32)

    @pl.when(j == pl.num_programs(j_axis) - 1)
    def _():
        h = x_ref[...] + 0.5 * acc_ref[...]
        if final:
            h = _rms(h) * fgain_ref[...]
        if shared:
            for b in range(BATCH):
                o_ref[b] = h
        else:
            o_ref[...] = h


def _ffn_weights(gain, wg, wu, wd, fgain):
    return gain.reshape(1, -1), wg.astype(BF16), wu.astype(BF16), wd.astype(BF16), fgain.reshape(1, -1)


def _ffn(x2, weights, final, to_layout):
    tm = FFN_TM
    if to_layout:
        out_shape = jax.ShapeDtypeStruct((ROWS, D_MODEL), F32)
        out_spec = pl.BlockSpec((pl.Element(tm), pl.Element(D_MODEL)), lambda i, j: (_x_row(i, tm), 0))
    else:
        out_shape = jax.ShapeDtypeStruct((X_ROWS, D_MODEL), F32)
        out_spec = pl.BlockSpec((tm, D_MODEL), lambda i, j: (i, 0))
    return pl.pallas_call(
        functools.partial(_ffn_body, final=final, j_axis=1, shared=False),
        out_shape=out_shape,
        grid=(X_ROWS // tm, D_FF // FFN_TF),
        in_specs=[
            pl.BlockSpec((tm, D_MODEL), lambda i, j: (i, 0)),
            pl.BlockSpec((1, D_MODEL), lambda i, j: (0, 0)),
            pl.BlockSpec((D_MODEL, FFN_TF), lambda i, j: (0, j)),
            pl.BlockSpec((D_MODEL, FFN_TF), lambda i, j: (0, j)),
            pl.BlockSpec((FFN_TF, D_MODEL), lambda i, j: (j, 0)),
            pl.BlockSpec((1, D_MODEL), lambda i, j: (0, 0)),
        ],
        out_specs=out_spec,
        scratch_shapes=[pltpu.VMEM((tm, D_MODEL), BF16), pltpu.VMEM((tm, D_MODEL), F32)],
        compiler_params=_params(("parallel", "arbitrary"), 48),
        name="ffn_final" if final else "ffn",
    )(x2, *weights)


def _ffn_shared(rows, weights, buf):
    return pl.pallas_call(
        functools.partial(_ffn_body, final=False, j_axis=0, shared=True),
        out_shape=jax.ShapeDtypeStruct((BATCH, TL, D_MODEL), F32),
        grid=(D_FF // FFN_TF,),
        in_specs=[
            pl.BlockSpec((X0, D_MODEL), lambda j: (0, 0)),
            pl.BlockSpec((1, D_MODEL), lambda j: (0, 0)),
            pl.BlockSpec((D_MODEL, FFN_TF), lambda j: (0, j)),
            pl.BlockSpec((D_MODEL, FFN_TF), lambda j: (0, j)),
            pl.BlockSpec((FFN_TF, D_MODEL), lambda j: (j, 0)),
            pl.BlockSpec((1, D_MODEL), lambda j: (0, 0)),
            pl.BlockSpec(memory_space=pl.ANY),
        ],
        out_specs=pl.BlockSpec((BATCH, X0, D_MODEL), lambda j: (0, 0, 0)),
        scratch_shapes=[pltpu.VMEM((X0, D_MODEL), BF16), pltpu.VMEM((X0, D_MODEL), F32)],
        input_output_aliases={6: 0},
        compiler_params=_params(("arbitrary",), 32),
        name="ffn_shared",
    )(rows, *weights, buf.reshape(BATCH, TL, D_MODEL))


INP_TM = 512
INP_TN = 1024
LR_PAD = 128


def _inproj_body(x_ref, gain_ref, w_ref, wlr_ref, w2_ref, b2_ref, *rest, j_axis, shared):
    p_ref, lg_ref, xn_ref = rest[-3:]
    j = pl.program_id(j_axis)

    def put(ref, val):
        if shared:
            for b in range(BATCH):
                ref[b] = val
        else:
            ref[...] = val

    @pl.when(j == 0)
    def _():
        xn = (_rms(x_ref[...]) * gain_ref[...]).astype(BF16)
        xn_ref[...] = xn
        lr = jnp.dot(xn, wlr_ref[...], preferred_element_type=F32).astype(BF16)
        z = jnp.dot(lr, w2_ref[...], preferred_element_type=F32) + b2_ref[...]
        put(lg_ref, (jnp.minimum(z, 0.0) - jnp.log1p(jnp.exp(-jnp.abs(z)))) * (1.0 / GATE_NORMALIZER))

    put(p_ref, jnp.dot(xn_ref[...], w_ref[...], preferred_element_type=F32))


def _inproj_weights(gain, w_in, gk_w2, gk_b2):
    w_main = w_in[:, :P_MAIN].astype(BF16)
    w_lr = jnp.pad(w_in[:, P_MAIN:], ((0, 0), (0, LR_PAD - 2 * GATE_RANK))).astype(BF16)
    w2 = jnp.zeros((LR_PAD, 2 * GLA_KEY_WIDTH), F32)
    w2 = w2.at[:GATE_RANK, :GLA_KEY_WIDTH].set(gk_w2[0])
    w2 = w2.at[GATE_RANK:2 * GATE_RANK, GLA_KEY_WIDTH:].set(gk_w2[1]).astype(BF16)
    return gain.reshape(1, -1), w_main, w_lr, w2, gk_b2.reshape(1, 2 * GLA_KEY_WIDTH)


def _inproj(hbuf, weights):
    tm, tn = INP_TM, INP_TN
    row = lambda i, j: (_x_row(i, tm), 0)
    return pl.pallas_call(
        functools.partial(_inproj_body, j_axis=1, shared=False),
        out_shape=(jax.ShapeDtypeStruct((ROWS, P_MAIN), F32),
                   jax.ShapeDtypeStruct((ROWS, 2 * GLA_KEY_WIDTH), F32)),
        grid=(X_ROWS // tm, P_MAIN // tn),
        in_specs=[
            pl.BlockSpec((pl.Element(tm), pl.Element(D_MODEL)), row),
            pl.BlockSpec((1, D_MODEL), lambda i, j: (0, 0)),
            pl.BlockSpec((D_MODEL, tn), lambda i, j: (0, j)),
            pl.BlockSpec((D_MODEL, LR_PAD), lambda i, j: (0, 0)),
            pl.BlockSpec((LR_PAD, 2 * GLA_KEY_WIDTH), lambda i, j: (0, 0)),
            pl.BlockSpec((1, 2 * GLA_KEY_WIDTH), lambda i, j: (0, 0)),
        ],
        out_specs=(pl.BlockSpec((pl.Element(tm), pl.Element(tn)),
                                lambda i, j: (_x_row(i, tm), pl.multiple_of(j * tn, tn))),
                   pl.BlockSpec((pl.Element(tm), pl.Element(2 * GLA_KEY_WIDTH)), row)),
        scratch_shapes=[pltpu.VMEM((tm, D_MODEL), BF16)],
        compiler_params=_params(("parallel", "arbitrary"), 40),
        name="inproj",
    )(hbuf, *weights)


def _inproj_shared(hbuf, weights, pbuf, lgbuf):
    tn = INP_TN
    return pl.pallas_call(
        functools.partial(_inproj_body, j_axis=0, shared=True),
        out_shape=(jax.ShapeDtypeStruct((BATCH, TL, P_MAIN), F32),
                   jax.ShapeDtypeStruct((BATCH, TL, 2 * GLA_KEY_WIDTH), F32)),
        grid=(P_MAIN // tn,),
        in_specs=[
            pl.BlockSpec((None, X0, D_MODEL), lambda j: (0, 0, 0)),
            pl.BlockSpec((1, D_MODEL), lambda j: (0, 0)),
            pl.BlockSpec((D_MODEL, tn), lambda j: (0, j)),
            pl.BlockSpec((D_MODEL, LR_PAD), lambda j: (0, 0)),
            pl.BlockSpec((LR_PAD, 2 * GLA_KEY_WIDTH), lambda j: (0, 0)),
            pl.BlockSpec((1, 2 * GLA_KEY_WIDTH), lambda j: (0, 0)),
            pl.BlockSpec(memory_space=pl.ANY),
            pl.BlockSpec(memory_space=pl.ANY),
        ],
        out_specs=(pl.BlockSpec((BATCH, X0, tn), lambda j: (0, 0, j)),
                   pl.BlockSpec((BATCH, X0, 2 * GLA_KEY_WIDTH), lambda j: (0, 0, 0))),
        scratch_shapes=[pltpu.VMEM((X0, D_MODEL), BF16)],
        input_output_aliases={6: 0, 7: 1},
        compiler_params=_params(("arbitrary",), 32),
        name="inproj_shared",
    )(hbuf.reshape(BATCH, TL, D_MODEL), *weights, pbuf.reshape(BATCH, TL, P_MAIN),
      lgbuf.reshape(BATCH, TL, 2 * GLA_KEY_WIDTH))


FILT_TR = 640
FEAT_PAD = 128


def _filt_tables():
    pos = np.arange(TFFT, dtype=np.float64)
    t = pos / (L_TOK - 1)
    w = (2.0 * np.pi / L_TOK) * pos
    bands = 1e-4 + np.arange(FILT_BANDS, dtype=np.float64) * ((FILT_BANDS - 1 - 1e-4) / (FILT_BANDS - 1))
    ang = w[:, None] * bands[None, :]
    feats = np.zeros((TFFT, FEAT_PAD), np.float64)
    feats[:, 0] = t
    feats[:, 1:1 + FILT_BANDS] = np.cos(ang)
    feats[:, 1 + FILT_BANDS:FILT_EMB] = -np.sin(ang)
    lo, hi = np.log(1e-2) / 1.5, np.log(1e-2) / 0.3
    deltas = np.abs(lo + np.arange(HY_WIDTH, dtype=np.float64) * ((hi - lo) / (HY_WIDTH - 1)))
    return feats.astype(np.float32), deltas.astype(np.float32).reshape(1, HY_WIDTH)


def _filt_body(feat_ref, w1_ref, b1_ref, w2_ref, b2_ref, w3_ref, fr_ref, dl_ref, h_ref):
    hp = lax.Precision.HIGHEST
    feats = feat_ref[...]
    fr = fr_ref[...]
    z = jnp.sin(fr * (jnp.dot(feats, w1_ref[...], precision=hp, preferred_element_type=F32) + b1_ref[...]))
    z = jnp.sin(fr * (jnp.dot(z, w2_ref[...], precision=hp, preferred_element_type=F32) + b2_ref[...]))
    hh = jnp.dot(z, w3_ref[...], precision=hp, preferred_element_type=F32)
    pos = pl.program_id(0) * FILT_TR + lax.broadcasted_iota(jnp.int32, (FILT_TR, 1), 0)
    win = jnp.exp(-feats[:, 0:1] * dl_ref[...])
    win = jnp.where(pos < L_TOK, win, 0.0)
    hf = hh[:, :HY_WIDTH] * win
    hb = hh[:, HY_WIDTH:] * win
    h_ref[0] = jnp.where(pos == 0, hf + hb, hf)
    h_ref[1] = jnp.where(pos == 0, 0.0, hb)


def _filters(w1, b1, w2, b2, w3, freq):
    feats, deltas = _filt_tables()
    w1p = jnp.pad(w1, ((0, FEAT_PAD - FILT_EMB), (0, 0)))
    full = lambda shape: pl.BlockSpec(shape, lambda i: (0, 0))
    return pl.pallas_call(
        _filt_body,
        out_shape=jax.ShapeDtypeStruct((2, TFFT, HY_WIDTH), F32),
        grid=(TFFT // FILT_TR,),
        in_specs=[
            pl.BlockSpec((FILT_TR, FEAT_PAD), lambda i: (i, 0)),
            full((FEAT_PAD, FILT_HIDDEN)), full((1, FILT_HIDDEN)),
            full((FILT_HIDDEN, FILT_HIDDEN)), full((1, FILT_HIDDEN)),
            full((FILT_HIDDEN, 2 * HY_WIDTH)), full((1, FILT_HIDDEN)), full((1, HY_WIDTH)),
        ],
        out_specs=pl.BlockSpec((2, FILT_TR, HY_WIDTH), lambda i: (0, i, 0)),
        compiler_params=_params(("parallel",), 40),
        name="filt",
    )(jnp.asarray(feats), w1p, b1.reshape(1, -1), w2, b2.reshape(1, -1), w3, freq.reshape(1, -1),
      jnp.asarray(deltas))


def _dft_tables():
    n1 = np.arange(FFT_N1)
    ang1 = 2.0 * np.pi * ((n1[:, None] * n1[None, :]) % FFT_N1) / FFT_N1
    c1, s1 = np.cos(ang1)[:, :FFT_R1], np.sin(ang1)[:, :FFT_R1]
    zero = np.zeros_like(c1)
    f1 = np.block([[c1, s1], [-s1, c1]])
    fg = np.block([[c1, zero], [-s1, zero], [zero, c1], [zero, -s1]])
    f3 = np.block([[c1.T, -s1.T], [s1.T, c1.T]])
    k1 = np.arange(FFT_N1)[:, None, None]
    k2 = np.arange(FFT_N2)[None, :, None]
    n2 = np.arange(FFT_N2)[None, None, :]
    ang2 = 2.0 * np.pi * ((n2 * (k1 + FFT_N1 * k2)) % NFFT) / NFFT
    c2, s2 = np.cos(ang2), np.sin(ang2)
    mf = np.concatenate([np.concatenate([c2, s2], axis=2), np.concatenate([-s2, c2], axis=2)], axis=1)
    c2t, s2t = np.swapaxes(c2, 1, 2), np.swapaxes(s2, 1, 2)
    mi = np.concatenate([np.concatenate([c2t, -s2t], axis=2), np.concatenate([s2t, c2t], axis=2)], axis=1)
    as16 = lambda a: jnp.asarray(a.astype(np.float32)).astype(BF16)
    return as16(f1), as16(fg), as16(f3), as16(mf), as16(mi)


LMM_SB = 16


def _lmm_body(f_ref, x_ref, o_ref, scr_ref, *, scale, stage_in):
    f = f_ref[...]
    if stage_in:
        scr_ref[...] = x_ref[...].astype(F32)
    src = scr_ref if stage_in else x_ref
    dst = o_ref if stage_in else scr_ref
    for s in range(LMM_SB):
        r = jnp.dot(f, src[:, s, :].astype(BF16), preferred_element_type=F32)
        dst[:, s, :] = r * scale if scale != 1.0 else r
    if not stage_in:
        o_ref[...] = scr_ref[...].astype(o_ref.dtype)


def _lmm(f, x3, out_dtype, scale=1.0, name="lmm"):
    m, k = f.shape
    stage_in = x3.dtype != F32
    assert stage_in != (out_dtype != F32)
    scr_rows = k if stage_in else m
    return pl.pallas_call(
        functools.partial(_lmm_body, scale=scale, stage_in=stage_in),
        out_shape=jax.ShapeDtypeStruct((m, FFT_N2, HY_WIDTH), out_dtype),
        grid=(FFT_N2 // LMM_SB,),
        in_specs=[pl.BlockSpec((m, k), lambda j: (0, 0)),
                  pl.BlockSpec((k, LMM_SB, HY_WIDTH), lambda j: (0, j, 0))],
        out_specs=pl.BlockSpec((m, LMM_SB, HY_WIDTH), lambda j: (0, j, 0)),
        scratch_shapes=[pltpu.VMEM((scr_rows, LMM_SB, HY_WIDTH), F32)],
        compiler_params=_params(("parallel",), 56),
        name=name,
    )(f, x3)


def _gspec_body(mf_ref, a_ref, g_ref):
    mf = mf_ref[...]
    xf = jnp.dot(mf, a_ref[0:2].reshape(2 * FFT_N2, HY_WIDTH), preferred_element_type=F32)
    xr = jnp.dot(mf, a_ref[2:4].reshape(2 * FFT_N2, HY_WIDTH), preferred_element_type=F32)
    re = lax.broadcasted_iota(jnp.int32, (2 * FFT_N2, 1), 0) < FFT_N2
    g_ref[...] = (xf + jnp.where(re, xr, -xr)).astype(BF16).reshape(2, 1, FFT_N2, HY_WIDTH)


def _gspec(mf, ag):
    return pl.pallas_call(
        _gspec_body,
        out_shape=jax.ShapeDtypeStruct((2, FFT_N1, FFT_N2, HY_WIDTH), BF16),
        grid=(FFT_N1,),
        in_specs=[pl.BlockSpec((None, 2 * FFT_N2, 2 * FFT_N2), lambda i: (i, 0, 0)),
                  pl.BlockSpec((4, 1, FFT_N2, HY_WIDTH), lambda i: (0, i, 0, 0))],
        out_specs=pl.BlockSpec((2, 1, FFT_N2, HY_WIDTH), lambda i: (0, i, 0, 0)),
        compiler_params=_params(("parallel",), 24),
        name="gspec",
    )(mf, ag)


def _spec_body(mf_ref, mi_ref, a_ref, g_ref, o_ref):
    a = a_ref[...].reshape(2 * FFT_N2, HY_WIDTH)
    x = jnp.dot(mf_ref[...], a, preferred_element_type=F32)
    xr, xi = x[:FFT_N2], x[FFT_N2:]
    gr, gi = g_ref[0, 0].astype(F32), g_ref[1, 0].astype(F32)
    y = jnp.concatenate([xr * gr - xi * gi, xr * gi + xi * gr], axis=0).astype(BF16)
    o_ref[...] = jnp.dot(mi_ref[...], y, preferred_element_type=F32).astype(BF16).reshape(
        2, 1, FFT_N2, HY_WIDTH)


def _spec(mf, mi, a, g):
    blk = pl.BlockSpec((2, 1, FFT_N2, HY_WIDTH), lambda i: (0, i, 0, 0))
    mat = pl.BlockSpec((None, 2 * FFT_N2, 2 * FFT_N2), lambda i: (i, 0, 0))
    return pl.pallas_call(
        _spec_body,
        out_shape=jax.ShapeDtypeStruct((2, FFT_N1, FFT_N2, HY_WIDTH), BF16),
        grid=(FFT_N1,),
        in_specs=[mat, mat, blk, blk],
        out_specs=blk,
        compiler_params=_params(("parallel",), 24),
        name="spec",
    )(mf, mi, a, g)


HY_CB = 256


def _short_conv(p_ref, w_ref, b_ref):
    p = p_ref[...]
    w = w_ref[...]
    prev = pltpu.roll(p, 1, 0)
    nxt = pltpu.roll(p, TL - 1, 0)
    return b_ref[...] + prev * w[0:1] + p * w[1:2] + nxt * w[2:3]


def _uconv_body(x1_ref, vh_ref, w1_ref, wv_ref, b1_ref, bv_ref, u_ref):
    u = _short_conv(vh_ref, wv_ref, bv_ref) * _short_conv(x1_ref, w1_ref, b1_ref)
    row = lax.broadcasted_iota(jnp.int32, (TL, 1), 0)
    u_ref[pl.ds(0, TL), :] = jnp.where(row >= PAD, u, 0.0)
    u_ref[pl.ds(TL, TFFT - TL), :] = jnp.zeros((TFFT - TL, HY_CB), F32)


def _hy_specs(first_block):
    nb = HY_WIDTH // HY_CB
    return (pl.BlockSpec((None, TL, HY_CB), lambda b, j: (b, 0, first_block * nb + j)),
            pl.BlockSpec((3, HY_CB), lambda b, j: (0, first_block * nb + j)),
            pl.BlockSpec((1, HY_CB), lambda b, j: (0, first_block * nb + j)))


def _uconv(p3, conv_w, conv_b):
    x1, w1, b1 = _hy_specs(1)
    vh, wv, bv = _hy_specs(2)
    return pl.pallas_call(
        _uconv_body,
        out_shape=jax.ShapeDtypeStruct((BATCH, TFFT, HY_WIDTH), F32),
        grid=(BATCH, HY_WIDTH // HY_CB),
        in_specs=[x1, vh, w1, wv, b1, bv],
        out_specs=pl.BlockSpec((None, TFFT, HY_CB), lambda b, j: (b, 0, j)),
        compiler_params=_params(("parallel", "parallel"), 48),
        name="uconv",
    )(p3, p3, conv_w, conv_w, conv_b, conv_b)


def _ymix_body(y_ref, x0_ref, x1_ref, vh_ref, w0_ref, w1_ref, wv_ref, b0_ref, b1_ref, bv_ref,
               d_ref, gain_ref, o_ref):
    u = _short_conv(vh_ref, wv_ref, bv_ref) * _short_conv(x1_ref, w1_ref, b1_ref)
    yy = (y_ref[...] + d_ref[...] * u) * _short_conv(x0_ref, w0_ref, b0_ref)
    gain = gain_ref[...]
    for s in range(0, HY_CB, HY_GROUP):
        o_ref[:, s:s + HY_GROUP] = (_rms(yy[X0:, s:s + HY_GROUP]) * gain[:, s:s + HY_GROUP]).astype(BF16)


def _ymix(y3, p3, conv_w, conv_b, hyena_d, hyena_norm):
    x0, w0, b0 = _hy_specs(0)
    x1, w1, b1 = _hy_specs(1)
    vh, wv, bv = _hy_specs(2)
    vec = pl.BlockSpec((1, HY_CB), lambda b, j: (0, j))
    return pl.pallas_call(
        _ymix_body,
        out_shape=jax.ShapeDtypeStruct((BATCH, SEQ, HY_WIDTH), BF16),
        grid=(BATCH, HY_WIDTH // HY_CB),
        in_specs=[pl.BlockSpec((None, TL, HY_CB), lambda b, j: (b, 0, j)),
                  x0, x1, vh, w0, w1, wv, b0, b1, bv, vec, vec],
        out_specs=pl.BlockSpec((None, SEQ, HY_CB), lambda b, j: (b, 0, j)),
        compiler_params=_params(("parallel", "parallel"), 56),
        name="ymix",
    )(y3, p3, p3, p3, conv_w, conv_w, conv_w, conv_b, conv_b, conv_b,
      hyena_d.reshape(1, -1), hyena_norm.reshape(1, -1))


N_CHUNK = TL // CHUNK
X_CHUNK0 = X0 // CHUNK
N_SUB = CHUNK // SUB


def _gla_chunk(c, rev, q_ref, k_ref, v_ref, g_ref, st_ref, o_ref, tri, ones, row, col):
    rows = pl.ds(pl.multiple_of(c * CHUNK, CHUNK), CHUNK)
    q = q_ref[rows, :] * (GLA_DK ** -0.5)
    k = k_ref[rows, :]
    v = v_ref[rows, :].astype(BF16)
    g = g_ref[rows, :]
    b = jnp.dot(tri, g, precision=lax.Precision.HIGHEST, preferred_element_type=F32)

    rsub = row % SUB
    terms = []
    for d in range(SUB):
        if d == 0:
            kr, br = k, b
        else:
            sh = CHUNK - d if rev else d
            kr, br = pltpu.roll(k, sh, 0), pltpu.roll(b, sh, 0)
        valid = (rsub + d < SUB) if rev else (rsub >= d)
        t = q * kr * jnp.exp(jnp.minimum(b - br, 0.0))
        terms.append(jnp.where(valid, t, 0.0).astype(BF16))
    sums = jnp.dot(jnp.concatenate(terms, axis=0), ones, preferred_element_type=F32)
    a = jnp.zeros((CHUNK, CHUNK), F32)
    for d in range(SUB):
        tgt = row + d if rev else row - d
        a = jnp.where(col == tgt, sums[d * CHUNK:(d + 1) * CHUNK, :CHUNK], a)

    rblk = row // SUB
    cblk = col // SUB
    for jb in (range(1, N_SUB) if rev else range(N_SUB - 1)):
        e = jb * SUB if rev else jb * SUB + SUB - 1
        ref = b[e:e + 1, :]
        qh = (q * jnp.exp(jnp.minimum(b - ref, 0.0))).astype(BF16)
        kh = (k * jnp.exp(jnp.minimum(ref - b, 0.0))).astype(BF16)
        pm = lax.dot_general(qh, kh, (((1,), (1,)), ((), ())), preferred_element_type=F32)
        side = jnp.where(cblk == jb, rblk, jb)
        a = jnp.where((side < jb) if rev else (side > jb), pm, a)

    st = st_ref[...]
    qt = (q * jnp.exp(b)).astype(BF16)
    o = lax.dot_general(qt, st.astype(BF16), (((1,), (1,)), ((), ())), preferred_element_type=F32)
    o = o + jnp.dot(a.astype(BF16), v, preferred_element_type=F32)
    o_ref[rows, :] = o

    bend = b[0:1, :] if rev else b[CHUNK - 1:CHUNK, :]
    kt = (k * jnp.exp(bend - b)).astype(BF16)
    st_ref[...] = st * jnp.exp(bend) + lax.dot_general(
        v, kt, (((0,), (0,)), ((), ())), preferred_element_type=F32)


def _gla_body(q_ref, k_ref, v_ref, og_ref, gf_ref, gb_ref, gain_ref, o_ref, of_ref, ob_ref, sf_ref, sb_ref):
    sf_ref[...] = jnp.zeros_like(sf_ref)
    sb_ref[...] = jnp.zeros_like(sb_ref)
    row = lax.broadcasted_iota(jnp.int32, (CHUNK, 1), 0)
    col = lax.broadcasted_iota(jnp.int32, (CHUNK, CHUNK), 1)
    rr = lax.broadcasted_iota(jnp.int32, (CHUNK, CHUNK), 0)
    tril = (col <= rr).astype(F32)
    triu = (col >= rr).astype(F32)
    ones = jnp.ones((GLA_DK, GLA_DK), BF16)

    def step(c, carry):
        _gla_chunk(c, False, q_ref, k_ref, v_ref, gf_ref, sf_ref, of_ref, tril, ones, row, col)
        _gla_chunk(N_CHUNK - 1 - c, True, q_ref, k_ref, v_ref, gb_ref, sb_ref, ob_ref, triu, ones, row, col)
        return carry

    lax.fori_loop(0, N_CHUNK, step, 0)

    def fin(c, carry):
        rows = pl.ds(pl.multiple_of(c * CHUNK, CHUNK), CHUNK)
        o = _rms(of_ref[rows, :] + ob_ref[rows, :]) * gain_ref[...]
        out_rows = pl.ds(pl.multiple_of((c - X_CHUNK0) * CHUNK, CHUNK), CHUNK)
        o_ref[out_rows, :] = (o * _silu(og_ref[rows, :])).astype(BF16)
        return carry

    lax.fori_loop(X_CHUNK0, N_CHUNK, fin, 0)


def _gla(p3, lg3, gla_norm):
    qb = (3 * HY_WIDTH) // GLA_DK
    kb = qb + GLA_HEADS
    vb = (3 * HY_WIDTH + 2 * GLA_KEY_WIDTH) // GLA_DV
    gb = vb + GLA_HEADS
    sdk = lambda first: pl.BlockSpec((None, TL, GLA_DK), lambda b, h: (b, 0, first + h))
    sdv = lambda first: pl.BlockSpec((None, TL, GLA_DV), lambda b, h: (b, 0, first + h))
    return pl.pallas_call(
        _gla_body,
        out_shape=jax.ShapeDtypeStruct((BATCH, SEQ, GLA_WIDTH), BF16),
        grid=(BATCH, GLA_HEADS),
        in_specs=[sdk(qb), sdk(kb), sdv(vb), sdv(gb), sdk(0), sdk(GLA_HEADS),
                  pl.BlockSpec((1, GLA_DV), lambda b, h: (0, 0))],
        out_specs=pl.BlockSpec((None, SEQ, GLA_DV), lambda b, h: (b, 0, h)),
        scratch_shapes=[pltpu.VMEM((TL, GLA_DV), F32), pltpu.VMEM((TL, GLA_DV), F32),
                        pltpu.VMEM((GLA_DV, GLA_DK), F32), pltpu.VMEM((GLA_DV, GLA_DK), F32)],
        compiler_params=_params(("parallel", "parallel"), 56),
        name="gla",
    )(p3, p3, p3, p3, lg3, lg3, gla_norm.reshape(1, -1))


OUT_TM = 512


def _outproj_body(h_ref, yh_ref, yg_ref, wh_ref, wg_ref, o_ref):
    o_ref[...] = (h_ref[...] + jnp.dot(yh_ref[...], wh_ref[...], preferred_element_type=F32)
                  + jnp.dot(yg_ref[...], wg_ref[...], preferred_element_type=F32))


def _outproj(hbuf, yh, yg, w_out):
    wo = w_out.astype(BF16)
    half = lambda i: pl.BlockSpec((HY_WIDTH, D_MODEL), lambda r: (i, 0))
    return pl.pallas_call(
        _outproj_body,
        out_shape=jax.ShapeDtypeStruct((X_ROWS, D_MODEL), F32),
        grid=(X_ROWS // OUT_TM,),
        in_specs=[pl.BlockSpec((pl.Element(OUT_TM), pl.Element(D_MODEL)), lambda r: (_x_row(r, OUT_TM), 0)),
                  pl.BlockSpec((OUT_TM, HY_WIDTH), lambda r: (r, 0)),
                  pl.BlockSpec((OUT_TM, GLA_WIDTH), lambda r: (r, 0)),
                  half(0), half(1)],
        out_specs=pl.BlockSpec((OUT_TM, D_MODEL), lambda r: (r, 0)),
        compiler_params=_params(("parallel",), 48),
        name="outproj",
    )(hbuf, yh, yg, wo, wo)


def _long_conv(u3, filt):
    f1, fg, f3, mf, mi = _dft_tables()
    as_n1 = lambda a: a.reshape(2 * FFT_R1, FFT_N2, HY_WIDTH)
    ag = _lmm(fg, as_n1(filt), BF16, name="lmm_g")
    gs = _gspec(mf, ag.reshape(4, FFT_N1, FFT_N2, HY_WIDTH))
    a = _lmm(f1, as_n1(u3), BF16, name="lmm_fwd")
    bm = _spec(mf, mi, a.reshape(2, FFT_N1, FFT_N2, HY_WIDTH), gs)
    y = _lmm(f3, bm.reshape(2 * FFT_N1, FFT_N2, HY_WIDTH), F32, scale=1.0 / NFFT, name="lmm_inv")
    return y.reshape(BATCH, TFFT, HY_WIDTH)


def kernel(x, meta_tokens, ffn1_norm, ffn1_w_gate, ffn1_w_up, ffn1_w_down, mix_norm, w_in, conv_w, conv_b,
           filt_w1, filt_b1, filt_w2, filt_b2, filt_w3, filt_freq, hyena_d, hyena_norm, gk_w2, gk_b2,
           gla_norm, w_out, ffn2_norm, ffn2_w_gate, ffn2_w_up, ffn2_w_down, final_norm):
    assert x.shape == (BATCH, SEQ, D_MODEL) and ffn1_norm.shape[0] == 1

    w1 = _ffn_weights(ffn1_norm[0], ffn1_w_gate[0], ffn1_w_up[0], ffn1_w_down[0], final_norm)
    hbuf = _ffn(x.reshape(X_ROWS, D_MODEL), w1, final=False, to_layout=True)
    hbuf = _ffn_shared(_meta_rows(meta_tokens), w1, hbuf).reshape(ROWS, D_MODEL)

    wi = _inproj_weights(mix_norm[0], w_in[0], gk_w2[0], gk_b2[0])
    pbuf, lgbuf = _inproj(hbuf, wi)
    p3, lg3 = _inproj_shared(hbuf, wi, pbuf, lgbuf)

    filt = _filters(filt_w1[0], filt_b1[0], filt_w2[0], filt_b2[0], filt_w3[0], filt_freq[0])
    cw, cb = conv_w[0], conv_b[0].reshape(1, -1)
    y3 = _long_conv(_uconv(p3, cw, cb), filt)
    yh = _ymix(y3, p3, cw, cb, hyena_d[0], hyena_norm[0])
    yg = _gla(p3, lg3, gla_norm[0])

    h2 = _outproj(hbuf, yh.reshape(X_ROWS, HY_WIDTH), yg.reshape(X_ROWS, GLA_WIDTH), w_out[0])
    w2 = _ffn_weights(ffn2_norm[0], ffn2_w_gate[0], ffn2_w_up[0], ffn2_w_down[0], final_norm)
    return _ffn(h2, w2, final=True, to_layout=False).reshape(BATCH, SEQ, D_MODEL)
```

```python
import functools

import numpy as np
import jax
import jax.numpy as jnp
from jax import lax
from jax.experimental import pallas as pl
from jax.experimental.pallas import tpu as pltpu

F32 = jnp.float32
BF16 = jnp.bfloat16

D_MODEL = 2048
BATCH = 2
SEQ = 4096
N_META = 16
L_TOK = SEQ + N_META
PAD = 112
X0 = PAD + N_META
TL = PAD + L_TOK
ROWS = BATCH * TL
X_ROWS = BATCH * SEQ
HY_WIDTH = 1024
HY_GROUPS = 8
HY_GROUP = HY_WIDTH // HY_GROUPS
FILT_EMB = 33
FILT_BANDS = 16
FILT_HIDDEN = 64
GLA_WIDTH = 1024
GLA_HEADS = 4
GLA_KEY_WIDTH = 512
GLA_DK = 128
GLA_DV = 256
GATE_RANK = 16
GATE_NORMALIZER = 16.0
CHUNK = 64
SUB = 16
D_FF = 5632
P_MAIN = 3 * HY_WIDTH + 2 * GLA_KEY_WIDTH + 2 * GLA_WIDTH
EPS = 1e-6

FFT_N1 = 72
FFT_N2 = 128
NFFT = FFT_N1 * FFT_N2
FFT_R1 = 40
TFFT = FFT_R1 * FFT_N2

MIB = 1024 * 1024
SUBLANE = 8


def _params(sem, vmem_mib):
    return pltpu.CompilerParams(dimension_semantics=sem, vmem_limit_bytes=vmem_mib * MIB)


def _rms(x):
    return x * lax.rsqrt(jnp.mean(x * x, axis=-1, keepdims=True) + EPS)


def _silu(x):
    return x * jax.nn.sigmoid(x)


def _x_row(i, tm):
    per_batch = SEQ // tm
    r8 = (i // per_batch) * (TL // SUBLANE) + X0 // SUBLANE + (i % per_batch) * (tm // SUBLANE)
    return pl.multiple_of(r8 * SUBLANE, SUBLANE)


def _meta_rows(meta_tokens):
    return jnp.concatenate([jnp.zeros((PAD, D_MODEL), F32), meta_tokens.astype(F32)], axis=0)


FFN_TM = 512
FFN_TF = 512


def _ffn_body(x_ref, gain_ref, wg_ref, wu_ref, wd_ref, fgain_ref, *rest, final, j_axis, shared):
    o_ref, xn_ref, acc_ref = rest[-3:]
    j = pl.program_id(j_axis)

    @pl.when(j == 0)
    def _():
        xn_ref[...] = (_rms(x_ref[...]) * gain_ref[...]).astype(BF16)
        acc_ref[...] = jnp.zeros_like(acc_ref)

    xn = xn_ref[...]
    g = jnp.dot(xn, wg_ref[...], preferred_element_type=F32)
    u = jnp.dot(xn, wu_ref[...], preferred_element_type=F32)
    a = (_silu(g) * u).astype(BF16)
    acc_ref[...] += jnp.dot(a, wd_ref[...], preferred_element_type=F32)

    @pl.when(j == pl.num_programs(j_axis) - 1)
    def _():
        h = x_ref[...] + 0.5 * acc_ref[...]
        if final:
            h = _rms(h) * fgain_ref[...]
        if shared:
            for b in range(BATCH):
                o_ref[b] = h
        else:
            o_ref[...] = h


def _ffn_weights(gain, wg, wu, wd, fgain):
    return gain.reshape(1, -1), wg.astype(BF16), wu.astype(BF16), wd.astype(BF16), fgain.reshape(1, -1)


def _ffn(x2, weights, final, to_layout):
    tm = FFN_TM
    if to_layout:
        out_shape = jax.ShapeDtypeStruct((ROWS, D_MODEL), F32)
        out_spec = pl.BlockSpec((pl.Element(tm), pl.Element(D_MODEL)), lambda i, j: (_x_row(i, tm), 0))
    else:
        out_shape = jax.ShapeDtypeStruct((X_ROWS, D_MODEL), F32)
        out_spec = pl.BlockSpec((tm, D_MODEL), lambda i, j: (i, 0))
    return pl.pallas_call(
        functools.partial(_ffn_body, final=final, j_axis=1, shared=False),
        out_shape=out_shape,
        grid=(X_ROWS // tm, D_FF // FFN_TF),
        in_specs=[
            pl.BlockSpec((tm, D_MODEL), lambda i, j: (i, 0)),
            pl.BlockSpec((1, D_MODEL), lambda i, j: (0, 0)),
            pl.BlockSpec((D_MODEL, FFN_TF), lambda i, j: (0, j)),
            pl.BlockSpec((D_MODEL, FFN_TF), lambda i, j: (0, j)),
            pl.BlockSpec((FFN_TF, D_MODEL), lambda i, j: (j, 0)),
            pl.BlockSpec((1, D_MODEL), lambda i, j: (0, 0)),
        ],
        out_specs=out_spec,
        scratch_shapes=[pltpu.VMEM((tm, D_MODEL), BF16), pltpu.VMEM((tm, D_MODEL), F32)],
        compiler_params=_params(("parallel", "arbitrary"), 48),
        name="ffn_final" if final else "ffn",
    )(x2, *weights)


def _ffn_shared(rows, weights, buf):
    return pl.pallas_call(
        functools.partial(_ffn_body, final=False, j_axis=0, shared=True),
        out_shape=jax.ShapeDtypeStruct((BATCH, TL, D_MODEL), F32),
        grid=(D_FF // FFN_TF,),
        in_specs=[
            pl.BlockSpec((X0, D_MODEL), lambda j: (0, 0)),
            pl.BlockSpec((1, D_MODEL), lambda j: (0, 0)),
            pl.BlockSpec((D_MODEL, FFN_TF), lambda j: (0, j)),
            pl.BlockSpec((D_MODEL, FFN_TF), lambda j: (0, j)),
            pl.BlockSpec((FFN_TF, D_MODEL), lambda j: (j, 0)),
            pl.BlockSpec((1, D_MODEL), lambda j: (0, 0)),
            pl.BlockSpec(memory_space=pl.ANY),
        ],
        out_specs=pl.BlockSpec((BATCH, X0, D_MODEL), lambda j: (0, 0, 0)),
        scratch_shapes=[pltpu.VMEM((X0, D_MODEL), BF16), pltpu.VMEM((X0, D_MODEL), F32)],
        input_output_aliases={6: 0},
        compiler_params=_params(("arbitrary",), 32),
        name="ffn_shared",
    )(rows, *weights, buf.reshape(BATCH, TL, D_MODEL))


INP_TM = 512
INP_TN = 1024
LR_PAD = 128


def _inproj_body(x_ref, gain_ref, w_ref, wlr_ref, w2_ref, b2_ref, *rest, j_axis, shared):
    p_ref, lg_ref, xn_ref = rest[-3:]
    j = pl.program_id(j_axis)

    def put(ref, val):
        if shared:
            for b in range(BATCH):
                ref[b] = val
        else:
            ref[...] = val

    @pl.when(j == 0)
    def _():
        xn = (_rms(x_ref[...]) * gain_ref[...]).astype(BF16)
        xn_ref[...] = xn
        lr = jnp.dot(xn, wlr_ref[...], preferred_element_type=F32).astype(BF16)
        z = jnp.dot(lr, w2_ref[...], preferred_element_type=F32) + b2_ref[...]
        put(lg_ref, (jnp.minimum(z, 0.0) - jnp.log1p(jnp.exp(-jnp.abs(z)))) * (1.0 / GATE_NORMALIZER))

    put(p_ref, jnp.dot(xn_ref[...], w_ref[...], preferred_element_type=F32))


def _inproj_weights(gain, w_in, gk_w2, gk_b2):
    w_main = w_in[:, :P_MAIN].astype(BF16)
    w_lr = jnp.pad(w_in[:, P_MAIN:], ((0, 0), (0, LR_PAD - 2 * GATE_RANK))).astype(BF16)
    w2 = jnp.zeros((LR_PAD, 2 * GLA_KEY_WIDTH), F32)
    w2 = w2.at[:GATE_RANK, :GLA_KEY_WIDTH].set(gk_w2[0])
    w2 = w2.at[GATE_RANK:2 * GATE_RANK, GLA_KEY_WIDTH:].set(gk_w2[1]).astype(BF16)
    return gain.reshape(1, -1), w_main, w_lr, w2, gk_b2.reshape(1, 2 * GLA_KEY_WIDTH)


def _inproj(hbuf, weights):
    tm, tn = INP_TM, INP_TN
    row = lambda i, j: (_x_row(i, tm), 0)
    return pl.pallas_call(
        functools.partial(_inproj_body, j_axis=1, shared=False),
        out_shape=(jax.ShapeDtypeStruct((ROWS, P_MAIN), F32),
                   jax.ShapeDtypeStruct((ROWS, 2 * GLA_KEY_WIDTH), F32)),
        grid=(X_ROWS // tm, P_MAIN // tn),
        in_specs=[
            pl.BlockSpec((pl.Element(tm), pl.Element(D_MODEL)), row),
            pl.BlockSpec((1, D_MODEL), lambda i, j: (0, 0)),
            pl.BlockSpec((D_MODEL, tn), lambda i, j: (0, j)),
            pl.BlockSpec((D_MODEL, LR_PAD), lambda i, j: (0, 0)),
            pl.BlockSpec((LR_PAD, 2 * GLA_KEY_WIDTH), lambda i, j: (0, 0)),
            pl.BlockSpec((1, 2 * GLA_KEY_WIDTH), lambda i, j: (0, 0)),
        ],
        out_specs=(pl.BlockSpec((pl.Element(tm), pl.Element(tn)),
                                lambda i, j: (_x_row(i, tm), pl.multiple_of(j * tn, tn))),
                   pl.BlockSpec((pl.Element(tm), pl.Element(2 * GLA_KEY_WIDTH)), row)),
        scratch_shapes=[pltpu.VMEM((tm, D_MODEL), BF16)],
        compiler_params=_params(("parallel", "arbitrary"), 40),
        name="inproj",
    )(hbuf, *weights)


def _inproj_shared(hbuf, weights, pbuf, lgbuf):
    tn = INP_TN
    return pl.pallas_call(
        functools.partial(_inproj_body, j_axis=0, shared=True),
        out_shape=(jax.ShapeDtypeStruct((BATCH, TL, P_MAIN), F32),
                   jax.ShapeDtypeStruct((BATCH, TL, 2 * GLA_KEY_WIDTH), F32)),
        grid=(P_MAIN // tn,),
        in_specs=[
            pl.BlockSpec((None, X0, D_MODEL), lambda j: (0, 0, 0)),
            pl.BlockSpec((1, D_MODEL), lambda j: (0, 0)),
            pl.BlockSpec((D_MODEL, tn), lambda j: (0, j)),
            pl.BlockSpec((D_MODEL, LR_PAD), lambda j: (0, 0)),
            pl.BlockSpec((LR_PAD, 2 * GLA_KEY_WIDTH), lambda j: (0, 0)),
            pl.BlockSpec((1, 2 * GLA_KEY_WIDTH), lambda j: (0, 0)),
            pl.BlockSpec(memory_space=pl.ANY),
            pl.BlockSpec(memory_space=pl.ANY),
        ],
        out_specs=(pl.BlockSpec((BATCH, X0, tn), lambda j: (0, 0, j)),
                   pl.BlockSpec((BATCH, X0, 2 * GLA_KEY_WIDTH), lambda j: (0, 0, 0))),
        scratch_shapes=[pltpu.VMEM((X0, D_MODEL), BF16)],
        input_output_aliases={6: 0, 7: 1},
        compiler_params=_params(("arbitrary",), 32),
        name="inproj_shared",
    )(hbuf.reshape(BATCH, TL, D_MODEL), *weights, pbuf.reshape(BATCH, TL, P_MAIN),
      lgbuf.reshape(BATCH, TL, 2 * GLA_KEY_WIDTH))


FILT_TR = 640
FEAT_PAD = 128


def _filt_tables():
    pos = np.arange(TFFT, dtype=np.float64)
    t = pos / (L_TOK - 1)
    w = (2.0 * np.pi / L_TOK) * pos
    bands = 1e-4 + np.arange(FILT_BANDS, dtype=np.float64) * ((FILT_BANDS - 1 - 1e-4) / (FILT_BANDS - 1))
    ang = w[:, None] * bands[None, :]
    feats = np.zeros((TFFT, FEAT_PAD), np.float64)
    feats[:, 0] = t
    feats[:, 1:1 + FILT_BANDS] = np.cos(ang)
    feats[:, 1 + FILT_BANDS:FILT_EMB] = -np.sin(ang)
    lo, hi = np.log(1e-2) / 1.5, np.log(1e-2) / 0.3
    deltas = np.abs(lo + np.arange(HY_WIDTH, dtype=np.float64) * ((hi - lo) / (HY_WIDTH - 1)))
    return feats.astype(np.float32), deltas.astype(np.float32).reshape(1, HY_WIDTH)


def _filt_body(feat_ref, w1_ref, b1_ref, w2_ref, b2_ref, w3_ref, fr_ref, dl_ref, h_ref):
    hp = lax.Precision.HIGHEST
    feats = feat_ref[...]
    fr = fr_ref[...]
    z = jnp.sin(fr * (jnp.dot(feats, w1_ref[...], precision=hp, preferred_element_type=F32) + b1_ref[...]))
    z = jnp.sin(fr * (jnp.dot(z, w2_ref[...], precision=hp, preferred_element_type=F32) + b2_ref[...]))
    hh = jnp.dot(z, w3_ref[...], precision=hp, preferred_element_type=F32)
    pos = pl.program_id(0) * FILT_TR + lax.broadcasted_iota(jnp.int32, (FILT_TR, 1), 0)
    win = jnp.exp(-feats[:, 0:1] * dl_ref[...])
    win = jnp.where(pos < L_TOK, win, 0.0)
    hf = hh[:, :HY_WIDTH] * win
    hb = hh[:, HY_WIDTH:] * win
    h_ref[0] = jnp.where(pos == 0, hf + hb, hf)
    h_ref[1] = jnp.where(pos == 0, 0.0, hb)


def _filters(w1, b1, w2, b2, w3, freq):
    feats, deltas = _filt_tables()
    w1p = jnp.pad(w1, ((0, FEAT_PAD - FILT_EMB), (0, 0)))
    full = lambda shape: pl.BlockSpec(shape, lambda i: (0, 0))
    return pl.pallas_call(
        _filt_body,
        out_shape=jax.ShapeDtypeStruct((2, TFFT, HY_WIDTH), F32),
        grid=(TFFT // FILT_TR,),
        in_specs=[
            pl.BlockSpec((FILT_TR, FEAT_PAD), lambda i: (i, 0)),
            full((FEAT_PAD, FILT_HIDDEN)), full((1, FILT_HIDDEN)),
            full((FILT_HIDDEN, FILT_HIDDEN)), full((1, FILT_HIDDEN)),
            full((FILT_HIDDEN, 2 * HY_WIDTH)), full((1, FILT_HIDDEN)), full((1, HY_WIDTH)),
        ],
        out_specs=pl.BlockSpec((2, FILT_TR, HY_WIDTH), lambda i: (0, i, 0)),
        compiler_params=_params(("parallel",), 40),
        name="filt",
    )(jnp.asarray(feats), w1p, b1.reshape(1, -1), w2, b2.reshape(1, -1), w3, freq.reshape(1, -1),
      jnp.asarray(deltas))


def _dft_tables():
    n1 = np.arange(FFT_N1)
    ang1 = 2.0 * np.pi * ((n1[:, None] * n1[None, :]) % FFT_N1) / FFT_N1
    c1, s1 = np.cos(ang1)[:, :FFT_R1], np.sin(ang1)[:, :FFT_R1]
    zero = np.zeros_like(c1)
    f1 = np.block([[c1, s1], [-s1, c1]])
    fg = np.block([[c1, zero], [-s1, zero], [zero, c1], [zero, -s1]])
    f3 = np.block([[c1.T, -s1.T], [s1.T, c1.T]])
    k1 = np.arange(FFT_N1)[:, None, None]
    k2 = np.arange(FFT_N2)[None, :, None]
    n2 = np.arange(FFT_N2)[None, None, :]
    ang2 = 2.0 * np.pi * ((n2 * (k1 + FFT_N1 * k2)) % NFFT) / NFFT
    c2, s2 = np.cos(ang2), np.sin(ang2)
    mf = np.concatenate([np.concatenate([c2, s2], axis=2), np.concatenate([-s2, c2], axis=2)], axis=1)
    c2t, s2t = np.swapaxes(c2, 1, 2), np.swapaxes(s2, 1, 2)
    mi = np.concatenate([np.concatenate([c2t, -s2t], axis=2), np.concatenate([s2t, c2t], axis=2)], axis=1)
    as16 = lambda a: jnp.asarray(a.astype(np.float32)).astype(BF16)
    return as16(f1), as16(fg), as16(f3), as16(mf), as16(mi)


LMM_SB = 16


def _lmm_body(f_ref, x_ref, o_ref, scr_ref, *, scale, stage_in):
    f = f_ref[...]
    if stage_in:
        scr_ref[...] = x_ref[...].astype(F32)
    src = scr_ref if stage_in else x_ref
    dst = o_ref if stage_in else scr_ref
    for s in range(LMM_SB):
        r = jnp.dot(f, src[:, s, :].astype(BF16), preferred_element_type=F32)
        dst[:, s, :] = r * scale if scale != 1.0 else r
    if not stage_in:
        o_ref[...] = scr_ref[...].astype(o_ref.dtype)


def _lmm(f, x3, out_dtype, scale=1.0, name="lmm"):
    m, k = f.shape
    stage_in = x3.dtype != F32
    assert stage_in != (out_dtype != F32)
    scr_rows = k if stage_in else m
    return pl.pallas_call(
        functools.partial(_lmm_body, scale=scale, stage_in=stage_in),
        out_shape=jax.ShapeDtypeStruct((m, FFT_N2, HY_WIDTH), out_dtype),
        grid=(FFT_N2 // LMM_SB,),
        in_specs=[pl.BlockSpec((m, k), lambda j: (0, 0)),
                  pl.BlockSpec((k, LMM_SB, HY_WIDTH), lambda j: (0, j, 0))],
        out_specs=pl.BlockSpec((m, LMM_SB, HY_WIDTH), lambda j: (0, j, 0)),
        scratch_shapes=[pltpu.VMEM((scr_rows, LMM_SB, HY_WIDTH), F32)],
        compiler_params=_params(("parallel",), 56),
        name=name,
    )(f, x3)


def _gspec_body(mf_ref, a_ref, g_ref):
    mf = mf_ref[...]
    xf = jnp.dot(mf, a_ref[0:2].reshape(2 * FFT_N2, HY_WIDTH), preferred_element_type=F32)
    xr = jnp.dot(mf, a_ref[2:4].reshape(2 * FFT_N2, HY_WIDTH), preferred_element_type=F32)
    re = lax.broadcasted_iota(jnp.int32, (2 * FFT_N2, 1), 0) < FFT_N2
    g_ref[...] = (xf + jnp.where(re, xr, -xr)).astype(BF16).reshape(2, 1, FFT_N2, HY_WIDTH)


def _gspec(mf, ag):
    return pl.pallas_call(
        _gspec_body,
        out_shape=jax.ShapeDtypeStruct((2, FFT_N1, FFT_N2, HY_WIDTH), BF16),
        grid=(FFT_N1,),
        in_specs=[pl.BlockSpec((None, 2 * FFT_N2, 2 * FFT_N2), lambda i: (i, 0, 0)),
                  pl.BlockSpec((4, 1, FFT_N2, HY_WIDTH), lambda i: (0, i, 0, 0))],
        out_specs=pl.BlockSpec((2, 1, FFT_N2, HY_WIDTH), lambda i: (0, i, 0, 0)),
        compiler_params=_params(("parallel",), 24),
        name="gspec",
    )(mf, ag)


def _spec_body(mf_ref, mi_ref, a_ref, g_ref, o_ref):
    a = a_ref[...].reshape(2 * FFT_N2, HY_WIDTH)
    x = jnp.dot(mf_ref[...], a, preferred_element_type=F32)
    xr, xi = x[:FFT_N2], x[FFT_N2:]
    gr, gi = g_ref[0, 0].astype(F32), g_ref[1, 0].astype(F32)
    y = jnp.concatenate([xr * gr - xi * gi, xr * gi + xi * gr], axis=0).astype(BF16)
    o_ref[...] = jnp.dot(mi_ref[...], y, preferred_element_type=F32).astype(BF16).reshape(
        2, 1, FFT_N2, HY_WIDTH)


def _spec(mf, mi, a, g):
    blk = pl.BlockSpec((2, 1, FFT_N2, HY_WIDTH), lambda i: (0, i, 0, 0))
    mat = pl.BlockSpec((None, 2 * FFT_N2, 2 * FFT_N2), lambda i: (i, 0, 0))
    return pl.pallas_call(
        _spec_body,
        out_shape=jax.ShapeDtypeStruct((2, FFT_N1, FFT_N2, HY_WIDTH), BF16),
        grid=(FFT_N1,),
        in_specs=[mat, mat, blk, blk],
        out_specs=blk,
        compiler_params=_params(("parallel",), 24),
        name="spec",
    )(mf, mi, a, g)


HY_CB = 256


def _short_conv(p_ref, w_ref, b_ref):
    p = p_ref[...]
    w = w_ref[...]
    prev = pltpu.roll(p, 1, 0)
    nxt = pltpu.roll(p, TL - 1, 0)
    return b_ref[...] + prev * w[0:1] + p * w[1:2] + nxt * w[2:3]


def _uconv_body(x1_ref, vh_ref, w1_ref, wv_ref, b1_ref, bv_ref, u_ref):
    u = _short_conv(vh_ref, wv_ref, bv_ref) * _short_conv(x1_ref, w1_ref, b1_ref)
    row = lax.broadcasted_iota(jnp.int32, (TL, 1), 0)
    u_ref[pl.ds(0, TL), :] = jnp.where(row >= PAD, u, 0.0)
    u_ref[pl.ds(TL, TFFT - TL), :] = jnp.zeros((TFFT - TL, HY_CB), F32)


def _hy_specs(first_block):
    nb = HY_WIDTH // HY_CB
    return (pl.BlockSpec((None, TL, HY_CB), lambda b, j: (b, 0, first_block * nb + j)),
            pl.BlockSpec((3, HY_CB), lambda b, j: (0, first_block * nb + j)),
            pl.BlockSpec((1, HY_CB), lambda b, j: (0, first_block * nb + j)))


def _uconv(p3, conv_w, conv_b):
    x1, w1, b1 = _hy_specs(1)
    vh, wv, bv = _hy_specs(2)
    return pl.pallas_call(
        _uconv_body,
        out_shape=jax.ShapeDtypeStruct((BATCH, TFFT, HY_WIDTH), F32),
        grid=(BATCH, HY_WIDTH // HY_CB),
        in_specs=[x1, vh, w1, wv, b1, bv],
        out_specs=pl.BlockSpec((None, TFFT, HY_CB), lambda b, j: (b, 0, j)),
        compiler_params=_params(("parallel", "parallel"), 48),
        name="uconv",
    )(p3, p3, conv_w, conv_w, conv_b, conv_b)


def _ymix_body(y_ref, x0_ref, x1_ref, vh_ref, w0_ref, w1_ref, wv_ref, b0_ref, b1_ref, bv_ref,
               d_ref, gain_ref, o_ref):
    u = _short_conv(vh_ref, wv_ref, bv_ref) * _short_conv(x1_ref, w1_ref, b1_ref)
    yy = (y_ref[...] + d_ref[...] * u) * _short_conv(x0_ref, w0_ref, b0_ref)
    gain = gain_ref[...]
    for s in range(0, HY_CB, HY_GROUP):
        o_ref[:, s:s + HY_GROUP] = (_rms(yy[X0:, s:s + HY_GROUP]) * gain[:, s:s + HY_GROUP]).astype(BF16)


def _ymix(y3, p3, conv_w, conv_b, hyena_d, hyena_norm):
    x0, w0, b0 = _hy_specs(0)
    x1, w1, b1 = _hy_specs(1)
    vh, wv, bv = _hy_specs(2)
    vec = pl.BlockSpec((1, HY_CB), lambda b, j: (0, j))
    return pl.pallas_call(
        _ymix_body,
        out_shape=jax.ShapeDtypeStruct((BATCH, SEQ, HY_WIDTH), BF16),
        grid=(BATCH, HY_WIDTH // HY_CB),
        in_specs=[pl.BlockSpec((None, TL, HY_CB), lambda b, j: (b, 0, j)),
                  x0, x1, vh, w0, w1, wv, b0, b1, bv, vec, vec],
        out_specs=pl.BlockSpec((None, SEQ, HY_CB), lambda b, j: (b, 0, j)),
        compiler_params=_params(("parallel", "parallel"), 56),
        name="ymix",
    )(y3, p3, p3, p3, conv_w, conv_w, conv_w, conv_b, conv_b, conv_b,
      hyena_d.reshape(1, -1), hyena_norm.reshape(1, -1))


N_CHUNK = TL // CHUNK
X_CHUNK0 = X0 // CHUNK
N_SUB = CHUNK // SUB
SAFE_BLOCK_DECAY = -60.0


def _scores_exact(q, k, b, rev, ones, row, col):
    rsub = row % SUB
    terms = []
    for d in range(SUB):
        if d == 0:
            kr, br = k, b
        else:
            sh = CHUNK - d if rev else d
            kr, br = pltpu.roll(k, sh, 0), pltpu.roll(b, sh, 0)
        valid = (rsub + d < SUB) if rev else (rsub >= d)
        t = q * kr * jnp.exp(jnp.minimum(b - br, 0.0))
        terms.append(jnp.where(valid, t, 0.0).astype(BF16))
    sums = jnp.dot(jnp.concatenate(terms, axis=0), ones, preferred_element_type=F32)
    a = jnp.zeros((CHUNK, CHUNK), F32)
    for d in range(SUB):
        tgt = row + d if rev else row - d
        a = jnp.where(col == tgt, sums[d * CHUNK:(d + 1) * CHUNK, :CHUNK], a)

    rblk = row // SUB
    cblk = col // SUB
    for jb in (range(1, N_SUB) if rev else range(N_SUB - 1)):
        e = jb * SUB if rev else jb * SUB + SUB - 1
        ref = b[e:e + 1, :]
        qh = (q * jnp.exp(jnp.minimum(b - ref, 0.0))).astype(BF16)
        kh = (k * jnp.exp(jnp.minimum(ref - b, 0.0))).astype(BF16)
        pm = lax.dot_general(qh, kh, (((1,), (1,)), ((), ())), preferred_element_type=F32)
        side = jnp.where(cblk == jb, rblk, jb)
        a = jnp.where((side < jb) if rev else (side > jb), pm, a)

    bend = b[0:1, :] if rev else b[CHUNK - 1:CHUNK, :]
    return a, q * jnp.exp(b), k * jnp.exp(bend - b), bend


def _scores_fast(q, k, b, rev, row, col):
    order = list(range(N_SUB))[::-1] if rev else list(range(N_SUB))
    pos = {blk: p for p, blk in enumerate(order)}
    edge = lambda blk: blk * SUB if rev else blk * SUB + SUB - 1
    e = [b[edge(blk):edge(blk) + 1, :] for blk in order]
    s = [jnp.zeros((1, GLA_DK), F32)] + e[:-1]

    def by_row(vals):
        return jnp.concatenate([jnp.broadcast_to(vals[pos[blk]], (SUB, GLA_DK)) for blk in range(N_SUB)],
                               axis=0)

    srow, erow = by_row(s), by_row(e)
    qh = q * jnp.exp(b - srow)
    kh = k * jnp.exp(erow - b)
    kd = k * jnp.exp(srow - b)

    lhs = []
    for pj in range(N_SUB - 1):
        for blk in range(N_SUB):
            piece = qh[blk * SUB:(blk + 1) * SUB, :]
            p = pos[blk]
            if p <= pj:
                piece = jnp.zeros_like(piece)
            elif p > pj + 1:
                piece = piece * jnp.exp(s[p] - e[pj])
            lhs.append(piece.astype(BF16))
    contract = (((1,), (1,)), ((), ()))
    cross = lax.dot_general(jnp.concatenate(lhs, axis=0), kh.astype(BF16), contract,
                            preferred_element_type=F32)
    diag = lax.dot_general(qh.astype(BF16), kd.astype(BF16), contract, preferred_element_type=F32)

    rblk = row // SUB
    cblk = col // SUB
    a = jnp.zeros((CHUNK, CHUNK), F32)
    for pj in range(N_SUB - 1):
        a = jnp.where(cblk == order[pj], cross[pj * CHUNK:(pj + 1) * CHUNK], a)
    causal = (col >= row) if rev else (col <= row)
    a = jnp.where(cblk == rblk, jnp.where(causal, diag, 0.0), a)

    bend = e[-1]
    return a, qh * jnp.exp(srow), kh * jnp.exp(bend - erow), bend


def _gla_chunk(c, rev, fast, q_ref, k_ref, v_ref, b_ref, st_ref, o_ref, ones, row, col):
    rows = pl.ds(pl.multiple_of(c * CHUNK, CHUNK), CHUNK)
    q = q_ref[rows, :] * (GLA_DK ** -0.5)
    k = k_ref[rows, :]
    v = v_ref[rows, :].astype(BF16)
    b = b_ref[rows, :]
    if fast:
        a, qt, kt, bend = _scores_fast(q, k, b, rev, row, col)
    else:
        a, qt, kt, bend = _scores_exact(q, k, b, rev, ones, row, col)

    st = st_ref[...]
    o = lax.dot_general(qt.astype(BF16), st.astype(BF16), (((1,), (1,)), ((), ())),
                        preferred_element_type=F32)
    o_ref[rows, :] = o + jnp.dot(a.astype(BF16), v, preferred_element_type=F32)
    st_ref[...] = st * jnp.exp(bend) + lax.dot_general(
        v, kt.astype(BF16), (((0,), (0,)), ((), ())), preferred_element_type=F32)


def _gla_body(q_ref, k_ref, v_ref, og_ref, gf_ref, gb_ref, gain_ref, o_ref,
              of_ref, ob_ref, bf_ref, bb_ref, sf_ref, sb_ref):
    sf_ref[...] = jnp.zeros_like(sf_ref)
    sb_ref[...] = jnp.zeros_like(sb_ref)
    row = lax.broadcasted_iota(jnp.int32, (CHUNK, 1), 0)
    col = lax.broadcasted_iota(jnp.int32, (CHUNK, CHUNK), 1)
    rr = lax.broadcasted_iota(jnp.int32, (CHUNK, CHUNK), 0)
    tril = (col <= rr).astype(F32)
    triu = (col >= rr).astype(F32)
    ones = jnp.ones((GLA_DK, GLA_DK), BF16)

    def cumulate(c, low):
        rows = pl.ds(pl.multiple_of(c * CHUNK, CHUNK), CHUNK)
        hp = lax.Precision.HIGHEST
        gf, gb = gf_ref[rows, :], gb_ref[rows, :]
        bf_ref[rows, :] = jnp.dot(tril, gf, precision=hp, preferred_element_type=F32)
        bb_ref[rows, :] = jnp.dot(triu, gb, precision=hp, preferred_element_type=F32)
        blocks = jnp.minimum(gf, gb).reshape(N_SUB, SUB, GLA_DK)
        return jnp.minimum(low, jnp.min(jnp.sum(blocks, axis=1)))

    low = lax.fori_loop(0, N_CHUNK, cumulate, jnp.float32(0.0))

    def scan(fast):
        def step(c, carry):
            _gla_chunk(c, False, fast, q_ref, k_ref, v_ref, bf_ref, sf_ref, of_ref, ones, row, col)
            _gla_chunk(N_CHUNK - 1 - c, True, fast, q_ref, k_ref, v_ref, bb_ref, sb_ref, ob_ref,
                       ones, row, col)
            return carry

        lax.fori_loop(0, N_CHUNK, step, 0)

    safe = low > SAFE_BLOCK_DECAY
    pl.when(safe)(lambda: scan(True))
    pl.when(jnp.logical_not(safe))(lambda: scan(False))

    def fin(c, carry):
        rows = pl.ds(pl.multiple_of(c * CHUNK, CHUNK), CHUNK)
        o = _rms(of_ref[rows, :] + ob_ref[rows, :]) * gain_ref[...]
        out_rows = pl.ds(pl.multiple_of((c - X_CHUNK0) * CHUNK, CHUNK), CHUNK)
        o_ref[out_rows, :] = (o * _silu(og_ref[rows, :])).astype(BF16)
        return carry

    lax.fori_loop(X_CHUNK0, N_CHUNK, fin, 0)


def _gla(p3, lg3, gla_norm):
    qb = (3 * HY_WIDTH) // GLA_DK
    kb = qb + GLA_HEADS
    vb = (3 * HY_WIDTH + 2 * GLA_KEY_WIDTH) // GLA_DV
    gb = vb + GLA_HEADS
    sdk = lambda first: pl.BlockSpec((None, TL, GLA_DK), lambda b, h: (b, 0, first + h))
    sdv = lambda first: pl.BlockSpec((None, TL, GLA_DV), lambda b, h: (b, 0, first + h))
    return pl.pallas_call(
        _gla_body,
        out_shape=jax.ShapeDtypeStruct((BATCH, SEQ, GLA_WIDTH), BF16),
        grid=(BATCH, GLA_HEADS),
        in_specs=[sdk(qb), sdk(kb), sdv(vb), sdv(gb), sdk(0), sdk(GLA_HEADS),
                  pl.BlockSpec((1, GLA_DV), lambda b, h: (0, 0))],
        out_specs=pl.BlockSpec((None, SEQ, GLA_DV), lambda b, h: (b, 0, h)),
        scratch_shapes=[pltpu.VMEM((TL, GLA_DV), F32), pltpu.VMEM((TL, GLA_DV), F32),
                        pltpu.VMEM((TL, GLA_DK), F32), pltpu.VMEM((TL, GLA_DK), F32),
                        pltpu.VMEM((GLA_DV, GLA_DK), F32), pltpu.VMEM((GLA_DV, GLA_DK), F32)],
        compiler_params=_params(("parallel", "parallel"), 58),
        name="gla",
    )(p3, p3, p3, p3, lg3, lg3, gla_norm.reshape(1, -1))


OUT_TM = 512


def _outproj_body(h_ref, yh_ref, yg_ref, wh_ref, wg_ref, o_ref):
    o_ref[...] = (h_ref[...] + jnp.dot(yh_ref[...], wh_ref[...], preferred_element_type=F32)
                  + jnp.dot(yg_ref[...], wg_ref[...], preferred_element_type=F32))


def _outproj(hbuf, yh, yg, w_out):
    wo = w_out.astype(BF16)
    half = lambda i: pl.BlockSpec((HY_WIDTH, D_MODEL), lambda r: (i, 0))
    return pl.pallas_call(
        _outproj_body,
        out_shape=jax.ShapeDtypeStruct((X_ROWS, D_MODEL), F32),
        grid=(X_ROWS // OUT_TM,),
        in_specs=[pl.BlockSpec((pl.Element(OUT_TM), pl.Element(D_MODEL)), lambda r: (_x_row(r, OUT_TM), 0)),
                  pl.BlockSpec((OUT_TM, HY_WIDTH), lambda r: (r, 0)),
                  pl.BlockSpec((OUT_TM, GLA_WIDTH), lambda r: (r, 0)),
                  half(0), half(1)],
        out_specs=pl.BlockSpec((OUT_TM, D_MODEL), lambda r: (r, 0)),
        compiler_params=_params(("parallel",), 48),
        name="outproj",
    )(hbuf, yh, yg, wo, wo)


def _long_conv(u3, filt):
    f1, fg, f3, mf, mi = _dft_tables()
    as_n1 = lambda a: a.reshape(2 * FFT_R1, FFT_N2, HY_WIDTH)
    ag = _lmm(fg, as_n1(filt), BF16, name="lmm_g")
    gs = _gspec(mf, ag.reshape(4, FFT_N1, FFT_N2, HY_WIDTH))
    a = _lmm(f1, as_n1(u3), BF16, name="lmm_fwd")
    bm = _spec(mf, mi, a.reshape(2, FFT_N1, FFT_N2, HY_WIDTH), gs)
    y = _lmm(f3, bm.reshape(2 * FFT_N1, FFT_N2, HY_WIDTH), F32, scale=1.0 / NFFT, name="lmm_inv")
    return y.reshape(BATCH, TFFT, HY_WIDTH)


def kernel(x, meta_tokens, ffn1_norm, ffn1_w_gate, ffn1_w_up, ffn1_w_down, mix_norm, w_in, conv_w, conv_b,
           filt_w1, filt_b1, filt_w2, filt_b2, filt_w3, filt_freq, hyena_d, hyena_norm, gk_w2, gk_b2,
           gla_norm, w_out, ffn2_norm, ffn2_w_gate, ffn2_w_up, ffn2_w_down, final_norm):
    assert x.shape == (BATCH, SEQ, D_MODEL) and ffn1_norm.shape[0] == 1

    w1 = _ffn_weights(ffn1_norm[0], ffn1_w_gate[0], ffn1_w_up[0], ffn1_w_down[0], final_norm)
    hbuf = _ffn(x.reshape(X_ROWS, D_MODEL), w1, final=False, to_layout=True)
    hbuf = _ffn_shared(_meta_rows(meta_tokens), w1, hbuf).reshape(ROWS, D_MODEL)

    wi = _inproj_weights(mix_norm[0], w_in[0], gk_w2[0], gk_b2[0])
    pbuf, lgbuf = _inproj(hbuf, wi)
    p3, lg3 = _inproj_shared(hbuf, wi, pbuf, lgbuf)

    filt = _filters(filt_w1[0], filt_b1[0], filt_w2[0], filt_b2[0], filt_w3[0], filt_freq[0])
    cw, cb = conv_w[0], conv_b[0].reshape(1, -1)
    y3 = _long_conv(_uconv(p3, cw, cb), filt)
    yh = _ymix(y3, p3, cw, cb, hyena_d[0], hyena_norm[0])
    yg = _gla(p3, lg3, gla_norm[0])

    h2 = _outproj(hbuf, yh.reshape(X_ROWS, HY_WIDTH), yg.reshape(X_ROWS, GLA_WIDTH), w_out[0])
    w2 = _ffn_weights(ffn2_norm[0], ffn2_w_gate[0], ffn2_w_up[0], ffn2_w_down[0], final_norm)
    return _ffn(h2, w2, final=True, to_layout=False).reshape(BATCH, SEQ, D_MODEL)
```

```python
import functools

import numpy as np
import jax
import jax.numpy as jnp
from jax import lax
from jax.experimental import pallas as pl
from jax.experimental.pallas import tpu as pltpu

F32 = jnp.float32
BF16 = jnp.bfloat16

D_MODEL = 2048
BATCH = 2
SEQ = 4096
N_META = 16
L_TOK = SEQ + N_META
PAD = 112
X0 = PAD + N_META
TL = PAD + L_TOK
ROWS = BATCH * TL
X_ROWS = BATCH * SEQ
HY_WIDTH = 1024
HY_GROUPS = 8
HY_GROUP = HY_WIDTH // HY_GROUPS
FILT_EMB = 33
FILT_BANDS = 16
FILT_HIDDEN = 64
GLA_WIDTH = 1024
GLA_HEADS = 4
GLA_KEY_WIDTH = 512
GLA_DK = 128
GLA_DV = 256
GATE_RANK = 16
GATE_NORMALIZER = 16.0
CHUNK = 64
SUB = 16
D_FF = 5632
P_MAIN = 3 * HY_WIDTH + 2 * GLA_KEY_WIDTH + 2 * GLA_WIDTH
EPS = 1e-6

FFT_N1 = 72
FFT_N2 = 128
NFFT = FFT_N1 * FFT_N2
FFT_R1 = 40
TFFT = FFT_R1 * FFT_N2

MIB = 1024 * 1024
SUBLANE = 8


def _params(sem, vmem_mib):
    return pltpu.CompilerParams(dimension_semantics=sem, vmem_limit_bytes=vmem_mib * MIB)


def _rms(x):
    return x * lax.rsqrt(jnp.mean(x * x, axis=-1, keepdims=True) + EPS)


def _silu(x):
    return x * jax.nn.sigmoid(x)


LAY_TM = 528


def _x_row(i, tm, unit=SUBLANE):
    per_batch = SEQ // tm
    r = (i // per_batch) * (TL // unit) + X0 // unit + (i % per_batch) * (tm // unit)
    return pl.multiple_of(r * unit, unit)


def _lay_src_row(i):
    per_batch = TL // LAY_TM
    r = (i // per_batch) * (SEQ // SUBLANE) + jnp.maximum(
        (i % per_batch) * (LAY_TM // SUBLANE) - X0 // SUBLANE, 0)
    return pl.multiple_of(r * SUBLANE, SUBLANE)


def _meta_rows(meta_tokens):
    return jnp.concatenate([jnp.zeros((PAD, D_MODEL), F32), meta_tokens.astype(F32)], axis=0)


FFN_TM = 512
FFN_TF = 512


def _ffn_body(x_ref, gain_ref, wg_ref, wu_ref, wd_ref, fgain_ref, *rest, final, layout):
    if layout:
        shared_ref, o_ref, xn_ref, acc_ref, xin_ref = rest
    else:
        o_ref, xn_ref, acc_ref = rest
        xin_ref = x_ref
    j = pl.program_id(1)

    @pl.when(j == 0)
    def _():
        if layout:
            first = pl.program_id(0) % (TL // LAY_TM) == 0

            @pl.when(first)
            def _():
                xin_ref[0:X0, :] = shared_ref[...]
                xin_ref[X0:LAY_TM, :] = x_ref[0:LAY_TM - X0, :]

            @pl.when(jnp.logical_not(first))
            def _():
                xin_ref[...] = x_ref[...]

        xn_ref[...] = (_rms(xin_ref[...]) * gain_ref[...]).astype(BF16)
        acc_ref[...] = jnp.zeros_like(acc_ref)

    xn = xn_ref[...]
    g = jnp.dot(xn, wg_ref[...], preferred_element_type=F32)
    u = jnp.dot(xn, wu_ref[...], preferred_element_type=F32)
    a = (_silu(g) * u).astype(BF16)
    acc_ref[...] += jnp.dot(a, wd_ref[...], preferred_element_type=F32)

    @pl.when(j == pl.num_programs(1) - 1)
    def _():
        h = xin_ref[...] + 0.5 * acc_ref[...]
        if final:
            h = _rms(h) * fgain_ref[...]
        o_ref[...] = h


def _ffn_weights(gain, wg, wu, wd, fgain):
    return gain.reshape(1, -1), wg.astype(BF16), wu.astype(BF16), wd.astype(BF16), fgain.reshape(1, -1)


def _ffn(x2, weights, final, shared_rows=None):
    layout = shared_rows is not None
    tm = LAY_TM if layout else FFN_TM
    rows = ROWS if layout else X_ROWS
    wspecs = [
        pl.BlockSpec((1, D_MODEL), lambda i, j: (0, 0)),
        pl.BlockSpec((D_MODEL, FFN_TF), lambda i, j: (0, j)),
        pl.BlockSpec((D_MODEL, FFN_TF), lambda i, j: (0, j)),
        pl.BlockSpec((FFN_TF, D_MODEL), lambda i, j: (j, 0)),
        pl.BlockSpec((1, D_MODEL), lambda i, j: (0, 0)),
    ]
    scratch = [pltpu.VMEM((tm, D_MODEL), BF16), pltpu.VMEM((tm, D_MODEL), F32)]
    if layout:
        x_spec = pl.BlockSpec((pl.Element(tm), pl.Element(D_MODEL)), lambda i, j: (_lay_src_row(i), 0))
        extra_specs = [pl.BlockSpec((X0, D_MODEL), lambda i, j: (0, 0))]
        extra = (shared_rows,)
        scratch.append(pltpu.VMEM((tm, D_MODEL), F32))
    else:
        x_spec = pl.BlockSpec((tm, D_MODEL), lambda i, j: (i, 0))
        extra_specs, extra = [], ()
    return pl.pallas_call(
        functools.partial(_ffn_body, final=final, layout=layout),
        out_shape=jax.ShapeDtypeStruct((rows, D_MODEL), F32),
        grid=(rows // tm, D_FF // FFN_TF),
        in_specs=[x_spec] + wspecs + extra_specs,
        out_specs=pl.BlockSpec((tm, D_MODEL), lambda i, j: (i, 0)),
        scratch_shapes=scratch,
        compiler_params=_params(("parallel", "arbitrary"), 52),
        name="ffn_final" if final else "ffn",
    )(x2, *weights, *extra)


INP_TN = 1024
LR_PAD = 128


def _inproj_body(x_ref, gain_ref, w_ref, wlr_ref, w2_ref, b2_ref, p_ref, lg_ref, xn_ref):
    @pl.when(pl.program_id(1) == 0)
    def _():
        xn = (_rms(x_ref[...]) * gain_ref[...]).astype(BF16)
        xn_ref[...] = xn
        lr = jnp.dot(xn, wlr_ref[...], preferred_element_type=F32).astype(BF16)
        z = jnp.dot(lr, w2_ref[...], preferred_element_type=F32) + b2_ref[...]
        lg_ref[...] = (jnp.minimum(z, 0.0) - jnp.log1p(jnp.exp(-jnp.abs(z)))) * (1.0 / GATE_NORMALIZER)

    p_ref[...] = jnp.dot(xn_ref[...], w_ref[...], preferred_element_type=F32)


def _inproj_weights(gain, w_in, gk_w2, gk_b2):
    w_main = w_in[:, :P_MAIN].astype(BF16)
    w_lr = jnp.pad(w_in[:, P_MAIN:], ((0, 0), (0, LR_PAD - 2 * GATE_RANK))).astype(BF16)
    w2 = jnp.zeros((LR_PAD, 2 * GLA_KEY_WIDTH), F32)
    w2 = w2.at[:GATE_RANK, :GLA_KEY_WIDTH].set(gk_w2[0])
    w2 = w2.at[GATE_RANK:2 * GATE_RANK, GLA_KEY_WIDTH:].set(gk_w2[1]).astype(BF16)
    return gain.reshape(1, -1), w_main, w_lr, w2, gk_b2.reshape(1, 2 * GLA_KEY_WIDTH)


def _inproj(hbuf, weights):
    tm, tn = LAY_TM, INP_TN
    return pl.pallas_call(
        _inproj_body,
        out_shape=(jax.ShapeDtypeStruct((ROWS, P_MAIN), F32),
                   jax.ShapeDtypeStruct((ROWS, 2 * GLA_KEY_WIDTH), F32)),
        grid=(ROWS // tm, P_MAIN // tn),
        in_specs=[
            pl.BlockSpec((tm, D_MODEL), lambda i, j: (i, 0)),
            pl.BlockSpec((1, D_MODEL), lambda i, j: (0, 0)),
            pl.BlockSpec((D_MODEL, tn), lambda i, j: (0, j)),
            pl.BlockSpec((D_MODEL, LR_PAD), lambda i, j: (0, 0)),
            pl.BlockSpec((LR_PAD, 2 * GLA_KEY_WIDTH), lambda i, j: (0, 0)),
            pl.BlockSpec((1, 2 * GLA_KEY_WIDTH), lambda i, j: (0, 0)),
        ],
        out_specs=(pl.BlockSpec((tm, tn), lambda i, j: (i, j)),
                   pl.BlockSpec((tm, 2 * GLA_KEY_WIDTH), lambda i, j: (i, 0))),
        scratch_shapes=[pltpu.VMEM((tm, D_MODEL), BF16)],
        compiler_params=_params(("parallel", "arbitrary"), 40),
        name="inproj",
    )(hbuf, *weights)


FILT_TR = 640
FEAT_PAD = 128


def _filt_tables():
    pos = np.arange(TFFT, dtype=np.float64)
    t = pos / (L_TOK - 1)
    w = (2.0 * np.pi / L_TOK) * pos
    bands = 1e-4 + np.arange(FILT_BANDS, dtype=np.float64) * ((FILT_BANDS - 1 - 1e-4) / (FILT_BANDS - 1))
    ang = w[:, None] * bands[None, :]
    feats = np.zeros((TFFT, FEAT_PAD), np.float64)
    feats[:, 0] = t
    feats[:, 1:1 + FILT_BANDS] = np.cos(ang)
    feats[:, 1 + FILT_BANDS:FILT_EMB] = -np.sin(ang)
    lo, hi = np.log(1e-2) / 1.5, np.log(1e-2) / 0.3
    deltas = np.abs(lo + np.arange(HY_WIDTH, dtype=np.float64) * ((hi - lo) / (HY_WIDTH - 1)))
    return feats.astype(np.float32), deltas.astype(np.float32).reshape(1, HY_WIDTH)


def _filt_body(feat_ref, w1_ref, b1_ref, w2_ref, b2_ref, w3_ref, fr_ref, dl_ref, h_ref):
    hp = lax.Precision.HIGHEST
    feats = feat_ref[...]
    fr = fr_ref[...]
    z = jnp.sin(fr * (jnp.dot(feats, w1_ref[...], precision=hp, preferred_element_type=F32) + b1_ref[...]))
    z = jnp.sin(fr * (jnp.dot(z, w2_ref[...], precision=hp, preferred_element_type=F32) + b2_ref[...]))
    hh = jnp.dot(z, w3_ref[...], precision=hp, preferred_element_type=F32)
    pos = pl.program_id(0) * FILT_TR + lax.broadcasted_iota(jnp.int32, (FILT_TR, 1), 0)
    win = jnp.exp(-feats[:, 0:1] * dl_ref[...])
    win = jnp.where(pos < L_TOK, win, 0.0)
    hf = hh[:, :HY_WIDTH] * win
    hb = hh[:, HY_WIDTH:] * win
    h_ref[0] = jnp.where(pos == 0, hf + hb, hf)
    h_ref[1] = jnp.where(pos == 0, 0.0, hb)


def _filters(w1, b1, w2, b2, w3, freq):
    feats, deltas = _filt_tables()
    w1p = jnp.pad(w1, ((0, FEAT_PAD - FILT_EMB), (0, 0)))
    full = lambda shape: pl.BlockSpec(shape, lambda i: (0, 0))
    return pl.pallas_call(
        _filt_body,
        out_shape=jax.ShapeDtypeStruct((2, TFFT, HY_WIDTH), F32),
        grid=(TFFT // FILT_TR,),
        in_specs=[
            pl.BlockSpec((FILT_TR, FEAT_PAD), lambda i: (i, 0)),
            full((FEAT_PAD, FILT_HIDDEN)), full((1, FILT_HIDDEN)),
            full((FILT_HIDDEN, FILT_HIDDEN)), full((1, FILT_HIDDEN)),
            full((FILT_HIDDEN, 2 * HY_WIDTH)), full((1, FILT_HIDDEN)), full((1, HY_WIDTH)),
        ],
        out_specs=pl.BlockSpec((2, FILT_TR, HY_WIDTH), lambda i: (0, i, 0)),
        compiler_params=_params(("parallel",), 40),
        name="filt",
    )(jnp.asarray(feats), w1p, b1.reshape(1, -1), w2, b2.reshape(1, -1), w3, freq.reshape(1, -1),
      jnp.asarray(deltas))


def _dft_tables():
    n1 = np.arange(FFT_N1)
    ang1 = 2.0 * np.pi * ((n1[:, None] * n1[None, :]) % FFT_N1) / FFT_N1
    c1, s1 = np.cos(ang1)[:, :FFT_R1], np.sin(ang1)[:, :FFT_R1]
    zero = np.zeros_like(c1)
    f1 = np.block([[c1, s1], [-s1, c1]])
    fg = np.block([[c1, zero], [-s1, zero], [zero, c1], [zero, -s1]])
    f3 = np.block([[c1.T, -s1.T], [s1.T, c1.T]])
    k1 = np.arange(FFT_N1)[:, None, None]
    k2 = np.arange(FFT_N2)[None, :, None]
    n2 = np.arange(FFT_N2)[None, None, :]
    ang2 = 2.0 * np.pi * ((n2 * (k1 + FFT_N1 * k2)) % NFFT) / NFFT
    c2, s2 = np.cos(ang2), np.sin(ang2)
    mf = np.concatenate([np.concatenate([c2, s2], axis=2), np.concatenate([-s2, c2], axis=2)], axis=1)
    c2t, s2t = np.swapaxes(c2, 1, 2), np.swapaxes(s2, 1, 2)
    mi = np.concatenate([np.concatenate([c2t, -s2t], axis=2), np.concatenate([s2t, c2t], axis=2)], axis=1)
    as16 = lambda a: jnp.asarray(a.astype(np.float32)).astype(BF16)
    return as16(f1), as16(fg), as16(f3), as16(mf), as16(mi)


LMM_SB = 16


def _lmm_body(f_ref, x_ref, o_ref, scr_ref, *, scale, stage_in):
    f = f_ref[...]
    if stage_in:
        scr_ref[...] = x_ref[...].astype(F32)
    src = scr_ref if stage_in else x_ref
    dst = o_ref if stage_in else scr_ref
    for s in range(LMM_SB):
        r = jnp.dot(f, src[:, s, :].astype(BF16), preferred_element_type=F32)
        dst[:, s, :] = r * scale if scale != 1.0 else r
    if not stage_in:
        o_ref[...] = scr_ref[...].astype(o_ref.dtype)


def _lmm(f, x3, out_dtype, scale=1.0, name="lmm"):
    m, k = f.shape
    stage_in = x3.dtype != F32
    assert stage_in != (out_dtype != F32)
    scr_rows = k if stage_in else m
    return pl.pallas_call(
        functools.partial(_lmm_body, scale=scale, stage_in=stage_in),
        out_shape=jax.ShapeDtypeStruct((m, FFT_N2, HY_WIDTH), out_dtype),
        grid=(FFT_N2 // LMM_SB,),
        in_specs=[pl.BlockSpec((m, k), lambda j: (0, 0)),
                  pl.BlockSpec((k, LMM_SB, HY_WIDTH), lambda j: (0, j, 0))],
        out_specs=pl.BlockSpec((m, LMM_SB, HY_WIDTH), lambda j: (0, j, 0)),
        scratch_shapes=[pltpu.VMEM((scr_rows, LMM_SB, HY_WIDTH), F32)],
        compiler_params=_params(("parallel",), 56),
        name=name,
    )(f, x3)


def _gspec_body(mf_ref, a_ref, g_ref):
    mf = mf_ref[...]
    xf = jnp.dot(mf, a_ref[0:2].reshape(2 * FFT_N2, HY_WIDTH), preferred_element_type=F32)
    xr = jnp.dot(mf, a_ref[2:4].reshape(2 * FFT_N2, HY_WIDTH), preferred_element_type=F32)
    re = lax.broadcasted_iota(jnp.int32, (2 * FFT_N2, 1), 0) < FFT_N2
    g_ref[...] = (xf + jnp.where(re, xr, -xr)).astype(BF16).reshape(2, 1, FFT_N2, HY_WIDTH)


def _gspec(mf, ag):
    return pl.pallas_call(
        _gspec_body,
        out_shape=jax.ShapeDtypeStruct((2, FFT_N1, FFT_N2, HY_WIDTH), BF16),
        grid=(FFT_N1,),
        in_specs=[pl.BlockSpec((None, 2 * FFT_N2, 2 * FFT_N2), lambda i: (i, 0, 0)),
                  pl.BlockSpec((4, 1, FFT_N2, HY_WIDTH), lambda i: (0, i, 0, 0))],
        out_specs=pl.BlockSpec((2, 1, FFT_N2, HY_WIDTH), lambda i: (0, i, 0, 0)),
        compiler_params=_params(("parallel",), 24),
        name="gspec",
    )(mf, ag)


def _spec_body(mf_ref, mi_ref, a_ref, g_ref, o_ref):
    a = a_ref[...].reshape(2 * FFT_N2, HY_WIDTH)
    x = jnp.dot(mf_ref[...], a, preferred_element_type=F32)
    xr, xi = x[:FFT_N2], x[FFT_N2:]
    gr, gi = g_ref[0, 0].astype(F32), g_ref[1, 0].astype(F32)
    y = jnp.concatenate([xr * gr - xi * gi, xr * gi + xi * gr], axis=0).astype(BF16)
    o_ref[...] = jnp.dot(mi_ref[...], y, preferred_element_type=F32).astype(BF16).reshape(
        2, 1, FFT_N2, HY_WIDTH)


def _spec(mf, mi, a, g):
    blk = pl.BlockSpec((2, 1, FFT_N2, HY_WIDTH), lambda i: (0, i, 0, 0))
    mat = pl.BlockSpec((None, 2 * FFT_N2, 2 * FFT_N2), lambda i: (i, 0, 0))
    return pl.pallas_call(
        _spec_body,
        out_shape=jax.ShapeDtypeStruct((2, FFT_N1, FFT_N2, HY_WIDTH), BF16),
        grid=(FFT_N1,),
        in_specs=[mat, mat, blk, blk],
        out_specs=blk,
        compiler_params=_params(("parallel",), 24),
        name="spec",
    )(mf, mi, a, g)


HY_CB = 256


def _short_conv(p_ref, w_ref, b_ref):
    p = p_ref[...]
    w = w_ref[...]
    prev = pltpu.roll(p, 1, 0)
    nxt = pltpu.roll(p, TL - 1, 0)
    return b_ref[...] + prev * w[0:1] + p * w[1:2] + nxt * w[2:3]


def _uconv_body(x1_ref, vh_ref, w1_ref, wv_ref, b1_ref, bv_ref, u_ref):
    u = _short_conv(vh_ref, wv_ref, bv_ref) * _short_conv(x1_ref, w1_ref, b1_ref)
    row = lax.broadcasted_iota(jnp.int32, (TL, 1), 0)
    u_ref[pl.ds(0, TL), :] = jnp.where(row >= PAD, u, 0.0)
    u_ref[pl.ds(TL, TFFT - TL), :] = jnp.zeros((TFFT - TL, HY_CB), F32)


def _hy_specs(first_block):
    nb = HY_WIDTH // HY_CB
    return (pl.BlockSpec((None, TL, HY_CB), lambda b, j: (b, 0, first_block * nb + j)),
            pl.BlockSpec((3, HY_CB), lambda b, j: (0, first_block * nb + j)),
            pl.BlockSpec((1, HY_CB), lambda b, j: (0, first_block * nb + j)))


def _uconv(p3, conv_w, conv_b):
    x1, w1, b1 = _hy_specs(1)
    vh, wv, bv = _hy_specs(2)
    return pl.pallas_call(
        _uconv_body,
        out_shape=jax.ShapeDtypeStruct((BATCH, TFFT, HY_WIDTH), F32),
        grid=(BATCH, HY_WIDTH // HY_CB),
        in_specs=[x1, vh, w1, wv, b1, bv],
        out_specs=pl.BlockSpec((None, TFFT, HY_CB), lambda b, j: (b, 0, j)),
        compiler_params=_params(("parallel", "parallel"), 48),
        name="uconv",
    )(p3, p3, conv_w, conv_w, conv_b, conv_b)


def _ymix_body(y_ref, x0_ref, x1_ref, vh_ref, w0_ref, w1_ref, wv_ref, b0_ref, b1_ref, bv_ref,
               d_ref, gain_ref, o_ref):
    u = _short_conv(vh_ref, wv_ref, bv_ref) * _short_conv(x1_ref, w1_ref, b1_ref)
    yy = (y_ref[...] + d_ref[...] * u) * _short_conv(x0_ref, w0_ref, b0_ref)
    gain = gain_ref[...]
    for s in range(0, HY_CB, HY_GROUP):
        o_ref[:, s:s + HY_GROUP] = (_rms(yy[X0:, s:s + HY_GROUP]) * gain[:, s:s + HY_GROUP]).astype(BF16)


def _ymix(y3, p3, conv_w, conv_b, hyena_d, hyena_norm):
    x0, w0, b0 = _hy_specs(0)
    x1, w1, b1 = _hy_specs(1)
    vh, wv, bv = _hy_specs(2)
    vec = pl.BlockSpec((1, HY_CB), lambda b, j: (0, j))
    return pl.pallas_call(
        _ymix_body,
        out_shape=jax.ShapeDtypeStruct((BATCH, SEQ, HY_WIDTH), BF16),
        grid=(BATCH, HY_WIDTH // HY_CB),
        in_specs=[pl.BlockSpec((None, TL, HY_CB), lambda b, j: (b, 0, j)),
                  x0, x1, vh, w0, w1, wv, b0, b1, bv, vec, vec],
        out_specs=pl.BlockSpec((None, SEQ, HY_CB), lambda b, j: (b, 0, j)),
        compiler_params=_params(("parallel", "parallel"), 56),
        name="ymix",
    )(y3, p3, p3, p3, conv_w, conv_w, conv_w, conv_b, conv_b, conv_b,
      hyena_d.reshape(1, -1), hyena_norm.reshape(1, -1))


N_SUB = CHUNK // SUB
SAFE_BLOCK_DECAY = -60.0


def _scores_exact(q, k, b, rev, ones, row, col):
    rsub = row % SUB
    terms = []
    for d in range(SUB):
        if d == 0:
            kr, br = k, b
        else:
            sh = CHUNK - d if rev else d
            kr, br = pltpu.roll(k, sh, 0), pltpu.roll(b, sh, 0)
        valid = (rsub + d < SUB) if rev else (rsub >= d)
        t = q * kr * jnp.exp(jnp.minimum(b - br, 0.0))
        terms.append(jnp.where(valid, t, 0.0).astype(BF16))
    sums = jnp.dot(jnp.concatenate(terms, axis=0), ones, preferred_element_type=F32)
    a = jnp.zeros((CHUNK, CHUNK), F32)
    for d in range(SUB):
        tgt = row + d if rev else row - d
        a = jnp.where(col == tgt, sums[d * CHUNK:(d + 1) * CHUNK, :CHUNK], a)

    rblk = row // SUB
    cblk = col // SUB
    for jb in (range(1, N_SUB) if rev else range(N_SUB - 1)):
        e = jb * SUB if rev else jb * SUB + SUB - 1
        ref = b[e:e + 1, :]
        qh = (q * jnp.exp(jnp.minimum(b - ref, 0.0))).astype(BF16)
        kh = (k * jnp.exp(jnp.minimum(ref - b, 0.0))).astype(BF16)
        pm = lax.dot_general(qh, kh, (((1,), (1,)), ((), ())), preferred_element_type=F32)
        side = jnp.where(cblk == jb, rblk, jb)
        a = jnp.where((side < jb) if rev else (side > jb), pm, a)

    bend = b[0:1, :] if rev else b[CHUNK - 1:CHUNK, :]
    return a, q * jnp.exp(b), k * jnp.exp(bend - b), bend


def _scores_fast(q, k, b, rev, row, col):
    order = list(range(N_SUB))[::-1] if rev else list(range(N_SUB))
    pos = {blk: p for p, blk in enumerate(order)}
    edge = lambda blk: blk * SUB if rev else blk * SUB + SUB - 1
    e = [b[edge(blk):edge(blk) + 1, :] for blk in order]
    s = [jnp.zeros((1, GLA_DK), F32)] + e[:-1]

    def by_row(vals):
        return jnp.concatenate([jnp.broadcast_to(vals[pos[blk]], (SUB, GLA_DK)) for blk in range(N_SUB)],
                               axis=0)

    srow, erow = by_row(s), by_row(e)
    qh = q * jnp.exp(b - srow)
    kh = k * jnp.exp(erow - b)
    kd = k * jnp.exp(srow - b)

    lhs = []
    for pj in range(N_SUB - 1):
        for blk in range(N_SUB):
            piece = qh[blk * SUB:(blk + 1) * SUB, :]
            p = pos[blk]
            if p <= pj:
                piece = jnp.zeros_like(piece)
            elif p > pj + 1:
                piece = piece * jnp.exp(s[p] - e[pj])
            lhs.append(piece.astype(BF16))
    contract = (((1,), (1,)), ((), ()))
    cross = lax.dot_general(jnp.concatenate(lhs, axis=0), kh.astype(BF16), contract,
                            preferred_element_type=F32)
    diag = lax.dot_general(qh.astype(BF16), kd.astype(BF16), contract, preferred_element_type=F32)

    rblk = row // SUB
    cblk = col // SUB
    a = jnp.zeros((CHUNK, CHUNK), F32)
    for pj in range(N_SUB - 1):
        a = jnp.where(cblk == order[pj], cross[pj * CHUNK:(pj + 1) * CHUNK], a)
    causal = (col >= row) if rev else (col <= row)
    a = jnp.where(cblk == rblk, jnp.where(causal, diag, 0.0), a)

    bend = e[-1]
    return a, qh * jnp.exp(srow), kh * jnp.exp(bend - erow), bend


GLA_RB = 384


def _gla_sweep_body(*refs, rev):
    if rev:
        q_ref, k_ref, v_ref, g_ref, o_ref, st_ref = refs
    else:
        q_ref, k_ref, v_ref, g_ref, og_ref, ob_ref, gain_ref, o_ref, st_ref = refs

    @pl.when(pl.program_id(1) == 0)
    def _():
        st_ref[...] = jnp.zeros_like(st_ref)

    row = lax.broadcasted_iota(jnp.int32, (CHUNK, 1), 0)
    col = lax.broadcasted_iota(jnp.int32, (CHUNK, CHUNK), 1)
    rr = lax.broadcasted_iota(jnp.int32, (CHUNK, CHUNK), 0)
    tri = ((col >= rr) if rev else (col <= rr)).astype(F32)
    ones = jnp.ones((GLA_DK, GLA_DK), BF16)
    n_chunk = GLA_RB // CHUNK

    def run(fast):
        def step(t, carry):
            c = n_chunk - 1 - t if rev else t
            rows = pl.ds(pl.multiple_of(c * CHUNK, CHUNK), CHUNK)
            b_all = jnp.dot(tri, g_ref[rows, :], precision=lax.Precision.HIGHEST, preferred_element_type=F32)
            for h in range(GLA_HEADS):
                kc = slice(h * GLA_DK, (h + 1) * GLA_DK)
                vc = slice(h * GLA_DV, (h + 1) * GLA_DV)
                q = q_ref[rows, kc] * (GLA_DK ** -0.5)
                k = k_ref[rows, kc]
                v = v_ref[rows, vc].astype(BF16)
                b = b_all[:, kc]
                if fast:
                    a, qt, kt, bend = _scores_fast(q, k, b, rev, row, col)
                else:
                    a, qt, kt, bend = _scores_exact(q, k, b, rev, ones, row, col)
                st = st_ref[h]
                o = lax.dot_general(qt.astype(BF16), st.astype(BF16), (((1,), (1,)), ((), ())),
                                    preferred_element_type=F32)
                o = o + jnp.dot(a.astype(BF16), v, preferred_element_type=F32)
                st_ref[h] = st * jnp.exp(bend) + lax.dot_general(
                    v, kt.astype(BF16), (((0,), (0,)), ((), ())), preferred_element_type=F32)
                if rev:
                    o_ref[rows, vc] = o
                else:
                    o = _rms(o + ob_ref[rows, vc]) * gain_ref[...]
                    o_ref[rows, vc] = (o * _silu(og_ref[rows, vc])).astype(BF16)
            return carry

        lax.fori_loop(0, n_chunk, step, 0)

    low = jnp.min(jnp.sum(g_ref[...].reshape(GLA_RB // SUB, SUB, GLA_KEY_WIDTH), axis=1))
    safe = low > SAFE_BLOCK_DECAY
    pl.when(safe)(lambda: run(True))
    pl.when(jnp.logical_not(safe))(lambda: run(False))


def _gla_sweep(p3, lg3, rev, ob=None, gla_norm=None):
    nb = TL // GLA_RB
    blk = (lambda i: nb - 1 - i) if rev else (lambda i: i)
    key_blocks = P_MAIN // GLA_KEY_WIDTH
    q_col = 3 * HY_WIDTH // GLA_KEY_WIDTH
    v_col = (3 * HY_WIDTH + 2 * GLA_KEY_WIDTH) // GLA_WIDTH
    assert key_blocks * GLA_KEY_WIDTH == P_MAIN
    narrow = lambda col: pl.BlockSpec((None, GLA_RB, GLA_KEY_WIDTH), lambda b, i: (b, blk(i), col))
    wide = lambda col: pl.BlockSpec((None, GLA_RB, GLA_WIDTH), lambda b, i: (b, blk(i), col))
    in_specs = [narrow(q_col), narrow(q_col + 1), wide(v_col), narrow(1 if rev else 0)]
    args = [p3, p3, p3, lg3]
    if not rev:
        in_specs += [wide(v_col + 1), wide(0), pl.BlockSpec((1, GLA_DV), lambda b, i: (0, 0))]
        args += [p3, ob, gla_norm.reshape(1, -1)]
    return pl.pallas_call(
        functools.partial(_gla_sweep_body, rev=rev),
        out_shape=jax.ShapeDtypeStruct((BATCH, TL, GLA_WIDTH), F32 if rev else BF16),
        grid=(BATCH, nb),
        in_specs=in_specs,
        out_specs=wide(0),
        scratch_shapes=[pltpu.VMEM((GLA_HEADS, GLA_DV, GLA_DK), F32)],
        compiler_params=_params(("parallel", "arbitrary"), 32),
        name="gla_down" if rev else "gla_up",
    )(*args)


OUT_TM = 512


def _outproj_body(h_ref, yh_ref, yg_ref, wh_ref, wg_ref, o_ref):
    o_ref[...] = (h_ref[...] + jnp.dot(yh_ref[...], wh_ref[...], preferred_element_type=F32)
                  + jnp.dot(yg_ref[...], wg_ref[...], preferred_element_type=F32))


BF16_SUBLANE = 16


def _outproj(hbuf, yh, ygbuf, w_out):
    wo = w_out.astype(BF16)
    half = lambda i: pl.BlockSpec((HY_WIDTH, D_MODEL), lambda r: (i, 0))
    return pl.pallas_call(
        _outproj_body,
        out_shape=jax.ShapeDtypeStruct((X_ROWS, D_MODEL), F32),
        grid=(X_ROWS // OUT_TM,),
        in_specs=[pl.BlockSpec((pl.Element(OUT_TM), pl.Element(D_MODEL)), lambda r: (_x_row(r, OUT_TM), 0)),
                  pl.BlockSpec((OUT_TM, HY_WIDTH), lambda r: (r, 0)),
                  pl.BlockSpec((pl.Element(OUT_TM), pl.Element(GLA_WIDTH)),
                               lambda r: (_x_row(r, OUT_TM, BF16_SUBLANE), 0)),
                  half(0), half(1)],
        out_specs=pl.BlockSpec((OUT_TM, D_MODEL), lambda r: (r, 0)),
        compiler_params=_params(("parallel",), 48),
        name="outproj",
    )(hbuf, yh, ygbuf, wo, wo)


def _long_conv(u3, filt):
    f1, fg, f3, mf, mi = _dft_tables()
    as_n1 = lambda a: a.reshape(2 * FFT_R1, FFT_N2, HY_WIDTH)
    ag = _lmm(fg, as_n1(filt), BF16, name="lmm_g")
    gs = _gspec(mf, ag.reshape(4, FFT_N1, FFT_N2, HY_WIDTH))
    a = _lmm(f1, as_n1(u3), BF16, name="lmm_fwd")
    bm = _spec(mf, mi, a.reshape(2, FFT_N1, FFT_N2, HY_WIDTH), gs)
    y = _lmm(f3, bm.reshape(2 * FFT_N1, FFT_N2, HY_WIDTH), F32, scale=1.0 / NFFT, name="lmm_inv")
    return y.reshape(BATCH, TFFT, HY_WIDTH)


def kernel(x, meta_tokens, ffn1_norm, ffn1_w_gate, ffn1_w_up, ffn1_w_down, mix_norm, w_in, conv_w, conv_b,
           filt_w1, filt_b1, filt_w2, filt_b2, filt_w3, filt_freq, hyena_d, hyena_norm, gk_w2, gk_b2,
           gla_norm, w_out, ffn2_norm, ffn2_w_gate, ffn2_w_up, ffn2_w_down, final_norm):
    assert x.shape == (BATCH, SEQ, D_MODEL) and ffn1_norm.shape[0] == 1

    w1 = _ffn_weights(ffn1_norm[0], ffn1_w_gate[0], ffn1_w_up[0], ffn1_w_down[0], final_norm)
    hbuf = _ffn(x.reshape(X_ROWS, D_MODEL), w1, final=False, shared_rows=_meta_rows(meta_tokens))

    p, lg = _inproj(hbuf, _inproj_weights(mix_norm[0], w_in[0], gk_w2[0], gk_b2[0]))
    p3 = p.reshape(BATCH, TL, P_MAIN)
    lg3 = lg.reshape(BATCH, TL, 2 * GLA_KEY_WIDTH)

    filt = _filters(filt_w1[0], filt_b1[0], filt_w2[0], filt_b2[0], filt_w3[0], filt_freq[0])
    cw, cb = conv_w[0], conv_b[0].reshape(1, -1)
    y3 = _long_conv(_uconv(p3, cw, cb), filt)
    yh = _ymix(y3, p3, cw, cb, hyena_d[0], hyena_norm[0])
    yg = _gla_sweep(p3, lg3, rev=False, ob=_gla_sweep(p3, lg3, rev=True), gla_norm=gla_norm[0])

    h2 = _outproj(hbuf, yh.reshape(X_ROWS, HY_WIDTH), yg.reshape(ROWS, GLA_WIDTH), w_out[0])
    w2 = _ffn_weights(ffn2_norm[0], ffn2_w_gate[0], ffn2_w_up[0], ffn2_w_down[0], final_norm)
    return _ffn(h2, w2, final=True).reshape(BATCH, SEQ, D_MODEL)
```

```python
import functools

import numpy as np
import jax
import jax.numpy as jnp
from jax import lax
from jax.experimental import pallas as pl
from jax.experimental.pallas import tpu as pltpu

F32 = jnp.float32
BF16 = jnp.bfloat16

D_MODEL = 2048
BATCH = 2
SEQ = 4096
N_META = 16
L_TOK = SEQ + N_META
PAD = 112
X0 = PAD + N_META
TL = PAD + L_TOK
ROWS = BATCH * TL
X_ROWS = BATCH * SEQ
HY_WIDTH = 1024
HY_GROUPS = 8
HY_GROUP = HY_WIDTH // HY_GROUPS
FILT_EMB = 33
FILT_BANDS = 16
FILT_HIDDEN = 64
GLA_WIDTH = 1024
GLA_HEADS = 4
GLA_KEY_WIDTH = 512
GLA_DK = 128
GLA_DV = 256
GATE_RANK = 16
GATE_NORMALIZER = 16.0
CHUNK = 64
SUB = 16
D_FF = 5632
P_MAIN = 3 * HY_WIDTH + 2 * GLA_KEY_WIDTH + 2 * GLA_WIDTH
EPS = 1e-6

FFT_N1 = 72
FFT_N2 = 128
NFFT = FFT_N1 * FFT_N2
FFT_R1 = 40
TFFT = FFT_R1 * FFT_N2

MIB = 1024 * 1024
SUBLANE = 8
BF16_SUBLANE = 16
LANE = 128


def _params(sem, vmem_mib):
    return pltpu.CompilerParams(dimension_semantics=sem, vmem_limit_bytes=vmem_mib * MIB)


def _rms(x):
    return x * lax.rsqrt(jnp.mean(x * x, axis=-1, keepdims=True) + EPS)


def _silu(x):
    return x * jax.nn.sigmoid(x)


LAY_TM = 528


def _x_row(i, tm, unit=SUBLANE):
    per_batch = SEQ // tm
    r = (i // per_batch) * (TL // unit) + X0 // unit + (i % per_batch) * (tm // unit)
    return pl.multiple_of(r * unit, unit)


def _lay_src_row(i):
    per_batch = TL // LAY_TM
    r = (i // per_batch) * (SEQ // SUBLANE) + jnp.maximum(
        (i % per_batch) * (LAY_TM // SUBLANE) - X0 // SUBLANE, 0)
    return pl.multiple_of(r * SUBLANE, SUBLANE)


def _meta_rows(meta_tokens):
    return jnp.concatenate([jnp.zeros((PAD, D_MODEL), F32), meta_tokens.astype(F32)], axis=0)


FFN_TM = 512
FFN_TF = 512


def _ffn_body(x_ref, gain_ref, wg_ref, wu_ref, wd_ref, fgain_ref, *rest, final, layout):
    if layout:
        shared_ref, o_ref, xn_ref, acc_ref, xin_ref = rest
    else:
        o_ref, xn_ref, acc_ref = rest
        xin_ref = x_ref
    j = pl.program_id(1)

    @pl.when(j == 0)
    def _():
        if layout:
            first = pl.program_id(0) % (TL // LAY_TM) == 0

            @pl.when(first)
            def _():
                xin_ref[0:X0, :] = shared_ref[...]
                xin_ref[X0:LAY_TM, :] = x_ref[0:LAY_TM - X0, :]

            @pl.when(jnp.logical_not(first))
            def _():
                xin_ref[...] = x_ref[...]

        xn_ref[...] = (_rms(xin_ref[...]) * gain_ref[...]).astype(BF16)
        acc_ref[...] = jnp.zeros_like(acc_ref)

    xn = xn_ref[...]
    g = jnp.dot(xn, wg_ref[...], preferred_element_type=F32)
    u = jnp.dot(xn, wu_ref[...], preferred_element_type=F32)
    a = (_silu(g) * u).astype(BF16)
    acc_ref[...] += jnp.dot(a, wd_ref[...], preferred_element_type=F32)

    @pl.when(j == pl.num_programs(1) - 1)
    def _():
        h = xin_ref[...] + 0.5 * acc_ref[...]
        if final:
            h = _rms(h) * fgain_ref[...]
        o_ref[...] = h


CAST_STEPS = 4


def _cast_body(x_ref, o_ref):
    o_ref[...] = x_ref[...].astype(BF16)


def _to_bf16(w, cols=None):
    rows = w.shape[0]
    cols = w.shape[1] if cols is None else cols
    tr = rows // CAST_STEPS
    assert tr * CAST_STEPS == rows and tr % BF16_SUBLANE == 0 and cols % LANE == 0
    return pl.pallas_call(
        _cast_body,
        out_shape=jax.ShapeDtypeStruct((rows, cols), BF16),
        grid=(CAST_STEPS,),
        in_specs=[pl.BlockSpec((tr, cols), lambda i: (i, 0))],
        out_specs=pl.BlockSpec((tr, cols), lambda i: (i, 0)),
        compiler_params=_params(("parallel",), 48),
        name="to_bf16",
    )(w)


def _ffn_weights(gain, wg, wu, wd, fgain):
    return gain.reshape(1, -1), _to_bf16(wg), _to_bf16(wu), _to_bf16(wd), fgain.reshape(1, -1)


def _ffn(x2, weights, final, shared_rows=None):
    layout = shared_rows is not None
    tm = LAY_TM if layout else FFN_TM
    rows = ROWS if layout else X_ROWS
    wspecs = [
        pl.BlockSpec((1, D_MODEL), lambda i, j: (0, 0)),
        pl.BlockSpec((D_MODEL, FFN_TF), lambda i, j: (0, j)),
        pl.BlockSpec((D_MODEL, FFN_TF), lambda i, j: (0, j)),
        pl.BlockSpec((FFN_TF, D_MODEL), lambda i, j: (j, 0)),
        pl.BlockSpec((1, D_MODEL), lambda i, j: (0, 0)),
    ]
    scratch = [pltpu.VMEM((tm, D_MODEL), BF16), pltpu.VMEM((tm, D_MODEL), F32)]
    if layout:
        x_spec = pl.BlockSpec((pl.Element(tm), pl.Element(D_MODEL)), lambda i, j: (_lay_src_row(i), 0))
        extra_specs = [pl.BlockSpec((X0, D_MODEL), lambda i, j: (0, 0))]
        extra = (shared_rows,)
        scratch.append(pltpu.VMEM((tm, D_MODEL), F32))
    else:
        x_spec = pl.BlockSpec((tm, D_MODEL), lambda i, j: (i, 0))
        extra_specs, extra = [], ()
    return pl.pallas_call(
        functools.partial(_ffn_body, final=final, layout=layout),
        out_shape=jax.ShapeDtypeStruct((rows, D_MODEL), F32),
        grid=(rows // tm, D_FF // FFN_TF),
        in_specs=[x_spec] + wspecs + extra_specs,
        out_specs=pl.BlockSpec((tm, D_MODEL), lambda i, j: (i, 0)),
        scratch_shapes=scratch,
        compiler_params=_params(("parallel", "arbitrary"), 52),
        name="ffn_final" if final else "ffn",
    )(x2, *weights, *extra)


INP_TN = 1024
LR_PAD = 128


def _inproj_body(x_ref, gain_ref, w_ref, wlr_ref, w2_ref, b2_ref, p_ref, lg_ref, xn_ref):
    @pl.when(pl.program_id(1) == 0)
    def _():
        xn = (_rms(x_ref[...]) * gain_ref[...]).astype(BF16)
        xn_ref[...] = xn
        lr = jnp.dot(xn, wlr_ref[...], preferred_element_type=F32).astype(BF16)
        z = jnp.dot(lr, w2_ref[...], preferred_element_type=F32) + b2_ref[...]
        lg_ref[...] = (jnp.minimum(z, 0.0) - jnp.log1p(jnp.exp(-jnp.abs(z)))) * (1.0 / GATE_NORMALIZER)

    p_ref[...] = jnp.dot(xn_ref[...], w_ref[...], preferred_element_type=F32)


def _inproj_weights(gain, w_in, gk_w2, gk_b2):
    w_main = _to_bf16(w_in, P_MAIN)
    w_lr = jnp.pad(w_in[:, P_MAIN:], ((0, 0), (0, LR_PAD - 2 * GATE_RANK))).astype(BF16)
    w2 = jnp.zeros((LR_PAD, 2 * GLA_KEY_WIDTH), F32)
    w2 = w2.at[:GATE_RANK, :GLA_KEY_WIDTH].set(gk_w2[0])
    w2 = w2.at[GATE_RANK:2 * GATE_RANK, GLA_KEY_WIDTH:].set(gk_w2[1]).astype(BF16)
    return gain.reshape(1, -1), w_main, w_lr, w2, gk_b2.reshape(1, 2 * GLA_KEY_WIDTH)


def _inproj(hbuf, weights):
    tm, tn = LAY_TM, INP_TN
    return pl.pallas_call(
        _inproj_body,
        out_shape=(jax.ShapeDtypeStruct((ROWS, P_MAIN), F32),
                   jax.ShapeDtypeStruct((ROWS, 2 * GLA_KEY_WIDTH), F32)),
        grid=(ROWS // tm, P_MAIN // tn),
        in_specs=[
            pl.BlockSpec((tm, D_MODEL), lambda i, j: (i, 0)),
            pl.BlockSpec((1, D_MODEL), lambda i, j: (0, 0)),
            pl.BlockSpec((D_MODEL, tn), lambda i, j: (0, j)),
            pl.BlockSpec((D_MODEL, LR_PAD), lambda i, j: (0, 0)),
            pl.BlockSpec((LR_PAD, 2 * GLA_KEY_WIDTH), lambda i, j: (0, 0)),
            pl.BlockSpec((1, 2 * GLA_KEY_WIDTH), lambda i, j: (0, 0)),
        ],
        out_specs=(pl.BlockSpec((tm, tn), lambda i, j: (i, j)),
                   pl.BlockSpec((tm, 2 * GLA_KEY_WIDTH), lambda i, j: (i, 0))),
        scratch_shapes=[pltpu.VMEM((tm, D_MODEL), BF16)],
        compiler_params=_params(("parallel", "arbitrary"), 40),
        name="inproj",
    )(hbuf, *weights)


FILT_TR = 640
FEAT_PAD = 128


def _filt_tables():
    pos = np.arange(TFFT, dtype=np.float64)
    t = pos / (L_TOK - 1)
    w = (2.0 * np.pi / L_TOK) * pos
    bands = 1e-4 + np.arange(FILT_BANDS, dtype=np.float64) * ((FILT_BANDS - 1 - 1e-4) / (FILT_BANDS - 1))
    ang = w[:, None] * bands[None, :]
    feats = np.zeros((TFFT, FEAT_PAD), np.float64)
    feats[:, 0] = t
    feats[:, 1:1 + FILT_BANDS] = np.cos(ang)
    feats[:, 1 + FILT_BANDS:FILT_EMB] = -np.sin(ang)
    lo, hi = np.log(1e-2) / 1.5, np.log(1e-2) / 0.3
    deltas = np.abs(lo + np.arange(HY_WIDTH, dtype=np.float64) * ((hi - lo) / (HY_WIDTH - 1)))
    return feats.astype(np.float32), deltas.astype(np.float32).reshape(1, HY_WIDTH)


def _filt_body(feat_ref, w1_ref, b1_ref, w2_ref, b2_ref, w3_ref, fr_ref, dl_ref, h_ref):
    hp = lax.Precision.HIGHEST
    feats = feat_ref[...]
    fr = fr_ref[...]
    z = jnp.sin(fr * (jnp.dot(feats, w1_ref[...], precision=hp, preferred_element_type=F32) + b1_ref[...]))
    z = jnp.sin(fr * (jnp.dot(z, w2_ref[...], precision=hp, preferred_element_type=F32) + b2_ref[...]))
    hh = jnp.dot(z, w3_ref[...], precision=hp, preferred_element_type=F32)
    pos = pl.program_id(0) * FILT_TR + lax.broadcasted_iota(jnp.int32, (FILT_TR, 1), 0)
    win = jnp.exp(-feats[:, 0:1] * dl_ref[...])
    win = jnp.where(pos < L_TOK, win, 0.0)
    hf = hh[:, :HY_WIDTH] * win
    hb = hh[:, HY_WIDTH:] * win
    h_ref[0] = jnp.where(pos == 0, hf + hb, hf)
    h_ref[1] = jnp.where(pos == 0, 0.0, hb)


def _filters(w1, b1, w2, b2, w3, freq):
    feats, deltas = _filt_tables()
    w1p = jnp.pad(w1, ((0, FEAT_PAD - FILT_EMB), (0, 0)))
    full = lambda shape: pl.BlockSpec(shape, lambda i: (0, 0))
    return pl.pallas_call(
        _filt_body,
        out_shape=jax.ShapeDtypeStruct((2, TFFT, HY_WIDTH), F32),
        grid=(TFFT // FILT_TR,),
        in_specs=[
            pl.BlockSpec((FILT_TR, FEAT_PAD), lambda i: (i, 0)),
            full((FEAT_PAD, FILT_HIDDEN)), full((1, FILT_HIDDEN)),
            full((FILT_HIDDEN, FILT_HIDDEN)), full((1, FILT_HIDDEN)),
            full((FILT_HIDDEN, 2 * HY_WIDTH)), full((1, FILT_HIDDEN)), full((1, HY_WIDTH)),
        ],
        out_specs=pl.BlockSpec((2, FILT_TR, HY_WIDTH), lambda i: (0, i, 0)),
        compiler_params=_params(("parallel",), 40),
        name="filt",
    )(jnp.asarray(feats), w1p, b1.reshape(1, -1), w2, b2.reshape(1, -1), w3, freq.reshape(1, -1),
      jnp.asarray(deltas))


def _dft_tables():
    n1 = np.arange(FFT_N1)
    ang1 = 2.0 * np.pi * ((n1[:, None] * n1[None, :]) % FFT_N1) / FFT_N1
    c1, s1 = np.cos(ang1)[:, :FFT_R1], np.sin(ang1)[:, :FFT_R1]
    zero = np.zeros_like(c1)
    f1 = np.block([[c1, s1], [-s1, c1]])
    fg = np.block([[c1, zero], [-s1, zero], [zero, c1], [zero, -s1]])
    f3 = np.block([[c1.T, -s1.T], [s1.T, c1.T]])
    k1 = np.arange(FFT_N1)[:, None, None]
    k2 = np.arange(FFT_N2)[None, :, None]
    n2 = np.arange(FFT_N2)[None, None, :]
    ang2 = 2.0 * np.pi * ((n2 * (k1 + FFT_N1 * k2)) % NFFT) / NFFT
    c2, s2 = np.cos(ang2), np.sin(ang2)
    mf = np.concatenate([np.concatenate([c2, s2], axis=2), np.concatenate([-s2, c2], axis=2)], axis=1)
    c2t, s2t = np.swapaxes(c2, 1, 2), np.swapaxes(s2, 1, 2)
    mi = np.concatenate([np.concatenate([c2t, -s2t], axis=2), np.concatenate([s2t, c2t], axis=2)], axis=1)
    as16 = lambda a: jnp.asarray(a.astype(np.float32)).astype(BF16)
    return as16(f1), as16(fg), as16(f3), as16(mf), as16(mi)


LMM_PITCH = FFT_N2 + SUBLANE
LMM_MC = 2 * FFT_N1


def _lmm_body(f_ref, x_ref, o_ref, xs_ref, os_ref, *, scale):
    m, k = f_ref.shape
    for g in range(k):
        xs_ref[pl.ds(g * LMM_PITCH, FFT_N2), :] = x_ref[pl.ds(g * FFT_N2, FFT_N2), :].astype(F32)
    for m0 in range(0, m, LMM_MC):
        f = f_ref[m0:min(m0 + LMM_MC, m), :]

        def slab(s, carry):
            x = xs_ref[pl.ds(s, k, stride=LMM_PITCH), :].astype(BF16)
            r = jnp.dot(f, x, preferred_element_type=F32)
            os_ref[pl.ds(s, f.shape[0], stride=LMM_PITCH), :] = r * scale if scale != 1.0 else r
            return carry

        lax.fori_loop(0, FFT_N2, slab, 0, unroll=4)
        for g in range(f.shape[0]):
            o_ref[pl.ds((m0 + g) * FFT_N2, FFT_N2), :] = os_ref[pl.ds(g * LMM_PITCH, FFT_N2), :].astype(
                o_ref.dtype)


def _lmm(f, x3, out_dtype, scale=1.0, name="lmm"):
    m, k = f.shape
    mc = min(m, LMM_MC)
    assert m % mc == 0
    out = pl.pallas_call(
        functools.partial(_lmm_body, scale=scale),
        out_shape=jax.ShapeDtypeStruct((m * FFT_N2, HY_WIDTH), out_dtype),
        grid=(HY_WIDTH // LANE,),
        in_specs=[pl.BlockSpec((m, k), lambda j: (0, 0)),
                  pl.BlockSpec((k * FFT_N2, LANE), lambda j: (0, j))],
        out_specs=pl.BlockSpec((m * FFT_N2, LANE), lambda j: (0, j)),
        scratch_shapes=[pltpu.VMEM((k * LMM_PITCH, LANE), F32), pltpu.VMEM((mc * LMM_PITCH, LANE), F32)],
        compiler_params=_params(("parallel",), 56),
        name=name,
    )(f, x3.reshape(k * FFT_N2, HY_WIDTH))
    return out.reshape(m, FFT_N2, HY_WIDTH)


def _gspec_body(mf_ref, a_ref, g_ref):
    mf = mf_ref[...]
    xf = jnp.dot(mf, a_ref[0:2].reshape(2 * FFT_N2, HY_WIDTH), preferred_element_type=F32)
    xr = jnp.dot(mf, a_ref[2:4].reshape(2 * FFT_N2, HY_WIDTH), preferred_element_type=F32)
    re = lax.broadcasted_iota(jnp.int32, (2 * FFT_N2, 1), 0) < FFT_N2
    g_ref[...] = (xf + jnp.where(re, xr, -xr)).astype(BF16).reshape(2, 1, FFT_N2, HY_WIDTH)


def _gspec(mf, ag):
    return pl.pallas_call(
        _gspec_body,
        out_shape=jax.ShapeDtypeStruct((2, FFT_N1, FFT_N2, HY_WIDTH), BF16),
        grid=(FFT_N1,),
        in_specs=[pl.BlockSpec((None, 2 * FFT_N2, 2 * FFT_N2), lambda i: (i, 0, 0)),
                  pl.BlockSpec((4, 1, FFT_N2, HY_WIDTH), lambda i: (0, i, 0, 0))],
        out_specs=pl.BlockSpec((2, 1, FFT_N2, HY_WIDTH), lambda i: (0, i, 0, 0)),
        compiler_params=_params(("parallel",), 24),
        name="gspec",
    )(mf, ag)


def _spec_body(mf_ref, mi_ref, a_ref, g_ref, o_ref):
    a = a_ref[...].reshape(2 * FFT_N2, HY_WIDTH)
    x = jnp.dot(mf_ref[...], a, preferred_element_type=F32)
    xr, xi = x[:FFT_N2], x[FFT_N2:]
    gr, gi = g_ref[0, 0].astype(F32), g_ref[1, 0].astype(F32)
    y = jnp.concatenate([xr * gr - xi * gi, xr * gi + xi * gr], axis=0).astype(BF16)
    o_ref[...] = jnp.dot(mi_ref[...], y, preferred_element_type=F32).astype(BF16).reshape(
        2, 1, FFT_N2, HY_WIDTH)


def _spec(mf, mi, a, g):
    blk = pl.BlockSpec((2, 1, FFT_N2, HY_WIDTH), lambda i: (0, i, 0, 0))
    mat = pl.BlockSpec((None, 2 * FFT_N2, 2 * FFT_N2), lambda i: (i, 0, 0))
    return pl.pallas_call(
        _spec_body,
        out_shape=jax.ShapeDtypeStruct((2, FFT_N1, FFT_N2, HY_WIDTH), BF16),
        grid=(FFT_N1,),
        in_specs=[mat, mat, blk, blk],
        out_specs=blk,
        compiler_params=_params(("parallel",), 24),
        name="spec",
    )(mf, mi, a, g)


HY_CB = 256


def _short_conv(p_ref, w_ref, b_ref):
    p = p_ref[...]
    w = w_ref[...]
    prev = pltpu.roll(p, 1, 0)
    nxt = pltpu.roll(p, TL - 1, 0)
    return b_ref[...] + prev * w[0:1] + p * w[1:2] + nxt * w[2:3]


def _uconv_body(x1_ref, vh_ref, w1_ref, wv_ref, b1_ref, bv_ref, u_ref):
    u = _short_conv(vh_ref, wv_ref, bv_ref) * _short_conv(x1_ref, w1_ref, b1_ref)
    row = lax.broadcasted_iota(jnp.int32, (TL, 1), 0)
    u_ref[pl.ds(0, TL), :] = jnp.where(row >= PAD, u, 0.0)
    u_ref[pl.ds(TL, TFFT - TL), :] = jnp.zeros((TFFT - TL, HY_CB), F32)


def _hy_specs(first_block):
    nb = HY_WIDTH // HY_CB
    return (pl.BlockSpec((None, TL, HY_CB), lambda b, j: (b, 0, first_block * nb + j)),
            pl.BlockSpec((3, HY_CB), lambda b, j: (0, first_block * nb + j)),
            pl.BlockSpec((1, HY_CB), lambda b, j: (0, first_block * nb + j)))


def _uconv(p3, conv_w, conv_b):
    x1, w1, b1 = _hy_specs(1)
    vh, wv, bv = _hy_specs(2)
    return pl.pallas_call(
        _uconv_body,
        out_shape=jax.ShapeDtypeStruct((BATCH, TFFT, HY_WIDTH), F32),
        grid=(BATCH, HY_WIDTH // HY_CB),
        in_specs=[x1, vh, w1, wv, b1, bv],
        out_specs=pl.BlockSpec((None, TFFT, HY_CB), lambda b, j: (b, 0, j)),
        compiler_params=_params(("parallel", "parallel"), 48),
        name="uconv",
    )(p3, p3, conv_w, conv_w, conv_b, conv_b)


def _ymix_body(y_ref, x0_ref, x1_ref, vh_ref, w0_ref, w1_ref, wv_ref, b0_ref, b1_ref, bv_ref,
               d_ref, gain_ref, o_ref):
    u = _short_conv(vh_ref, wv_ref, bv_ref) * _short_conv(x1_ref, w1_ref, b1_ref)
    yy = (y_ref[...] + d_ref[...] * u) * _short_conv(x0_ref, w0_ref, b0_ref)
    gain = gain_ref[...]
    for s in range(0, HY_CB, HY_GROUP):
        o_ref[:, s:s + HY_GROUP] = (_rms(yy[X0:, s:s + HY_GROUP]) * gain[:, s:s + HY_GROUP]).astype(BF16)


def _ymix(y3, p3, conv_w, conv_b, hyena_d, hyena_norm):
    x0, w0, b0 = _hy_specs(0)
    x1, w1, b1 = _hy_specs(1)
    vh, wv, bv = _hy_specs(2)
    vec = pl.BlockSpec((1, HY_CB), lambda b, j: (0, j))
    return pl.pallas_call(
        _ymix_body,
        out_shape=jax.ShapeDtypeStruct((BATCH, SEQ, HY_WIDTH), BF16),
        grid=(BATCH, HY_WIDTH // HY_CB),
        in_specs=[pl.BlockSpec((None, TL, HY_CB), lambda b, j: (b, 0, j)),
                  x0, x1, vh, w0, w1, wv, b0, b1, bv, vec, vec],
        out_specs=pl.BlockSpec((None, SEQ, HY_CB), lambda b, j: (b, 0, j)),
        compiler_params=_params(("parallel", "parallel"), 56),
        name="ymix",
    )(y3, p3, p3, p3, conv_w, conv_w, conv_w, conv_b, conv_b, conv_b,
      hyena_d.reshape(1, -1), hyena_norm.reshape(1, -1))


N_SUB = CHUNK // SUB
SAFE_BLOCK_DECAY = -60.0


def _scores_exact(q, k, b, rev, ones, row, col):
    rsub = row % SUB
    terms = []
    for d in range(SUB):
        if d == 0:
            kr, br = k, b
        else:
            sh = CHUNK - d if rev else d
            kr, br = pltpu.roll(k, sh, 0), pltpu.roll(b, sh, 0)
        valid = (rsub + d < SUB) if rev else (rsub >= d)
        t = q * kr * jnp.exp(jnp.minimum(b - br, 0.0))
        terms.append(jnp.where(valid, t, 0.0).astype(BF16))
    sums = jnp.dot(jnp.concatenate(terms, axis=0), ones, preferred_element_type=F32)
    a = jnp.zeros((CHUNK, CHUNK), F32)
    for d in range(SUB):
        tgt = row + d if rev else row - d
        a = jnp.where(col == tgt, sums[d * CHUNK:(d + 1) * CHUNK, :CHUNK], a)

    rblk = row // SUB
    cblk = col // SUB
    for jb in (range(1, N_SUB) if rev else range(N_SUB - 1)):
        e = jb * SUB if rev else jb * SUB + SUB - 1
        ref = b[e:e + 1, :]
        qh = (q * jnp.exp(jnp.minimum(b - ref, 0.0))).astype(BF16)
        kh = (k * jnp.exp(jnp.minimum(ref - b, 0.0))).astype(BF16)
        pm = lax.dot_general(qh, kh, (((1,), (1,)), ((), ())), preferred_element_type=F32)
        side = jnp.where(cblk == jb, rblk, jb)
        a = jnp.where((side < jb) if rev else (side > jb), pm, a)

    bend = b[0:1, :] if rev else b[CHUNK - 1:CHUNK, :]
    return a, q * jnp.exp(b), k * jnp.exp(bend - b), bend


def _scores_fast(q, k, b, rev, row, col):
    order = list(range(N_SUB))[::-1] if rev else list(range(N_SUB))
    pos = {blk: p for p, blk in enumerate(order)}
    edge = lambda blk: blk * SUB if rev else blk * SUB + SUB - 1
    e = [b[edge(blk):edge(blk) + 1, :] for blk in order]
    s = [jnp.zeros((1, GLA_DK), F32)] + e[:-1]

    def by_row(vals):
        return jnp.concatenate([jnp.broadcast_to(vals[pos[blk]], (SUB, GLA_DK)) for blk in range(N_SUB)],
                               axis=0)

    srow, erow = by_row(s), by_row(e)
    qh = q * jnp.exp(b - srow)
    kh = k * jnp.exp(erow - b)
    kd = k * jnp.exp(srow - b)

    lhs = []
    for pj in range(N_SUB - 1):
        for blk in range(N_SUB):
            piece = qh[blk * SUB:(blk + 1) * SUB, :]
            p = pos[blk]
            if p <= pj:
                piece = jnp.zeros_like(piece)
            elif p > pj + 1:
                piece = piece * jnp.exp(s[p] - e[pj])
            lhs.append(piece.astype(BF16))
    contract = (((1,), (1,)), ((), ()))
    cross = lax.dot_general(jnp.concatenate(lhs, axis=0), kh.astype(BF16), contract,
                            preferred_element_type=F32)
    diag = lax.dot_general(qh.astype(BF16), kd.astype(BF16), contract, preferred_element_type=F32)

    rblk = row // SUB
    cblk = col // SUB
    a = jnp.zeros((CHUNK, CHUNK), F32)
    for pj in range(N_SUB - 1):
        a = jnp.where(cblk == order[pj], cross[pj * CHUNK:(pj + 1) * CHUNK], a)
    causal = (col >= row) if rev else (col <= row)
    a = jnp.where(cblk == rblk, jnp.where(causal, diag, 0.0), a)

    bend = e[-1]
    return a, qh * jnp.exp(srow), kh * jnp.exp(bend - erow), bend


GLA_RB = 384


def _gla_sweep_body(*refs, rev):
    if rev:
        q_ref, k_ref, v_ref, g_ref, o_ref, st_ref = refs
    else:
        q_ref, k_ref, v_ref, g_ref, og_ref, ob_ref, gain_ref, o_ref, st_ref = refs

    @pl.when(pl.program_id(1) == 0)
    def _():
        st_ref[...] = jnp.zeros_like(st_ref)

    row = lax.broadcasted_iota(jnp.int32, (CHUNK, 1), 0)
    col = lax.broadcasted_iota(jnp.int32, (CHUNK, CHUNK), 1)
    rr = lax.broadcasted_iota(jnp.int32, (CHUNK, CHUNK), 0)
    tri = ((col >= rr) if rev else (col <= rr)).astype(F32)
    ones = jnp.ones((GLA_DK, GLA_DK), BF16)
    n_chunk = GLA_RB // CHUNK

    def run(fast):
        def step(t, carry):
            c = n_chunk - 1 - t if rev else t
            rows = pl.ds(pl.multiple_of(c * CHUNK, CHUNK), CHUNK)
            b_all = jnp.dot(tri, g_ref[rows, :], precision=lax.Precision.HIGHEST, preferred_element_type=F32)
            for h in range(GLA_HEADS):
                kc = slice(h * GLA_DK, (h + 1) * GLA_DK)
                vc = slice(h * GLA_DV, (h + 1) * GLA_DV)
                q = q_ref[rows, kc] * (GLA_DK ** -0.5)
                k = k_ref[rows, kc]
                v = v_ref[rows, vc].astype(BF16)
                b = b_all[:, kc]
                if fast:
                    a, qt, kt, bend = _scores_fast(q, k, b, rev, row, col)
                else:
                    a, qt, kt, bend = _scores_exact(q, k, b, rev, ones, row, col)
                st = st_ref[h]
                o = lax.dot_general(qt.astype(BF16), st.astype(BF16), (((1,), (1,)), ((), ())),
                                    preferred_element_type=F32)
                o = o + jnp.dot(a.astype(BF16), v, preferred_element_type=F32)
                st_ref[h] = st * jnp.exp(bend) + lax.dot_general(
                    v, kt.astype(BF16), (((0,), (0,)), ((), ())), preferred_element_type=F32)
                if rev:
                    o_ref[rows, vc] = o
                else:
                    o = _rms(o + ob_ref[rows, vc]) * gain_ref[...]
                    o_ref[rows, vc] = (o * _silu(og_ref[rows, vc])).astype(BF16)
            return carry

        lax.fori_loop(0, n_chunk, step, 0)

    low = jnp.min(jnp.sum(g_ref[...].reshape(GLA_RB // SUB, SUB, GLA_KEY_WIDTH), axis=1))
    safe = low > SAFE_BLOCK_DECAY
    pl.when(safe)(lambda: run(True))
    pl.when(jnp.logical_not(safe))(lambda: run(False))


def _gla_sweep(p3, lg3, rev, ob=None, gla_norm=None):
    nb = TL // GLA_RB
    blk = (lambda i: nb - 1 - i) if rev else (lambda i: i)
    key_blocks = P_MAIN // GLA_KEY_WIDTH
    q_col = 3 * HY_WIDTH // GLA_KEY_WIDTH
    v_col = (3 * HY_WIDTH + 2 * GLA_KEY_WIDTH) // GLA_WIDTH
    assert key_blocks * GLA_KEY_WIDTH == P_MAIN
    narrow = lambda col: pl.BlockSpec((None, GLA_RB, GLA_KEY_WIDTH), lambda b, i: (b, blk(i), col))
    wide = lambda col: pl.BlockSpec((None, GLA_RB, GLA_WIDTH), lambda b, i: (b, blk(i), col))
    in_specs = [narrow(q_col), narrow(q_col + 1), wide(v_col), narrow(1 if rev else 0)]
    args = [p3, p3, p3, lg3]
    if not rev:
        in_specs += [wide(v_col + 1), wide(0), pl.BlockSpec((1, GLA_DV), lambda b, i: (0, 0))]
        args += [p3, ob, gla_norm.reshape(1, -1)]
    return pl.pallas_call(
        functools.partial(_gla_sweep_body, rev=rev),
        out_shape=jax.ShapeDtypeStruct((BATCH, TL, GLA_WIDTH), F32 if rev else BF16),
        grid=(BATCH, nb),
        in_specs=in_specs,
        out_specs=wide(0),
        scratch_shapes=[pltpu.VMEM((GLA_HEADS, GLA_DV, GLA_DK), F32)],
        compiler_params=_params(("parallel", "arbitrary"), 32),
        name="gla_down" if rev else "gla_up",
    )(*args)


OUT_TM = 512


def _outproj_body(h_ref, yh_ref, yg_ref, wh_ref, wg_ref, o_ref):
    o_ref[...] = (h_ref[...] + jnp.dot(yh_ref[...], wh_ref[...], preferred_element_type=F32)
                  + jnp.dot(yg_ref[...], wg_ref[...], preferred_element_type=F32))


def _outproj(hbuf, yh, ygbuf, w_out):
    wo = _to_bf16(w_out)
    half = lambda i: pl.BlockSpec((HY_WIDTH, D_MODEL), lambda r: (i, 0))
    return pl.pallas_call(
        _outproj_body,
        out_shape=jax.ShapeDtypeStruct((X_ROWS, D_MODEL), F32),
        grid=(X_ROWS // OUT_TM,),
        in_specs=[pl.BlockSpec((pl.Element(OUT_TM), pl.Element(D_MODEL)), lambda r: (_x_row(r, OUT_TM), 0)),
                  pl.BlockSpec((OUT_TM, HY_WIDTH), lambda r: (r, 0)),
                  pl.BlockSpec((pl.Element(OUT_TM), pl.Element(GLA_WIDTH)),
                               lambda r: (_x_row(r, OUT_TM, BF16_SUBLANE), 0)),
                  half(0), half(1)],
        out_specs=pl.BlockSpec((OUT_TM, D_MODEL), lambda r: (r, 0)),
        compiler_params=_params(("parallel",), 48),
        name="outproj",
    )(hbuf, yh, ygbuf, wo, wo)


def _long_conv(u3, filt):
    f1, fg, f3, mf, mi = _dft_tables()
    as_n1 = lambda a: a.reshape(2 * FFT_R1, FFT_N2, HY_WIDTH)
    ag = _lmm(fg, as_n1(filt), BF16, name="lmm_g")
    gs = _gspec(mf, ag.reshape(4, FFT_N1, FFT_N2, HY_WIDTH))
    a = _lmm(f1, as_n1(u3), BF16, name="lmm_fwd")
    bm = _spec(mf, mi, a.reshape(2, FFT_N1, FFT_N2, HY_WIDTH), gs)
    y = _lmm(f3, bm.reshape(2 * FFT_N1, FFT_N2, HY_WIDTH), F32, scale=1.0 / NFFT, name="lmm_inv")
    return y.reshape(BATCH, TFFT, HY_WIDTH)


def kernel(x, meta_tokens, ffn1_norm, ffn1_w_gate, ffn1_w_up, ffn1_w_down, mix_norm, w_in, conv_w, conv_b,
           filt_w1, filt_b1, filt_w2, filt_b2, filt_w3, filt_freq, hyena_d, hyena_norm, gk_w2, gk_b2,
           gla_norm, w_out, ffn2_norm, ffn2_w_gate, ffn2_w_up, ffn2_w_down, final_norm):
    assert x.shape == (BATCH, SEQ, D_MODEL) and ffn1_norm.shape[0] == 1

    w1 = _ffn_weights(ffn1_norm[0], ffn1_w_gate[0], ffn1_w_up[0], ffn1_w_down[0], final_norm)
    hbuf = _ffn(x.reshape(X_ROWS, D_MODEL), w1, final=False, shared_rows=_meta_rows(meta_tokens))

    p, lg = _inproj(hbuf, _inproj_weights(mix_norm[0], w_in[0], gk_w2[0], gk_b2[0]))
    p3 = p.reshape(BATCH, TL, P_MAIN)
    lg3 = lg.reshape(BATCH, TL, 2 * GLA_KEY_WIDTH)

    filt = _filters(filt_w1[0], filt_b1[0], filt_w2[0], filt_b2[0], filt_w3[0], filt_freq[0])
    cw, cb = conv_w[0], conv_b[0].reshape(1, -1)
    y3 = _long_conv(_uconv(p3, cw, cb), filt)
    yh = _ymix(y3, p3, cw, cb, hyena_d[0], hyena_norm[0])
    yg = _gla_sweep(p3, lg3, rev=False, ob=_gla_sweep(p3, lg3, rev=True), gla_norm=gla_norm[0])

    h2 = _outproj(hbuf, yh.reshape(X_ROWS, HY_WIDTH), yg.reshape(ROWS, GLA_WIDTH), w_out[0])
    w2 = _ffn_weights(ffn2_norm[0], ffn2_w_gate[0], ffn2_w_up[0], ffn2_w_down[0], final_norm)
    return _ffn(h2, w2, final=True).reshape(BATCH, SEQ, D_MODEL)
```

```python
import functools

import numpy as np
import jax
import jax.numpy as jnp
from jax import lax
from jax.experimental import pallas as pl
from jax.experimental.pallas import tpu as pltpu

F32 = jnp.float32
BF16 = jnp.bfloat16

D_MODEL = 2048
BATCH = 2
SEQ = 4096
N_META = 16
L_TOK = SEQ + N_META
PAD = 112
X0 = PAD + N_META
TL = PAD + L_TOK
ROWS = BATCH * TL
X_ROWS = BATCH * SEQ
HY_WIDTH = 1024
HY_GROUPS = 8
HY_GROUP = HY_WIDTH // HY_GROUPS
FILT_EMB = 33
FILT_BANDS = 16
FILT_HIDDEN = 64
GLA_WIDTH = 1024
GLA_HEADS = 4
GLA_KEY_WIDTH = 512
GLA_DK = 128
GLA_DV = 256
GATE_RANK = 16
GATE_NORMALIZER = 16.0
CHUNK = 64
SUB = 16
D_FF = 5632
P_MAIN = 3 * HY_WIDTH + 2 * GLA_KEY_WIDTH + 2 * GLA_WIDTH
EPS = 1e-6

FFT_N1 = 72
FFT_N2 = 128
NFFT = FFT_N1 * FFT_N2
FFT_R1 = 40
TFFT = FFT_R1 * FFT_N2

MIB = 1024 * 1024
SUBLANE = 8
BF16_SUBLANE = 16
LANE = 128


def _params(sem, vmem_mib):
    return pltpu.CompilerParams(dimension_semantics=sem, vmem_limit_bytes=vmem_mib * MIB)


def _rms(x):
    return x * lax.rsqrt(jnp.mean(x * x, axis=-1, keepdims=True) + EPS)


def _silu(x):
    return x * jax.nn.sigmoid(x)


LAY_TM = 528


def _x_row(i, tm, unit=SUBLANE):
    per_batch = SEQ // tm
    r = (i // per_batch) * (TL // unit) + X0 // unit + (i % per_batch) * (tm // unit)
    return pl.multiple_of(r * unit, unit)


def _lay_src_row(i):
    per_batch = TL // LAY_TM
    r = (i // per_batch) * (SEQ // SUBLANE) + jnp.maximum(
        (i % per_batch) * (LAY_TM // SUBLANE) - X0 // SUBLANE, 0)
    return pl.multiple_of(r * SUBLANE, SUBLANE)


def _meta_rows(meta_tokens):
    return jnp.concatenate([jnp.zeros((PAD, D_MODEL), F32), meta_tokens.astype(F32)], axis=0)


FFN_TM = 512
FFN_TF = 512


def _ffn_body(x_ref, gain_ref, wg_ref, wu_ref, wd_ref, fgain_ref, *rest, final, layout):
    if layout:
        shared_ref, o_ref, xn_ref, acc_ref, xin_ref = rest
    else:
        o_ref, xn_ref, acc_ref = rest
        xin_ref = x_ref
    j = pl.program_id(1)

    @pl.when(j == 0)
    def _():
        if layout:
            first = pl.program_id(0) % (TL // LAY_TM) == 0

            @pl.when(first)
            def _():
                xin_ref[0:X0, :] = shared_ref[...]
                xin_ref[X0:LAY_TM, :] = x_ref[0:LAY_TM - X0, :]

            @pl.when(jnp.logical_not(first))
            def _():
                xin_ref[...] = x_ref[...]

        xn_ref[...] = (_rms(xin_ref[...]) * gain_ref[...]).astype(BF16)
        acc_ref[...] = jnp.zeros_like(acc_ref)

    xn = xn_ref[...]
    g = jnp.dot(xn, wg_ref[...], preferred_element_type=F32)
    u = jnp.dot(xn, wu_ref[...], preferred_element_type=F32)
    a = (_silu(g) * u).astype(BF16)
    acc_ref[...] += jnp.dot(a, wd_ref[...], preferred_element_type=F32)

    @pl.when(j == pl.num_programs(1) - 1)
    def _():
        h = xin_ref[...] + 0.5 * acc_ref[...]
        if final:
            h = _rms(h) * fgain_ref[...]
        o_ref[...] = h


CAST_STEPS = 4


def _cast_body(x_ref, o_ref):
    o_ref[...] = x_ref[...].astype(BF16)


def _to_bf16(w, cols=None):
    rows = w.shape[0]
    cols = w.shape[1] if cols is None else cols
    tr = rows // CAST_STEPS
    assert tr * CAST_STEPS == rows and tr % BF16_SUBLANE == 0 and cols % LANE == 0
    return pl.pallas_call(
        _cast_body,
        out_shape=jax.ShapeDtypeStruct((rows, cols), BF16),
        grid=(CAST_STEPS,),
        in_specs=[pl.BlockSpec((tr, cols), lambda i: (i, 0))],
        out_specs=pl.BlockSpec((tr, cols), lambda i: (i, 0)),
        compiler_params=_params(("parallel",), 48),
        name="to_bf16",
    )(w)


def _ffn_weights(gain, wg, wu, wd, fgain):
    return gain.reshape(1, -1), _to_bf16(wg), _to_bf16(wu), _to_bf16(wd), fgain.reshape(1, -1)


def _ffn(x2, weights, final, shared_rows=None):
    layout = shared_rows is not None
    tm = LAY_TM if layout else FFN_TM
    rows = ROWS if layout else X_ROWS
    wspecs = [
        pl.BlockSpec((1, D_MODEL), lambda i, j: (0, 0)),
        pl.BlockSpec((D_MODEL, FFN_TF), lambda i, j: (0, j)),
        pl.BlockSpec((D_MODEL, FFN_TF), lambda i, j: (0, j)),
        pl.BlockSpec((FFN_TF, D_MODEL), lambda i, j: (j, 0)),
        pl.BlockSpec((1, D_MODEL), lambda i, j: (0, 0)),
    ]
    scratch = [pltpu.VMEM((tm, D_MODEL), BF16), pltpu.VMEM((tm, D_MODEL), F32)]
    if layout:
        x_spec = pl.BlockSpec((pl.Element(tm), pl.Element(D_MODEL)), lambda i, j: (_lay_src_row(i), 0))
        extra_specs = [pl.BlockSpec((X0, D_MODEL), lambda i, j: (0, 0))]
        extra = (shared_rows,)
        scratch.append(pltpu.VMEM((tm, D_MODEL), F32))
    else:
        x_spec = pl.BlockSpec((tm, D_MODEL), lambda i, j: (i, 0))
        extra_specs, extra = [], ()
    return pl.pallas_call(
        functools.partial(_ffn_body, final=final, layout=layout),
        out_shape=jax.ShapeDtypeStruct((rows, D_MODEL), F32),
        grid=(rows // tm, D_FF // FFN_TF),
        in_specs=[x_spec] + wspecs + extra_specs,
        out_specs=pl.BlockSpec((tm, D_MODEL), lambda i, j: (i, 0)),
        scratch_shapes=scratch,
        compiler_params=_params(("parallel", "arbitrary"), 52),
        name="ffn_final" if final else "ffn",
    )(x2, *weights, *extra)


INP_TM = 2 * LAY_TM
INP_TN = 1024
LR_PAD = 128


def _inproj_body(x_ref, gain_ref, w_ref, wlr_ref, w2_ref, b2_ref, p_ref, lg_ref, xn_ref):
    @pl.when(pl.program_id(1) == 0)
    def _():
        xn = (_rms(x_ref[...]) * gain_ref[...]).astype(BF16)
        xn_ref[...] = xn
        lr = jnp.dot(xn, wlr_ref[...], preferred_element_type=F32).astype(BF16)
        z = jnp.dot(lr, w2_ref[...], preferred_element_type=F32) + b2_ref[...]
        lg_ref[...] = (jnp.minimum(z, 0.0) - jnp.log1p(jnp.exp(-jnp.abs(z)))) * (1.0 / GATE_NORMALIZER)

    p_ref[...] = jnp.dot(xn_ref[...], w_ref[...], preferred_element_type=F32)


def _split_w_in_body(w_ref, main_ref, lr_ref):
    main_ref[...] = w_ref[:, :P_MAIN].astype(BF16)
    lr_ref[...] = jnp.zeros_like(lr_ref)
    lr_ref[:, :2 * GATE_RANK] = w_ref[:, P_MAIN:].astype(BF16)


def _split_w_in(w_in):
    tr = D_MODEL // CAST_STEPS
    return pl.pallas_call(
        _split_w_in_body,
        out_shape=(jax.ShapeDtypeStruct((D_MODEL, P_MAIN), BF16), jax.ShapeDtypeStruct((D_MODEL, LR_PAD), BF16)),
        grid=(CAST_STEPS,),
        in_specs=[pl.BlockSpec((tr, P_MAIN + 2 * GATE_RANK), lambda i: (i, 0))],
        out_specs=(pl.BlockSpec((tr, P_MAIN), lambda i: (i, 0)), pl.BlockSpec((tr, LR_PAD), lambda i: (i, 0))),
        compiler_params=_params(("parallel",), 48),
        name="split_w_in",
    )(w_in)


def _inproj_weights(gain, w_in, gk_w2, gk_b2):
    w_main, w_lr = _split_w_in(w_in)
    w2 = jnp.zeros((LR_PAD, 2 * GLA_KEY_WIDTH), F32)
    w2 = w2.at[:GATE_RANK, :GLA_KEY_WIDTH].set(gk_w2[0])
    w2 = w2.at[GATE_RANK:2 * GATE_RANK, GLA_KEY_WIDTH:].set(gk_w2[1]).astype(BF16)
    return gain.reshape(1, -1), w_main, w_lr, w2, gk_b2.reshape(1, 2 * GLA_KEY_WIDTH)


def _inproj(hbuf, weights):
    tm, tn = INP_TM, INP_TN
    return pl.pallas_call(
        _inproj_body,
        out_shape=(jax.ShapeDtypeStruct((ROWS, P_MAIN), F32),
                   jax.ShapeDtypeStruct((ROWS, 2 * GLA_KEY_WIDTH), F32)),
        grid=(ROWS // tm, P_MAIN // tn),
        in_specs=[
            pl.BlockSpec((tm, D_MODEL), lambda i, j: (i, 0)),
            pl.BlockSpec((1, D_MODEL), lambda i, j: (0, 0)),
            pl.BlockSpec((D_MODEL, tn), lambda i, j: (0, j)),
            pl.BlockSpec((D_MODEL, LR_PAD), lambda i, j: (0, 0)),
            pl.BlockSpec((LR_PAD, 2 * GLA_KEY_WIDTH), lambda i, j: (0, 0)),
            pl.BlockSpec((1, 2 * GLA_KEY_WIDTH), lambda i, j: (0, 0)),
        ],
        out_specs=(pl.BlockSpec((tm, tn), lambda i, j: (i, j)),
                   pl.BlockSpec((tm, 2 * GLA_KEY_WIDTH), lambda i, j: (i, 0))),
        scratch_shapes=[pltpu.VMEM((tm, D_MODEL), BF16)],
        compiler_params=_params(("parallel", "arbitrary"), 56),
        name="inproj",
    )(hbuf, *weights)


FILT_TR = 640
FEAT_PAD = 128


def _filt_tables():
    pos = np.arange(TFFT, dtype=np.float64)
    t = pos / (L_TOK - 1)
    w = (2.0 * np.pi / L_TOK) * pos
    bands = 1e-4 + np.arange(FILT_BANDS, dtype=np.float64) * ((FILT_BANDS - 1 - 1e-4) / (FILT_BANDS - 1))
    ang = w[:, None] * bands[None, :]
    feats = np.zeros((TFFT, FEAT_PAD), np.float64)
    feats[:, 0] = t
    feats[:, 1:1 + FILT_BANDS] = np.cos(ang)
    feats[:, 1 + FILT_BANDS:FILT_EMB] = -np.sin(ang)
    lo, hi = np.log(1e-2) / 1.5, np.log(1e-2) / 0.3
    deltas = np.abs(lo + np.arange(HY_WIDTH, dtype=np.float64) * ((hi - lo) / (HY_WIDTH - 1)))
    return feats.astype(np.float32), deltas.astype(np.float32).reshape(1, HY_WIDTH)


def _filt_body(feat_ref, w1_ref, b1_ref, w2_ref, b2_ref, w3_ref, fr_ref, dl_ref, h_ref):
    hp = lax.Precision.HIGHEST
    feats = feat_ref[...]
    fr = fr_ref[...]
    z = jnp.sin(fr * (jnp.dot(feats, w1_ref[...], precision=hp, preferred_element_type=F32) + b1_ref[...]))
    z = jnp.sin(fr * (jnp.dot(z, w2_ref[...], precision=hp, preferred_element_type=F32) + b2_ref[...]))
    hh = jnp.dot(z, w3_ref[...], precision=hp, preferred_element_type=F32)
    pos = pl.program_id(0) * FILT_TR + lax.broadcasted_iota(jnp.int32, (FILT_TR, 1), 0)
    win = jnp.exp(-feats[:, 0:1] * dl_ref[...])
    win = jnp.where(pos < L_TOK, win, 0.0)
    hf = hh[:, :HY_WIDTH] * win
    hb = hh[:, HY_WIDTH:] * win
    h_ref[0] = jnp.where(pos == 0, hf + hb, hf)
    h_ref[1] = jnp.where(pos == 0, 0.0, hb)


def _filters(w1, b1, w2, b2, w3, freq):
    feats, deltas = _filt_tables()
    w1p = jnp.pad(w1, ((0, FEAT_PAD - FILT_EMB), (0, 0)))
    full = lambda shape: pl.BlockSpec(shape, lambda i: (0, 0))
    return pl.pallas_call(
        _filt_body,
        out_shape=jax.ShapeDtypeStruct((2, TFFT, HY_WIDTH), F32),
        grid=(TFFT // FILT_TR,),
        in_specs=[
            pl.BlockSpec((FILT_TR, FEAT_PAD), lambda i: (i, 0)),
            full((FEAT_PAD, FILT_HIDDEN)), full((1, FILT_HIDDEN)),
            full((FILT_HIDDEN, FILT_HIDDEN)), full((1, FILT_HIDDEN)),
            full((FILT_HIDDEN, 2 * HY_WIDTH)), full((1, FILT_HIDDEN)), full((1, HY_WIDTH)),
        ],
        out_specs=pl.BlockSpec((2, FILT_TR, HY_WIDTH), lambda i: (0, i, 0)),
        compiler_params=_params(("parallel",), 40),
        name="filt",
    )(jnp.asarray(feats), w1p, b1.reshape(1, -1), w2, b2.reshape(1, -1), w3, freq.reshape(1, -1),
      jnp.asarray(deltas))


def _dft_tables():
    n1 = np.arange(FFT_N1)
    ang1 = 2.0 * np.pi * ((n1[:, None] * n1[None, :]) % FFT_N1) / FFT_N1
    c1, s1 = np.cos(ang1)[:, :FFT_R1], np.sin(ang1)[:, :FFT_R1]
    zero = np.zeros_like(c1)
    f1 = np.block([[c1, s1], [-s1, c1]])
    fg = np.block([[c1, zero], [-s1, zero], [zero, c1], [zero, -s1]])
    f3 = np.block([[c1.T, -s1.T], [s1.T, c1.T]])
    k1 = np.arange(FFT_N1)[:, None, None]
    k2 = np.arange(FFT_N2)[None, :, None]
    n2 = np.arange(FFT_N2)[None, None, :]
    ang2 = 2.0 * np.pi * ((n2 * (k1 + FFT_N1 * k2)) % NFFT) / NFFT
    c2, s2 = np.cos(ang2), np.sin(ang2)
    mf = np.concatenate([np.concatenate([c2, s2], axis=2), np.concatenate([-s2, c2], axis=2)], axis=1)
    c2t, s2t = np.swapaxes(c2, 1, 2), np.swapaxes(s2, 1, 2)
    mi = np.concatenate([np.concatenate([c2t, -s2t], axis=2), np.concatenate([s2t, c2t], axis=2)], axis=1)
    as16 = lambda a: jnp.asarray(a.astype(np.float32)).astype(BF16)
    return as16(f1), as16(fg), as16(f3), as16(mf), as16(mi)


LMM_PITCH = FFT_N2 + SUBLANE
LMM_MC = 2 * FFT_N1


def _lmm_body(f_ref, x_ref, o_ref, xs_ref, os_ref, *, scale):
    m, k = f_ref.shape
    for g in range(k):
        xs_ref[pl.ds(g * LMM_PITCH, FFT_N2), :] = x_ref[pl.ds(g * FFT_N2, FFT_N2), :].astype(F32)
    for m0 in range(0, m, LMM_MC):
        f = f_ref[m0:min(m0 + LMM_MC, m), :]

        def slab(s, carry):
            x = xs_ref[pl.ds(s, k, stride=LMM_PITCH), :].astype(BF16)
            r = jnp.dot(f, x, preferred_element_type=F32)
            os_ref[pl.ds(s, f.shape[0], stride=LMM_PITCH), :] = r * scale if scale != 1.0 else r
            return carry

        lax.fori_loop(0, FFT_N2, slab, 0, unroll=4)
        for g in range(f.shape[0]):
            o_ref[pl.ds((m0 + g) * FFT_N2, FFT_N2), :] = os_ref[pl.ds(g * LMM_PITCH, FFT_N2), :].astype(
                o_ref.dtype)


def _lmm(f, x3, out_dtype, scale=1.0, name="lmm"):
    m, k = f.shape
    mc = min(m, LMM_MC)
    assert m % mc == 0
    out = pl.pallas_call(
        functools.partial(_lmm_body, scale=scale),
        out_shape=jax.ShapeDtypeStruct((m * FFT_N2, HY_WIDTH), out_dtype),
        grid=(HY_WIDTH // LANE,),
        in_specs=[pl.BlockSpec((m, k), lambda j: (0, 0)),
                  pl.BlockSpec((k * FFT_N2, LANE), lambda j: (0, j))],
        out_specs=pl.BlockSpec((m * FFT_N2, LANE), lambda j: (0, j)),
        scratch_shapes=[pltpu.VMEM((k * LMM_PITCH, LANE), F32), pltpu.VMEM((mc * LMM_PITCH, LANE), F32)],
        compiler_params=_params(("parallel",), 56),
        name=name,
    )(f, x3.reshape(k * FFT_N2, HY_WIDTH))
    return out.reshape(m, FFT_N2, HY_WIDTH)


def _gspec_body(mf_ref, a_ref, g_ref):
    mf = mf_ref[...]
    xf = jnp.dot(mf, a_ref[0:2].reshape(2 * FFT_N2, HY_WIDTH), preferred_element_type=F32)
    xr = jnp.dot(mf, a_ref[2:4].reshape(2 * FFT_N2, HY_WIDTH), preferred_element_type=F32)
    re = lax.broadcasted_iota(jnp.int32, (2 * FFT_N2, 1), 0) < FFT_N2
    g_ref[...] = (xf + jnp.where(re, xr, -xr)).astype(BF16).reshape(2, 1, FFT_N2, HY_WIDTH)


def _gspec(mf, ag):
    return pl.pallas_call(
        _gspec_body,
        out_shape=jax.ShapeDtypeStruct((2, FFT_N1, FFT_N2, HY_WIDTH), BF16),
        grid=(FFT_N1,),
        in_specs=[pl.BlockSpec((None, 2 * FFT_N2, 2 * FFT_N2), lambda i: (i, 0, 0)),
                  pl.BlockSpec((4, 1, FFT_N2, HY_WIDTH), lambda i: (0, i, 0, 0))],
        out_specs=pl.BlockSpec((2, 1, FFT_N2, HY_WIDTH), lambda i: (0, i, 0, 0)),
        compiler_params=_params(("parallel",), 24),
        name="gspec",
    )(mf, ag)


def _spec_body(mf_ref, mi_ref, a_ref, g_ref, o_ref):
    a = a_ref[...].reshape(2 * FFT_N2, HY_WIDTH)
    x = jnp.dot(mf_ref[...], a, preferred_element_type=F32)
    xr, xi = x[:FFT_N2], x[FFT_N2:]
    gr, gi = g_ref[0, 0].astype(F32), g_ref[1, 0].astype(F32)
    y = jnp.concatenate([xr * gr - xi * gi, xr * gi + xi * gr], axis=0).astype(BF16)
    o_ref[...] = jnp.dot(mi_ref[...], y, preferred_element_type=F32).astype(BF16).reshape(
        2, 1, FFT_N2, HY_WIDTH)


def _spec(mf, mi, a, g):
    blk = pl.BlockSpec((2, 1, FFT_N2, HY_WIDTH), lambda i: (0, i, 0, 0))
    mat = pl.BlockSpec((None, 2 * FFT_N2, 2 * FFT_N2), lambda i: (i, 0, 0))
    return pl.pallas_call(
        _spec_body,
        out_shape=jax.ShapeDtypeStruct((2, FFT_N1, FFT_N2, HY_WIDTH), BF16),
        grid=(FFT_N1,),
        in_specs=[mat, mat, blk, blk],
        out_specs=blk,
        compiler_params=_params(("parallel",), 24),
        name="spec",
    )(mf, mi, a, g)


HY_CB = 256


def _short_conv(p_ref, w_ref, b_ref):
    p = p_ref[...]
    w = w_ref[...]
    prev = pltpu.roll(p, 1, 0)
    nxt = pltpu.roll(p, TL - 1, 0)
    return b_ref[...] + prev * w[0:1] + p * w[1:2] + nxt * w[2:3]


def _uconv_body(x1_ref, vh_ref, w1_ref, wv_ref, b1_ref, bv_ref, u_ref):
    u = _short_conv(vh_ref, wv_ref, bv_ref) * _short_conv(x1_ref, w1_ref, b1_ref)
    row = lax.broadcasted_iota(jnp.int32, (TL, 1), 0)
    u_ref[pl.ds(0, TL), :] = jnp.where(row >= PAD, u, 0.0)
    u_ref[pl.ds(TL, TFFT - TL), :] = jnp.zeros((TFFT - TL, HY_CB), F32)


def _hy_specs(first_block):
    nb = HY_WIDTH // HY_CB
    return (pl.BlockSpec((None, TL, HY_CB), lambda b, j: (b, 0, first_block * nb + j)),
            pl.BlockSpec((3, HY_CB), lambda b, j: (0, first_block * nb + j)),
            pl.BlockSpec((1, HY_CB), lambda b, j: (0, first_block * nb + j)))


def _uconv(p3, conv_w, conv_b):
    x1, w1, b1 = _hy_specs(1)
    vh, wv, bv = _hy_specs(2)
    return pl.pallas_call(
        _uconv_body,
        out_shape=jax.ShapeDtypeStruct((BATCH, TFFT, HY_WIDTH), F32),
        grid=(BATCH, HY_WIDTH // HY_CB),
        in_specs=[x1, vh, w1, wv, b1, bv],
        out_specs=pl.BlockSpec((None, TFFT, HY_CB), lambda b, j: (b, 0, j)),
        compiler_params=_params(("parallel", "parallel"), 48),
        name="uconv",
    )(p3, p3, conv_w, conv_w, conv_b, conv_b)


def _ymix_body(y_ref, x0_ref, x1_ref, vh_ref, w0_ref, w1_ref, wv_ref, b0_ref, b1_ref, bv_ref,
               d_ref, gain_ref, o_ref):
    u = _short_conv(vh_ref, wv_ref, bv_ref) * _short_conv(x1_ref, w1_ref, b1_ref)
    yy = (y_ref[...] + d_ref[...] * u) * _short_conv(x0_ref, w0_ref, b0_ref)
    gain = gain_ref[...]
    for s in range(0, HY_CB, HY_GROUP):
        o_ref[:, s:s + HY_GROUP] = (_rms(yy[X0:, s:s + HY_GROUP]) * gain[:, s:s + HY_GROUP]).astype(BF16)


def _ymix(y3, p3, conv_w, conv_b, hyena_d, hyena_norm):
    x0, w0, b0 = _hy_specs(0)
    x1, w1, b1 = _hy_specs(1)
    vh, wv, bv = _hy_specs(2)
    vec = pl.BlockSpec((1, HY_CB), lambda b, j: (0, j))
    return pl.pallas_call(
        _ymix_body,
        out_shape=jax.ShapeDtypeStruct((BATCH, SEQ, HY_WIDTH), BF16),
        grid=(BATCH, HY_WIDTH // HY_CB),
        in_specs=[pl.BlockSpec((None, TL, HY_CB), lambda b, j: (b, 0, j)),
                  x0, x1, vh, w0, w1, wv, b0, b1, bv, vec, vec],
        out_specs=pl.BlockSpec((None, SEQ, HY_CB), lambda b, j: (b, 0, j)),
        compiler_params=_params(("parallel", "parallel"), 56),
        name="ymix",
    )(y3, p3, p3, p3, conv_w, conv_w, conv_w, conv_b, conv_b, conv_b,
      hyena_d.reshape(1, -1), hyena_norm.reshape(1, -1))


N_SUB = CHUNK // SUB
SAFE_BLOCK_DECAY = -60.0


def _scores_exact(q, k, b, rev, ones, row, col):
    rsub = row % SUB
    terms = []
    for d in range(SUB):
        if d == 0:
            kr, br = k, b
        else:
            sh = CHUNK - d if rev else d
            kr, br = pltpu.roll(k, sh, 0), pltpu.roll(b, sh, 0)
        valid = (rsub + d < SUB) if rev else (rsub >= d)
        t = q * kr * jnp.exp(jnp.minimum(b - br, 0.0))
        terms.append(jnp.where(valid, t, 0.0).astype(BF16))
    sums = jnp.dot(jnp.concatenate(terms, axis=0), ones, preferred_element_type=F32)
    a = jnp.zeros((CHUNK, CHUNK), F32)
    for d in range(SUB):
        tgt = row + d if rev else row - d
        a = jnp.where(col == tgt, sums[d * CHUNK:(d + 1) * CHUNK, :CHUNK], a)

    rblk = row // SUB
    cblk = col // SUB
    for jb in (range(1, N_SUB) if rev else range(N_SUB - 1)):
        e = jb * SUB if rev else jb * SUB + SUB - 1
        ref = b[e:e + 1, :]
        qh = (q * jnp.exp(jnp.minimum(b - ref, 0.0))).astype(BF16)
        kh = (k * jnp.exp(jnp.minimum(ref - b, 0.0))).astype(BF16)
        pm = lax.dot_general(qh, kh, (((1,), (1,)), ((), ())), preferred_element_type=F32)
        side = jnp.where(cblk == jb, rblk, jb)
        a = jnp.where((side < jb) if rev else (side > jb), pm, a)

    bend = b[0:1, :] if rev else b[CHUNK - 1:CHUNK, :]
    return a, q * jnp.exp(b), k * jnp.exp(bend - b), bend


def _scores_fast(q, k, b, rev, row, col):
    order = list(range(N_SUB))[::-1] if rev else list(range(N_SUB))
    pos = {blk: p for p, blk in enumerate(order)}
    edge = lambda blk: blk * SUB if rev else blk * SUB + SUB - 1
    e = [b[edge(blk):edge(blk) + 1, :] for blk in order]
    s = [jnp.zeros((1, GLA_DK), F32)] + e[:-1]

    def by_row(vals):
        return jnp.concatenate([jnp.broadcast_to(vals[pos[blk]], (SUB, GLA_DK)) for blk in range(N_SUB)],
                               axis=0)

    srow, erow = by_row(s), by_row(e)
    qh = q * jnp.exp(b - srow)
    kh = k * jnp.exp(erow - b)
    kd = k * jnp.exp(srow - b)

    lhs = []
    for pj in range(N_SUB - 1):
        for blk in range(N_SUB):
            piece = qh[blk * SUB:(blk + 1) * SUB, :]
            p = pos[blk]
            if p <= pj:
                piece = jnp.zeros_like(piece)
            elif p > pj + 1:
                piece = piece * jnp.exp(s[p] - e[pj])
            lhs.append(piece.astype(BF16))
    contract = (((1,), (1,)), ((), ()))
    cross = lax.dot_general(jnp.concatenate(lhs, axis=0), kh.astype(BF16), contract,
                            preferred_element_type=F32)
    diag = lax.dot_general(qh.astype(BF16), kd.astype(BF16), contract, preferred_element_type=F32)

    rblk = row // SUB
    cblk = col // SUB
    a = jnp.zeros((CHUNK, CHUNK), F32)
    for pj in range(N_SUB - 1):
        a = jnp.where(cblk == order[pj], cross[pj * CHUNK:(pj + 1) * CHUNK], a)
    causal = (col >= row) if rev else (col <= row)
    a = jnp.where(cblk == rblk, jnp.where(causal, diag, 0.0), a)

    bend = e[-1]
    return a, qh * jnp.exp(srow), kh * jnp.exp(bend - erow), bend


GLA_RB = 384


def _gla_sweep_body(*refs, rev):
    if rev:
        q_ref, k_ref, v_ref, g_ref, o_ref, st_ref = refs
    else:
        q_ref, k_ref, v_ref, g_ref, og_ref, ob_ref, gain_ref, o_ref, st_ref = refs

    @pl.when(pl.program_id(1) == 0)
    def _():
        st_ref[...] = jnp.zeros_like(st_ref)

    row = lax.broadcasted_iota(jnp.int32, (CHUNK, 1), 0)
    col = lax.broadcasted_iota(jnp.int32, (CHUNK, CHUNK), 1)
    rr = lax.broadcasted_iota(jnp.int32, (CHUNK, CHUNK), 0)
    tri = ((col >= rr) if rev else (col <= rr)).astype(F32)
    ones = jnp.ones((GLA_DK, GLA_DK), BF16)
    n_chunk = GLA_RB // CHUNK

    def run(fast):
        def step(t, carry):
            c = n_chunk - 1 - t if rev else t
            rows = pl.ds(pl.multiple_of(c * CHUNK, CHUNK), CHUNK)
            b_all = jnp.dot(tri, g_ref[rows, :], precision=lax.Precision.HIGHEST, preferred_element_type=F32)
            pending = []
            for h in range(GLA_HEADS):
                kc = slice(h * GLA_DK, (h + 1) * GLA_DK)
                vc = slice(h * GLA_DV, (h + 1) * GLA_DV)
                q = q_ref[rows, kc] * (GLA_DK ** -0.5)
                k = k_ref[rows, kc]
                v = v_ref[rows, vc].astype(BF16)
                b = b_all[:, kc]
                if fast:
                    a, qt, kt, bend = _scores_fast(q, k, b, rev, row, col)
                else:
                    a, qt, kt, bend = _scores_exact(q, k, b, rev, ones, row, col)
                st = st_ref[h]
                o = lax.dot_general(qt.astype(BF16), st.astype(BF16), (((1,), (1,)), ((), ())),
                                    preferred_element_type=F32)
                st_ref[h] = st * jnp.exp(bend) + lax.dot_general(
                    v, kt.astype(BF16), (((0,), (0,)), ((), ())), preferred_element_type=F32)
                pending.append((vc, o, a, v))
            for vc, o, a, v in pending:
                o = o + jnp.dot(a.astype(BF16), v, preferred_element_type=F32)
                if rev:
                    o_ref[rows, vc] = o
                else:
                    o = _rms(o + ob_ref[rows, vc]) * gain_ref[...]
                    o_ref[rows, vc] = (o * _silu(og_ref[rows, vc])).astype(BF16)
            return carry

        lax.fori_loop(0, n_chunk, step, 0)

    low = jnp.min(jnp.sum(g_ref[...].reshape(GLA_RB // SUB, SUB, GLA_KEY_WIDTH), axis=1))
    safe = low > SAFE_BLOCK_DECAY
    pl.when(safe)(lambda: run(True))
    pl.when(jnp.logical_not(safe))(lambda: run(False))


def _gla_sweep(p3, lg3, rev, ob=None, gla_norm=None):
    nb = TL // GLA_RB
    blk = (lambda i: nb - 1 - i) if rev else (lambda i: i)
    key_blocks = P_MAIN // GLA_KEY_WIDTH
    q_col = 3 * HY_WIDTH // GLA_KEY_WIDTH
    v_col = (3 * HY_WIDTH + 2 * GLA_KEY_WIDTH) // GLA_WIDTH
    assert key_blocks * GLA_KEY_WIDTH == P_MAIN
    narrow = lambda col: pl.BlockSpec((None, GLA_RB, GLA_KEY_WIDTH), lambda b, i: (b, blk(i), col))
    wide = lambda col: pl.BlockSpec((None, GLA_RB, GLA_WIDTH), lambda b, i: (b, blk(i), col))
    in_specs = [narrow(q_col), narrow(q_col + 1), wide(v_col), narrow(1 if rev else 0)]
    args = [p3, p3, p3, lg3]
    if not rev:
        in_specs += [wide(v_col + 1), wide(0), pl.BlockSpec((1, GLA_DV), lambda b, i: (0, 0))]
        args += [p3, ob, gla_norm.reshape(1, -1)]
    return pl.pallas_call(
        functools.partial(_gla_sweep_body, rev=rev),
        out_shape=jax.ShapeDtypeStruct((BATCH, TL, GLA_WIDTH), F32 if rev else BF16),
        grid=(BATCH, nb),
        in_specs=in_specs,
        out_specs=wide(0),
        scratch_shapes=[pltpu.VMEM((GLA_HEADS, GLA_DV, GLA_DK), F32)],
        compiler_params=_params(("parallel", "arbitrary"), 32),
        name="gla_down" if rev else "gla_up",
    )(*args)


OUT_TM = 512


def _outproj_body(h_ref, yh_ref, yg_ref, wh_ref, wg_ref, o_ref):
    o_ref[...] = (h_ref[...] + jnp.dot(yh_ref[...], wh_ref[...], preferred_element_type=F32)
                  + jnp.dot(yg_ref[...], wg_ref[...], preferred_element_type=F32))


def _outproj(hbuf, yh, ygbuf, w_out):
    wo = _to_bf16(w_out)
    half = lambda i: pl.BlockSpec((HY_WIDTH, D_MODEL), lambda r: (i, 0))
    return pl.pallas_call(
        _outproj_body,
        out_shape=jax.ShapeDtypeStruct((X_ROWS, D_MODEL), F32),
        grid=(X_ROWS // OUT_TM,),
        in_specs=[pl.BlockSpec((pl.Element(OUT_TM), pl.Element(D_MODEL)), lambda r: (_x_row(r, OUT_TM), 0)),
                  pl.BlockSpec((OUT_TM, HY_WIDTH), lambda r: (r, 0)),
                  pl.BlockSpec((pl.Element(OUT_TM), pl.Element(GLA_WIDTH)),
                               lambda r: (_x_row(r, OUT_TM, BF16_SUBLANE), 0)),
                  half(0), half(1)],
        out_specs=pl.BlockSpec((OUT_TM, D_MODEL), lambda r: (r, 0)),
        compiler_params=_params(("parallel",), 48),
        name="outproj",
    )(hbuf, yh, ygbuf, wo, wo)


def _long_conv(u3, filt):
    f1, fg, f3, mf, mi = _dft_tables()
    as_n1 = lambda a: a.reshape(2 * FFT_R1, FFT_N2, HY_WIDTH)
    ag = _lmm(fg, as_n1(filt), BF16, name="lmm_g")
    gs = _gspec(mf, ag.reshape(4, FFT_N1, FFT_N2, HY_WIDTH))
    a = _lmm(f1, as_n1(u3), BF16, name="lmm_fwd")
    bm = _spec(mf, mi, a.reshape(2, FFT_N1, FFT_N2, HY_WIDTH), gs)
    y = _lmm(f3, bm.reshape(2 * FFT_N1, FFT_N2, HY_WIDTH), F32, scale=1.0 / NFFT, name="lmm_inv")
    return y.reshape(BATCH, TFFT, HY_WIDTH)


def kernel(x, meta_tokens, ffn1_norm, ffn1_w_gate, ffn1_w_up, ffn1_w_down, mix_norm, w_in, conv_w, conv_b,
           filt_w1, filt_b1, filt_w2, filt_b2, filt_w3, filt_freq, hyena_d, hyena_norm, gk_w2, gk_b2,
           gla_norm, w_out, ffn2_norm, ffn2_w_gate, ffn2_w_up, ffn2_w_down, final_norm):
    assert x.shape == (BATCH, SEQ, D_MODEL) and ffn1_norm.shape[0] == 1

    w1 = _ffn_weights(ffn1_norm[0], ffn1_w_gate[0], ffn1_w_up[0], ffn1_w_down[0], final_norm)
    hbuf = _ffn(x.reshape(X_ROWS, D_MODEL), w1, final=False, shared_rows=_meta_rows(meta_tokens))

    p, lg = _inproj(hbuf, _inproj_weights(mix_norm[0], w_in[0], gk_w2[0], gk_b2[0]))
    p3 = p.reshape(BATCH, TL, P_MAIN)
    lg3 = lg.reshape(BATCH, TL, 2 * GLA_KEY_WIDTH)

    filt = _filters(filt_w1[0], filt_b1[0], filt_w2[0], filt_b2[0], filt_w3[0], filt_freq[0])
    cw, cb = conv_w[0], conv_b[0].reshape(1, -1)
    y3 = _long_conv(_uconv(p3, cw, cb), filt)
    yh = _ymix(y3, p3, cw, cb, hyena_d[0], hyena_norm[0])
    yg = _gla_sweep(p3, lg3, rev=False, ob=_gla_sweep(p3, lg3, rev=True), gla_norm=gla_norm[0])

    h2 = _outproj(hbuf, yh.reshape(X_ROWS, HY_WIDTH), yg.reshape(ROWS, GLA_WIDTH), w_out[0])
    w2 = _ffn_weights(ffn2_norm[0], ffn2_w_gate[0], ffn2_w_up[0], ffn2_w_down[0], final_norm)
    return _ffn(h2, w2, final=True).reshape(BATCH, SEQ, D_MODEL)
```

```python
import functools

import numpy as np
import jax
import jax.numpy as jnp
from jax import lax
from jax.experimental import pallas as pl
from jax.experimental.pallas import tpu as pltpu

F32 = jnp.float32
BF16 = jnp.bfloat16

D_MODEL = 2048
BATCH = 2
SEQ = 4096
N_META = 16
L_TOK = SEQ + N_META
PAD = 112
X0 = PAD + N_META
TL = PAD + L_TOK
ROWS = BATCH * TL
X_ROWS = BATCH * SEQ
HY_WIDTH = 1024
HY_GROUPS = 8
HY_GROUP = HY_WIDTH // HY_GROUPS
FILT_EMB = 33
FILT_BANDS = 16
FILT_HIDDEN = 64
GLA_WIDTH = 1024
GLA_HEADS = 4
GLA_KEY_WIDTH = 512
GLA_DK = 128
GLA_DV = 256
GATE_RANK = 16
GATE_NORMALIZER = 16.0
CHUNK = 64
SUB = 16
D_FF = 5632
P_MAIN = 3 * HY_WIDTH + 2 * GLA_KEY_WIDTH + 2 * GLA_WIDTH
EPS = 1e-6

FFT_N1 = 72
FFT_N2 = 128
NFFT = FFT_N1 * FFT_N2
FFT_R1 = 40
TFFT = FFT_R1 * FFT_N2

MIB = 1024 * 1024
SUBLANE = 8
BF16_SUBLANE = 16
LANE = 128


def _params(sem, vmem_mib):
    return pltpu.CompilerParams(dimension_semantics=sem, vmem_limit_bytes=vmem_mib * MIB)


def _rms(x):
    return x * lax.rsqrt(jnp.mean(x * x, axis=-1, keepdims=True) + EPS)


def _silu(x):
    return x * jax.nn.sigmoid(x)


LAY_TM = 528


def _x_row(i, tm, unit=SUBLANE):
    per_batch = SEQ // tm
    r = (i // per_batch) * (TL // unit) + X0 // unit + (i % per_batch) * (tm // unit)
    return pl.multiple_of(r * unit, unit)


def _lay_src_row(i):
    per_batch = TL // LAY_TM
    r = (i // per_batch) * (SEQ // SUBLANE) + jnp.maximum(
        (i % per_batch) * (LAY_TM // SUBLANE) - X0 // SUBLANE, 0)
    return pl.multiple_of(r * SUBLANE, SUBLANE)


def _meta_rows(meta_tokens):
    return jnp.concatenate([jnp.zeros((PAD, D_MODEL), F32), meta_tokens.astype(F32)], axis=0)


FFN_TM = 512
FFN_TF = 512


def _ffn_body(x_ref, gain_ref, wg_ref, wu_ref, wd_ref, fgain_ref, *rest, final, layout):
    if layout:
        shared_ref, o_ref, xn_ref, acc_ref, xin_ref = rest
    else:
        o_ref, xn_ref, acc_ref = rest
        xin_ref = x_ref
    j = pl.program_id(1)

    @pl.when(j == 0)
    def _():
        if layout:
            first = pl.program_id(0) % (TL // LAY_TM) == 0

            @pl.when(first)
            def _():
                xin_ref[0:X0, :] = shared_ref[...]
                xin_ref[X0:LAY_TM, :] = x_ref[0:LAY_TM - X0, :]

            @pl.when(jnp.logical_not(first))
            def _():
                xin_ref[...] = x_ref[...]

        xn_ref[...] = (_rms(xin_ref[...]) * gain_ref[...]).astype(BF16)
        acc_ref[...] = jnp.zeros_like(acc_ref)

    xn = xn_ref[...]
    g = jnp.dot(xn, wg_ref[...], preferred_element_type=F32)
    u = jnp.dot(xn, wu_ref[...], preferred_element_type=F32)
    a = (_silu(g) * u).astype(BF16)
    acc_ref[...] += jnp.dot(a, wd_ref[...], preferred_element_type=F32)

    @pl.when(j == pl.num_programs(1) - 1)
    def _():
        h = xin_ref[...] + 0.5 * acc_ref[...]
        if final:
            h = _rms(h) * fgain_ref[...]
        o_ref[...] = h


CAST_STEPS = 4


def _cast_body(x_ref, o_ref):
    o_ref[...] = x_ref[...].astype(BF16)


def _to_bf16(w, cols=None):
    rows = w.shape[0]
    cols = w.shape[1] if cols is None else cols
    tr = rows // CAST_STEPS
    assert tr * CAST_STEPS == rows and tr % BF16_SUBLANE == 0 and cols % LANE == 0
    return pl.pallas_call(
        _cast_body,
        out_shape=jax.ShapeDtypeStruct((rows, cols), BF16),
        grid=(CAST_STEPS,),
        in_specs=[pl.BlockSpec((tr, cols), lambda i: (i, 0))],
        out_specs=pl.BlockSpec((tr, cols), lambda i: (i, 0)),
        compiler_params=_params(("parallel",), 48),
        name="to_bf16",
    )(w)


def _ffn_weights(gain, wg, wu, wd, fgain):
    return gain.reshape(1, -1), _to_bf16(wg), _to_bf16(wu), _to_bf16(wd), fgain.reshape(1, -1)


def _ffn(x2, weights, final, shared_rows=None):
    layout = shared_rows is not None
    tm = LAY_TM if layout else FFN_TM
    rows = ROWS if layout else X_ROWS
    wspecs = [
        pl.BlockSpec((1, D_MODEL), lambda i, j: (0, 0)),
        pl.BlockSpec((D_MODEL, FFN_TF), lambda i, j: (0, j)),
        pl.BlockSpec((D_MODEL, FFN_TF), lambda i, j: (0, j)),
        pl.BlockSpec((FFN_TF, D_MODEL), lambda i, j: (j, 0)),
        pl.BlockSpec((1, D_MODEL), lambda i, j: (0, 0)),
    ]
    scratch = [pltpu.VMEM((tm, D_MODEL), BF16), pltpu.VMEM((tm, D_MODEL), F32)]
    if layout:
        x_spec = pl.BlockSpec((pl.Element(tm), pl.Element(D_MODEL)), lambda i, j: (_lay_src_row(i), 0))
        extra_specs = [pl.BlockSpec((X0, D_MODEL), lambda i, j: (0, 0))]
        extra = (shared_rows,)
        scratch.append(pltpu.VMEM((tm, D_MODEL), F32))
    else:
        x_spec = pl.BlockSpec((tm, D_MODEL), lambda i, j: (i, 0))
        extra_specs, extra = [], ()
    return pl.pallas_call(
        functools.partial(_ffn_body, final=final, layout=layout),
        out_shape=jax.ShapeDtypeStruct((rows, D_MODEL), F32),
        grid=(rows // tm, D_FF // FFN_TF),
        in_specs=[x_spec] + wspecs + extra_specs,
        out_specs=pl.BlockSpec((tm, D_MODEL), lambda i, j: (i, 0)),
        scratch_shapes=scratch,
        compiler_params=_params(("parallel", "arbitrary"), 52),
        name="ffn_final" if final else "ffn",
    )(x2, *weights, *extra)


INP_TM = 2 * LAY_TM
INP_TN = 1024
LR_PAD = 128


def _inproj_body(x_ref, gain_ref, w_ref, wlr_ref, w2_ref, b2_ref, p_ref, lg_ref, xn_ref):
    @pl.when(pl.program_id(1) == 0)
    def _():
        xn = (_rms(x_ref[...]) * gain_ref[...]).astype(BF16)
        xn_ref[...] = xn
        lr = jnp.dot(xn, wlr_ref[...], preferred_element_type=F32).astype(BF16)
        z = jnp.dot(lr, w2_ref[...], preferred_element_type=F32) + b2_ref[...]
        lg_ref[...] = (jnp.minimum(z, 0.0) - jnp.log1p(jnp.exp(-jnp.abs(z)))) * (1.0 / GATE_NORMALIZER)

    p_ref[...] = jnp.dot(xn_ref[...], w_ref[...], preferred_element_type=F32)


SPLIT_TC = 1024


def _split_w_in_body(w_ref, wlr_ref, main_ref, lr_ref):
    main_ref[...] = w_ref[...].T.astype(BF16)

    @pl.when(pl.program_id(0) == 0)
    def _():
        rows = jnp.concatenate([wlr_ref[...], jnp.zeros((LR_PAD - 2 * GATE_RANK, D_MODEL), F32)], axis=0)
        lr_ref[...] = rows.T.astype(BF16)


def _split_w_in(w_in):
    w_t = jnp.transpose(w_in)
    return pl.pallas_call(
        _split_w_in_body,
        out_shape=(jax.ShapeDtypeStruct((D_MODEL, P_MAIN), BF16), jax.ShapeDtypeStruct((D_MODEL, LR_PAD), BF16)),
        grid=(P_MAIN // SPLIT_TC,),
        in_specs=[pl.BlockSpec((SPLIT_TC, D_MODEL), lambda i: (i, 0)),
                  pl.BlockSpec((2 * GATE_RANK, D_MODEL), lambda i: (P_MAIN // (2 * GATE_RANK), 0))],
        out_specs=(pl.BlockSpec((D_MODEL, SPLIT_TC), lambda i: (0, i)),
                   pl.BlockSpec((D_MODEL, LR_PAD), lambda i: (0, 0))),
        compiler_params=_params(("arbitrary",), 48),
        name="split_w_in",
    )(w_t, w_t)


def _inproj_weights(gain, w_in, gk_w2, gk_b2):
    w_main, w_lr = _split_w_in(w_in)
    w2 = jnp.zeros((LR_PAD, 2 * GLA_KEY_WIDTH), F32)
    w2 = w2.at[:GATE_RANK, :GLA_KEY_WIDTH].set(gk_w2[0])
    w2 = w2.at[GATE_RANK:2 * GATE_RANK, GLA_KEY_WIDTH:].set(gk_w2[1]).astype(BF16)
    return gain.reshape(1, -1), w_main, w_lr, w2, gk_b2.reshape(1, 2 * GLA_KEY_WIDTH)


def _inproj(hbuf, weights):
    tm, tn = INP_TM, INP_TN
    return pl.pallas_call(
        _inproj_body,
        out_shape=(jax.ShapeDtypeStruct((ROWS, P_MAIN), F32),
                   jax.ShapeDtypeStruct((ROWS, 2 * GLA_KEY_WIDTH), F32)),
        grid=(ROWS // tm, P_MAIN // tn),
        in_specs=[
            pl.BlockSpec((tm, D_MODEL), lambda i, j: (i, 0)),
            pl.BlockSpec((1, D_MODEL), lambda i, j: (0, 0)),
            pl.BlockSpec((D_MODEL, tn), lambda i, j: (0, j)),
            pl.BlockSpec((D_MODEL, LR_PAD), lambda i, j: (0, 0)),
            pl.BlockSpec((LR_PAD, 2 * GLA_KEY_WIDTH), lambda i, j: (0, 0)),
            pl.BlockSpec((1, 2 * GLA_KEY_WIDTH), lambda i, j: (0, 0)),
        ],
        out_specs=(pl.BlockSpec((tm, tn), lambda i, j: (i, j)),
                   pl.BlockSpec((tm, 2 * GLA_KEY_WIDTH), lambda i, j: (i, 0))),
        scratch_shapes=[pltpu.VMEM((tm, D_MODEL), BF16)],
        compiler_params=_params(("parallel", "arbitrary"), 56),
        name="inproj",
    )(hbuf, *weights)


FILT_TR = 640
FEAT_PAD = 128


def _filt_tables():
    pos = np.arange(TFFT, dtype=np.float64)
    t = pos / (L_TOK - 1)
    w = (2.0 * np.pi / L_TOK) * pos
    bands = 1e-4 + np.arange(FILT_BANDS, dtype=np.float64) * ((FILT_BANDS - 1 - 1e-4) / (FILT_BANDS - 1))
    ang = w[:, None] * bands[None, :]
    feats = np.zeros((TFFT, FEAT_PAD), np.float64)
    feats[:, 0] = t
    feats[:, 1:1 + FILT_BANDS] = np.cos(ang)
    feats[:, 1 + FILT_BANDS:FILT_EMB] = -np.sin(ang)
    lo, hi = np.log(1e-2) / 1.5, np.log(1e-2) / 0.3
    deltas = np.abs(lo + np.arange(HY_WIDTH, dtype=np.float64) * ((hi - lo) / (HY_WIDTH - 1)))
    return feats.astype(np.float32), deltas.astype(np.float32).reshape(1, HY_WIDTH)


def _filt_body(feat_ref, w1_ref, b1_ref, w2_ref, b2_ref, w3_ref, fr_ref, dl_ref, h_ref):
    hp = lax.Precision.HIGHEST
    feats = feat_ref[...]
    fr = fr_ref[...]
    z = jnp.sin(fr * (jnp.dot(feats, w1_ref[...], precision=hp, preferred_element_type=F32) + b1_ref[...]))
    z = jnp.sin(fr * (jnp.dot(z, w2_ref[...], precision=hp, preferred_element_type=F32) + b2_ref[...]))
    hh = jnp.dot(z, w3_ref[...], precision=hp, preferred_element_type=F32)
    pos = pl.program_id(0) * FILT_TR + lax.broadcasted_iota(jnp.int32, (FILT_TR, 1), 0)
    win = jnp.exp(-feats[:, 0:1] * dl_ref[...])
    win = jnp.where(pos < L_TOK, win, 0.0)
    hf = hh[:, :HY_WIDTH] * win
    hb = hh[:, HY_WIDTH:] * win
    h_ref[0] = jnp.where(pos == 0, hf + hb, hf)
    h_ref[1] = jnp.where(pos == 0, 0.0, hb)


def _filters(w1, b1, w2, b2, w3, freq):
    feats, deltas = _filt_tables()
    w1p = jnp.pad(w1, ((0, FEAT_PAD - FILT_EMB), (0, 0)))
    full = lambda shape: pl.BlockSpec(shape, lambda i: (0, 0))
    return pl.pallas_call(
        _filt_body,
        out_shape=jax.ShapeDtypeStruct((2, TFFT, HY_WIDTH), F32),
        grid=(TFFT // FILT_TR,),
        in_specs=[
            pl.BlockSpec((FILT_TR, FEAT_PAD), lambda i: (i, 0)),
            full((FEAT_PAD, FILT_HIDDEN)), full((1, FILT_HIDDEN)),
            full((FILT_HIDDEN, FILT_HIDDEN)), full((1, FILT_HIDDEN)),
            full((FILT_HIDDEN, 2 * HY_WIDTH)), full((1, FILT_HIDDEN)), full((1, HY_WIDTH)),
        ],
        out_specs=pl.BlockSpec((2, FILT_TR, HY_WIDTH), lambda i: (0, i, 0)),
        compiler_params=_params(("parallel",), 40),
        name="filt",
    )(jnp.asarray(feats), w1p, b1.reshape(1, -1), w2, b2.reshape(1, -1), w3, freq.reshape(1, -1),
      jnp.asarray(deltas))


def _dft_tables_np():
    n1 = np.arange(FFT_N1)
    ang1 = 2.0 * np.pi * ((n1[:, None] * n1[None, :]) % FFT_N1) / FFT_N1
    c1, s1 = np.cos(ang1)[:, :FFT_R1], np.sin(ang1)[:, :FFT_R1]
    f1 = np.block([[c1, s1], [-s1, c1]])
    f3 = np.block([[c1.T, -s1.T], [s1.T, c1.T]])
    k1 = np.arange(FFT_N1)[:, None, None]
    k2 = np.arange(FFT_N2)[None, :, None]
    n2 = np.arange(FFT_N2)[None, None, :]
    ang2 = 2.0 * np.pi * ((n2 * (k1 + FFT_N1 * k2)) % NFFT) / NFFT
    c2, s2 = np.cos(ang2), np.sin(ang2)
    mf = np.concatenate([np.concatenate([c2, s2], axis=2), np.concatenate([-s2, c2], axis=2)], axis=1)
    c2t, s2t = np.swapaxes(c2, 1, 2), np.swapaxes(s2, 1, 2)
    mi = np.concatenate([np.concatenate([c2t, -s2t], axis=2), np.concatenate([s2t, c2t], axis=2)], axis=1)
    k1r = np.arange(FFT_N1)
    k2r = np.arange(FFT_N2)
    perm = np.where(k1r[:, None] == 0, (FFT_N2 - k2r[None, :]) % FFT_N2, FFT_N2 - 1 - k2r[None, :])
    rows = np.concatenate([perm, perm + FFT_N2], axis=1)
    mirror = mf[((FFT_N1 - k1r) % FFT_N1)[:, None], rows, :]
    return f1, f3, mf, mi, mirror


def _dft_tables():
    return tuple(jnp.asarray(a.astype(np.float32)).astype(BF16) for a in _dft_tables_np())


LMM_PITCH = FFT_N2 + SUBLANE
LMM_MC = 2 * FFT_N1


def _lmm_body(f_ref, x_ref, o_ref, xs_ref, os_ref, *, scale):
    m, k = f_ref.shape
    for g in range(k):
        xs_ref[pl.ds(g * LMM_PITCH, FFT_N2), :] = x_ref[pl.ds(g * FFT_N2, FFT_N2), :].astype(F32)
    for m0 in range(0, m, LMM_MC):
        f = f_ref[m0:min(m0 + LMM_MC, m), :]

        def slab(s, carry):
            x = xs_ref[pl.ds(s, k, stride=LMM_PITCH), :].astype(BF16)
            r = jnp.dot(f, x, preferred_element_type=F32)
            os_ref[pl.ds(s, f.shape[0], stride=LMM_PITCH), :] = r * scale if scale != 1.0 else r
            return carry

        lax.fori_loop(0, FFT_N2, slab, 0, unroll=8)
        for g in range(f.shape[0]):
            o_ref[pl.ds((m0 + g) * FFT_N2, FFT_N2), :] = os_ref[pl.ds(g * LMM_PITCH, FFT_N2), :].astype(
                o_ref.dtype)


def _lmm(f, x3, out_dtype, scale=1.0, name="lmm"):
    m, k = f.shape
    mc = min(m, LMM_MC)
    assert m % mc == 0
    out = pl.pallas_call(
        functools.partial(_lmm_body, scale=scale),
        out_shape=jax.ShapeDtypeStruct((m * FFT_N2, HY_WIDTH), out_dtype),
        grid=(HY_WIDTH // LANE,),
        in_specs=[pl.BlockSpec((m, k), lambda j: (0, 0)),
                  pl.BlockSpec((k * FFT_N2, LANE), lambda j: (0, j))],
        out_specs=pl.BlockSpec((m * FFT_N2, LANE), lambda j: (0, j)),
        scratch_shapes=[pltpu.VMEM((k * LMM_PITCH, LANE), F32), pltpu.VMEM((mc * LMM_PITCH, LANE), F32)],
        compiler_params=_params(("parallel",), 56),
        name=name,
    )(f, x3.reshape(k * FFT_N2, HY_WIDTH))
    return out.reshape(m, FFT_N2, HY_WIDTH)


def _gspec_body(mf_ref, mr_ref, a_ref, am_ref, g_ref):
    z = jnp.dot(mf_ref[...], a_ref[...].reshape(2 * FFT_N2, HY_WIDTH), preferred_element_type=F32)
    zm = jnp.dot(mr_ref[...], am_ref[...].reshape(2 * FFT_N2, HY_WIDTH), preferred_element_type=F32)
    a, b = z[:FFT_N2], z[FFT_N2:]
    am, bm = zm[:FFT_N2], zm[FFT_N2:]
    g = 0.5 * jnp.concatenate([a + am + b + bm, b - bm + a - am], axis=0)
    g_ref[...] = g.astype(BF16).reshape(2, 1, FFT_N2, HY_WIDTH)


def _gspec(mf, mirror, zs):
    blk = lambda at: pl.BlockSpec((2, 1, FFT_N2, HY_WIDTH), lambda i: (0, at(i), 0, 0))
    mat = pl.BlockSpec((None, 2 * FFT_N2, 2 * FFT_N2), lambda i: (i, 0, 0))
    return pl.pallas_call(
        _gspec_body,
        out_shape=jax.ShapeDtypeStruct((2, FFT_N1, FFT_N2, HY_WIDTH), BF16),
        grid=(FFT_N1,),
        in_specs=[mat, mat, blk(lambda i: i), blk(lambda i: (FFT_N1 - i) % FFT_N1)],
        out_specs=blk(lambda i: i),
        compiler_params=_params(("parallel",), 24),
        name="gspec",
    )(mf, mirror, zs, zs)


def _spec_body(mf_ref, mi_ref, a_ref, g_ref, o_ref):
    a = a_ref[...].reshape(2 * FFT_N2, HY_WIDTH)
    x = jnp.dot(mf_ref[...], a, preferred_element_type=F32)
    xr, xi = x[:FFT_N2], x[FFT_N2:]
    gr, gi = g_ref[0, 0].astype(F32), g_ref[1, 0].astype(F32)
    y = jnp.concatenate([xr * gr - xi * gi, xr * gi + xi * gr], axis=0).astype(BF16)
    o_ref[...] = jnp.dot(mi_ref[...], y, preferred_element_type=F32).astype(BF16).reshape(
        2, 1, FFT_N2, HY_WIDTH)


def _spec(mf, mi, a, g):
    blk = pl.BlockSpec((2, 1, FFT_N2, HY_WIDTH), lambda i: (0, i, 0, 0))
    mat = pl.BlockSpec((None, 2 * FFT_N2, 2 * FFT_N2), lambda i: (i, 0, 0))
    return pl.pallas_call(
        _spec_body,
        out_shape=jax.ShapeDtypeStruct((2, FFT_N1, FFT_N2, HY_WIDTH), BF16),
        grid=(FFT_N1,),
        in_specs=[mat, mat, blk, blk],
        out_specs=blk,
        compiler_params=_params(("parallel",), 24),
        name="spec",
    )(mf, mi, a, g)


HY_CB = 256


def _short_conv(p_ref, w_ref, b_ref):
    p = p_ref[...]
    w = w_ref[...]
    prev = pltpu.roll(p, 1, 0)
    nxt = pltpu.roll(p, TL - 1, 0)
    return b_ref[...] + prev * w[0:1] + p * w[1:2] + nxt * w[2:3]


def _uconv_body(x1_ref, vh_ref, w1_ref, wv_ref, b1_ref, bv_ref, u_ref):
    u = _short_conv(vh_ref, wv_ref, bv_ref) * _short_conv(x1_ref, w1_ref, b1_ref)
    row = lax.broadcasted_iota(jnp.int32, (TL, 1), 0)
    u_ref[pl.ds(0, TL), :] = jnp.where(row >= PAD, u, 0.0)
    u_ref[pl.ds(TL, TFFT - TL), :] = jnp.zeros((TFFT - TL, HY_CB), F32)


def _hy_specs(first_block):
    nb = HY_WIDTH // HY_CB
    return (pl.BlockSpec((None, TL, HY_CB), lambda b, j: (b, 0, first_block * nb + j)),
            pl.BlockSpec((3, HY_CB), lambda b, j: (0, first_block * nb + j)),
            pl.BlockSpec((1, HY_CB), lambda b, j: (0, first_block * nb + j)))


def _uconv(p3, conv_w, conv_b):
    x1, w1, b1 = _hy_specs(1)
    vh, wv, bv = _hy_specs(2)
    return pl.pallas_call(
        _uconv_body,
        out_shape=jax.ShapeDtypeStruct((BATCH, TFFT, HY_WIDTH), F32),
        grid=(BATCH, HY_WIDTH // HY_CB),
        in_specs=[x1, vh, w1, wv, b1, bv],
        out_specs=pl.BlockSpec((None, TFFT, HY_CB), lambda b, j: (b, 0, j)),
        compiler_params=_params(("parallel", "parallel"), 48),
        name="uconv",
    )(p3, p3, conv_w, conv_w, conv_b, conv_b)


def _ymix_body(y_ref, x0_ref, x1_ref, vh_ref, w0_ref, w1_ref, wv_ref, b0_ref, b1_ref, bv_ref,
               d_ref, gain_ref, o_ref):
    u = _short_conv(vh_ref, wv_ref, bv_ref) * _short_conv(x1_ref, w1_ref, b1_ref)
    yy = (y_ref[...] + d_ref[...] * u) * _short_conv(x0_ref, w0_ref, b0_ref)
    gain = gain_ref[...]
    for s in range(0, HY_CB, HY_GROUP):
        o_ref[:, s:s + HY_GROUP] = (_rms(yy[X0:, s:s + HY_GROUP]) * gain[:, s:s + HY_GROUP]).astype(BF16)


def _ymix(y3, p3, conv_w, conv_b, hyena_d, hyena_norm):
    x0, w0, b0 = _hy_specs(0)
    x1, w1, b1 = _hy_specs(1)
    vh, wv, bv = _hy_specs(2)
    vec = pl.BlockSpec((1, HY_CB), lambda b, j: (0, j))
    return pl.pallas_call(
        _ymix_body,
        out_shape=jax.ShapeDtypeStruct((BATCH, SEQ, HY_WIDTH), BF16),
        grid=(BATCH, HY_WIDTH // HY_CB),
        in_specs=[pl.BlockSpec((None, TL, HY_CB), lambda b, j: (b, 0, j)),
                  x0, x1, vh, w0, w1, wv, b0, b1, bv, vec, vec],
        out_specs=pl.BlockSpec((None, SEQ, HY_CB), lambda b, j: (b, 0, j)),
        compiler_params=_params(("parallel", "parallel"), 56),
        name="ymix",
    )(y3, p3, p3, p3, conv_w, conv_w, conv_w, conv_b, conv_b, conv_b,
      hyena_d.reshape(1, -1), hyena_norm.reshape(1, -1))


N_SUB = CHUNK // SUB
SAFE_BLOCK_DECAY = -60.0


def _scores_exact(q, k, b, rev, ones, row, col):
    rsub = row % SUB
    terms = []
    for d in range(SUB):
        if d == 0:
            kr, br = k, b
        else:
            sh = CHUNK - d if rev else d
            kr, br = pltpu.roll(k, sh, 0), pltpu.roll(b, sh, 0)
        valid = (rsub + d < SUB) if rev else (rsub >= d)
        t = q * kr * jnp.exp(jnp.minimum(b - br, 0.0))
        terms.append(jnp.where(valid, t, 0.0).astype(BF16))
    sums = jnp.dot(jnp.concatenate(terms, axis=0), ones, preferred_element_type=F32)
    a = jnp.zeros((CHUNK, CHUNK), F32)
    for d in range(SUB):
        tgt = row + d if rev else row - d
        a = jnp.where(col == tgt, sums[d * CHUNK:(d + 1) * CHUNK, :CHUNK], a)

    rblk = row // SUB
    cblk = col // SUB
    for jb in (range(1, N_SUB) if rev else range(N_SUB - 1)):
        e = jb * SUB if rev else jb * SUB + SUB - 1
        ref = b[e:e + 1, :]
        qh = (q * jnp.exp(jnp.minimum(b - ref, 0.0))).astype(BF16)
        kh = (k * jnp.exp(jnp.minimum(ref - b, 0.0))).astype(BF16)
        pm = lax.dot_general(qh, kh, (((1,), (1,)), ((), ())), preferred_element_type=F32)
        side = jnp.where(cblk == jb, rblk, jb)
        a = jnp.where((side < jb) if rev else (side > jb), pm, a)

    bend = b[0:1, :] if rev else b[CHUNK - 1:CHUNK, :]
    return a, q * jnp.exp(b), k * jnp.exp(bend - b), bend


def _scores_fast(q, k, b, rev, row, col):
    order = list(range(N_SUB))[::-1] if rev else list(range(N_SUB))
    pos = {blk: p for p, blk in enumerate(order)}
    edge = lambda blk: blk * SUB if rev else blk * SUB + SUB - 1
    e = [b[edge(blk):edge(blk) + 1, :] for blk in order]
    s = [jnp.zeros((1, GLA_DK), F32)] + e[:-1]

    def by_row(vals):
        return jnp.concatenate([jnp.broadcast_to(vals[pos[blk]], (SUB, GLA_DK)) for blk in range(N_SUB)],
                               axis=0)

    srow, erow = by_row(s), by_row(e)
    qh = q * jnp.exp(b - srow)
    kh = k * jnp.exp(erow - b)
    kd = k * jnp.exp(srow - b)

    lhs = []
    for pj in range(N_SUB - 1):
        for blk in range(N_SUB):
            piece = qh[blk * SUB:(blk + 1) * SUB, :]
            p = pos[blk]
            if p <= pj:
                piece = jnp.zeros_like(piece)
            elif p > pj + 1:
                piece = piece * jnp.exp(s[p] - e[pj])
            lhs.append(piece.astype(BF16))
    contract = (((1,), (1,)), ((), ()))
    cross = lax.dot_general(jnp.concatenate(lhs, axis=0), kh.astype(BF16), contract,
                            preferred_element_type=F32)
    diag = lax.dot_general(qh.astype(BF16), kd.astype(BF16), contract, preferred_element_type=F32)

    rblk = row // SUB
    cblk = col // SUB
    a = jnp.zeros((CHUNK, CHUNK), F32)
    for pj in range(N_SUB - 1):
        a = jnp.where(cblk == order[pj], cross[pj * CHUNK:(pj + 1) * CHUNK], a)
    causal = (col >= row) if rev else (col <= row)
    a = jnp.where(cblk == rblk, jnp.where(causal, diag, 0.0), a)

    bend = e[-1]
    return a, qh * jnp.exp(srow), kh * jnp.exp(bend - erow), bend


GLA_RB = 384


def _gla_sweep_body(*refs, rev):
    if rev:
        q_ref, k_ref, v_ref, g_ref, o_ref, st_ref = refs
    else:
        q_ref, k_ref, v_ref, g_ref, og_ref, ob_ref, gain_ref, o_ref, st_ref = refs

    @pl.when(pl.program_id(1) == 0)
    def _():
        st_ref[...] = jnp.zeros_like(st_ref)

    row = lax.broadcasted_iota(jnp.int32, (CHUNK, 1), 0)
    col = lax.broadcasted_iota(jnp.int32, (CHUNK, CHUNK), 1)
    rr = lax.broadcasted_iota(jnp.int32, (CHUNK, CHUNK), 0)
    tri = ((col >= rr) if rev else (col <= rr)).astype(F32)
    ones = jnp.ones((GLA_DK, GLA_DK), BF16)
    n_chunk = GLA_RB // CHUNK

    def run(fast):
        def step(t, carry):
            c = n_chunk - 1 - t if rev else t
            rows = pl.ds(pl.multiple_of(c * CHUNK, CHUNK), CHUNK)
            b_all = jnp.dot(tri, g_ref[rows, :], precision=lax.Precision.HIGHEST, preferred_element_type=F32)
            pending = []
            for h in range(GLA_HEADS):
                kc = slice(h * GLA_DK, (h + 1) * GLA_DK)
                vc = slice(h * GLA_DV, (h + 1) * GLA_DV)
                q = q_ref[rows, kc] * (GLA_DK ** -0.5)
                k = k_ref[rows, kc]
                v = v_ref[rows, vc].astype(BF16)
                b = b_all[:, kc]
                if fast:
                    a, qt, kt, bend = _scores_fast(q, k, b, rev, row, col)
                else:
                    a, qt, kt, bend = _scores_exact(q, k, b, rev, ones, row, col)
                st = st_ref[h]
                o = lax.dot_general(qt.astype(BF16), st.astype(BF16), (((1,), (1,)), ((), ())),
                                    preferred_element_type=F32)
                st_ref[h] = st * jnp.exp(bend) + lax.dot_general(
                    v, kt.astype(BF16), (((0,), (0,)), ((), ())), preferred_element_type=F32)
                pending.append((vc, o, a, v))
            for vc, o, a, v in pending:
                o = o + jnp.dot(a.astype(BF16), v, preferred_element_type=F32)
                if rev:
                    o_ref[rows, vc] = o
                else:
                    o = _rms(o + ob_ref[rows, vc]) * gain_ref[...]
                    o_ref[rows, vc] = (o * _silu(og_ref[rows, vc])).astype(BF16)
            return carry

        lax.fori_loop(0, n_chunk, step, 0)

    low = jnp.min(jnp.sum(g_ref[...].reshape(GLA_RB // SUB, SUB, GLA_KEY_WIDTH), axis=1))
    safe = low > SAFE_BLOCK_DECAY
    pl.when(safe)(lambda: run(True))
    pl.when(jnp.logical_not(safe))(lambda: run(False))


def _gla_sweep(p3, lg3, rev, ob=None, gla_norm=None):
    nb = TL // GLA_RB
    blk = (lambda i: nb - 1 - i) if rev else (lambda i: i)
    key_blocks = P_MAIN // GLA_KEY_WIDTH
    q_col = 3 * HY_WIDTH // GLA_KEY_WIDTH
    v_col = (3 * HY_WIDTH + 2 * GLA_KEY_WIDTH) // GLA_WIDTH
    assert key_blocks * GLA_KEY_WIDTH == P_MAIN
    narrow = lambda col: pl.BlockSpec((None, GLA_RB, GLA_KEY_WIDTH), lambda b, i: (b, blk(i), col))
    wide = lambda col: pl.BlockSpec((None, GLA_RB, GLA_WIDTH), lambda b, i: (b, blk(i), col))
    in_specs = [narrow(q_col), narrow(q_col + 1), wide(v_col), narrow(1 if rev else 0)]
    args = [p3, p3, p3, lg3]
    if not rev:
        in_specs += [wide(v_col + 1), wide(0), pl.BlockSpec((1, GLA_DV), lambda b, i: (0, 0))]
        args += [p3, ob, gla_norm.reshape(1, -1)]
    return pl.pallas_call(
        functools.partial(_gla_sweep_body, rev=rev),
        out_shape=jax.ShapeDtypeStruct((BATCH, TL, GLA_WIDTH), F32 if rev else BF16),
        grid=(BATCH, nb),
        in_specs=in_specs,
        out_specs=wide(0),
        scratch_shapes=[pltpu.VMEM((GLA_HEADS, GLA_DV, GLA_DK), F32)],
        compiler_params=_params(("parallel", "arbitrary"), 32),
        name="gla_down" if rev else "gla_up",
    )(*args)


OUT_TM = 512


def _outproj_body(h_ref, yh_ref, yg_ref, wh_ref, wg_ref, o_ref):
    o_ref[...] = (h_ref[...] + jnp.dot(yh_ref[...], wh_ref[...], preferred_element_type=F32)
                  + jnp.dot(yg_ref[...], wg_ref[...], preferred_element_type=F32))


def _outproj(hbuf, yh, ygbuf, w_out):
    wo = _to_bf16(w_out)
    half = lambda i: pl.BlockSpec((HY_WIDTH, D_MODEL), lambda r: (i, 0))
    return pl.pallas_call(
        _outproj_body,
        out_shape=jax.ShapeDtypeStruct((X_ROWS, D_MODEL), F32),
        grid=(X_ROWS // OUT_TM,),
        in_specs=[pl.BlockSpec((pl.Element(OUT_TM), pl.Element(D_MODEL)), lambda r: (_x_row(r, OUT_TM), 0)),
                  pl.BlockSpec((OUT_TM, HY_WIDTH), lambda r: (r, 0)),
                  pl.BlockSpec((pl.Element(OUT_TM), pl.Element(GLA_WIDTH)),
                               lambda r: (_x_row(r, OUT_TM, BF16_SUBLANE), 0)),
                  half(0), half(1)],
        out_specs=pl.BlockSpec((OUT_TM, D_MODEL), lambda r: (r, 0)),
        compiler_params=_params(("parallel",), 48),
        name="outproj",
    )(hbuf, yh, ygbuf, wo, wo)


def _long_conv(u3, filt):
    f1, f3, mf, mi, mirror = _dft_tables()
    as_n1 = lambda a: a.reshape(2 * FFT_R1, FFT_N2, HY_WIDTH)
    zs = _lmm(f1, as_n1(filt), BF16, name="lmm_g")
    gs = _gspec(mf, mirror, zs.reshape(2, FFT_N1, FFT_N2, HY_WIDTH))
    a = _lmm(f1, as_n1(u3), BF16, name="lmm_fwd")
    bm = _spec(mf, mi, a.reshape(2, FFT_N1, FFT_N2, HY_WIDTH), gs)
    y = _lmm(f3, bm.reshape(2 * FFT_N1, FFT_N2, HY_WIDTH), F32, scale=1.0 / NFFT, name="lmm_inv")
    return y.reshape(BATCH, TFFT, HY_WIDTH)


def kernel(x, meta_tokens, ffn1_norm, ffn1_w_gate, ffn1_w_up, ffn1_w_down, mix_norm, w_in, conv_w, conv_b,
           filt_w1, filt_b1, filt_w2, filt_b2, filt_w3, filt_freq, hyena_d, hyena_norm, gk_w2, gk_b2,
           gla_norm, w_out, ffn2_norm, ffn2_w_gate, ffn2_w_up, ffn2_w_down, final_norm):
    assert x.shape == (BATCH, SEQ, D_MODEL) and ffn1_norm.shape[0] == 1

    w1 = _ffn_weights(ffn1_norm[0], ffn1_w_gate[0], ffn1_w_up[0], ffn1_w_down[0], final_norm)
    hbuf = _ffn(x.reshape(X_ROWS, D_MODEL), w1, final=False, shared_rows=_meta_rows(meta_tokens))

    p, lg = _inproj(hbuf, _inproj_weights(mix_norm[0], w_in[0], gk_w2[0], gk_b2[0]))
    p3 = p.reshape(BATCH, TL, P_MAIN)
    lg3 = lg.reshape(BATCH, TL, 2 * GLA_KEY_WIDTH)

    filt = _filters(filt_w1[0], filt_b1[0], filt_w2[0], filt_b2[0], filt_w3[0], filt_freq[0])
    cw, cb = conv_w[0], conv_b[0].reshape(1, -1)
    y3 = _long_conv(_uconv(p3, cw, cb), filt)
    yh = _ymix(y3, p3, cw, cb, hyena_d[0], hyena_norm[0])
    yg = _gla_sweep(p3, lg3, rev=False, ob=_gla_sweep(p3, lg3, rev=True), gla_norm=gla_norm[0])

    h2 = _outproj(hbuf, yh.reshape(X_ROWS, HY_WIDTH), yg.reshape(ROWS, GLA_WIDTH), w_out[0])
    w2 = _ffn_weights(ffn2_norm[0], ffn2_w_gate[0], ffn2_w_up[0], ffn2_w_down[0], final_norm)
    return _ffn(h2, w2, final=True).reshape(BATCH, SEQ, D_MODEL)
```

```python
import functools

import numpy as np
import jax
import jax.numpy as jnp
from jax import lax
from jax.experimental import pallas as pl
from jax.experimental.pallas import tpu as pltpu

F32 = jnp.float32
BF16 = jnp.bfloat16

D_MODEL = 2048
BATCH = 2
SEQ = 4096
N_META = 16
L_TOK = SEQ + N_META
PAD = 112
X0 = PAD + N_META
TL = PAD + L_TOK
ROWS = BATCH * TL
X_ROWS = BATCH * SEQ
HY_WIDTH = 1024
HY_GROUPS = 8
HY_GROUP = HY_WIDTH // HY_GROUPS
FILT_EMB = 33
FILT_BANDS = 16
FILT_HIDDEN = 64
GLA_WIDTH = 1024
GLA_HEADS = 4
GLA_KEY_WIDTH = 512
GLA_DK = 128
GLA_DV = 256
GATE_RANK = 16
GATE_NORMALIZER = 16.0
CHUNK = 64
SUB = 16
D_FF = 5632
P_MAIN = 3 * HY_WIDTH + 2 * GLA_KEY_WIDTH + 2 * GLA_WIDTH
EPS = 1e-6

FFT_N1 = 72
FFT_N2 = 128
NFFT = FFT_N1 * FFT_N2
FFT_R1 = 40
TFFT = FFT_R1 * FFT_N2

MIB = 1024 * 1024
SUBLANE = 8
BF16_SUBLANE = 16
LANE = 128


def _params(sem, vmem_mib):
    return pltpu.CompilerParams(dimension_semantics=sem, vmem_limit_bytes=vmem_mib * MIB)


def _rms(x):
    return x * lax.rsqrt(jnp.mean(x * x, axis=-1, keepdims=True) + EPS)


def _silu(x):
    return x * jax.nn.sigmoid(x)


LAY_TM = 528


def _x_row(i, tm, unit=SUBLANE):
    per_batch = SEQ // tm
    r = (i // per_batch) * (TL // unit) + X0 // unit + (i % per_batch) * (tm // unit)
    return pl.multiple_of(r * unit, unit)


def _lay_src_row(i):
    per_batch = TL // LAY_TM
    r = (i // per_batch) * (SEQ // SUBLANE) + jnp.maximum(
        (i % per_batch) * (LAY_TM // SUBLANE) - X0 // SUBLANE, 0)
    return pl.multiple_of(r * SUBLANE, SUBLANE)


def _meta_rows(meta_tokens):
    return jnp.concatenate([jnp.zeros((PAD, D_MODEL), F32), meta_tokens.astype(F32)], axis=0)


FFN_TM = 512
FFN_TF = 512


FFN_HEAD_TF = 256


def _ffn_begin(x_ref, shared_ref, xin_ref, gain_ref, xn_ref, acc_ref, first):
    if shared_ref is not None:
        @pl.when(first)
        def _():
            xin_ref[0:X0, :] = shared_ref[...]
            xin_ref[X0:LAY_TM, :] = x_ref[0:LAY_TM - X0, :]

        @pl.when(jnp.logical_not(first))
        def _():
            xin_ref[...] = x_ref[...]

    xn_ref[...] = (_rms(xin_ref[...]) * gain_ref[...]).astype(BF16)
    acc_ref[...] = jnp.zeros_like(acc_ref)


def _ffn_step(xn_ref, wg, wu, wd, acc_ref):
    xn = xn_ref[...]
    g = jnp.dot(xn, wg, preferred_element_type=F32)
    u = jnp.dot(xn, wu, preferred_element_type=F32)
    a = (_silu(g) * u).astype(BF16)
    acc_ref[...] += jnp.dot(a, wd, preferred_element_type=F32)


def _ffn_result(xin_ref, acc_ref, fgain_ref, final):
    h = xin_ref[...] + 0.5 * acc_ref[...]
    return _rms(h) * fgain_ref[...] if final else h


def _ffn_head_body(x_ref, gain_ref, wg_ref, wu_ref, wd_ref, fgain_ref, *rest, final, layout):
    if layout:
        shared_ref, o_ref, wg16_ref, wu16_ref, wd16_ref, xn_ref, acc_ref, xin_ref = rest
    else:
        o_ref, wg16_ref, wu16_ref, wd16_ref, xn_ref, acc_ref = rest
        shared_ref, xin_ref = None, x_ref
    j = pl.program_id(0)

    @pl.when(j == 0)
    def _():
        _ffn_begin(x_ref, shared_ref, xin_ref, gain_ref, xn_ref, acc_ref, True)

    wg, wu, wd = wg_ref[...].astype(BF16), wu_ref[...].astype(BF16), wd_ref[...].astype(BF16)
    wg16_ref[...] = wg
    wu16_ref[...] = wu
    wd16_ref[...] = wd
    _ffn_step(xn_ref, wg, wu, wd, acc_ref)

    @pl.when(j == pl.num_programs(0) - 1)
    def _():
        o_ref[...] = _ffn_result(xin_ref, acc_ref, fgain_ref, final)


def _ffn_body(x_ref, gain_ref, wg_ref, wu_ref, wd_ref, fgain_ref, head_ref, *rest, final, layout):
    if layout:
        shared_ref, o_ref, xn_ref, acc_ref, xin_ref = rest
    else:
        o_ref, xn_ref, acc_ref = rest
        shared_ref, xin_ref = None, x_ref
    i, j = pl.program_id(0), pl.program_id(1)
    last = j == pl.num_programs(1) - 1

    @pl.when(i > 0)
    def _():
        @pl.when(j == 0)
        def _():
            _ffn_begin(x_ref, shared_ref, xin_ref, gain_ref, xn_ref, acc_ref, i % (TL // LAY_TM) == 0)

        _ffn_step(xn_ref, wg_ref[...], wu_ref[...], wd_ref[...], acc_ref)

        @pl.when(last)
        def _():
            o_ref[...] = _ffn_result(xin_ref, acc_ref, fgain_ref, final)

    @pl.when(jnp.logical_and(i == 0, last))
    def _():
        o_ref[...] = head_ref[...]


CAST_STEPS = 4


def _cast_body(x_ref, o_ref):
    o_ref[...] = x_ref[...].astype(BF16)


def _to_bf16(w, cols=None):
    rows = w.shape[0]
    cols = w.shape[1] if cols is None else cols
    tr = rows // CAST_STEPS
    assert tr * CAST_STEPS == rows and tr % BF16_SUBLANE == 0 and cols % LANE == 0
    return pl.pallas_call(
        _cast_body,
        out_shape=jax.ShapeDtypeStruct((rows, cols), BF16),
        grid=(CAST_STEPS,),
        in_specs=[pl.BlockSpec((tr, cols), lambda i: (i, 0))],
        out_specs=pl.BlockSpec((tr, cols), lambda i: (i, 0)),
        compiler_params=_params(("parallel",), 48),
        name="to_bf16",
    )(w)


def _ffn(x2, gain, wg, wu, wd, fgain, final, shared_rows=None):
    layout = shared_rows is not None
    tm = LAY_TM if layout else FFN_TM
    rows = ROWS if layout else X_ROWS
    gain, fgain = gain.reshape(1, -1), fgain.reshape(1, -1)
    scratch = [pltpu.VMEM((tm, D_MODEL), BF16), pltpu.VMEM((tm, D_MODEL), F32)]
    extra = ()
    if layout:
        extra = (shared_rows,)
        scratch.append(pltpu.VMEM((tm, D_MODEL), F32))
    name = "ffn_final" if final else "ffn"

    tf = FFN_HEAD_TF
    vec1 = pl.BlockSpec((1, D_MODEL), lambda j: (0, 0))
    wide1 = pl.BlockSpec((D_MODEL, tf), lambda j: (0, j))
    tall1 = pl.BlockSpec((tf, D_MODEL), lambda j: (j, 0))
    tile1 = pl.BlockSpec((tm, D_MODEL), lambda j: (0, 0))
    head, wg16, wu16, wd16 = pl.pallas_call(
        functools.partial(_ffn_head_body, final=final, layout=layout),
        out_shape=(jax.ShapeDtypeStruct((tm, D_MODEL), F32), jax.ShapeDtypeStruct(wg.shape, BF16),
                   jax.ShapeDtypeStruct(wu.shape, BF16), jax.ShapeDtypeStruct(wd.shape, BF16)),
        grid=(D_FF // tf,),
        in_specs=[tile1, vec1, wide1, wide1, tall1, vec1] + [pl.BlockSpec((X0, D_MODEL), lambda j: (0, 0))] * layout,
        out_specs=(tile1, wide1, wide1, tall1),
        scratch_shapes=scratch,
        compiler_params=_params(("arbitrary",), 56),
        name=name + "_head",
    )(x2, gain, wg, wu, wd, fgain, *extra)

    col = lambda i, j: jnp.where(i == 0, 0, j)
    vec = pl.BlockSpec((1, D_MODEL), lambda i, j: (0, 0))
    wide = pl.BlockSpec((D_MODEL, FFN_TF), lambda i, j: (0, col(i, j)))
    tall = pl.BlockSpec((FFN_TF, D_MODEL), lambda i, j: (col(i, j), 0))
    if layout:
        x_spec = pl.BlockSpec((pl.Element(tm), pl.Element(D_MODEL)), lambda i, j: (_lay_src_row(i), 0))
    else:
        x_spec = pl.BlockSpec((tm, D_MODEL), lambda i, j: (i, 0))
    return pl.pallas_call(
        functools.partial(_ffn_body, final=final, layout=layout),
        out_shape=jax.ShapeDtypeStruct((rows, D_MODEL), F32),
        grid=(rows // tm, D_FF // FFN_TF),
        in_specs=[x_spec, vec, wide, wide, tall, vec, pl.BlockSpec((tm, D_MODEL), lambda i, j: (0, 0))]
        + [pl.BlockSpec((X0, D_MODEL), lambda i, j: (0, 0))] * layout,
        out_specs=pl.BlockSpec((tm, D_MODEL), lambda i, j: (i, 0)),
        scratch_shapes=scratch,
        compiler_params=_params(("parallel", "arbitrary"), 56),
        name=name,
    )(x2, gain, wg16, wu16, wd16, fgain, head, *extra)


INP_TM = 2 * LAY_TM
INP_TN = 1024
LR_PAD = 128


def _inproj_body(x_ref, gain_ref, w_ref, wlr_ref, w2_ref, b2_ref, p_ref, lg_ref, xn_ref):
    @pl.when(pl.program_id(1) == 0)
    def _():
        xn = (_rms(x_ref[...]) * gain_ref[...]).astype(BF16)
        xn_ref[...] = xn
        lr = jnp.dot(xn, wlr_ref[...], preferred_element_type=F32).astype(BF16)
        z = jnp.dot(lr, w2_ref[...], preferred_element_type=F32) + b2_ref[...]
        lg_ref[...] = (jnp.minimum(z, 0.0) - jnp.log1p(jnp.exp(-jnp.abs(z)))) * (1.0 / GATE_NORMALIZER)

    p_ref[...] = jnp.dot(xn_ref[...], w_ref[...], preferred_element_type=F32)


SPLIT_TC = 1024


def _split_w_in_body(w_ref, wlr_ref, main_ref, lr_ref):
    main_ref[...] = w_ref[...].T.astype(BF16)

    @pl.when(pl.program_id(0) == 0)
    def _():
        rows = jnp.concatenate([wlr_ref[...], jnp.zeros((LR_PAD - 2 * GATE_RANK, D_MODEL), F32)], axis=0)
        lr_ref[...] = rows.T.astype(BF16)


def _split_w_in(w_in):
    w_t = jnp.transpose(w_in)
    return pl.pallas_call(
        _split_w_in_body,
        out_shape=(jax.ShapeDtypeStruct((D_MODEL, P_MAIN), BF16), jax.ShapeDtypeStruct((D_MODEL, LR_PAD), BF16)),
        grid=(P_MAIN // SPLIT_TC,),
        in_specs=[pl.BlockSpec((SPLIT_TC, D_MODEL), lambda i: (i, 0)),
                  pl.BlockSpec((2 * GATE_RANK, D_MODEL), lambda i: (P_MAIN // (2 * GATE_RANK), 0))],
        out_specs=(pl.BlockSpec((D_MODEL, SPLIT_TC), lambda i: (0, i)),
                   pl.BlockSpec((D_MODEL, LR_PAD), lambda i: (0, 0))),
        compiler_params=_params(("arbitrary",), 48),
        name="split_w_in",
    )(w_t, w_t)


def _inproj_weights(gain, w_in, gk_w2, gk_b2):
    w_main, w_lr = _split_w_in(w_in)
    w2 = jnp.zeros((LR_PAD, 2 * GLA_KEY_WIDTH), F32)
    w2 = w2.at[:GATE_RANK, :GLA_KEY_WIDTH].set(gk_w2[0])
    w2 = w2.at[GATE_RANK:2 * GATE_RANK, GLA_KEY_WIDTH:].set(gk_w2[1]).astype(BF16)
    return gain.reshape(1, -1), w_main, w_lr, w2, gk_b2.reshape(1, 2 * GLA_KEY_WIDTH)


def _inproj(hbuf, weights):
    tm, tn = INP_TM, INP_TN
    return pl.pallas_call(
        _inproj_body,
        out_shape=(jax.ShapeDtypeStruct((ROWS, P_MAIN), F32),
                   jax.ShapeDtypeStruct((ROWS, 2 * GLA_KEY_WIDTH), F32)),
        grid=(ROWS // tm, P_MAIN // tn),
        in_specs=[
            pl.BlockSpec((tm, D_MODEL), lambda i, j: (i, 0)),
            pl.BlockSpec((1, D_MODEL), lambda i, j: (0, 0)),
            pl.BlockSpec((D_MODEL, tn), lambda i, j: (0, j)),
            pl.BlockSpec((D_MODEL, LR_PAD), lambda i, j: (0, 0)),
            pl.BlockSpec((LR_PAD, 2 * GLA_KEY_WIDTH), lambda i, j: (0, 0)),
            pl.BlockSpec((1, 2 * GLA_KEY_WIDTH), lambda i, j: (0, 0)),
        ],
        out_specs=(pl.BlockSpec((tm, tn), lambda i, j: (i, j)),
                   pl.BlockSpec((tm, 2 * GLA_KEY_WIDTH), lambda i, j: (i, 0))),
        scratch_shapes=[pltpu.VMEM((tm, D_MODEL), BF16)],
        compiler_params=_params(("parallel", "arbitrary"), 56),
        name="inproj",
    )(hbuf, *weights)


FILT_TR = 640
FEAT_PAD = 128


def _filt_tables():
    pos = np.arange(TFFT, dtype=np.float64)
    t = pos / (L_TOK - 1)
    w = (2.0 * np.pi / L_TOK) * pos
    bands = 1e-4 + np.arange(FILT_BANDS, dtype=np.float64) * ((FILT_BANDS - 1 - 1e-4) / (FILT_BANDS - 1))
    ang = w[:, None] * bands[None, :]
    feats = np.zeros((TFFT, FEAT_PAD), np.float64)
    feats[:, 0] = t
    feats[:, 1:1 + FILT_BANDS] = np.cos(ang)
    feats[:, 1 + FILT_BANDS:FILT_EMB] = -np.sin(ang)
    lo, hi = np.log(1e-2) / 1.5, np.log(1e-2) / 0.3
    deltas = np.abs(lo + np.arange(HY_WIDTH, dtype=np.float64) * ((hi - lo) / (HY_WIDTH - 1)))
    return feats.astype(np.float32), deltas.astype(np.float32).reshape(1, HY_WIDTH)


def _filt_body(feat_ref, w1_ref, b1_ref, w2_ref, b2_ref, w3_ref, fr_ref, dl_ref, h_ref):
    hp = lax.Precision.HIGHEST
    feats = feat_ref[...]
    fr = fr_ref[...]
    z = jnp.sin(fr * (jnp.dot(feats, w1_ref[...], precision=hp, preferred_element_type=F32) + b1_ref[...]))
    z = jnp.sin(fr * (jnp.dot(z, w2_ref[...], precision=hp, preferred_element_type=F32) + b2_ref[...]))
    hh = jnp.dot(z, w3_ref[...], precision=hp, preferred_element_type=F32)
    pos = pl.program_id(0) * FILT_TR + lax.broadcasted_iota(jnp.int32, (FILT_TR, 1), 0)
    win = jnp.exp(-feats[:, 0:1] * dl_ref[...])
    win = jnp.where(pos < L_TOK, win, 0.0)
    hf = hh[:, :HY_WIDTH] * win
    hb = hh[:, HY_WIDTH:] * win
    h_ref[0] = jnp.where(pos == 0, hf + hb, hf)
    h_ref[1] = jnp.where(pos == 0, 0.0, hb)


def _filters(w1, b1, w2, b2, w3, freq):
    feats, deltas = _filt_tables()
    w1p = jnp.pad(w1, ((0, FEAT_PAD - FILT_EMB), (0, 0)))
    full = lambda shape: pl.BlockSpec(shape, lambda i: (0, 0))
    return pl.pallas_call(
        _filt_body,
        out_shape=jax.ShapeDtypeStruct((2, TFFT, HY_WIDTH), F32),
        grid=(TFFT // FILT_TR,),
        in_specs=[
            pl.BlockSpec((FILT_TR, FEAT_PAD), lambda i: (i, 0)),
            full((FEAT_PAD, FILT_HIDDEN)), full((1, FILT_HIDDEN)),
            full((FILT_HIDDEN, FILT_HIDDEN)), full((1, FILT_HIDDEN)),
            full((FILT_HIDDEN, 2 * HY_WIDTH)), full((1, FILT_HIDDEN)), full((1, HY_WIDTH)),
        ],
        out_specs=pl.BlockSpec((2, FILT_TR, HY_WIDTH), lambda i: (0, i, 0)),
        compiler_params=_params(("parallel",), 40),
        name="filt",
    )(jnp.asarray(feats), w1p, b1.reshape(1, -1), w2, b2.reshape(1, -1), w3, freq.reshape(1, -1),
      jnp.asarray(deltas))


def _dft_tables_np():
    n1 = np.arange(FFT_N1)
    ang1 = 2.0 * np.pi * ((n1[:, None] * n1[None, :]) % FFT_N1) / FFT_N1
    c1, s1 = np.cos(ang1)[:, :FFT_R1], np.sin(ang1)[:, :FFT_R1]
    f1 = np.block([[c1, s1], [-s1, c1]])
    f3 = np.block([[c1.T, -s1.T], [s1.T, c1.T]])
    k1 = np.arange(FFT_N1)[:, None, None]
    k2 = np.arange(FFT_N2)[None, :, None]
    n2 = np.arange(FFT_N2)[None, None, :]
    ang2 = 2.0 * np.pi * ((n2 * (k1 + FFT_N1 * k2)) % NFFT) / NFFT
    c2, s2 = np.cos(ang2), np.sin(ang2)
    mf = np.concatenate([np.concatenate([c2, s2], axis=2), np.concatenate([-s2, c2], axis=2)], axis=1)
    c2t, s2t = np.swapaxes(c2, 1, 2), np.swapaxes(s2, 1, 2)
    mi = np.concatenate([np.concatenate([c2t, -s2t], axis=2), np.concatenate([s2t, c2t], axis=2)], axis=1)
    k1r = np.arange(FFT_N1)
    k2r = np.arange(FFT_N2)
    perm = np.where(k1r[:, None] == 0, (FFT_N2 - k2r[None, :]) % FFT_N2, FFT_N2 - 1 - k2r[None, :])
    rows = np.concatenate([perm, perm + FFT_N2], axis=1)
    mirror = mf[((FFT_N1 - k1r) % FFT_N1)[:, None], rows, :]
    return f1, f3, mf, mi, mirror


def _dft_tables():
    return tuple(jnp.asarray(a.astype(np.float32)).astype(BF16) for a in _dft_tables_np())


LMM_PITCH = FFT_N2 + SUBLANE
LMM_MC = 2 * FFT_N1


def _lmm_body(f_ref, x_ref, o_ref, xs_ref, os_ref, *, scale):
    m, k = f_ref.shape
    for g in range(k):
        xs_ref[pl.ds(g * LMM_PITCH, FFT_N2), :] = x_ref[pl.ds(g * FFT_N2, FFT_N2), :].astype(F32)
    for m0 in range(0, m, LMM_MC):
        f = f_ref[m0:min(m0 + LMM_MC, m), :]

        def slab(s, carry):
            x = xs_ref[pl.ds(s, k, stride=LMM_PITCH), :].astype(BF16)
            r = jnp.dot(f, x, preferred_element_type=F32)
            os_ref[pl.ds(s, f.shape[0], stride=LMM_PITCH), :] = r * scale if scale != 1.0 else r
            return carry

        lax.fori_loop(0, FFT_N2, slab, 0, unroll=8)
        for g in range(f.shape[0]):
            o_ref[pl.ds((m0 + g) * FFT_N2, FFT_N2), :] = os_ref[pl.ds(g * LMM_PITCH, FFT_N2), :].astype(
                o_ref.dtype)


def _lmm(f, x3, out_dtype, scale=1.0, name="lmm"):
    m, k = f.shape
    mc = min(m, LMM_MC)
    assert m % mc == 0
    out = pl.pallas_call(
        functools.partial(_lmm_body, scale=scale),
        out_shape=jax.ShapeDtypeStruct((m * FFT_N2, HY_WIDTH), out_dtype),
        grid=(HY_WIDTH // LANE,),
        in_specs=[pl.BlockSpec((m, k), lambda j: (0, 0)),
                  pl.BlockSpec((k * FFT_N2, LANE), lambda j: (0, j))],
        out_specs=pl.BlockSpec((m * FFT_N2, LANE), lambda j: (0, j)),
        scratch_shapes=[pltpu.VMEM((k * LMM_PITCH, LANE), F32), pltpu.VMEM((mc * LMM_PITCH, LANE), F32)],
        compiler_params=_params(("parallel",), 56),
        name=name,
    )(f, x3.reshape(k * FFT_N2, HY_WIDTH))
    return out.reshape(m, FFT_N2, HY_WIDTH)


def _gspec_body(mf_ref, mr_ref, a_ref, am_ref, g_ref):
    z = jnp.dot(mf_ref[...], a_ref[...].reshape(2 * FFT_N2, HY_WIDTH), preferred_element_type=F32)
    zm = jnp.dot(mr_ref[...], am_ref[...].reshape(2 * FFT_N2, HY_WIDTH), preferred_element_type=F32)
    a, b = z[:FFT_N2], z[FFT_N2:]
    am, bm = zm[:FFT_N2], zm[FFT_N2:]
    g = 0.5 * jnp.concatenate([a + am + b + bm, b - bm + a - am], axis=0)
    g_ref[...] = g.astype(BF16).reshape(2, 1, FFT_N2, HY_WIDTH)


def _gspec(mf, mirror, zs):
    blk = lambda at: pl.BlockSpec((2, 1, FFT_N2, HY_WIDTH), lambda i: (0, at(i), 0, 0))
    mat = pl.BlockSpec((None, 2 * FFT_N2, 2 * FFT_N2), lambda i: (i, 0, 0))
    return pl.pallas_call(
        _gspec_body,
        out_shape=jax.ShapeDtypeStruct((2, FFT_N1, FFT_N2, HY_WIDTH), BF16),
        grid=(FFT_N1,),
        in_specs=[mat, mat, blk(lambda i: i), blk(lambda i: (FFT_N1 - i) % FFT_N1)],
        out_specs=blk(lambda i: i),
        compiler_params=_params(("parallel",), 24),
        name="gspec",
    )(mf, mirror, zs, zs)


def _spec_body(mf_ref, mi_ref, a_ref, g_ref, o_ref):
    a = a_ref[...].reshape(2 * FFT_N2, HY_WIDTH)
    x = jnp.dot(mf_ref[...], a, preferred_element_type=F32)
    xr, xi = x[:FFT_N2], x[FFT_N2:]
    gr, gi = g_ref[0, 0].astype(F32), g_ref[1, 0].astype(F32)
    y = jnp.concatenate([xr * gr - xi * gi, xr * gi + xi * gr], axis=0).astype(BF16)
    o_ref[...] = jnp.dot(mi_ref[...], y, preferred_element_type=F32).astype(BF16).reshape(
        2, 1, FFT_N2, HY_WIDTH)


def _spec(mf, mi, a, g):
    blk = pl.BlockSpec((2, 1, FFT_N2, HY_WIDTH), lambda i: (0, i, 0, 0))
    mat = pl.BlockSpec((None, 2 * FFT_N2, 2 * FFT_N2), lambda i: (i, 0, 0))
    return pl.pallas_call(
        _spec_body,
        out_shape=jax.ShapeDtypeStruct((2, FFT_N1, FFT_N2, HY_WIDTH), BF16),
        grid=(FFT_N1,),
        in_specs=[mat, mat, blk, blk],
        out_specs=blk,
        compiler_params=_params(("parallel",), 24),
        name="spec",
    )(mf, mi, a, g)


HY_CB = 256


def _short_conv(p_ref, w_ref, b_ref):
    p = p_ref[...]
    w = w_ref[...]
    prev = pltpu.roll(p, 1, 0)
    nxt = pltpu.roll(p, TL - 1, 0)
    return b_ref[...] + prev * w[0:1] + p * w[1:2] + nxt * w[2:3]


def _uconv_body(x1_ref, vh_ref, w1_ref, wv_ref, b1_ref, bv_ref, u_ref):
    u = _short_conv(vh_ref, wv_ref, bv_ref) * _short_conv(x1_ref, w1_ref, b1_ref)
    row = lax.broadcasted_iota(jnp.int32, (TL, 1), 0)
    u_ref[pl.ds(0, TL), :] = jnp.where(row >= PAD, u, 0.0)
    u_ref[pl.ds(TL, TFFT - TL), :] = jnp.zeros((TFFT - TL, HY_CB), F32)


def _hy_specs(first_block):
    nb = HY_WIDTH // HY_CB
    return (pl.BlockSpec((None, TL, HY_CB), lambda b, j: (b, 0, first_block * nb + j)),
            pl.BlockSpec((3, HY_CB), lambda b, j: (0, first_block * nb + j)),
            pl.BlockSpec((1, HY_CB), lambda b, j: (0, first_block * nb + j)))


def _uconv(p3, conv_w, conv_b):
    x1, w1, b1 = _hy_specs(1)
    vh, wv, bv = _hy_specs(2)
    return pl.pallas_call(
        _uconv_body,
        out_shape=jax.ShapeDtypeStruct((BATCH, TFFT, HY_WIDTH), F32),
        grid=(BATCH, HY_WIDTH // HY_CB),
        in_specs=[x1, vh, w1, wv, b1, bv],
        out_specs=pl.BlockSpec((None, TFFT, HY_CB), lambda b, j: (b, 0, j)),
        compiler_params=_params(("parallel", "parallel"), 48),
        name="uconv",
    )(p3, p3, conv_w, conv_w, conv_b, conv_b)


def _ymix_body(y_ref, x0_ref, x1_ref, vh_ref, w0_ref, w1_ref, wv_ref, b0_ref, b1_ref, bv_ref,
               d_ref, gain_ref, o_ref):
    u = _short_conv(vh_ref, wv_ref, bv_ref) * _short_conv(x1_ref, w1_ref, b1_ref)
    yy = (y_ref[...] + d_ref[...] * u) * _short_conv(x0_ref, w0_ref, b0_ref)
    gain = gain_ref[...]
    for s in range(0, HY_CB, HY_GROUP):
        o_ref[:, s:s + HY_GROUP] = (_rms(yy[X0:, s:s + HY_GROUP]) * gain[:, s:s + HY_GROUP]).astype(BF16)


def _ymix(y3, p3, conv_w, conv_b, hyena_d, hyena_norm):
    x0, w0, b0 = _hy_specs(0)
    x1, w1, b1 = _hy_specs(1)
    vh, wv, bv = _hy_specs(2)
    vec = pl.BlockSpec((1, HY_CB), lambda b, j: (0, j))
    return pl.pallas_call(
        _ymix_body,
        out_shape=jax.ShapeDtypeStruct((BATCH, SEQ, HY_WIDTH), BF16),
        grid=(BATCH, HY_WIDTH // HY_CB),
        in_specs=[pl.BlockSpec((None, TL, HY_CB), lambda b, j: (b, 0, j)),
                  x0, x1, vh, w0, w1, wv, b0, b1, bv, vec, vec],
        out_specs=pl.BlockSpec((None, SEQ, HY_CB), lambda b, j: (b, 0, j)),
        compiler_params=_params(("parallel", "parallel"), 56),
        name="ymix",
    )(y3, p3, p3, p3, conv_w, conv_w, conv_w, conv_b, conv_b, conv_b,
      hyena_d.reshape(1, -1), hyena_norm.reshape(1, -1))


N_SUB = CHUNK // SUB
SAFE_BLOCK_DECAY = -60.0


def _scores_exact(q, k, b, rev, ones, row, col):
    rsub = row % SUB
    terms = []
    for d in range(SUB):
        if d == 0:
            kr, br = k, b
        else:
            sh = CHUNK - d if rev else d
            kr, br = pltpu.roll(k, sh, 0), pltpu.roll(b, sh, 0)
        valid = (rsub + d < SUB) if rev else (rsub >= d)
        t = q * kr * jnp.exp(jnp.minimum(b - br, 0.0))
        terms.append(jnp.where(valid, t, 0.0).astype(BF16))
    sums = jnp.dot(jnp.concatenate(terms, axis=0), ones, preferred_element_type=F32)
    a = jnp.zeros((CHUNK, CHUNK), F32)
    for d in range(SUB):
        tgt = row + d if rev else row - d
        a = jnp.where(col == tgt, sums[d * CHUNK:(d + 1) * CHUNK, :CHUNK], a)

    rblk = row // SUB
    cblk = col // SUB
    for jb in (range(1, N_SUB) if rev else range(N_SUB - 1)):
        e = jb * SUB if rev else jb * SUB + SUB - 1
        ref = b[e:e + 1, :]
        qh = (q * jnp.exp(jnp.minimum(b - ref, 0.0))).astype(BF16)
        kh = (k * jnp.exp(jnp.minimum(ref - b, 0.0))).astype(BF16)
        pm = lax.dot_general(qh, kh, (((1,), (1,)), ((), ())), preferred_element_type=F32)
        side = jnp.where(cblk == jb, rblk, jb)
        a = jnp.where((side < jb) if rev else (side > jb), pm, a)

    bend = b[0:1, :] if rev else b[CHUNK - 1:CHUNK, :]
    return a, q * jnp.exp(b), k * jnp.exp(bend - b), bend


def _scores_fast(q, k, b, rev, row, col):
    order = list(range(N_SUB))[::-1] if rev else list(range(N_SUB))
    pos = {blk: p for p, blk in enumerate(order)}
    edge = lambda blk: blk * SUB if rev else blk * SUB + SUB - 1
    e = [b[edge(blk):edge(blk) + 1, :] for blk in order]
    s = [jnp.zeros((1, GLA_DK), F32)] + e[:-1]

    def by_row(vals):
        return jnp.concatenate([jnp.broadcast_to(vals[pos[blk]], (SUB, GLA_DK)) for blk in range(N_SUB)],
                               axis=0)

    srow, erow = by_row(s), by_row(e)
    qh = q * jnp.exp(b - srow)
    kh = k * jnp.exp(erow - b)
    kd = k * jnp.exp(srow - b)

    lhs = []
    for pj in range(N_SUB - 1):
        for blk in range(N_SUB):
            piece = qh[blk * SUB:(blk + 1) * SUB, :]
            p = pos[blk]
            if p <= pj:
                piece = jnp.zeros_like(piece)
            elif p > pj + 1:
                piece = piece * jnp.exp(s[p] - e[pj])
            lhs.append(piece.astype(BF16))
    contract = (((1,), (1,)), ((), ()))
    cross = lax.dot_general(jnp.concatenate(lhs, axis=0), kh.astype(BF16), contract,
                            preferred_element_type=F32)
    diag = lax.dot_general(qh.astype(BF16), kd.astype(BF16), contract, preferred_element_type=F32)

    rblk = row // SUB
    cblk = col // SUB
    a = jnp.zeros((CHUNK, CHUNK), F32)
    for pj in range(N_SUB - 1):
        a = jnp.where(cblk == order[pj], cross[pj * CHUNK:(pj + 1) * CHUNK], a)
    causal = (col >= row) if rev else (col <= row)
    a = jnp.where(cblk == rblk, jnp.where(causal, diag, 0.0), a)

    bend = e[-1]
    return a, qh * jnp.exp(srow), kh * jnp.exp(bend - erow), bend


GLA_RB = 384


def _gla_sweep_body(*refs, rev):
    if rev:
        q_ref, k_ref, v_ref, g_ref, o_ref, st_ref = refs
    else:
        q_ref, k_ref, v_ref, g_ref, og_ref, ob_ref, gain_ref, o_ref, st_ref = refs

    @pl.when(pl.program_id(1) == 0)
    def _():
        st_ref[...] = jnp.zeros_like(st_ref)

    row = lax.broadcasted_iota(jnp.int32, (CHUNK, 1), 0)
    col = lax.broadcasted_iota(jnp.int32, (CHUNK, CHUNK), 1)
    rr = lax.broadcasted_iota(jnp.int32, (CHUNK, CHUNK), 0)
    tri = ((col >= rr) if rev else (col <= rr)).astype(F32)
    ones = jnp.ones((GLA_DK, GLA_DK), BF16)
    n_chunk = GLA_RB // CHUNK

    def run(fast):
        def step(t, carry):
            c = n_chunk - 1 - t if rev else t
            rows = pl.ds(pl.multiple_of(c * CHUNK, CHUNK), CHUNK)
            b_all = jnp.dot(tri, g_ref[rows, :], precision=lax.Precision.HIGHEST, preferred_element_type=F32)
            pending = []
            for h in range(GLA_HEADS):
                kc = slice(h * GLA_DK, (h + 1) * GLA_DK)
                vc = slice(h * GLA_DV, (h + 1) * GLA_DV)
                q = q_ref[rows, kc] * (GLA_DK ** -0.5)
                k = k_ref[rows, kc]
                v = v_ref[rows, vc].astype(BF16)
                b = b_all[:, kc]
                if fast:
                    a, qt, kt, bend = _scores_fast(q, k, b, rev, row, col)
                else:
                    a, qt, kt, bend = _scores_exact(q, k, b, rev, ones, row, col)
                st = st_ref[h]
                o = lax.dot_general(qt.astype(BF16), st.astype(BF16), (((1,), (1,)), ((), ())),
                                    preferred_element_type=F32)
                st_ref[h] = st * jnp.exp(bend) + lax.dot_general(
                    v, kt.astype(BF16), (((0,), (0,)), ((), ())), preferred_element_type=F32)
                pending.append((vc, o, a, v))
            for vc, o, a, v in pending:
                o = o + jnp.dot(a.astype(BF16), v, preferred_element_type=F32)
                if rev:
                    o_ref[rows, vc] = o
                else:
                    o = _rms(o + ob_ref[rows, vc]) * gain_ref[...]
                    o_ref[rows, vc] = (o * _silu(og_ref[rows, vc])).astype(BF16)
            return carry

        lax.fori_loop(0, n_chunk, step, 0)

    low = jnp.min(jnp.sum(g_ref[...].reshape(GLA_RB // SUB, SUB, GLA_KEY_WIDTH), axis=1))
    safe = low > SAFE_BLOCK_DECAY
    pl.when(safe)(lambda: run(True))
    pl.when(jnp.logical_not(safe))(lambda: run(False))


def _gla_sweep(p3, lg3, rev, ob=None, gla_norm=None):
    nb = TL // GLA_RB
    blk = (lambda i: nb - 1 - i) if rev else (lambda i: i)
    key_blocks = P_MAIN // GLA_KEY_WIDTH
    q_col = 3 * HY_WIDTH // GLA_KEY_WIDTH
    v_col = (3 * HY_WIDTH + 2 * GLA_KEY_WIDTH) // GLA_WIDTH
    assert key_blocks * GLA_KEY_WIDTH == P_MAIN
    narrow = lambda col: pl.BlockSpec((None, GLA_RB, GLA_KEY_WIDTH), lambda b, i: (b, blk(i), col))
    wide = lambda col: pl.BlockSpec((None, GLA_RB, GLA_WIDTH), lambda b, i: (b, blk(i), col))
    in_specs = [narrow(q_col), narrow(q_col + 1), wide(v_col), narrow(1 if rev else 0)]
    args = [p3, p3, p3, lg3]
    if not rev:
        in_specs += [wide(v_col + 1), wide(0), pl.BlockSpec((1, GLA_DV), lambda b, i: (0, 0))]
        args += [p3, ob, gla_norm.reshape(1, -1)]
    return pl.pallas_call(
        functools.partial(_gla_sweep_body, rev=rev),
        out_shape=jax.ShapeDtypeStruct((BATCH, TL, GLA_WIDTH), F32 if rev else BF16),
        grid=(BATCH, nb),
        in_specs=in_specs,
        out_specs=wide(0),
        scratch_shapes=[pltpu.VMEM((GLA_HEADS, GLA_DV, GLA_DK), F32)],
        compiler_params=_params(("parallel", "arbitrary"), 32),
        name="gla_down" if rev else "gla_up",
    )(*args)


OUT_TM = 512


def _outproj_body(h_ref, yh_ref, yg_ref, wh_ref, wg_ref, o_ref):
    o_ref[...] = (h_ref[...] + jnp.dot(yh_ref[...], wh_ref[...], preferred_element_type=F32)
                  + jnp.dot(yg_ref[...], wg_ref[...], preferred_element_type=F32))


def _outproj(hbuf, yh, ygbuf, w_out):
    wo = _to_bf16(w_out)
    half = lambda i: pl.BlockSpec((HY_WIDTH, D_MODEL), lambda r: (i, 0))
    return pl.pallas_call(
        _outproj_body,
        out_shape=jax.ShapeDtypeStruct((X_ROWS, D_MODEL), F32),
        grid=(X_ROWS // OUT_TM,),
        in_specs=[pl.BlockSpec((pl.Element(OUT_TM), pl.Element(D_MODEL)), lambda r: (_x_row(r, OUT_TM), 0)),
                  pl.BlockSpec((OUT_TM, HY_WIDTH), lambda r: (r, 0)),
                  pl.BlockSpec((pl.Element(OUT_TM), pl.Element(GLA_WIDTH)),
                               lambda r: (_x_row(r, OUT_TM, BF16_SUBLANE), 0)),
                  half(0), half(1)],
        out_specs=pl.BlockSpec((OUT_TM, D_MODEL), lambda r: (r, 0)),
        compiler_params=_params(("parallel",), 48),
        name="outproj",
    )(hbuf, yh, ygbuf, wo, wo)


def _long_conv(u3, filt):
    f1, f3, mf, mi, mirror = _dft_tables()
    as_n1 = lambda a: a.reshape(2 * FFT_R1, FFT_N2, HY_WIDTH)
    zs = _lmm(f1, as_n1(filt), BF16, name="lmm_g")
    gs = _gspec(mf, mirror, zs.reshape(2, FFT_N1, FFT_N2, HY_WIDTH))
    a = _lmm(f1, as_n1(u3), BF16, name="lmm_fwd")
    bm = _spec(mf, mi, a.reshape(2, FFT_N1, FFT_N2, HY_WIDTH), gs)
    y = _lmm(f3, bm.reshape(2 * FFT_N1, FFT_N2, HY_WIDTH), F32, scale=1.0 / NFFT, name="lmm_inv")
    return y.reshape(BATCH, TFFT, HY_WIDTH)


def kernel(x, meta_tokens, ffn1_norm, ffn1_w_gate, ffn1_w_up, ffn1_w_down, mix_norm, w_in, conv_w, conv_b,
           filt_w1, filt_b1, filt_w2, filt_b2, filt_w3, filt_freq, hyena_d, hyena_norm, gk_w2, gk_b2,
           gla_norm, w_out, ffn2_norm, ffn2_w_gate, ffn2_w_up, ffn2_w_down, final_norm):
    assert x.shape == (BATCH, SEQ, D_MODEL) and ffn1_norm.shape[0] == 1

    hbuf = _ffn(x.reshape(X_ROWS, D_MODEL), ffn1_norm[0], ffn1_w_gate[0], ffn1_w_up[0], ffn1_w_down[0],
                final_norm, final=False, shared_rows=_meta_rows(meta_tokens))

    p, lg = _inproj(hbuf, _inproj_weights(mix_norm[0], w_in[0], gk_w2[0], gk_b2[0]))
    p3 = p.reshape(BATCH, TL, P_MAIN)
    lg3 = lg.reshape(BATCH, TL, 2 * GLA_KEY_WIDTH)

    filt = _filters(filt_w1[0], filt_b1[0], filt_w2[0], filt_b2[0], filt_w3[0], filt_freq[0])
    cw, cb = conv_w[0], conv_b[0].reshape(1, -1)
    y3 = _long_conv(_uconv(p3, cw, cb), filt)
    yh = _ymix(y3, p3, cw, cb, hyena_d[0], hyena_norm[0])
    yg = _gla_sweep(p3, lg3, rev=False, ob=_gla_sweep(p3, lg3, rev=True), gla_norm=gla_norm[0])

    h2 = _outproj(hbuf, yh.reshape(X_ROWS, HY_WIDTH), yg.reshape(ROWS, GLA_WIDTH), w_out[0])
    out = _ffn(h2, ffn2_norm[0], ffn2_w_gate[0], ffn2_w_up[0], ffn2_w_down[0], final_norm, final=True)
    return out.reshape(BATCH, SEQ, D_MODEL)
```

```python
import functools

import numpy as np
import jax
import jax.numpy as jnp
from jax import lax
from jax.experimental import pallas as pl
from jax.experimental.pallas import tpu as pltpu

F32 = jnp.float32
BF16 = jnp.bfloat16

D_MODEL = 2048
BATCH = 2
SEQ = 4096
N_META = 16
L_TOK = SEQ + N_META
PAD = 112
X0 = PAD + N_META
TL = PAD + L_TOK
ROWS = BATCH * TL
X_ROWS = BATCH * SEQ
HY_WIDTH = 1024
HY_GROUPS = 8
HY_GROUP = HY_WIDTH // HY_GROUPS
FILT_EMB = 33
FILT_BANDS = 16
FILT_HIDDEN = 64
GLA_WIDTH = 1024
GLA_HEADS = 4
GLA_KEY_WIDTH = 512
GLA_DK = 128
GLA_DV = 256
GATE_RANK = 16
GATE_NORMALIZER = 16.0
CHUNK = 64
SUB = 16
D_FF = 5632
P_MAIN = 3 * HY_WIDTH + 2 * GLA_KEY_WIDTH + 2 * GLA_WIDTH
EPS = 1e-6

FFT_N1 = 72
FFT_N2 = 128
NFFT = FFT_N1 * FFT_N2
FFT_R1 = 40
TFFT = FFT_R1 * FFT_N2

MIB = 1024 * 1024
SUBLANE = 8
BF16_SUBLANE = 16
LANE = 128


def _params(sem, vmem_mib):
    return pltpu.CompilerParams(dimension_semantics=sem, vmem_limit_bytes=vmem_mib * MIB)


def _rms(x):
    return x * lax.rsqrt(jnp.mean(x * x, axis=-1, keepdims=True) + EPS)


def _silu(x):
    return x * jax.nn.sigmoid(x)


LAY_TM = 528


def _x_row(i, tm, unit=SUBLANE):
    per_batch = SEQ // tm
    r = (i // per_batch) * (TL // unit) + X0 // unit + (i % per_batch) * (tm // unit)
    return pl.multiple_of(r * unit, unit)


def _lay_src_row(i):
    per_batch = TL // LAY_TM
    r = (i // per_batch) * (SEQ // SUBLANE) + jnp.maximum(
        (i % per_batch) * (LAY_TM // SUBLANE) - X0 // SUBLANE, 0)
    return pl.multiple_of(r * SUBLANE, SUBLANE)


def _meta_rows(meta_tokens):
    return jnp.concatenate([jnp.zeros((PAD, D_MODEL), F32), meta_tokens.astype(F32)], axis=0)


FFN_TM = 512
FFN_TF = 512


FFN_HEAD_TF = 256


def _ffn_begin(x_ref, shared_ref, xin_ref, gain_ref, xn_ref, acc_ref, first):
    if shared_ref is not None:
        @pl.when(first)
        def _():
            xin_ref[0:X0, :] = shared_ref[...]
            xin_ref[X0:LAY_TM, :] = x_ref[0:LAY_TM - X0, :]

        @pl.when(jnp.logical_not(first))
        def _():
            xin_ref[...] = x_ref[...]

    xn_ref[...] = (_rms(xin_ref[...]) * gain_ref[...]).astype(BF16)
    acc_ref[...] = jnp.zeros_like(acc_ref)


def _ffn_step(xn_ref, wg, wu, wd, acc_ref):
    xn = xn_ref[...]
    g = jnp.dot(xn, wg, preferred_element_type=F32)
    u = jnp.dot(xn, wu, preferred_element_type=F32)
    a = (_silu(g) * u).astype(BF16)
    acc_ref[...] += jnp.dot(a, wd, preferred_element_type=F32)


def _ffn_result(xin_ref, acc_ref, fgain_ref, final):
    h = xin_ref[...] + 0.5 * acc_ref[...]
    return _rms(h) * fgain_ref[...] if final else h


def _ffn_head_body(x_ref, gain_ref, wg_ref, wu_ref, wd_ref, fgain_ref, *rest, final, layout):
    if layout:
        shared_ref, o_ref, wg16_ref, wu16_ref, wd16_ref, xn_ref, acc_ref, xin_ref = rest
    else:
        o_ref, wg16_ref, wu16_ref, wd16_ref, xn_ref, acc_ref = rest
        shared_ref, xin_ref = None, x_ref
    j = pl.program_id(0)

    @pl.when(j == 0)
    def _():
        _ffn_begin(x_ref, shared_ref, xin_ref, gain_ref, xn_ref, acc_ref, True)

    wg, wu, wd = wg_ref[...].astype(BF16), wu_ref[...].astype(BF16), wd_ref[...].astype(BF16)
    wg16_ref[...] = wg
    wu16_ref[...] = wu
    wd16_ref[...] = wd
    _ffn_step(xn_ref, wg, wu, wd, acc_ref)

    @pl.when(j == pl.num_programs(0) - 1)
    def _():
        o_ref[...] = _ffn_result(xin_ref, acc_ref, fgain_ref, final)


def _ffn_body(x_ref, gain_ref, wg_ref, wu_ref, wd_ref, fgain_ref, head_ref, *rest, final, layout):
    if layout:
        shared_ref, o_ref, xn_ref, acc_ref, xin_ref = rest
    else:
        o_ref, xn_ref, acc_ref = rest
        shared_ref, xin_ref = None, x_ref
    i, j = pl.program_id(0), pl.program_id(1)
    last = j == pl.num_programs(1) - 1

    @pl.when(i > 0)
    def _():
        @pl.when(j == 0)
        def _():
            _ffn_begin(x_ref, shared_ref, xin_ref, gain_ref, xn_ref, acc_ref, i % (TL // LAY_TM) == 0)

        _ffn_step(xn_ref, wg_ref[...], wu_ref[...], wd_ref[...], acc_ref)

        @pl.when(last)
        def _():
            o_ref[...] = _ffn_result(xin_ref, acc_ref, fgain_ref, final)

    @pl.when(jnp.logical_and(i == 0, last))
    def _():
        o_ref[...] = head_ref[...]


CAST_STEPS = 4


def _cast_body(x_ref, o_ref):
    o_ref[...] = x_ref[...].astype(BF16)


def _to_bf16(w, cols=None):
    rows = w.shape[0]
    cols = w.shape[1] if cols is None else cols
    tr = rows // CAST_STEPS
    assert tr * CAST_STEPS == rows and tr % BF16_SUBLANE == 0 and cols % LANE == 0
    return pl.pallas_call(
        _cast_body,
        out_shape=jax.ShapeDtypeStruct((rows, cols), BF16),
        grid=(CAST_STEPS,),
        in_specs=[pl.BlockSpec((tr, cols), lambda i: (i, 0))],
        out_specs=pl.BlockSpec((tr, cols), lambda i: (i, 0)),
        compiler_params=_params(("parallel",), 48),
        name="to_bf16",
    )(w)


def _ffn(x2, gain, wg, wu, wd, fgain, final, shared_rows=None):
    layout = shared_rows is not None
    tm = LAY_TM if layout else FFN_TM
    rows = ROWS if layout else X_ROWS
    gain, fgain = gain.reshape(1, -1), fgain.reshape(1, -1)
    scratch = [pltpu.VMEM((tm, D_MODEL), BF16), pltpu.VMEM((tm, D_MODEL), F32)]
    extra = ()
    if layout:
        extra = (shared_rows,)
        scratch.append(pltpu.VMEM((tm, D_MODEL), F32))
    name = "ffn_final" if final else "ffn"

    tf = FFN_HEAD_TF
    vec1 = pl.BlockSpec((1, D_MODEL), lambda j: (0, 0))
    wide1 = pl.BlockSpec((D_MODEL, tf), lambda j: (0, j))
    tall1 = pl.BlockSpec((tf, D_MODEL), lambda j: (j, 0))
    tile1 = pl.BlockSpec((tm, D_MODEL), lambda j: (0, 0))
    head, wg16, wu16, wd16 = pl.pallas_call(
        functools.partial(_ffn_head_body, final=final, layout=layout),
        out_shape=(jax.ShapeDtypeStruct((tm, D_MODEL), F32), jax.ShapeDtypeStruct(wg.shape, BF16),
                   jax.ShapeDtypeStruct(wu.shape, BF16), jax.ShapeDtypeStruct(wd.shape, BF16)),
        grid=(D_FF // tf,),
        in_specs=[tile1, vec1, wide1, wide1, tall1, vec1] + [pl.BlockSpec((X0, D_MODEL), lambda j: (0, 0))] * layout,
        out_specs=(tile1, wide1, wide1, tall1),
        scratch_shapes=scratch,
        compiler_params=_params(("arbitrary",), 56),
        name=name + "_head",
    )(x2, gain, wg, wu, wd, fgain, *extra)

    col = lambda i, j: jnp.where(i == 0, 0, j)
    vec = pl.BlockSpec((1, D_MODEL), lambda i, j: (0, 0))
    wide = pl.BlockSpec((D_MODEL, FFN_TF), lambda i, j: (0, col(i, j)))
    tall = pl.BlockSpec((FFN_TF, D_MODEL), lambda i, j: (col(i, j), 0))
    if layout:
        x_spec = pl.BlockSpec((pl.Element(tm), pl.Element(D_MODEL)), lambda i, j: (_lay_src_row(i), 0))
    else:
        x_spec = pl.BlockSpec((tm, D_MODEL), lambda i, j: (i, 0))
    return pl.pallas_call(
        functools.partial(_ffn_body, final=final, layout=layout),
        out_shape=jax.ShapeDtypeStruct((rows, D_MODEL), F32),
        grid=(rows // tm, D_FF // FFN_TF),
        in_specs=[x_spec, vec, wide, wide, tall, vec, pl.BlockSpec((tm, D_MODEL), lambda i, j: (0, 0))]
        + [pl.BlockSpec((X0, D_MODEL), lambda i, j: (0, 0))] * layout,
        out_specs=pl.BlockSpec((tm, D_MODEL), lambda i, j: (i, 0)),
        scratch_shapes=scratch,
        compiler_params=_params(("parallel", "arbitrary"), 56),
        name=name,
    )(x2, gain, wg16, wu16, wd16, fgain, head, *extra)


INP_TM = 2 * LAY_TM
INP_TN = 1024
LR_PAD = 128


def _inproj_body(x_ref, gain_ref, w_ref, wlr_ref, w2_ref, b2_ref, p_ref, lg_ref, xn_ref):
    @pl.when(pl.program_id(1) == 0)
    def _():
        xn = (_rms(x_ref[...]) * gain_ref[...]).astype(BF16)
        xn_ref[...] = xn
        lr = jnp.dot(xn, wlr_ref[...], preferred_element_type=F32).astype(BF16)
        z = jnp.dot(lr, w2_ref[...], preferred_element_type=F32) + b2_ref[...]
        lg_ref[...] = (jnp.minimum(z, 0.0) - jnp.log1p(jnp.exp(-jnp.abs(z)))) * (1.0 / GATE_NORMALIZER)

    p_ref[...] = jnp.dot(xn_ref[...], w_ref[...], preferred_element_type=F32).astype(BF16)


SPLIT_TC = 1024


def _split_w_in_body(w_ref, wlr_ref, main_ref, lr_ref):
    main_ref[...] = w_ref[...].T.astype(BF16)

    @pl.when(pl.program_id(0) == 0)
    def _():
        rows = jnp.concatenate([wlr_ref[...], jnp.zeros((LR_PAD - 2 * GATE_RANK, D_MODEL), F32)], axis=0)
        lr_ref[...] = rows.T.astype(BF16)


def _split_w_in(w_in):
    w_t = jnp.transpose(w_in)
    return pl.pallas_call(
        _split_w_in_body,
        out_shape=(jax.ShapeDtypeStruct((D_MODEL, P_MAIN), BF16), jax.ShapeDtypeStruct((D_MODEL, LR_PAD), BF16)),
        grid=(P_MAIN // SPLIT_TC,),
        in_specs=[pl.BlockSpec((SPLIT_TC, D_MODEL), lambda i: (i, 0)),
                  pl.BlockSpec((2 * GATE_RANK, D_MODEL), lambda i: (P_MAIN // (2 * GATE_RANK), 0))],
        out_specs=(pl.BlockSpec((D_MODEL, SPLIT_TC), lambda i: (0, i)),
                   pl.BlockSpec((D_MODEL, LR_PAD), lambda i: (0, 0))),
        compiler_params=_params(("arbitrary",), 48),
        name="split_w_in",
    )(w_t, w_t)


def _inproj_weights(gain, w_in, gk_w2, gk_b2):
    w_main, w_lr = _split_w_in(w_in)
    w2 = jnp.zeros((LR_PAD, 2 * GLA_KEY_WIDTH), F32)
    w2 = w2.at[:GATE_RANK, :GLA_KEY_WIDTH].set(gk_w2[0])
    w2 = w2.at[GATE_RANK:2 * GATE_RANK, GLA_KEY_WIDTH:].set(gk_w2[1]).astype(BF16)
    return gain.reshape(1, -1), w_main, w_lr, w2, gk_b2.reshape(1, 2 * GLA_KEY_WIDTH)


def _inproj(hbuf, weights):
    tm, tn = INP_TM, INP_TN
    return pl.pallas_call(
        _inproj_body,
        out_shape=(jax.ShapeDtypeStruct((ROWS, P_MAIN), BF16),
                   jax.ShapeDtypeStruct((ROWS, 2 * GLA_KEY_WIDTH), F32)),
        grid=(ROWS // tm, P_MAIN // tn),
        in_specs=[
            pl.BlockSpec((tm, D_MODEL), lambda i, j: (i, 0)),
            pl.BlockSpec((1, D_MODEL), lambda i, j: (0, 0)),
            pl.BlockSpec((D_MODEL, tn), lambda i, j: (0, j)),
            pl.BlockSpec((D_MODEL, LR_PAD), lambda i, j: (0, 0)),
            pl.BlockSpec((LR_PAD, 2 * GLA_KEY_WIDTH), lambda i, j: (0, 0)),
            pl.BlockSpec((1, 2 * GLA_KEY_WIDTH), lambda i, j: (0, 0)),
        ],
        out_specs=(pl.BlockSpec((tm, tn), lambda i, j: (i, j)),
                   pl.BlockSpec((tm, 2 * GLA_KEY_WIDTH), lambda i, j: (i, 0))),
        scratch_shapes=[pltpu.VMEM((tm, D_MODEL), BF16)],
        compiler_params=_params(("parallel", "arbitrary"), 56),
        name="inproj",
    )(hbuf, *weights)


FILT_TR = 640
FEAT_PAD = 128


def _filt_tables():
    pos = np.arange(TFFT, dtype=np.float64)
    t = pos / (L_TOK - 1)
    w = (2.0 * np.pi / L_TOK) * pos
    bands = 1e-4 + np.arange(FILT_BANDS, dtype=np.float64) * ((FILT_BANDS - 1 - 1e-4) / (FILT_BANDS - 1))
    ang = w[:, None] * bands[None, :]
    feats = np.zeros((TFFT, FEAT_PAD), np.float64)
    feats[:, 0] = t
    feats[:, 1:1 + FILT_BANDS] = np.cos(ang)
    feats[:, 1 + FILT_BANDS:FILT_EMB] = -np.sin(ang)
    lo, hi = np.log(1e-2) / 1.5, np.log(1e-2) / 0.3
    deltas = np.abs(lo + np.arange(HY_WIDTH, dtype=np.float64) * ((hi - lo) / (HY_WIDTH - 1)))
    return feats.astype(np.float32), deltas.astype(np.float32).reshape(1, HY_WIDTH)


def _filt_body(feat_ref, t_ref, w1_ref, b1_ref, w2_ref, b2_ref, w3_ref, fr_ref, dl_ref, h_ref):
    start = pl.program_id(0) * FILT_TR

    @pl.when(start < L_TOK)
    def _():
        hp = lax.Precision.HIGHEST
        fr = fr_ref[...]
        z = jnp.sin(fr * (jnp.dot(w1_ref[...], feat_ref[...], precision=hp, preferred_element_type=F32)
                          + b1_ref[...]))
        z = jnp.sin(fr * (jnp.dot(w2_ref[...], z, precision=hp, preferred_element_type=F32) + b2_ref[...]))
        hh = lax.dot_general(z, w3_ref[...], (((0,), (0,)), ((), ())), precision=hp,
                             preferred_element_type=F32)
        pos = start + lax.broadcasted_iota(jnp.int32, (FILT_TR, 1), 0)
        win = jnp.exp(-t_ref[...] * dl_ref[...])
        win = jnp.where(pos < L_TOK, win, 0.0)
        hf = hh[:, :HY_WIDTH] * win
        hb = hh[:, HY_WIDTH:] * win
        h_ref[0] = jnp.where(pos == 0, hf + hb, hf)
        h_ref[1] = jnp.where(pos == 0, 0.0, hb)

    @pl.when(start >= L_TOK)
    def _():
        h_ref[...] = jnp.zeros_like(h_ref)


def _filters(w1, b1, w2, b2, w3, freq):
    feats, deltas = _filt_tables()
    w1t = jnp.pad(w1, ((0, FEAT_PAD - FILT_EMB), (0, 0))).T
    col = lambda v: v.reshape(-1, 1)
    full = lambda shape: pl.BlockSpec(shape, lambda i: (0, 0))
    return pl.pallas_call(
        _filt_body,
        out_shape=jax.ShapeDtypeStruct((2, TFFT, HY_WIDTH), F32),
        grid=(TFFT // FILT_TR,),
        in_specs=[
            pl.BlockSpec((FEAT_PAD, FILT_TR), lambda i: (0, i)),
            pl.BlockSpec((FILT_TR, 1), lambda i: (i, 0)),
            full((FILT_HIDDEN, FEAT_PAD)), full((FILT_HIDDEN, 1)),
            full((FILT_HIDDEN, FILT_HIDDEN)), full((FILT_HIDDEN, 1)),
            full((FILT_HIDDEN, 2 * HY_WIDTH)), full((FILT_HIDDEN, 1)), full((1, HY_WIDTH)),
        ],
        out_specs=pl.BlockSpec((2, FILT_TR, HY_WIDTH), lambda i: (0, i, 0)),
        compiler_params=_params(("parallel",), 40),
        name="filt",
    )(jnp.asarray(feats.T), jnp.asarray(feats[:, 0:1]), w1t, col(b1), w2.T, col(b2), w3, col(freq),
      jnp.asarray(deltas))


def _dft_tables_np():
    n1 = np.arange(FFT_N1)
    ang1 = 2.0 * np.pi * ((n1[:, None] * n1[None, :]) % FFT_N1) / FFT_N1
    c1, s1 = np.cos(ang1)[:, :FFT_R1], np.sin(ang1)[:, :FFT_R1]
    f1 = np.block([[c1, s1], [-s1, c1]])
    f3 = np.block([[c1.T, -s1.T], [s1.T, c1.T]])
    k1 = np.arange(FFT_N1)[:, None, None]
    k2 = np.arange(FFT_N2)[None, :, None]
    n2 = np.arange(FFT_N2)[None, None, :]
    ang2 = 2.0 * np.pi * ((n2 * (k1 + FFT_N1 * k2)) % NFFT) / NFFT
    c2, s2 = np.cos(ang2), np.sin(ang2)
    mf = np.concatenate([np.concatenate([c2, s2], axis=2), np.concatenate([-s2, c2], axis=2)], axis=1)
    c2t, s2t = np.swapaxes(c2, 1, 2), np.swapaxes(s2, 1, 2)
    mi = np.concatenate([np.concatenate([c2t, -s2t], axis=2), np.concatenate([s2t, c2t], axis=2)], axis=1)
    k1r = np.arange(FFT_N1)
    k2r = np.arange(FFT_N2)
    perm = np.where(k1r[:, None] == 0, (FFT_N2 - k2r[None, :]) % FFT_N2, FFT_N2 - 1 - k2r[None, :])
    rows = np.concatenate([perm, perm + FFT_N2], axis=1)
    mirror = mf[((FFT_N1 - k1r) % FFT_N1)[:, None], rows, :]
    return f1, f3, mf, mi, mirror


def _dft_tables():
    return tuple(jnp.asarray(a.astype(np.float32)).astype(BF16) for a in _dft_tables_np())


LMM_PITCH = FFT_N2 + SUBLANE
LMM_MC = 2 * FFT_N1


def _lmm_body(f_ref, x_ref, o_ref, xs_ref, os_ref, *, scale):
    m, k = f_ref.shape
    for g in range(k):
        xs_ref[pl.ds(g * LMM_PITCH, FFT_N2), :] = x_ref[pl.ds(g * FFT_N2, FFT_N2), :].astype(F32)
    for m0 in range(0, m, LMM_MC):
        f = f_ref[m0:min(m0 + LMM_MC, m), :]

        def slab(s, carry):
            x = xs_ref[pl.ds(s, k, stride=LMM_PITCH), :].astype(BF16)
            r = jnp.dot(f, x, preferred_element_type=F32)
            os_ref[pl.ds(s, f.shape[0], stride=LMM_PITCH), :] = r * scale if scale != 1.0 else r
            return carry

        lax.fori_loop(0, FFT_N2, slab, 0, unroll=8)
        for g in range(f.shape[0]):
            o_ref[pl.ds((m0 + g) * FFT_N2, FFT_N2), :] = os_ref[pl.ds(g * LMM_PITCH, FFT_N2), :].astype(
                o_ref.dtype)


def _lmm(f, x3, out_dtype, scale=1.0, name="lmm"):
    m, k = f.shape
    mc = min(m, LMM_MC)
    assert m % mc == 0
    out = pl.pallas_call(
        functools.partial(_lmm_body, scale=scale),
        out_shape=jax.ShapeDtypeStruct((m * FFT_N2, HY_WIDTH), out_dtype),
        grid=(HY_WIDTH // LANE,),
        in_specs=[pl.BlockSpec((m, k), lambda j: (0, 0)),
                  pl.BlockSpec((k * FFT_N2, LANE), lambda j: (0, j))],
        out_specs=pl.BlockSpec((m * FFT_N2, LANE), lambda j: (0, j)),
        scratch_shapes=[pltpu.VMEM((k * LMM_PITCH, LANE), F32), pltpu.VMEM((mc * LMM_PITCH, LANE), F32)],
        compiler_params=_params(("parallel",), 56),
        name=name,
    )(f, x3.reshape(k * FFT_N2, HY_WIDTH))
    return out.reshape(m, FFT_N2, HY_WIDTH)


def _spec_body(mf_ref, mr_ref, mi_ref, a_ref, z_ref, zm_ref, o_ref):
    mf = mf_ref[...]
    flat = lambda ref: ref[...].reshape(2 * FFT_N2, HY_WIDTH)
    z = jnp.dot(mf, flat(z_ref), preferred_element_type=F32)
    zm = jnp.dot(mr_ref[...], flat(zm_ref), preferred_element_type=F32)
    a, b = z[:FFT_N2], z[FFT_N2:]
    am, bm = zm[:FFT_N2], zm[FFT_N2:]
    gr, gi = 0.5 * (a + am + b + bm), 0.5 * (b - bm + a - am)

    x = jnp.dot(mf, flat(a_ref), preferred_element_type=F32)
    xr, xi = x[:FFT_N2], x[FFT_N2:]
    y = jnp.concatenate([xr * gr - xi * gi, xr * gi + xi * gr], axis=0).astype(BF16)
    o_ref[...] = jnp.dot(mi_ref[...], y, preferred_element_type=F32).astype(BF16).reshape(
        2, 1, FFT_N2, HY_WIDTH)


def _spec(mf, mirror, mi, a, zs):
    blk = lambda at: pl.BlockSpec((2, 1, FFT_N2, HY_WIDTH), lambda i: (0, at(i), 0, 0))
    here = blk(lambda i: i)
    mat = pl.BlockSpec((None, 2 * FFT_N2, 2 * FFT_N2), lambda i: (i, 0, 0))
    return pl.pallas_call(
        _spec_body,
        out_shape=jax.ShapeDtypeStruct((2, FFT_N1, FFT_N2, HY_WIDTH), BF16),
        grid=(FFT_N1,),
        in_specs=[mat, mat, mat, here, here, blk(lambda i: (FFT_N1 - i) % FFT_N1)],
        out_specs=here,
        compiler_params=_params(("parallel",), 32),
        name="spec",
    )(mf, mirror, mi, a, zs, zs)


HY_CB = 256


def _short_conv(p_ref, w_ref, b_ref):
    p = p_ref[...].astype(F32)
    w = w_ref[...]
    prev = pltpu.roll(p, 1, 0)
    nxt = pltpu.roll(p, TL - 1, 0)
    return b_ref[...] + prev * w[0:1] + p * w[1:2] + nxt * w[2:3]


def _uconv_body(x1_ref, vh_ref, w1_ref, wv_ref, b1_ref, bv_ref, u_ref):
    u = _short_conv(vh_ref, wv_ref, bv_ref) * _short_conv(x1_ref, w1_ref, b1_ref)
    row = lax.broadcasted_iota(jnp.int32, (TL, 1), 0)
    u_ref[pl.ds(0, TL), :] = jnp.where(row >= PAD, u, 0.0)
    u_ref[pl.ds(TL, TFFT - TL), :] = jnp.zeros((TFFT - TL, HY_CB), F32)


def _hy_specs(first_block):
    nb = HY_WIDTH // HY_CB
    return (pl.BlockSpec((None, TL, HY_CB), lambda b, j: (b, 0, first_block * nb + j)),
            pl.BlockSpec((3, HY_CB), lambda b, j: (0, first_block * nb + j)),
            pl.BlockSpec((1, HY_CB), lambda b, j: (0, first_block * nb + j)))


def _uconv(p3, conv_w, conv_b):
    x1, w1, b1 = _hy_specs(1)
    vh, wv, bv = _hy_specs(2)
    return pl.pallas_call(
        _uconv_body,
        out_shape=jax.ShapeDtypeStruct((BATCH, TFFT, HY_WIDTH), F32),
        grid=(BATCH, HY_WIDTH // HY_CB),
        in_specs=[x1, vh, w1, wv, b1, bv],
        out_specs=pl.BlockSpec((None, TFFT, HY_CB), lambda b, j: (b, 0, j)),
        compiler_params=_params(("parallel", "parallel"), 48),
        name="uconv",
    )(p3, p3, conv_w, conv_w, conv_b, conv_b)


def _ymix_body(y_ref, x0_ref, x1_ref, vh_ref, w0_ref, w1_ref, wv_ref, b0_ref, b1_ref, bv_ref,
               d_ref, gain_ref, o_ref):
    u = _short_conv(vh_ref, wv_ref, bv_ref) * _short_conv(x1_ref, w1_ref, b1_ref)
    yy = (y_ref[...] + d_ref[...] * u) * _short_conv(x0_ref, w0_ref, b0_ref)
    gain = gain_ref[...]
    for s in range(0, HY_CB, HY_GROUP):
        o_ref[:, s:s + HY_GROUP] = (_rms(yy[X0:, s:s + HY_GROUP]) * gain[:, s:s + HY_GROUP]).astype(BF16)


def _ymix(y3, p3, conv_w, conv_b, hyena_d, hyena_norm):
    x0, w0, b0 = _hy_specs(0)
    x1, w1, b1 = _hy_specs(1)
    vh, wv, bv = _hy_specs(2)
    vec = pl.BlockSpec((1, HY_CB), lambda b, j: (0, j))
    return pl.pallas_call(
        _ymix_body,
        out_shape=jax.ShapeDtypeStruct((BATCH, SEQ, HY_WIDTH), BF16),
        grid=(BATCH, HY_WIDTH // HY_CB),
        in_specs=[pl.BlockSpec((None, TL, HY_CB), lambda b, j: (b, 0, j)),
                  x0, x1, vh, w0, w1, wv, b0, b1, bv, vec, vec],
        out_specs=pl.BlockSpec((None, SEQ, HY_CB), lambda b, j: (b, 0, j)),
        compiler_params=_params(("parallel", "parallel"), 56),
        name="ymix",
    )(y3, p3, p3, p3, conv_w, conv_w, conv_w, conv_b, conv_b, conv_b,
      hyena_d.reshape(1, -1), hyena_norm.reshape(1, -1))


N_SUB = CHUNK // SUB
SAFE_BLOCK_DECAY = -60.0


def _scores_exact(q, k, b, rev, ones, row, col):
    rsub = row % SUB
    terms = []
    for d in range(SUB):
        if d == 0:
            kr, br = k, b
        else:
            sh = CHUNK - d if rev else d
            kr, br = pltpu.roll(k, sh, 0), pltpu.roll(b, sh, 0)
        valid = (rsub + d < SUB) if rev else (rsub >= d)
        t = q * kr * jnp.exp(jnp.minimum(b - br, 0.0))
        terms.append(jnp.where(valid, t, 0.0).astype(BF16))
    sums = jnp.dot(jnp.concatenate(terms, axis=0), ones, preferred_element_type=F32)
    a = jnp.zeros((CHUNK, CHUNK), F32)
    for d in range(SUB):
        tgt = row + d if rev else row - d
        a = jnp.where(col == tgt, sums[d * CHUNK:(d + 1) * CHUNK, :CHUNK], a)

    rblk = row // SUB
    cblk = col // SUB
    for jb in (range(1, N_SUB) if rev else range(N_SUB - 1)):
        e = jb * SUB if rev else jb * SUB + SUB - 1
        ref = b[e:e + 1, :]
        qh = (q * jnp.exp(jnp.minimum(b - ref, 0.0))).astype(BF16)
        kh = (k * jnp.exp(jnp.minimum(ref - b, 0.0))).astype(BF16)
        pm = lax.dot_general(qh, kh, (((1,), (1,)), ((), ())), preferred_element_type=F32)
        side = jnp.where(cblk == jb, rblk, jb)
        a = jnp.where((side < jb) if rev else (side > jb), pm, a)

    bend = b[0:1, :] if rev else b[CHUNK - 1:CHUNK, :]
    return a, q * jnp.exp(b), k * jnp.exp(bend - b), bend


def _scores_fast(q, k, b, rev, row, col):
    order = list(range(N_SUB))[::-1] if rev else list(range(N_SUB))
    pos = {blk: p for p, blk in enumerate(order)}
    edge = lambda blk: blk * SUB if rev else blk * SUB + SUB - 1
    e = [b[edge(blk):edge(blk) + 1, :] for blk in order]
    s = [jnp.zeros((1, GLA_DK), F32)] + e[:-1]

    def by_row(vals):
        return jnp.concatenate([jnp.broadcast_to(vals[pos[blk]], (SUB, GLA_DK)) for blk in range(N_SUB)],
                               axis=0)

    srow, erow = by_row(s), by_row(e)
    qh = q * jnp.exp(b - srow)
    kh = k * jnp.exp(erow - b)
    kd = k * jnp.exp(srow - b)

    lhs = []
    for pj in range(N_SUB - 1):
        for blk in range(N_SUB):
            piece = qh[blk * SUB:(blk + 1) * SUB, :]
            p = pos[blk]
            if p <= pj:
                piece = jnp.zeros_like(piece)
            elif p > pj + 1:
                piece = piece * jnp.exp(s[p] - e[pj])
            lhs.append(piece.astype(BF16))
    contract = (((1,), (1,)), ((), ()))
    cross = lax.dot_general(jnp.concatenate(lhs, axis=0), kh.astype(BF16), contract,
                            preferred_element_type=F32)
    diag = lax.dot_general(qh.astype(BF16), kd.astype(BF16), contract, preferred_element_type=F32)

    rblk = row // SUB
    cblk = col // SUB
    a = jnp.zeros((CHUNK, CHUNK), F32)
    for pj in range(N_SUB - 1):
        a = jnp.where(cblk == order[pj], cross[pj * CHUNK:(pj + 1) * CHUNK], a)
    causal = (col >= row) if rev else (col <= row)
    a = jnp.where(cblk == rblk, jnp.where(causal, diag, 0.0), a)

    bend = e[-1]
    return a, qh * jnp.exp(srow), kh * jnp.exp(bend - erow), bend


GLA_RB = 384


def _gla_sweep_body(*refs, rev):
    if rev:
        q_ref, k_ref, v_ref, g_ref, o_ref, st_ref = refs
    else:
        q_ref, k_ref, v_ref, g_ref, og_ref, ob_ref, gain_ref, o_ref, st_ref = refs

    @pl.when(pl.program_id(1) == 0)
    def _():
        st_ref[...] = jnp.zeros_like(st_ref)

    row = lax.broadcasted_iota(jnp.int32, (CHUNK, 1), 0)
    col = lax.broadcasted_iota(jnp.int32, (CHUNK, CHUNK), 1)
    rr = lax.broadcasted_iota(jnp.int32, (CHUNK, CHUNK), 0)
    tri = ((col >= rr) if rev else (col <= rr)).astype(F32)
    ones = jnp.ones((GLA_DK, GLA_DK), BF16)
    n_chunk = GLA_RB // CHUNK

    def run(fast):
        def step(t, carry):
            c = n_chunk - 1 - t if rev else t
            rows = pl.ds(pl.multiple_of(c * CHUNK, CHUNK), CHUNK)
            b_all = jnp.dot(tri, g_ref[rows, :], precision=lax.Precision.HIGHEST, preferred_element_type=F32)
            pending = []
            for h in range(GLA_HEADS):
                kc = slice(h * GLA_DK, (h + 1) * GLA_DK)
                vc = slice(h * GLA_DV, (h + 1) * GLA_DV)
                q = q_ref[rows, kc].astype(F32) * (GLA_DK ** -0.5)
                k = k_ref[rows, kc].astype(F32)
                v = v_ref[rows, vc].astype(BF16)
                b = b_all[:, kc]
                if fast:
                    a, qt, kt, bend = _scores_fast(q, k, b, rev, row, col)
                else:
                    a, qt, kt, bend = _scores_exact(q, k, b, rev, ones, row, col)
                st = st_ref[h]
                o = lax.dot_general(qt.astype(BF16), st.astype(BF16), (((1,), (1,)), ((), ())),
                                    preferred_element_type=F32)
                st_ref[h] = st * jnp.exp(bend) + lax.dot_general(
                    v, kt.astype(BF16), (((0,), (0,)), ((), ())), preferred_element_type=F32)
                pending.append((vc, o, a, v))
            for vc, o, a, v in pending:
                o = o + jnp.dot(a.astype(BF16), v, preferred_element_type=F32)
                if rev:
                    o_ref[rows, vc] = o
                else:
                    o = _rms(o + ob_ref[rows, vc]) * gain_ref[...]
                    o_ref[rows, vc] = (o * _silu(og_ref[rows, vc].astype(F32))).astype(BF16)
            return carry

        lax.fori_loop(0, n_chunk, step, 0)

    low = jnp.min(jnp.sum(g_ref[...].reshape(GLA_RB // SUB, SUB, GLA_KEY_WIDTH), axis=1))
    safe = low > SAFE_BLOCK_DECAY
    pl.when(safe)(lambda: run(True))
    pl.when(jnp.logical_not(safe))(lambda: run(False))


def _gla_sweep(p3, lg3, rev, ob=None, gla_norm=None):
    nb = TL // GLA_RB
    blk = (lambda i: nb - 1 - i) if rev else (lambda i: i)
    key_blocks = P_MAIN // GLA_KEY_WIDTH
    q_col = 3 * HY_WIDTH // GLA_KEY_WIDTH
    v_col = (3 * HY_WIDTH + 2 * GLA_KEY_WIDTH) // GLA_WIDTH
    assert key_blocks * GLA_KEY_WIDTH == P_MAIN
    narrow = lambda col: pl.BlockSpec((None, GLA_RB, GLA_KEY_WIDTH), lambda b, i: (b, blk(i), col))
    wide = lambda col: pl.BlockSpec((None, GLA_RB, GLA_WIDTH), lambda b, i: (b, blk(i), col))
    in_specs = [narrow(q_col), narrow(q_col + 1), wide(v_col), narrow(1 if rev else 0)]
    args = [p3, p3, p3, lg3]
    if not rev:
        in_specs += [wide(v_col + 1), wide(0), pl.BlockSpec((1, GLA_DV), lambda b, i: (0, 0))]
        args += [p3, ob, gla_norm.reshape(1, -1)]
    return pl.pallas_call(
        functools.partial(_gla_sweep_body, rev=rev),
        out_shape=jax.ShapeDtypeStruct((BATCH, TL, GLA_WIDTH), F32 if rev else BF16),
        grid=(BATCH, nb),
        in_specs=in_specs,
        out_specs=wide(0),
        scratch_shapes=[pltpu.VMEM((GLA_HEADS, GLA_DV, GLA_DK), F32)],
        compiler_params=_params(("parallel", "arbitrary"), 32),
        name="gla_down" if rev else "gla_up",
    )(*args)


OUT_TM = 512


def _outproj_body(h_ref, yh_ref, yg_ref, wh_ref, wg_ref, o_ref):
    o_ref[...] = (h_ref[...] + jnp.dot(yh_ref[...], wh_ref[...], preferred_element_type=F32)
                  + jnp.dot(yg_ref[...], wg_ref[...], preferred_element_type=F32))


def _outproj(hbuf, yh, ygbuf, w_out):
    wo = _to_bf16(w_out)
    half = lambda i: pl.BlockSpec((HY_WIDTH, D_MODEL), lambda r: (i, 0))
    return pl.pallas_call(
        _outproj_body,
        out_shape=jax.ShapeDtypeStruct((X_ROWS, D_MODEL), F32),
        grid=(X_ROWS // OUT_TM,),
        in_specs=[pl.BlockSpec((pl.Element(OUT_TM), pl.Element(D_MODEL)), lambda r: (_x_row(r, OUT_TM), 0)),
                  pl.BlockSpec((OUT_TM, HY_WIDTH), lambda r: (r, 0)),
                  pl.BlockSpec((pl.Element(OUT_TM), pl.Element(GLA_WIDTH)),
                               lambda r: (_x_row(r, OUT_TM, BF16_SUBLANE), 0)),
                  half(0), half(1)],
        out_specs=pl.BlockSpec((OUT_TM, D_MODEL), lambda r: (r, 0)),
        compiler_params=_params(("parallel",), 48),
        name="outproj",
    )(hbuf, yh, ygbuf, wo, wo)


def _long_conv(u3, filt):
    f1, f3, mf, mi, mirror = _dft_tables()
    as_n1 = lambda a: a.reshape(2 * FFT_R1, FFT_N2, HY_WIDTH)
    by_k1 = lambda a: a.reshape(2, FFT_N1, FFT_N2, HY_WIDTH)
    zs = _lmm(f1, as_n1(filt), BF16, name="lmm_g")
    a = _lmm(f1, as_n1(u3), BF16, name="lmm_fwd")
    bm = _spec(mf, mirror, mi, by_k1(a), by_k1(zs))
    y = _lmm(f3, bm.reshape(2 * FFT_N1, FFT_N2, HY_WIDTH), F32, scale=1.0 / NFFT, name="lmm_inv")
    return y.reshape(BATCH, TFFT, HY_WIDTH)


def kernel(x, meta_tokens, ffn1_norm, ffn1_w_gate, ffn1_w_up, ffn1_w_down, mix_norm, w_in, conv_w, conv_b,
           filt_w1, filt_b1, filt_w2, filt_b2, filt_w3, filt_freq, hyena_d, hyena_norm, gk_w2, gk_b2,
           gla_norm, w_out, ffn2_norm, ffn2_w_gate, ffn2_w_up, ffn2_w_down, final_norm):
    assert x.shape == (BATCH, SEQ, D_MODEL) and ffn1_norm.shape[0] == 1

    hbuf = _ffn(x.reshape(X_ROWS, D_MODEL), ffn1_norm[0], ffn1_w_gate[0], ffn1_w_up[0], ffn1_w_down[0],
                final_norm, final=False, shared_rows=_meta_rows(meta_tokens))

    p, lg = _inproj(hbuf, _inproj_weights(mix_norm[0], w_in[0], gk_w2[0], gk_b2[0]))
    p3 = p.reshape(BATCH, TL, P_MAIN)
    lg3 = lg.reshape(BATCH, TL, 2 * GLA_KEY_WIDTH)

    filt = _filters(filt_w1[0], filt_b1[0], filt_w2[0], filt_b2[0], filt_w3[0], filt_freq[0])
    cw, cb = conv_w[0], conv_b[0].reshape(1, -1)
    y3 = _long_conv(_uconv(p3, cw, cb), filt)
    yh = _ymix(y3, p3, cw, cb, hyena_d[0], hyena_norm[0])
    yg = _gla_sweep(p3, lg3, rev=False, ob=_gla_sweep(p3, lg3, rev=True), gla_norm=gla_norm[0])

    h2 = _outproj(hbuf, yh.reshape(X_ROWS, HY_WIDTH), yg.reshape(ROWS, GLA_WIDTH), w_out[0])
    out = _ffn(h2, ffn2_norm[0], ffn2_w_gate[0], ffn2_w_up[0], ffn2_w_down[0], final_norm, final=True)
    return out.reshape(BATCH, SEQ, D_MODEL)
```

```python
import functools

import numpy as np
import jax
import jax.numpy as jnp
from jax import lax
from jax.experimental import pallas as pl
from jax.experimental.pallas import tpu as pltpu

F32 = jnp.float32
BF16 = jnp.bfloat16

D_MODEL = 2048
BATCH = 2
SEQ = 4096
N_META = 16
L_TOK = SEQ + N_META
PAD = 112
X0 = PAD + N_META
TL = PAD + L_TOK
ROWS = BATCH * TL
X_ROWS = BATCH * SEQ
HY_WIDTH = 1024
HY_GROUPS = 8
HY_GROUP = HY_WIDTH // HY_GROUPS
FILT_EMB = 33
FILT_BANDS = 16
FILT_HIDDEN = 64
GLA_WIDTH = 1024
GLA_HEADS = 4
GLA_KEY_WIDTH = 512
GLA_DK = 128
GLA_DV = 256
GATE_RANK = 16
GATE_NORMALIZER = 16.0
CHUNK = 64
SUB = 16
D_FF = 5632
P_MAIN = 3 * HY_WIDTH + 2 * GLA_KEY_WIDTH + 2 * GLA_WIDTH
EPS = 1e-6

FFT_N1 = 72
FFT_N2 = 128
NFFT = FFT_N1 * FFT_N2
FFT_R1 = 40
TFFT = FFT_R1 * FFT_N2

MIB = 1024 * 1024
SUBLANE = 8
BF16_SUBLANE = 16
LANE = 128


def _params(sem, vmem_mib):
    return pltpu.CompilerParams(dimension_semantics=sem, vmem_limit_bytes=vmem_mib * MIB)


def _rms(x):
    return x * lax.rsqrt(jnp.mean(x * x, axis=-1, keepdims=True) + EPS)


def _silu(x):
    return x * jax.nn.sigmoid(x)


LAY_TM = 528


def _x_row(i, tm, unit=SUBLANE):
    per_batch = SEQ // tm
    r = (i // per_batch) * (TL // unit) + X0 // unit + (i % per_batch) * (tm // unit)
    return pl.multiple_of(r * unit, unit)


def _lay_src_row(i):
    per_batch = TL // LAY_TM
    r = (i // per_batch) * (SEQ // SUBLANE) + jnp.maximum(
        (i % per_batch) * (LAY_TM // SUBLANE) - X0 // SUBLANE, 0)
    return pl.multiple_of(r * SUBLANE, SUBLANE)


def _meta_rows(meta_tokens):
    return jnp.concatenate([jnp.zeros((PAD, D_MODEL), F32), meta_tokens.astype(F32)], axis=0)


FFN_TM = 512
FFN_TF = 512


FFN_HEAD_TF = 256


def _ffn_begin(x_ref, shared_ref, xin_ref, gain_ref, xn_ref, acc_ref, first):
    if shared_ref is not None:
        @pl.when(first)
        def _():
            xin_ref[0:X0, :] = shared_ref[...]
            xin_ref[X0:LAY_TM, :] = x_ref[0:LAY_TM - X0, :]

        @pl.when(jnp.logical_not(first))
        def _():
            xin_ref[...] = x_ref[...]

    xn_ref[...] = (_rms(xin_ref[...]) * gain_ref[...]).astype(BF16)
    acc_ref[...] = jnp.zeros_like(acc_ref)


def _ffn_step(xn_ref, wg, wu, wd, acc_ref):
    xn = xn_ref[...]
    g = jnp.dot(xn, wg, preferred_element_type=F32)
    u = jnp.dot(xn, wu, preferred_element_type=F32)
    a = (_silu(g) * u).astype(BF16)
    acc_ref[...] += jnp.dot(a, wd, preferred_element_type=F32)


def _ffn_result(xin_ref, acc_ref, fgain_ref, final):
    h = xin_ref[...] + 0.5 * acc_ref[...]
    return _rms(h) * fgain_ref[...] if final else h


def _ffn_head_body(x_ref, gain_ref, wg_ref, wu_ref, wd_ref, fgain_ref, *rest, final, layout):
    if layout:
        shared_ref, o_ref, wg16_ref, wu16_ref, wd16_ref, xn_ref, acc_ref, xin_ref = rest
    else:
        o_ref, wg16_ref, wu16_ref, wd16_ref, xn_ref, acc_ref = rest
        shared_ref, xin_ref = None, x_ref
    j = pl.program_id(0)

    @pl.when(j == 0)
    def _():
        _ffn_begin(x_ref, shared_ref, xin_ref, gain_ref, xn_ref, acc_ref, True)

    wg, wu, wd = wg_ref[...].astype(BF16), wu_ref[...].astype(BF16), wd_ref[...].astype(BF16)
    wg16_ref[...] = wg
    wu16_ref[...] = wu
    wd16_ref[...] = wd
    _ffn_step(xn_ref, wg, wu, wd, acc_ref)

    @pl.when(j == pl.num_programs(0) - 1)
    def _():
        o_ref[...] = _ffn_result(xin_ref, acc_ref, fgain_ref, final)


def _ffn_body(x_ref, gain_ref, wg_ref, wu_ref, wd_ref, fgain_ref, head_ref, *rest, final, layout):
    if layout:
        shared_ref, o_ref, xn_ref, acc_ref, xin_ref = rest
    else:
        o_ref, xn_ref, acc_ref = rest
        shared_ref, xin_ref = None, x_ref
    i, j = pl.program_id(0), pl.program_id(1)
    last = j == pl.num_programs(1) - 1

    @pl.when(i > 0)
    def _():
        @pl.when(j == 0)
        def _():
            _ffn_begin(x_ref, shared_ref, xin_ref, gain_ref, xn_ref, acc_ref, i % (TL // LAY_TM) == 0)

        _ffn_step(xn_ref, wg_ref[...], wu_ref[...], wd_ref[...], acc_ref)

        @pl.when(last)
        def _():
            o_ref[...] = _ffn_result(xin_ref, acc_ref, fgain_ref, final)

    @pl.when(jnp.logical_and(i == 0, last))
    def _():
        o_ref[...] = head_ref[...]


CAST_STEPS = 4


def _cast_body(x_ref, o_ref):
    o_ref[...] = x_ref[...].astype(BF16)


def _to_bf16(w, cols=None):
    rows = w.shape[0]
    cols = w.shape[1] if cols is None else cols
    tr = rows // CAST_STEPS
    assert tr * CAST_STEPS == rows and tr % BF16_SUBLANE == 0 and cols % LANE == 0
    return pl.pallas_call(
        _cast_body,
        out_shape=jax.ShapeDtypeStruct((rows, cols), BF16),
        grid=(CAST_STEPS,),
        in_specs=[pl.BlockSpec((tr, cols), lambda i: (i, 0))],
        out_specs=pl.BlockSpec((tr, cols), lambda i: (i, 0)),
        compiler_params=_params(("parallel",), 48),
        name="to_bf16",
    )(w)


def _ffn(x2, gain, wg, wu, wd, fgain, final, shared_rows=None):
    layout = shared_rows is not None
    tm = LAY_TM if layout else FFN_TM
    rows = ROWS if layout else X_ROWS
    gain, fgain = gain.reshape(1, -1), fgain.reshape(1, -1)
    scratch = [pltpu.VMEM((tm, D_MODEL), BF16), pltpu.VMEM((tm, D_MODEL), F32)]
    extra = ()
    if layout:
        extra = (shared_rows,)
        scratch.append(pltpu.VMEM((tm, D_MODEL), F32))
    name = "ffn_final" if final else "ffn"

    tf = FFN_HEAD_TF
    vec1 = pl.BlockSpec((1, D_MODEL), lambda j: (0, 0))
    wide1 = pl.BlockSpec((D_MODEL, tf), lambda j: (0, j))
    tall1 = pl.BlockSpec((tf, D_MODEL), lambda j: (j, 0))
    tile1 = pl.BlockSpec((tm, D_MODEL), lambda j: (0, 0))
    head, wg16, wu16, wd16 = pl.pallas_call(
        functools.partial(_ffn_head_body, final=final, layout=layout),
        out_shape=(jax.ShapeDtypeStruct((tm, D_MODEL), F32), jax.ShapeDtypeStruct(wg.shape, BF16),
                   jax.ShapeDtypeStruct(wu.shape, BF16), jax.ShapeDtypeStruct(wd.shape, BF16)),
        grid=(D_FF // tf,),
        in_specs=[tile1, vec1, wide1, wide1, tall1, vec1] + [pl.BlockSpec((X0, D_MODEL), lambda j: (0, 0))] * layout,
        out_specs=(tile1, wide1, wide1, tall1),
        scratch_shapes=scratch,
        compiler_params=_params(("arbitrary",), 56),
        name=name + "_head",
    )(x2, gain, wg, wu, wd, fgain, *extra)

    col = lambda i, j: jnp.where(i == 0, 0, j)
    vec = pl.BlockSpec((1, D_MODEL), lambda i, j: (0, 0))
    wide = pl.BlockSpec((D_MODEL, FFN_TF), lambda i, j: (0, col(i, j)))
    tall = pl.BlockSpec((FFN_TF, D_MODEL), lambda i, j: (col(i, j), 0))
    if layout:
        x_spec = pl.BlockSpec((pl.Element(tm), pl.Element(D_MODEL)), lambda i, j: (_lay_src_row(i), 0))
    else:
        x_spec = pl.BlockSpec((tm, D_MODEL), lambda i, j: (i, 0))
    return pl.pallas_call(
        functools.partial(_ffn_body, final=final, layout=layout),
        out_shape=jax.ShapeDtypeStruct((rows, D_MODEL), F32),
        grid=(rows // tm, D_FF // FFN_TF),
        in_specs=[x_spec, vec, wide, wide, tall, vec, pl.BlockSpec((tm, D_MODEL), lambda i, j: (0, 0))]
        + [pl.BlockSpec((X0, D_MODEL), lambda i, j: (0, 0))] * layout,
        out_specs=pl.BlockSpec((tm, D_MODEL), lambda i, j: (i, 0)),
        scratch_shapes=scratch,
        compiler_params=_params(("parallel", "arbitrary"), 56),
        name=name,
    )(x2, gain, wg16, wu16, wd16, fgain, head, *extra)


INP_TM = 2 * LAY_TM
INP_TN = 1024
LR_PAD = 128


def _inproj_body(x_ref, gain_ref, w_ref, wlr_ref, w2_ref, b2_ref, p_ref, lg_ref, xn_ref):
    @pl.when(pl.program_id(1) == 0)
    def _():
        xn = (_rms(x_ref[...]) * gain_ref[...]).astype(BF16)
        xn_ref[...] = xn
        lr = jnp.dot(xn, wlr_ref[...], preferred_element_type=F32).astype(BF16)
        z = jnp.dot(lr, w2_ref[...], preferred_element_type=F32) + b2_ref[...]
        lg_ref[...] = (jnp.minimum(z, 0.0) - jnp.log1p(jnp.exp(-jnp.abs(z)))) * (1.0 / GATE_NORMALIZER)

    p_ref[...] = jnp.dot(xn_ref[...], w_ref[...], preferred_element_type=F32).astype(BF16)


SPLIT_TC = 1024


def _split_w_in_body(w_ref, wlr_ref, main_ref, lr_ref):
    main_ref[...] = w_ref[...].T.astype(BF16)

    @pl.when(pl.program_id(0) == 0)
    def _():
        rows = jnp.concatenate([wlr_ref[...], jnp.zeros((LR_PAD - 2 * GATE_RANK, D_MODEL), F32)], axis=0)
        lr_ref[...] = rows.T.astype(BF16)


def _split_w_in(w_in):
    w_t = jnp.transpose(w_in)
    return pl.pallas_call(
        _split_w_in_body,
        out_shape=(jax.ShapeDtypeStruct((D_MODEL, P_MAIN), BF16), jax.ShapeDtypeStruct((D_MODEL, LR_PAD), BF16)),
        grid=(P_MAIN // SPLIT_TC,),
        in_specs=[pl.BlockSpec((SPLIT_TC, D_MODEL), lambda i: (i, 0)),
                  pl.BlockSpec((2 * GATE_RANK, D_MODEL), lambda i: (P_MAIN // (2 * GATE_RANK), 0))],
        out_specs=(pl.BlockSpec((D_MODEL, SPLIT_TC), lambda i: (0, i)),
                   pl.BlockSpec((D_MODEL, LR_PAD), lambda i: (0, 0))),
        compiler_params=_params(("arbitrary",), 48),
        name="split_w_in",
    )(w_t, w_t)


def _inproj_weights(gain, w_in, gk_w2, gk_b2):
    w_main, w_lr = _split_w_in(w_in)
    w2 = jnp.zeros((LR_PAD, 2 * GLA_KEY_WIDTH), F32)
    w2 = w2.at[:GATE_RANK, :GLA_KEY_WIDTH].set(gk_w2[0])
    w2 = w2.at[GATE_RANK:2 * GATE_RANK, GLA_KEY_WIDTH:].set(gk_w2[1]).astype(BF16)
    return gain.reshape(1, -1), w_main, w_lr, w2, gk_b2.reshape(1, 2 * GLA_KEY_WIDTH)


def _inproj(hbuf, weights):
    tm, tn = INP_TM, INP_TN
    return pl.pallas_call(
        _inproj_body,
        out_shape=(jax.ShapeDtypeStruct((ROWS, P_MAIN), BF16),
                   jax.ShapeDtypeStruct((ROWS, 2 * GLA_KEY_WIDTH), F32)),
        grid=(ROWS // tm, P_MAIN // tn),
        in_specs=[
            pl.BlockSpec((tm, D_MODEL), lambda i, j: (i, 0)),
            pl.BlockSpec((1, D_MODEL), lambda i, j: (0, 0)),
            pl.BlockSpec((D_MODEL, tn), lambda i, j: (0, j)),
            pl.BlockSpec((D_MODEL, LR_PAD), lambda i, j: (0, 0)),
            pl.BlockSpec((LR_PAD, 2 * GLA_KEY_WIDTH), lambda i, j: (0, 0)),
            pl.BlockSpec((1, 2 * GLA_KEY_WIDTH), lambda i, j: (0, 0)),
        ],
        out_specs=(pl.BlockSpec((tm, tn), lambda i, j: (i, j)),
                   pl.BlockSpec((tm, 2 * GLA_KEY_WIDTH), lambda i, j: (i, 0))),
        scratch_shapes=[pltpu.VMEM((tm, D_MODEL), BF16)],
        compiler_params=_params(("parallel", "arbitrary"), 56),
        name="inproj",
    )(hbuf, *weights)


FILT_TR = 640
FEAT_PAD = 128


def _filt_tables():
    pos = np.arange(TFFT, dtype=np.float64)
    t = pos / (L_TOK - 1)
    w = (2.0 * np.pi / L_TOK) * pos
    bands = 1e-4 + np.arange(FILT_BANDS, dtype=np.float64) * ((FILT_BANDS - 1 - 1e-4) / (FILT_BANDS - 1))
    ang = w[:, None] * bands[None, :]
    feats = np.zeros((TFFT, FEAT_PAD), np.float64)
    feats[:, 0] = t
    feats[:, 1:1 + FILT_BANDS] = np.cos(ang)
    feats[:, 1 + FILT_BANDS:FILT_EMB] = -np.sin(ang)
    lo, hi = np.log(1e-2) / 1.5, np.log(1e-2) / 0.3
    deltas = np.abs(lo + np.arange(HY_WIDTH, dtype=np.float64) * ((hi - lo) / (HY_WIDTH - 1)))
    return feats.astype(np.float32), deltas.astype(np.float32).reshape(1, HY_WIDTH)


def _filt_body(feat_ref, t_ref, w1_ref, b1_ref, w2_ref, b2_ref, w3_ref, fr_ref, dl_ref, h_ref):
    start = pl.program_id(0) * FILT_TR

    @pl.when(start < L_TOK)
    def _():
        hp = lax.Precision.HIGHEST
        fr = fr_ref[...]
        z = jnp.sin(fr * (jnp.dot(w1_ref[...], feat_ref[...], precision=hp, preferred_element_type=F32)
                          + b1_ref[...]))
        z = jnp.sin(fr * (jnp.dot(w2_ref[...], z, precision=hp, preferred_element_type=F32) + b2_ref[...]))
        hh = lax.dot_general(z, w3_ref[...], (((0,), (0,)), ((), ())), precision=hp,
                             preferred_element_type=F32)
        pos = start + lax.broadcasted_iota(jnp.int32, (FILT_TR, 1), 0)
        win = jnp.exp(-t_ref[...] * dl_ref[...])
        win = jnp.where(pos < L_TOK, win, 0.0)
        hf = hh[:, :HY_WIDTH] * win
        hb = hh[:, HY_WIDTH:] * win
        h_ref[0] = jnp.where(pos == 0, hf + hb, hf)
        h_ref[1] = jnp.where(pos == 0, 0.0, hb)

    @pl.when(start >= L_TOK)
    def _():
        h_ref[...] = jnp.zeros_like(h_ref)


def _filters(w1, b1, w2, b2, w3, freq):
    feats, deltas = _filt_tables()
    w1t = jnp.pad(w1, ((0, FEAT_PAD - FILT_EMB), (0, 0))).T
    col = lambda v: v.reshape(-1, 1)
    full = lambda shape: pl.BlockSpec(shape, lambda i: (0, 0))
    return pl.pallas_call(
        _filt_body,
        out_shape=jax.ShapeDtypeStruct((2, TFFT, HY_WIDTH), F32),
        grid=(TFFT // FILT_TR,),
        in_specs=[
            pl.BlockSpec((FEAT_PAD, FILT_TR), lambda i: (0, i)),
            pl.BlockSpec((FILT_TR, 1), lambda i: (i, 0)),
            full((FILT_HIDDEN, FEAT_PAD)), full((FILT_HIDDEN, 1)),
            full((FILT_HIDDEN, FILT_HIDDEN)), full((FILT_HIDDEN, 1)),
            full((FILT_HIDDEN, 2 * HY_WIDTH)), full((FILT_HIDDEN, 1)), full((1, HY_WIDTH)),
        ],
        out_specs=pl.BlockSpec((2, FILT_TR, HY_WIDTH), lambda i: (0, i, 0)),
        compiler_params=_params(("parallel",), 40),
        name="filt",
    )(jnp.asarray(feats.T), jnp.asarray(feats[:, 0:1]), w1t, col(b1), w2.T, col(b2), w3, col(freq),
      jnp.asarray(deltas))


def _dft_tables_np():
    n1 = np.arange(FFT_N1)
    ang1 = 2.0 * np.pi * ((n1[:, None] * n1[None, :]) % FFT_N1) / FFT_N1
    c1, s1 = np.cos(ang1)[:, :FFT_R1], np.sin(ang1)[:, :FFT_R1]
    f1 = np.block([[c1, s1], [-s1, c1]])
    f3 = np.block([[c1.T, -s1.T], [s1.T, c1.T]])
    k1 = np.arange(FFT_N1)[:, None, None]
    k2 = np.arange(FFT_N2)[None, :, None]
    n2 = np.arange(FFT_N2)[None, None, :]
    ang2 = 2.0 * np.pi * ((n2 * (k1 + FFT_N1 * k2)) % NFFT) / NFFT
    c2, s2 = np.cos(ang2), np.sin(ang2)
    mf = np.concatenate([np.concatenate([c2, s2], axis=2), np.concatenate([-s2, c2], axis=2)], axis=1)
    c2t, s2t = np.swapaxes(c2, 1, 2), np.swapaxes(s2, 1, 2)
    mi = np.concatenate([np.concatenate([c2t, -s2t], axis=2), np.concatenate([s2t, c2t], axis=2)], axis=1)
    k1r = np.arange(FFT_N1)
    k2r = np.arange(FFT_N2)
    perm = np.where(k1r[:, None] == 0, (FFT_N2 - k2r[None, :]) % FFT_N2, FFT_N2 - 1 - k2r[None, :])
    rows = np.concatenate([perm, perm + FFT_N2], axis=1)
    mirror = mf[((FFT_N1 - k1r) % FFT_N1)[:, None], rows, :]
    return f1, f3, mf, mi, mirror


def _dft_tables():
    return tuple(jnp.asarray(a.astype(np.float32)).astype(BF16) for a in _dft_tables_np())


LMM_PITCH = FFT_N2 + SUBLANE
LMM_MC = 2 * FFT_N1


def _lmm_body(f_ref, x_ref, o_ref, xs_ref, os_ref, *, scale):
    m, k = f_ref.shape
    for g in range(k):
        xs_ref[pl.ds(g * LMM_PITCH, FFT_N2), :] = x_ref[pl.ds(g * FFT_N2, FFT_N2), :].astype(F32)
    for m0 in range(0, m, LMM_MC):
        f = f_ref[m0:min(m0 + LMM_MC, m), :]

        def slab(s, carry):
            x = xs_ref[pl.ds(s, k, stride=LMM_PITCH), :].astype(BF16)
            r = jnp.dot(f, x, preferred_element_type=F32)
            os_ref[pl.ds(s, f.shape[0], stride=LMM_PITCH), :] = r * scale if scale != 1.0 else r
            return carry

        lax.fori_loop(0, FFT_N2, slab, 0, unroll=8)
        for g in range(f.shape[0]):
            o_ref[pl.ds((m0 + g) * FFT_N2, FFT_N2), :] = os_ref[pl.ds(g * LMM_PITCH, FFT_N2), :].astype(
                o_ref.dtype)


def _lmm(f, x3, out_dtype, scale=1.0, name="lmm"):
    m, k = f.shape
    mc = min(m, LMM_MC)
    assert m % mc == 0
    out = pl.pallas_call(
        functools.partial(_lmm_body, scale=scale),
        out_shape=jax.ShapeDtypeStruct((m * FFT_N2, HY_WIDTH), out_dtype),
        grid=(HY_WIDTH // LANE,),
        in_specs=[pl.BlockSpec((m, k), lambda j: (0, 0)),
                  pl.BlockSpec((k * FFT_N2, LANE), lambda j: (0, j))],
        out_specs=pl.BlockSpec((m * FFT_N2, LANE), lambda j: (0, j)),
        scratch_shapes=[pltpu.VMEM((k * LMM_PITCH, LANE), F32), pltpu.VMEM((mc * LMM_PITCH, LANE), F32)],
        compiler_params=_params(("parallel",), 56),
        name=name,
    )(f, x3.reshape(k * FFT_N2, HY_WIDTH))
    return out.reshape(m, FFT_N2, HY_WIDTH)


def _spec_body(mf_ref, mr_ref, mi_ref, a_ref, z_ref, zm_ref, o_ref):
    mf = mf_ref[...]
    flat = lambda ref: ref[...].reshape(2 * FFT_N2, HY_WIDTH)
    z = jnp.dot(mf, flat(z_ref), preferred_element_type=F32)
    zm = jnp.dot(mr_ref[...], flat(zm_ref), preferred_element_type=F32)
    a, b = z[:FFT_N2], z[FFT_N2:]
    am, bm = zm[:FFT_N2], zm[FFT_N2:]
    gr, gi = 0.5 * (a + am + b + bm), 0.5 * (b - bm + a - am)

    x = jnp.dot(mf, flat(a_ref), preferred_element_type=F32)
    xr, xi = x[:FFT_N2], x[FFT_N2:]
    y = jnp.concatenate([xr * gr - xi * gi, xr * gi + xi * gr], axis=0).astype(BF16)
    o_ref[...] = jnp.dot(mi_ref[...], y, preferred_element_type=F32).astype(BF16).reshape(
        2, 1, FFT_N2, HY_WIDTH)


def _spec(mf, mirror, mi, a, zs):
    blk = lambda at: pl.BlockSpec((2, 1, FFT_N2, HY_WIDTH), lambda i: (0, at(i), 0, 0))
    here = blk(lambda i: i)
    mat = pl.BlockSpec((None, 2 * FFT_N2, 2 * FFT_N2), lambda i: (i, 0, 0))
    return pl.pallas_call(
        _spec_body,
        out_shape=jax.ShapeDtypeStruct((2, FFT_N1, FFT_N2, HY_WIDTH), BF16),
        grid=(FFT_N1,),
        in_specs=[mat, mat, mat, here, here, blk(lambda i: (FFT_N1 - i) % FFT_N1)],
        out_specs=here,
        compiler_params=_params(("parallel",), 32),
        name="spec",
    )(mf, mirror, mi, a, zs, zs)


HY_CB = 256


def _short_conv(p_ref, w_ref, b_ref):
    p = p_ref[...].astype(F32)
    w = w_ref[...]
    prev = pltpu.roll(p, 1, 0)
    nxt = pltpu.roll(p, TL - 1, 0)
    return b_ref[...] + prev * w[0:1] + p * w[1:2] + nxt * w[2:3]


def _uconv_body(x1_ref, vh_ref, w1_ref, wv_ref, b1_ref, bv_ref, u_ref):
    u = _short_conv(vh_ref, wv_ref, bv_ref) * _short_conv(x1_ref, w1_ref, b1_ref)
    row = lax.broadcasted_iota(jnp.int32, (TL, 1), 0)
    u_ref[pl.ds(0, TL), :] = jnp.where(row >= PAD, u, 0.0)
    u_ref[pl.ds(TL, TFFT - TL), :] = jnp.zeros((TFFT - TL, HY_CB), F32)


def _hy_specs(first_block):
    nb = HY_WIDTH // HY_CB
    return (pl.BlockSpec((None, TL, HY_CB), lambda b, j: (b, 0, first_block * nb + j)),
            pl.BlockSpec((3, HY_CB), lambda b, j: (0, first_block * nb + j)),
            pl.BlockSpec((1, HY_CB), lambda b, j: (0, first_block * nb + j)))


def _uconv(p3, conv_w, conv_b):
    x1, w1, b1 = _hy_specs(1)
    vh, wv, bv = _hy_specs(2)
    return pl.pallas_call(
        _uconv_body,
        out_shape=jax.ShapeDtypeStruct((BATCH, TFFT, HY_WIDTH), F32),
        grid=(BATCH, HY_WIDTH // HY_CB),
        in_specs=[x1, vh, w1, wv, b1, bv],
        out_specs=pl.BlockSpec((None, TFFT, HY_CB), lambda b, j: (b, 0, j)),
        compiler_params=_params(("parallel", "parallel"), 48),
        name="uconv",
    )(p3, p3, conv_w, conv_w, conv_b, conv_b)


def _ymix_body(y_ref, x0_ref, x1_ref, vh_ref, w0_ref, w1_ref, wv_ref, b0_ref, b1_ref, bv_ref,
               d_ref, gain_ref, o_ref):
    u = _short_conv(vh_ref, wv_ref, bv_ref) * _short_conv(x1_ref, w1_ref, b1_ref)
    yy = (y_ref[...] + d_ref[...] * u) * _short_conv(x0_ref, w0_ref, b0_ref)
    gain = gain_ref[...]
    for s in range(0, HY_CB, HY_GROUP):
        o_ref[:, s:s + HY_GROUP] = (_rms(yy[X0:, s:s + HY_GROUP]) * gain[:, s:s + HY_GROUP]).astype(BF16)


def _ymix(y3, p3, conv_w, conv_b, hyena_d, hyena_norm):
    x0, w0, b0 = _hy_specs(0)
    x1, w1, b1 = _hy_specs(1)
    vh, wv, bv = _hy_specs(2)
    vec = pl.BlockSpec((1, HY_CB), lambda b, j: (0, j))
    return pl.pallas_call(
        _ymix_body,
        out_shape=jax.ShapeDtypeStruct((BATCH, SEQ, HY_WIDTH), BF16),
        grid=(BATCH, HY_WIDTH // HY_CB),
        in_specs=[pl.BlockSpec((None, TL, HY_CB), lambda b, j: (b, 0, j)),
                  x0, x1, vh, w0, w1, wv, b0, b1, bv, vec, vec],
        out_specs=pl.BlockSpec((None, SEQ, HY_CB), lambda b, j: (b, 0, j)),
        compiler_params=_params(("parallel", "parallel"), 56),
        name="ymix",
    )(y3, p3, p3, p3, conv_w, conv_w, conv_w, conv_b, conv_b, conv_b,
      hyena_d.reshape(1, -1), hyena_norm.reshape(1, -1))


N_SUB = CHUNK // SUB
SAFE_BLOCK_DECAY = -60.0


def _scores_exact(q, k, b, rev, ones, row, col):
    rsub = row % SUB
    terms = []
    for d in range(SUB):
        if d == 0:
            kr, br = k, b
        else:
            sh = CHUNK - d if rev else d
            kr, br = pltpu.roll(k, sh, 0), pltpu.roll(b, sh, 0)
        valid = (rsub + d < SUB) if rev else (rsub >= d)
        t = q * kr * jnp.exp(jnp.minimum(b - br, 0.0))
        terms.append(jnp.where(valid, t, 0.0).astype(BF16))
    sums = jnp.dot(jnp.concatenate(terms, axis=0), ones, preferred_element_type=F32)
    a = jnp.zeros((CHUNK, CHUNK), F32)
    for d in range(SUB):
        tgt = row + d if rev else row - d
        a = jnp.where(col == tgt, sums[d * CHUNK:(d + 1) * CHUNK, :CHUNK], a)

    rblk = row // SUB
    cblk = col // SUB
    for jb in (range(1, N_SUB) if rev else range(N_SUB - 1)):
        e = jb * SUB if rev else jb * SUB + SUB - 1
        ref = b[e:e + 1, :]
        qh = (q * jnp.exp(jnp.minimum(b - ref, 0.0))).astype(BF16)
        kh = (k * jnp.exp(jnp.minimum(ref - b, 0.0))).astype(BF16)
        pm = lax.dot_general(qh, kh, (((1,), (1,)), ((), ())), preferred_element_type=F32)
        side = jnp.where(cblk == jb, rblk, jb)
        a = jnp.where((side < jb) if rev else (side > jb), pm, a)

    bend = b[0:1, :] if rev else b[CHUNK - 1:CHUNK, :]
    return a, q * jnp.exp(b), k * jnp.exp(bend - b), bend


def _scores_fast(q, k, b, rev, row, col):
    order = list(range(N_SUB))[::-1] if rev else list(range(N_SUB))
    pos = {blk: p for p, blk in enumerate(order)}
    edge = lambda blk: blk * SUB if rev else blk * SUB + SUB - 1
    e = [b[edge(blk):edge(blk) + 1, :] for blk in order]
    s = [jnp.zeros((1, GLA_DK), F32)] + e[:-1]

    def by_row(vals):
        return jnp.concatenate([jnp.broadcast_to(vals[pos[blk]], (SUB, GLA_DK)) for blk in range(N_SUB)],
                               axis=0)

    srow, erow = by_row(s), by_row(e)
    qh = q * jnp.exp(b - srow)
    kh = k * jnp.exp(erow - b)
    kd = k * jnp.exp(srow - b)

    lhs = []
    for pj in range(N_SUB - 1):
        for blk in range(N_SUB):
            piece = qh[blk * SUB:(blk + 1) * SUB, :]
            p = pos[blk]
            if p <= pj:
                piece = jnp.zeros_like(piece)
            elif p > pj + 1:
                piece = piece * jnp.exp(s[p] - e[pj])
            lhs.append(piece.astype(BF16))
    contract = (((1,), (1,)), ((), ()))
    cross = lax.dot_general(jnp.concatenate(lhs, axis=0), kh.astype(BF16), contract,
                            preferred_element_type=F32)
    diag = lax.dot_general(qh.astype(BF16), kd.astype(BF16), contract, preferred_element_type=F32)

    rblk = row // SUB
    cblk = col // SUB
    a = jnp.zeros((CHUNK, CHUNK), F32)
    for pj in range(N_SUB - 1):
        a = jnp.where(cblk == order[pj], cross[pj * CHUNK:(pj + 1) * CHUNK], a)
    causal = (col >= row) if rev else (col <= row)
    a = jnp.where(cblk == rblk, jnp.where(causal, diag, 0.0), a)

    bend = e[-1]
    return a, qh * jnp.exp(srow), kh * jnp.exp(bend - erow), bend


GLA_RB = 384


def _gla_sweep_body(*refs, rev):
    if rev:
        q_ref, k_ref, v_ref, g_ref, o_ref, st_ref, b_ref = refs
    else:
        q_ref, k_ref, v_ref, g_ref, og_ref, ob_ref, gain_ref, o_ref, st_ref, b_ref = refs

    @pl.when(pl.program_id(1) == 0)
    def _():
        st_ref[...] = jnp.zeros_like(st_ref)

    row = lax.broadcasted_iota(jnp.int32, (CHUNK, 1), 0)
    col = lax.broadcasted_iota(jnp.int32, (CHUNK, CHUNK), 1)
    rr = lax.broadcasted_iota(jnp.int32, (CHUNK, CHUNK), 0)
    tri = ((col >= rr) if rev else (col <= rr)).astype(BF16)
    ones = jnp.ones((GLA_DK, GLA_DK), BF16)
    n_chunk = GLA_RB // CHUNK

    for c in range(n_chunk):
        g = g_ref[c * CHUNK:(c + 1) * CHUNK, :]
        g_hi = g.astype(BF16)
        rest = g - g_hi.astype(F32)
        g_mid = rest.astype(BF16)
        g_lo = (rest - g_mid.astype(F32)).astype(BF16)
        b_ref[c * CHUNK:(c + 1) * CHUNK, :] = (
            jnp.dot(tri, g_hi, preferred_element_type=F32) + jnp.dot(tri, g_mid, preferred_element_type=F32)
            + jnp.dot(tri, g_lo, preferred_element_type=F32))

    def run(fast):
        def step(t, carry):
            c = n_chunk - 1 - t if rev else t
            rows = pl.ds(pl.multiple_of(c * CHUNK, CHUNK), CHUNK)
            b_all = b_ref[rows, :]
            pending = []
            for h in range(GLA_HEADS):
                kc = slice(h * GLA_DK, (h + 1) * GLA_DK)
                vc = slice(h * GLA_DV, (h + 1) * GLA_DV)
                q = q_ref[rows, kc].astype(F32) * (GLA_DK ** -0.5)
                k = k_ref[rows, kc].astype(F32)
                v = v_ref[rows, vc].astype(BF16)
                b = b_all[:, kc]
                if fast:
                    a, qt, kt, bend = _scores_fast(q, k, b, rev, row, col)
                else:
                    a, qt, kt, bend = _scores_exact(q, k, b, rev, ones, row, col)
                st = st_ref[h]
                o = lax.dot_general(qt.astype(BF16), st.astype(BF16), (((1,), (1,)), ((), ())),
                                    preferred_element_type=F32)
                st_ref[h] = st * jnp.exp(bend) + lax.dot_general(
                    v, kt.astype(BF16), (((0,), (0,)), ((), ())), preferred_element_type=F32)
                pending.append((vc, o, a, v))
            for vc, o, a, v in pending:
                o = o + jnp.dot(a.astype(BF16), v, preferred_element_type=F32)
                if rev:
                    o_ref[rows, vc] = o
                else:
                    o = _rms(o + ob_ref[rows, vc]) * gain_ref[...]
                    o_ref[rows, vc] = (o * _silu(og_ref[rows, vc].astype(F32))).astype(BF16)
            return carry

        lax.fori_loop(0, n_chunk, step, 0, unroll=3 if fast else 1)

    low = jnp.min(jnp.sum(g_ref[...].reshape(GLA_RB // SUB, SUB, GLA_KEY_WIDTH), axis=1))
    safe = low > SAFE_BLOCK_DECAY
    pl.when(safe)(lambda: run(True))
    pl.when(jnp.logical_not(safe))(lambda: run(False))


def _gla_sweep(p3, lg3, rev, ob=None, gla_norm=None):
    nb = TL // GLA_RB
    blk = (lambda i: nb - 1 - i) if rev else (lambda i: i)
    key_blocks = P_MAIN // GLA_KEY_WIDTH
    q_col = 3 * HY_WIDTH // GLA_KEY_WIDTH
    v_col = (3 * HY_WIDTH + 2 * GLA_KEY_WIDTH) // GLA_WIDTH
    assert key_blocks * GLA_KEY_WIDTH == P_MAIN
    narrow = lambda col: pl.BlockSpec((None, GLA_RB, GLA_KEY_WIDTH), lambda b, i: (b, blk(i), col))
    wide = lambda col: pl.BlockSpec((None, GLA_RB, GLA_WIDTH), lambda b, i: (b, blk(i), col))
    in_specs = [narrow(q_col), narrow(q_col + 1), wide(v_col), narrow(1 if rev else 0)]
    args = [p3, p3, p3, lg3]
    if not rev:
        in_specs += [wide(v_col + 1), wide(0), pl.BlockSpec((1, GLA_DV), lambda b, i: (0, 0))]
        args += [p3, ob, gla_norm.reshape(1, -1)]
    return pl.pallas_call(
        functools.partial(_gla_sweep_body, rev=rev),
        out_shape=jax.ShapeDtypeStruct((BATCH, TL, GLA_WIDTH), F32 if rev else BF16),
        grid=(BATCH, nb),
        in_specs=in_specs,
        out_specs=wide(0),
        scratch_shapes=[pltpu.VMEM((GLA_HEADS, GLA_DV, GLA_DK), F32),
                        pltpu.VMEM((GLA_RB, GLA_KEY_WIDTH), F32)],
        compiler_params=_params(("parallel", "arbitrary"), 32),
        name="gla_down" if rev else "gla_up",
    )(*args)


OUT_TM = 512


def _outproj_body(h_ref, yh_ref, yg_ref, wh_ref, wg_ref, o_ref):
    o_ref[...] = (h_ref[...] + jnp.dot(yh_ref[...], wh_ref[...], preferred_element_type=F32)
                  + jnp.dot(yg_ref[...], wg_ref[...], preferred_element_type=F32))


def _outproj(hbuf, yh, ygbuf, w_out):
    wo = _to_bf16(w_out)
    half = lambda i: pl.BlockSpec((HY_WIDTH, D_MODEL), lambda r: (i, 0))
    return pl.pallas_call(
        _outproj_body,
        out_shape=jax.ShapeDtypeStruct((X_ROWS, D_MODEL), F32),
        grid=(X_ROWS // OUT_TM,),
        in_specs=[pl.BlockSpec((pl.Element(OUT_TM), pl.Element(D_MODEL)), lambda r: (_x_row(r, OUT_TM), 0)),
                  pl.BlockSpec((OUT_TM, HY_WIDTH), lambda r: (r, 0)),
                  pl.BlockSpec((pl.Element(OUT_TM), pl.Element(GLA_WIDTH)),
                               lambda r: (_x_row(r, OUT_TM, BF16_SUBLANE), 0)),
                  half(0), half(1)],
        out_specs=pl.BlockSpec((OUT_TM, D_MODEL), lambda r: (r, 0)),
        compiler_params=_params(("parallel",), 48),
        name="outproj",
    )(hbuf, yh, ygbuf, wo, wo)


def _long_conv(u3, filt):
    f1, f3, mf, mi, mirror = _dft_tables()
    as_n1 = lambda a: a.reshape(2 * FFT_R1, FFT_N2, HY_WIDTH)
    by_k1 = lambda a: a.reshape(2, FFT_N1, FFT_N2, HY_WIDTH)
    zs = _lmm(f1, as_n1(filt), BF16, name="lmm_g")
    a = _lmm(f1, as_n1(u3), BF16, name="lmm_fwd")
    bm = _spec(mf, mirror, mi, by_k1(a), by_k1(zs))
    y = _lmm(f3, bm.reshape(2 * FFT_N1, FFT_N2, HY_WIDTH), F32, scale=1.0 / NFFT, name="lmm_inv")
    return y.reshape(BATCH, TFFT, HY_WIDTH)


def kernel(x, meta_tokens, ffn1_norm, ffn1_w_gate, ffn1_w_up, ffn1_w_down, mix_norm, w_in, conv_w, conv_b,
           filt_w1, filt_b1, filt_w2, filt_b2, filt_w3, filt_freq, hyena_d, hyena_norm, gk_w2, gk_b2,
           gla_norm, w_out, ffn2_norm, ffn2_w_gate, ffn2_w_up, ffn2_w_down, final_norm):
    assert x.shape == (BATCH, SEQ, D_MODEL) and ffn1_norm.shape[0] == 1

    hbuf = _ffn(x.reshape(X_ROWS, D_MODEL), ffn1_norm[0], ffn1_w_gate[0], ffn1_w_up[0], ffn1_w_down[0],
                final_norm, final=False, shared_rows=_meta_rows(meta_tokens))

    p, lg = _inproj(hbuf, _inproj_weights(mix_norm[0], w_in[0], gk_w2[0], gk_b2[0]))
    p3 = p.reshape(BATCH, TL, P_MAIN)
    lg3 = lg.reshape(BATCH, TL, 2 * GLA_KEY_WIDTH)

    filt = _filters(filt_w1[0], filt_b1[0], filt_w2[0], filt_b2[0], filt_w3[0], filt_freq[0])
    cw, cb = conv_w[0], conv_b[0].reshape(1, -1)
    y3 = _long_conv(_uconv(p3, cw, cb), filt)
    yh = _ymix(y3, p3, cw, cb, hyena_d[0], hyena_norm[0])
    yg = _gla_sweep(p3, lg3, rev=False, ob=_gla_sweep(p3, lg3, rev=True), gla_norm=gla_norm[0])

    h2 = _outproj(hbuf, yh.reshape(X_ROWS, HY_WIDTH), yg.reshape(ROWS, GLA_WIDTH), w_out[0])
    out = _ffn(h2, ffn2_norm[0], ffn2_w_gate[0], ffn2_w_up[0], ffn2_w_down[0], final_norm, final=True)
    return out.reshape(BATCH, SEQ, D_MODEL)
```

```python
import functools

import numpy as np
import jax
import jax.numpy as jnp
from jax import lax
from jax.experimental import pallas as pl
from jax.experimental.pallas import tpu as pltpu

F32 = jnp.float32
BF16 = jnp.bfloat16

D_MODEL = 2048
BATCH = 2
SEQ = 4096
N_META = 16
L_TOK = SEQ + N_META
PAD = 112
X0 = PAD + N_META
TL = PAD + L_TOK
ROWS = BATCH * TL
X_ROWS = BATCH * SEQ
HY_WIDTH = 1024
HY_GROUPS = 8
HY_GROUP = HY_WIDTH // HY_GROUPS
FILT_EMB = 33
FILT_BANDS = 16
FILT_HIDDEN = 64
GLA_WIDTH = 1024
GLA_HEADS = 4
GLA_KEY_WIDTH = 512
GLA_DK = 128
GLA_DV = 256
GATE_RANK = 16
GATE_NORMALIZER = 16.0
CHUNK = 64
SUB = 16
D_FF = 5632
P_MAIN = 3 * HY_WIDTH + 2 * GLA_KEY_WIDTH + 2 * GLA_WIDTH
EPS = 1e-6

FFT_N1 = 72
FFT_N2 = 128
NFFT = FFT_N1 * FFT_N2
FFT_R1 = 40
TFFT = FFT_R1 * FFT_N2

MIB = 1024 * 1024
SUBLANE = 8
BF16_SUBLANE = 16
LANE = 128


def _params(sem, vmem_mib):
    return pltpu.CompilerParams(dimension_semantics=sem, vmem_limit_bytes=vmem_mib * MIB)


def _rms(x):
    return x * lax.rsqrt(jnp.mean(x * x, axis=-1, keepdims=True) + EPS)


def _silu(x):
    return x * jax.nn.sigmoid(x)


LAY_TM = 528


def _x_row(i, tm, unit=SUBLANE):
    per_batch = SEQ // tm
    r = (i // per_batch) * (TL // unit) + X0 // unit + (i % per_batch) * (tm // unit)
    return pl.multiple_of(r * unit, unit)


def _lay_src_row(i):
    per_batch = TL // LAY_TM
    r = (i // per_batch) * (SEQ // SUBLANE) + jnp.maximum(
        (i % per_batch) * (LAY_TM // SUBLANE) - X0 // SUBLANE, 0)
    return pl.multiple_of(r * SUBLANE, SUBLANE)


def _meta_rows(meta_tokens):
    return jnp.concatenate([jnp.zeros((PAD, D_MODEL), F32), meta_tokens.astype(F32)], axis=0)


FFN_TM = 512
FFN_TF = 512


FFN_HEAD_TF = 256


def _ffn_begin(x_ref, shared_ref, xin_ref, gain_ref, xn_ref, acc_ref, first):
    if shared_ref is not None:
        @pl.when(first)
        def _():
            xin_ref[0:X0, :] = shared_ref[...]
            xin_ref[X0:LAY_TM, :] = x_ref[0:LAY_TM - X0, :]

        @pl.when(jnp.logical_not(first))
        def _():
            xin_ref[...] = x_ref[...]

    xn_ref[...] = (_rms(xin_ref[...]) * gain_ref[...]).astype(BF16)
    acc_ref[...] = jnp.zeros_like(acc_ref)


def _ffn_step(xn_ref, wg, wu, wd, acc_ref):
    xn = xn_ref[...]
    g = jnp.dot(xn, wg, preferred_element_type=F32)
    u = jnp.dot(xn, wu, preferred_element_type=F32)
    a = (_silu(g) * u).astype(BF16)
    acc_ref[...] += jnp.dot(a, wd, preferred_element_type=F32)


def _ffn_result(xin_ref, acc_ref, fgain_ref, final):
    h = xin_ref[...] + 0.5 * acc_ref[...]
    return _rms(h) * fgain_ref[...] if final else h


def _ffn_head_body(x_ref, gain_ref, wg_ref, wu_ref, wd_ref, fgain_ref, *rest, final, layout):
    if layout:
        shared_ref, o_ref, wg16_ref, wu16_ref, wd16_ref, xn_ref, acc_ref, xin_ref = rest
    else:
        o_ref, wg16_ref, wu16_ref, wd16_ref, xn_ref, acc_ref = rest
        shared_ref, xin_ref = None, x_ref
    j = pl.program_id(0)

    @pl.when(j == 0)
    def _():
        _ffn_begin(x_ref, shared_ref, xin_ref, gain_ref, xn_ref, acc_ref, True)

    wg, wu, wd = wg_ref[...].astype(BF16), wu_ref[...].astype(BF16), wd_ref[...].astype(BF16)
    wg16_ref[...] = wg
    wu16_ref[...] = wu
    wd16_ref[...] = wd
    _ffn_step(xn_ref, wg, wu, wd, acc_ref)

    @pl.when(j == pl.num_programs(0) - 1)
    def _():
        o_ref[...] = _ffn_result(xin_ref, acc_ref, fgain_ref, final)


def _ffn_body(x_ref, gain_ref, wg_ref, wu_ref, wd_ref, fgain_ref, head_ref, *rest, final, layout):
    if layout:
        shared_ref, o_ref, xn_ref, acc_ref, xin_ref = rest
    else:
        o_ref, xn_ref, acc_ref = rest
        shared_ref, xin_ref = None, x_ref
    i, j = pl.program_id(0), pl.program_id(1)
    last = j == pl.num_programs(1) - 1

    @pl.when(i > 0)
    def _():
        @pl.when(j == 0)
        def _():
            _ffn_begin(x_ref, shared_ref, xin_ref, gain_ref, xn_ref, acc_ref, i % (TL // LAY_TM) == 0)

        _ffn_step(xn_ref, wg_ref[...], wu_ref[...], wd_ref[...], acc_ref)

        @pl.when(last)
        def _():
            o_ref[...] = _ffn_result(xin_ref, acc_ref, fgain_ref, final)

    @pl.when(jnp.logical_and(i == 0, last))
    def _():
        o_ref[...] = head_ref[...]


CAST_STEPS = 4


def _cast_body(x_ref, o_ref):
    o_ref[...] = x_ref[...].astype(BF16)


def _to_bf16(w, cols=None):
    rows = w.shape[0]
    cols = w.shape[1] if cols is None else cols
    tr = rows // CAST_STEPS
    assert tr * CAST_STEPS == rows and tr % BF16_SUBLANE == 0 and cols % LANE == 0
    return pl.pallas_call(
        _cast_body,
        out_shape=jax.ShapeDtypeStruct((rows, cols), BF16),
        grid=(CAST_STEPS,),
        in_specs=[pl.BlockSpec((tr, cols), lambda i: (i, 0))],
        out_specs=pl.BlockSpec((tr, cols), lambda i: (i, 0)),
        compiler_params=_params(("parallel",), 48),
        name="to_bf16",
    )(w)


def _ffn(x2, gain, wg, wu, wd, fgain, final, shared_rows=None):
    layout = shared_rows is not None
    tm = LAY_TM if layout else FFN_TM
    rows = ROWS if layout else X_ROWS
    gain, fgain = gain.reshape(1, -1), fgain.reshape(1, -1)
    scratch = [pltpu.VMEM((tm, D_MODEL), BF16), pltpu.VMEM((tm, D_MODEL), F32)]
    extra = ()
    if layout:
        extra = (shared_rows,)
        scratch.append(pltpu.VMEM((tm, D_MODEL), F32))
    name = "ffn_final" if final else "ffn"

    tf = FFN_HEAD_TF
    vec1 = pl.BlockSpec((1, D_MODEL), lambda j: (0, 0))
    wide1 = pl.BlockSpec((D_MODEL, tf), lambda j: (0, j))
    tall1 = pl.BlockSpec((tf, D_MODEL), lambda j: (j, 0))
    tile1 = pl.BlockSpec((tm, D_MODEL), lambda j: (0, 0))
    head, wg16, wu16, wd16 = pl.pallas_call(
        functools.partial(_ffn_head_body, final=final, layout=layout),
        out_shape=(jax.ShapeDtypeStruct((tm, D_MODEL), F32), jax.ShapeDtypeStruct(wg.shape, BF16),
                   jax.ShapeDtypeStruct(wu.shape, BF16), jax.ShapeDtypeStruct(wd.shape, BF16)),
        grid=(D_FF // tf,),
        in_specs=[tile1, vec1, wide1, wide1, tall1, vec1] + [pl.BlockSpec((X0, D_MODEL), lambda j: (0, 0))] * layout,
        out_specs=(tile1, wide1, wide1, tall1),
        scratch_shapes=scratch,
        compiler_params=_params(("arbitrary",), 56),
        name=name + "_head",
    )(x2, gain, wg, wu, wd, fgain, *extra)

    col = lambda i, j: jnp.where(i == 0, 0, j)
    vec = pl.BlockSpec((1, D_MODEL), lambda i, j: (0, 0))
    wide = pl.BlockSpec((D_MODEL, FFN_TF), lambda i, j: (0, col(i, j)))
    tall = pl.BlockSpec((FFN_TF, D_MODEL), lambda i, j: (col(i, j), 0))
    if layout:
        x_spec = pl.BlockSpec((pl.Element(tm), pl.Element(D_MODEL)), lambda i, j: (_lay_src_row(i), 0))
    else:
        x_spec = pl.BlockSpec((tm, D_MODEL), lambda i, j: (i, 0))
    return pl.pallas_call(
        functools.partial(_ffn_body, final=final, layout=layout),
        out_shape=jax.ShapeDtypeStruct((rows, D_MODEL), F32),
        grid=(rows // tm, D_FF // FFN_TF),
        in_specs=[x_spec, vec, wide, wide, tall, vec, pl.BlockSpec((tm, D_MODEL), lambda i, j: (0, 0))]
        + [pl.BlockSpec((X0, D_MODEL), lambda i, j: (0, 0))] * layout,
        out_specs=pl.BlockSpec((tm, D_MODEL), lambda i, j: (i, 0)),
        scratch_shapes=scratch,
        compiler_params=_params(("parallel", "arbitrary"), 56),
        name=name,
    )(x2, gain, wg16, wu16, wd16, fgain, head, *extra)


INP_TM = 2 * LAY_TM
INP_TN = 1536
LR_PAD = 128


def _inproj_body(x_ref, gain_ref, w_ref, wlr_ref, w2_ref, b2_ref, p_ref, lg_ref, xn_ref, lr_ref):
    @pl.when(pl.program_id(1) == 0)
    def _():
        xn = (_rms(x_ref[...]) * gain_ref[...]).astype(BF16)
        xn_ref[...] = xn
        lr_ref[...] = jnp.dot(xn, wlr_ref[...], preferred_element_type=F32).astype(BF16)

    z = jnp.dot(lr_ref[...], w2_ref[...], preferred_element_type=F32) + b2_ref[...]
    lg_ref[...] = (jnp.minimum(z, 0.0) - jnp.log(1.0 + jnp.exp(-jnp.abs(z)))) * (1.0 / GATE_NORMALIZER)
    p_ref[...] = jnp.dot(xn_ref[...], w_ref[...], preferred_element_type=F32).astype(BF16)


SPLIT_TC = 1024


def _split_w_in_body(w_ref, wlr_ref, main_ref, lr_ref):
    main_ref[...] = w_ref[...].T.astype(BF16)

    @pl.when(pl.program_id(0) == 0)
    def _():
        rows = jnp.concatenate([wlr_ref[...], jnp.zeros((LR_PAD - 2 * GATE_RANK, D_MODEL), F32)], axis=0)
        lr_ref[...] = rows.T.astype(BF16)


def _split_w_in(w_in):
    w_t = jnp.transpose(w_in)
    return pl.pallas_call(
        _split_w_in_body,
        out_shape=(jax.ShapeDtypeStruct((D_MODEL, P_MAIN), BF16), jax.ShapeDtypeStruct((D_MODEL, LR_PAD), BF16)),
        grid=(P_MAIN // SPLIT_TC,),
        in_specs=[pl.BlockSpec((SPLIT_TC, D_MODEL), lambda i: (i, 0)),
                  pl.BlockSpec((2 * GATE_RANK, D_MODEL), lambda i: (P_MAIN // (2 * GATE_RANK), 0))],
        out_specs=(pl.BlockSpec((D_MODEL, SPLIT_TC), lambda i: (0, i)),
                   pl.BlockSpec((D_MODEL, LR_PAD), lambda i: (0, 0))),
        compiler_params=_params(("arbitrary",), 48),
        name="split_w_in",
    )(w_t, w_t)


def _inproj_weights(gain, w_in, gk_w2, gk_b2):
    w_main, w_lr = _split_w_in(w_in)
    w2 = jnp.zeros((LR_PAD, 2 * GLA_KEY_WIDTH), F32)
    w2 = w2.at[:GATE_RANK, :GLA_KEY_WIDTH].set(gk_w2[0])
    w2 = w2.at[GATE_RANK:2 * GATE_RANK, GLA_KEY_WIDTH:].set(gk_w2[1]).astype(BF16)
    return gain.reshape(1, -1), w_main, w_lr, w2, gk_b2.reshape(1, 2 * GLA_KEY_WIDTH)


def _inproj(hbuf, weights):
    tm, tn = INP_TM, INP_TN
    steps = P_MAIN // tn
    tg = 2 * GLA_KEY_WIDTH // steps
    assert steps * tn == P_MAIN and steps * tg == 2 * GLA_KEY_WIDTH and tg % LANE == 0
    return pl.pallas_call(
        _inproj_body,
        out_shape=(jax.ShapeDtypeStruct((ROWS, P_MAIN), BF16),
                   jax.ShapeDtypeStruct((ROWS, 2 * GLA_KEY_WIDTH), F32)),
        grid=(ROWS // tm, steps),
        in_specs=[
            pl.BlockSpec((tm, D_MODEL), lambda i, j: (i, 0)),
            pl.BlockSpec((1, D_MODEL), lambda i, j: (0, 0)),
            pl.BlockSpec((D_MODEL, tn), lambda i, j: (0, j)),
            pl.BlockSpec((D_MODEL, LR_PAD), lambda i, j: (0, 0)),
            pl.BlockSpec((LR_PAD, tg), lambda i, j: (0, j)),
            pl.BlockSpec((1, tg), lambda i, j: (0, j)),
        ],
        out_specs=(pl.BlockSpec((tm, tn), lambda i, j: (i, j)),
                   pl.BlockSpec((tm, tg), lambda i, j: (i, j))),
        scratch_shapes=[pltpu.VMEM((tm, D_MODEL), BF16), pltpu.VMEM((tm, LR_PAD), BF16)],
        compiler_params=_params(("parallel", "arbitrary"), 56),
        name="inproj",
    )(hbuf, *weights)


FILT_TR = 640
FEAT_PAD = 128


def _filt_tables():
    pos = np.arange(TFFT, dtype=np.float64)
    t = pos / (L_TOK - 1)
    w = (2.0 * np.pi / L_TOK) * pos
    bands = 1e-4 + np.arange(FILT_BANDS, dtype=np.float64) * ((FILT_BANDS - 1 - 1e-4) / (FILT_BANDS - 1))
    ang = w[:, None] * bands[None, :]
    feats = np.zeros((TFFT, FEAT_PAD), np.float64)
    feats[:, 0] = t
    feats[:, 1:1 + FILT_BANDS] = np.cos(ang)
    feats[:, 1 + FILT_BANDS:FILT_EMB] = -np.sin(ang)
    lo, hi = np.log(1e-2) / 1.5, np.log(1e-2) / 0.3
    deltas = np.abs(lo + np.arange(HY_WIDTH, dtype=np.float64) * ((hi - lo) / (HY_WIDTH - 1)))
    return feats.astype(np.float32), deltas.astype(np.float32).reshape(1, HY_WIDTH)


def _filt_body(feat_ref, t_ref, w1_ref, b1_ref, w2_ref, b2_ref, w3_ref, fr_ref, dl_ref, h_ref):
    start = pl.program_id(0) * FILT_TR

    @pl.when(start < L_TOK)
    def _():
        hp = lax.Precision.HIGHEST
        fr = fr_ref[...]
        z = jnp.sin(fr * (jnp.dot(w1_ref[...], feat_ref[...], precision=hp, preferred_element_type=F32)
                          + b1_ref[...]))
        z = jnp.sin(fr * (jnp.dot(w2_ref[...], z, precision=hp, preferred_element_type=F32) + b2_ref[...]))
        hh = lax.dot_general(z, w3_ref[...], (((0,), (0,)), ((), ())), precision=hp,
                             preferred_element_type=F32)
        pos = start + lax.broadcasted_iota(jnp.int32, (FILT_TR, 1), 0)
        win = jnp.exp(-t_ref[...] * dl_ref[...])
        win = jnp.where(pos < L_TOK, win, 0.0)
        hf = hh[:, :HY_WIDTH] * win
        hb = hh[:, HY_WIDTH:] * win
        h_ref[0] = jnp.where(pos == 0, hf + hb, hf)
        h_ref[1] = jnp.where(pos == 0, 0.0, hb)

    @pl.when(start >= L_TOK)
    def _():
        h_ref[...] = jnp.zeros_like(h_ref)


def _filters(w1, b1, w2, b2, w3, freq):
    feats, deltas = _filt_tables()
    w1t = jnp.pad(w1, ((0, FEAT_PAD - FILT_EMB), (0, 0))).T
    col = lambda v: v.reshape(-1, 1)
    full = lambda shape: pl.BlockSpec(shape, lambda i: (0, 0))
    return pl.pallas_call(
        _filt_body,
        out_shape=jax.ShapeDtypeStruct((2, TFFT, HY_WIDTH), F32),
        grid=(TFFT // FILT_TR,),
        in_specs=[
            pl.BlockSpec((FEAT_PAD, FILT_TR), lambda i: (0, i)),
            pl.BlockSpec((FILT_TR, 1), lambda i: (i, 0)),
            full((FILT_HIDDEN, FEAT_PAD)), full((FILT_HIDDEN, 1)),
            full((FILT_HIDDEN, FILT_HIDDEN)), full((FILT_HIDDEN, 1)),
            full((FILT_HIDDEN, 2 * HY_WIDTH)), full((FILT_HIDDEN, 1)), full((1, HY_WIDTH)),
        ],
        out_specs=pl.BlockSpec((2, FILT_TR, HY_WIDTH), lambda i: (0, i, 0)),
        compiler_params=_params(("parallel",), 40),
        name="filt",
    )(jnp.asarray(feats.T), jnp.asarray(feats[:, 0:1]), w1t, col(b1), w2.T, col(b2), w3, col(freq),
      jnp.asarray(deltas))


def _dft_tables_np():
    n1 = np.arange(FFT_N1)
    ang1 = 2.0 * np.pi * ((n1[:, None] * n1[None, :]) % FFT_N1) / FFT_N1
    c1, s1 = np.cos(ang1)[:, :FFT_R1], np.sin(ang1)[:, :FFT_R1]
    f1 = np.block([[c1, s1], [-s1, c1]])
    f3 = np.block([[c1.T, -s1.T], [s1.T, c1.T]])
    k1 = np.arange(FFT_N1)[:, None, None]
    k2 = np.arange(FFT_N2)[None, :, None]
    n2 = np.arange(FFT_N2)[None, None, :]
    ang2 = 2.0 * np.pi * ((n2 * (k1 + FFT_N1 * k2)) % NFFT) / NFFT
    c2, s2 = np.cos(ang2), np.sin(ang2)
    mf = np.concatenate([np.concatenate([c2, s2], axis=2), np.concatenate([-s2, c2], axis=2)], axis=1)
    c2t, s2t = np.swapaxes(c2, 1, 2), np.swapaxes(s2, 1, 2)
    mi = np.concatenate([np.concatenate([c2t, -s2t], axis=2), np.concatenate([s2t, c2t], axis=2)], axis=1)
    k1r = np.arange(FFT_N1)
    k2r = np.arange(FFT_N2)
    perm = np.where(k1r[:, None] == 0, (FFT_N2 - k2r[None, :]) % FFT_N2, FFT_N2 - 1 - k2r[None, :])
    rows = np.concatenate([perm, perm + FFT_N2], axis=1)
    mirror = mf[((FFT_N1 - k1r) % FFT_N1)[:, None], rows, :]
    return f1, f3, mf, mi, mirror


def _dft_tables():
    return tuple(jnp.asarray(a.astype(np.float32)).astype(BF16) for a in _dft_tables_np())


LMM_PITCH = FFT_N2 + SUBLANE
LMM_MC = 2 * FFT_N1


def _lmm_body(f_ref, x_ref, o_ref, xs_ref, os_ref, *, scale):
    m, k = f_ref.shape
    for g in range(k):
        xs_ref[pl.ds(g * LMM_PITCH, FFT_N2), :] = x_ref[pl.ds(g * FFT_N2, FFT_N2), :].astype(F32)
    for m0 in range(0, m, LMM_MC):
        f = f_ref[m0:min(m0 + LMM_MC, m), :]

        def slab(s, carry):
            x = xs_ref[pl.ds(s, k, stride=LMM_PITCH), :].astype(BF16)
            r = jnp.dot(f, x, preferred_element_type=F32)
            os_ref[pl.ds(s, f.shape[0], stride=LMM_PITCH), :] = r * scale if scale != 1.0 else r
            return carry

        lax.fori_loop(0, FFT_N2, slab, 0, unroll=8)
        for g in range(f.shape[0]):
            o_ref[pl.ds((m0 + g) * FFT_N2, FFT_N2), :] = os_ref[pl.ds(g * LMM_PITCH, FFT_N2), :].astype(
                o_ref.dtype)


def _lmm(f, x3, out_dtype, scale=1.0, name="lmm"):
    m, k = f.shape
    mc = min(m, LMM_MC)
    assert m % mc == 0
    out = pl.pallas_call(
        functools.partial(_lmm_body, scale=scale),
        out_shape=jax.ShapeDtypeStruct((m * FFT_N2, HY_WIDTH), out_dtype),
        grid=(HY_WIDTH // LANE,),
        in_specs=[pl.BlockSpec((m, k), lambda j: (0, 0)),
                  pl.BlockSpec((k * FFT_N2, LANE), lambda j: (0, j))],
        out_specs=pl.BlockSpec((m * FFT_N2, LANE), lambda j: (0, j)),
        scratch_shapes=[pltpu.VMEM((k * LMM_PITCH, LANE), F32), pltpu.VMEM((mc * LMM_PITCH, LANE), F32)],
        compiler_params=_params(("parallel",), 56),
        name=name,
    )(f, x3.reshape(k * FFT_N2, HY_WIDTH))
    return out.reshape(m, FFT_N2, HY_WIDTH)


SPEC_K1 = 2


def _spec_body(mf_ref, mr_ref, mi_ref, a_ref, z_ref, *rest):
    zm_refs, o_ref = rest[:SPEC_K1], rest[SPEC_K1]
    for t in range(SPEC_K1):
        mf = mf_ref[t]
        z = jnp.dot(mf, z_ref[:, t].reshape(2 * FFT_N2, HY_WIDTH), preferred_element_type=F32)
        zm = jnp.dot(mr_ref[t], zm_refs[t][...].reshape(2 * FFT_N2, HY_WIDTH), preferred_element_type=F32)
        a, b = z[:FFT_N2], z[FFT_N2:]
        am, bm = zm[:FFT_N2], zm[FFT_N2:]
        gr, gi = 0.5 * (a + am + b + bm), 0.5 * (b - bm + a - am)

        x = jnp.dot(mf, a_ref[:, t].reshape(2 * FFT_N2, HY_WIDTH), preferred_element_type=F32)
        xr, xi = x[:FFT_N2], x[FFT_N2:]
        y = jnp.concatenate([xr * gr - xi * gi, xr * gi + xi * gr], axis=0).astype(BF16)
        o_ref[:, t] = jnp.dot(mi_ref[t], y, preferred_element_type=F32).astype(BF16).reshape(
            2, FFT_N2, HY_WIDTH)


def _spec(mf, mirror, mi, a, zs):
    here = pl.BlockSpec((2, SPEC_K1, FFT_N2, HY_WIDTH), lambda i: (0, i, 0, 0))
    mat = pl.BlockSpec((SPEC_K1, 2 * FFT_N2, 2 * FFT_N2), lambda i: (i, 0, 0))
    mirrored = [pl.BlockSpec((2, 1, FFT_N2, HY_WIDTH),
                             lambda i, t=t: (0, (FFT_N1 - (SPEC_K1 * i + t)) % FFT_N1, 0, 0))
                for t in range(SPEC_K1)]
    return pl.pallas_call(
        _spec_body,
        out_shape=jax.ShapeDtypeStruct((2, FFT_N1, FFT_N2, HY_WIDTH), BF16),
        grid=(FFT_N1 // SPEC_K1,),
        in_specs=[mat, mat, mat, here, here] + mirrored,
        out_specs=here,
        compiler_params=_params(("parallel",), 40),
        name="spec",
    )(mf, mirror, mi, a, zs, *([zs] * SPEC_K1))


HY_CB = 256


def _short_conv(p_ref, w_ref, b_ref):
    p = p_ref[...].astype(F32)
    w = w_ref[...]
    prev = pltpu.roll(p, 1, 0)
    nxt = pltpu.roll(p, TL - 1, 0)
    return b_ref[...] + prev * w[0:1] + p * w[1:2] + nxt * w[2:3]


def _uconv_body(x1_ref, vh_ref, w1_ref, wv_ref, b1_ref, bv_ref, u_ref):
    u = _short_conv(vh_ref, wv_ref, bv_ref) * _short_conv(x1_ref, w1_ref, b1_ref)
    row = lax.broadcasted_iota(jnp.int32, (TL, 1), 0)
    u_ref[pl.ds(0, TL), :] = jnp.where(row >= PAD, u, 0.0)
    u_ref[pl.ds(TL, TFFT - TL), :] = jnp.zeros((TFFT - TL, HY_CB), F32)


def _hy_specs(first_block):
    nb = HY_WIDTH // HY_CB
    return (pl.BlockSpec((None, TL, HY_CB), lambda b, j: (b, 0, first_block * nb + j)),
            pl.BlockSpec((3, HY_CB), lambda b, j: (0, first_block * nb + j)),
            pl.BlockSpec((1, HY_CB), lambda b, j: (0, first_block * nb + j)))


def _uconv(p3, conv_w, conv_b):
    x1, w1, b1 = _hy_specs(1)
    vh, wv, bv = _hy_specs(2)
    return pl.pallas_call(
        _uconv_body,
        out_shape=jax.ShapeDtypeStruct((BATCH, TFFT, HY_WIDTH), F32),
        grid=(BATCH, HY_WIDTH // HY_CB),
        in_specs=[x1, vh, w1, wv, b1, bv],
        out_specs=pl.BlockSpec((None, TFFT, HY_CB), lambda b, j: (b, 0, j)),
        compiler_params=_params(("parallel", "parallel"), 48),
        name="uconv",
    )(p3, p3, conv_w, conv_w, conv_b, conv_b)


def _ymix_body(y_ref, x0_ref, x1_ref, vh_ref, w0_ref, w1_ref, wv_ref, b0_ref, b1_ref, bv_ref,
               d_ref, gain_ref, o_ref):
    u = _short_conv(vh_ref, wv_ref, bv_ref) * _short_conv(x1_ref, w1_ref, b1_ref)
    yy = (y_ref[...] + d_ref[...] * u) * _short_conv(x0_ref, w0_ref, b0_ref)
    gain = gain_ref[...]
    for s in range(0, HY_CB, HY_GROUP):
        o_ref[:, s:s + HY_GROUP] = (_rms(yy[X0:, s:s + HY_GROUP]) * gain[:, s:s + HY_GROUP]).astype(BF16)


def _ymix(y3, p3, conv_w, conv_b, hyena_d, hyena_norm):
    x0, w0, b0 = _hy_specs(0)
    x1, w1, b1 = _hy_specs(1)
    vh, wv, bv = _hy_specs(2)
    vec = pl.BlockSpec((1, HY_CB), lambda b, j: (0, j))
    return pl.pallas_call(
        _ymix_body,
        out_shape=jax.ShapeDtypeStruct((BATCH, SEQ, HY_WIDTH), BF16),
        grid=(BATCH, HY_WIDTH // HY_CB),
        in_specs=[pl.BlockSpec((None, TL, HY_CB), lambda b, j: (b, 0, j)),
                  x0, x1, vh, w0, w1, wv, b0, b1, bv, vec, vec],
        out_specs=pl.BlockSpec((None, SEQ, HY_CB), lambda b, j: (b, 0, j)),
        compiler_params=_params(("parallel", "parallel"), 56),
        name="ymix",
    )(y3, p3, p3, p3, conv_w, conv_w, conv_w, conv_b, conv_b, conv_b,
      hyena_d.reshape(1, -1), hyena_norm.reshape(1, -1))


N_SUB = CHUNK // SUB
SAFE_BLOCK_DECAY = -60.0


def _scores_exact(q, k, b, rev, ones, row, col):
    rsub = row % SUB
    terms = []
    for d in range(SUB):
        if d == 0:
            kr, br = k, b
        else:
            sh = CHUNK - d if rev else d
            kr, br = pltpu.roll(k, sh, 0), pltpu.roll(b, sh, 0)
        valid = (rsub + d < SUB) if rev else (rsub >= d)
        t = q * kr * jnp.exp(jnp.minimum(b - br, 0.0))
        terms.append(jnp.where(valid, t, 0.0).astype(BF16))
    sums = jnp.dot(jnp.concatenate(terms, axis=0), ones, preferred_element_type=F32)
    a = jnp.zeros((CHUNK, CHUNK), F32)
    for d in range(SUB):
        tgt = row + d if rev else row - d
        a = jnp.where(col == tgt, sums[d * CHUNK:(d + 1) * CHUNK, :CHUNK], a)

    rblk = row // SUB
    cblk = col // SUB
    for jb in (range(1, N_SUB) if rev else range(N_SUB - 1)):
        e = jb * SUB if rev else jb * SUB + SUB - 1
        ref = b[e:e + 1, :]
        qh = (q * jnp.exp(jnp.minimum(b - ref, 0.0))).astype(BF16)
        kh = (k * jnp.exp(jnp.minimum(ref - b, 0.0))).astype(BF16)
        pm = lax.dot_general(qh, kh, (((1,), (1,)), ((), ())), preferred_element_type=F32)
        side = jnp.where(cblk == jb, rblk, jb)
        a = jnp.where((side < jb) if rev else (side > jb), pm, a)

    bend = b[0:1, :] if rev else b[CHUNK - 1:CHUNK, :]
    return a, q * jnp.exp(b), k * jnp.exp(bend - b), bend


def _scores_fast(q, k, b, rev, row, col):
    order = list(range(N_SUB))[::-1] if rev else list(range(N_SUB))
    pos = {blk: p for p, blk in enumerate(order)}
    edge = lambda blk: blk * SUB if rev else blk * SUB + SUB - 1
    e = [b[edge(blk):edge(blk) + 1, :] for blk in order]
    s = [jnp.zeros((1, GLA_DK), F32)] + e[:-1]

    def by_row(vals):
        return jnp.concatenate([jnp.broadcast_to(vals[pos[blk]], (SUB, GLA_DK)) for blk in range(N_SUB)],
                               axis=0)

    srow, erow = by_row(s), by_row(e)
    qh = q * jnp.exp(b - srow)
    kh = k * jnp.exp(erow - b)
    kd = k * jnp.exp(srow - b)

    lhs = []
    for pj in range(N_SUB - 1):
        for blk in range(N_SUB):
            piece = qh[blk * SUB:(blk + 1) * SUB, :]
            p = pos[blk]
            if p <= pj:
                piece = jnp.zeros_like(piece)
            elif p > pj + 1:
                piece = piece * jnp.exp(s[p] - e[pj])
            lhs.append(piece.astype(BF16))
    contract = (((1,), (1,)), ((), ()))
    cross = lax.dot_general(jnp.concatenate(lhs, axis=0), kh.astype(BF16), contract,
                            preferred_element_type=F32)
    diag = lax.dot_general(qh.astype(BF16), kd.astype(BF16), contract, preferred_element_type=F32)

    rblk = row // SUB
    cblk = col // SUB
    a = jnp.zeros((CHUNK, CHUNK), F32)
    for pj in range(N_SUB - 1):
        a = jnp.where(cblk == order[pj], cross[pj * CHUNK:(pj + 1) * CHUNK], a)
    causal = (col >= row) if rev else (col <= row)
    a = jnp.where(cblk == rblk, jnp.where(causal, diag, 0.0), a)

    bend = e[-1]
    return a, qh * jnp.exp(srow), kh * jnp.exp(bend - erow), bend


GLA_RB = 384


def _gla_sweep_body(*refs, rev):
    if rev:
        q_ref, k_ref, v_ref, g_ref, o_ref, st_ref, b_ref = refs
    else:
        q_ref, k_ref, v_ref, g_ref, og_ref, ob_ref, gain_ref, o_ref, st_ref, b_ref = refs

    @pl.when(pl.program_id(1) == 0)
    def _():
        st_ref[...] = jnp.zeros_like(st_ref)

    row = lax.broadcasted_iota(jnp.int32, (CHUNK, 1), 0)
    col = lax.broadcasted_iota(jnp.int32, (CHUNK, CHUNK), 1)
    rr = lax.broadcasted_iota(jnp.int32, (CHUNK, CHUNK), 0)
    tri = ((col >= rr) if rev else (col <= rr)).astype(BF16)
    ones = jnp.ones((GLA_DK, GLA_DK), BF16)
    n_chunk = GLA_RB // CHUNK

    for c in range(n_chunk):
        g = g_ref[c * CHUNK:(c + 1) * CHUNK, :]
        g_hi = g.astype(BF16)
        rest = g - g_hi.astype(F32)
        g_mid = rest.astype(BF16)
        g_lo = (rest - g_mid.astype(F32)).astype(BF16)
        b_ref[c * CHUNK:(c + 1) * CHUNK, :] = (
            jnp.dot(tri, g_hi, preferred_element_type=F32) + jnp.dot(tri, g_mid, preferred_element_type=F32)
            + jnp.dot(tri, g_lo, preferred_element_type=F32))

    def run(fast):
        def step(t, carry):
            c = n_chunk - 1 - t if rev else t
            rows = pl.ds(pl.multiple_of(c * CHUNK, CHUNK), CHUNK)
            b_all = b_ref[rows, :]
            pending = []
            for h in range(GLA_HEADS):
                kc = slice(h * GLA_DK, (h + 1) * GLA_DK)
                vc = slice(h * GLA_DV, (h + 1) * GLA_DV)
                q = q_ref[rows, kc].astype(F32) * (GLA_DK ** -0.5)
                k = k_ref[rows, kc].astype(F32)
                v = v_ref[rows, vc].astype(BF16)
                b = b_all[:, kc]
                if fast:
                    a, qt, kt, bend = _scores_fast(q, k, b, rev, row, col)
                else:
                    a, qt, kt, bend = _scores_exact(q, k, b, rev, ones, row, col)
                st = st_ref[h]
                o = lax.dot_general(qt.astype(BF16), st.astype(BF16), (((1,), (1,)), ((), ())),
                                    preferred_element_type=F32)
                st_ref[h] = st * jnp.exp(bend) + lax.dot_general(
                    v, kt.astype(BF16), (((0,), (0,)), ((), ())), preferred_element_type=F32)
                pending.append((vc, o, a, v))
            for vc, o, a, v in pending:
                o = o + jnp.dot(a.astype(BF16), v, preferred_element_type=F32)
                if rev:
                    o_ref[rows, vc] = o
                else:
                    o = _rms(o + ob_ref[rows, vc]) * gain_ref[...]
                    o_ref[rows, vc] = (o * _silu(og_ref[rows, vc].astype(F32))).astype(BF16)
            return carry

        lax.fori_loop(0, n_chunk, step, 0, unroll=3 if fast else 1)

    low = jnp.min(jnp.sum(g_ref[...].reshape(GLA_RB // SUB, SUB, GLA_KEY_WIDTH), axis=1))
    safe = low > SAFE_BLOCK_DECAY
    pl.when(safe)(lambda: run(True))
    pl.when(jnp.logical_not(safe))(lambda: run(False))


def _gla_sweep(p3, lg3, rev, ob=None, gla_norm=None):
    nb = TL // GLA_RB
    blk = (lambda i: nb - 1 - i) if rev else (lambda i: i)
    key_blocks = P_MAIN // GLA_KEY_WIDTH
    q_col = 3 * HY_WIDTH // GLA_KEY_WIDTH
    v_col = (3 * HY_WIDTH + 2 * GLA_KEY_WIDTH) // GLA_WIDTH
    assert key_blocks * GLA_KEY_WIDTH == P_MAIN
    narrow = lambda col: pl.BlockSpec((None, GLA_RB, GLA_KEY_WIDTH), lambda b, i: (b, blk(i), col))
    wide = lambda col: pl.BlockSpec((None, GLA_RB, GLA_WIDTH), lambda b, i: (b, blk(i), col))
    in_specs = [narrow(q_col), narrow(q_col + 1), wide(v_col), narrow(1 if rev else 0)]
    args = [p3, p3, p3, lg3]
    if not rev:
        in_specs += [wide(v_col + 1), wide(0), pl.BlockSpec((1, GLA_DV), lambda b, i: (0, 0))]
        args += [p3, ob, gla_norm.reshape(1, -1)]
    return pl.pallas_call(
        functools.partial(_gla_sweep_body, rev=rev),
        out_shape=jax.ShapeDtypeStruct((BATCH, TL, GLA_WIDTH), F32 if rev else BF16),
        grid=(BATCH, nb),
        in_specs=in_specs,
        out_specs=wide(0),
        scratch_shapes=[pltpu.VMEM((GLA_HEADS, GLA_DV, GLA_DK), F32),
                        pltpu.VMEM((GLA_RB, GLA_KEY_WIDTH), F32)],
        compiler_params=_params(("parallel", "arbitrary"), 32),
        name="gla_down" if rev else "gla_up",
    )(*args)


OUT_TM = 512


def _outproj_body(h_ref, yh_ref, yg_ref, wh_ref, wg_ref, o_ref):
    o_ref[...] = (h_ref[...] + jnp.dot(yh_ref[...], wh_ref[...], preferred_element_type=F32)
                  + jnp.dot(yg_ref[...], wg_ref[...], preferred_element_type=F32))


def _outproj(hbuf, yh, ygbuf, w_out):
    wo = _to_bf16(w_out)
    half = lambda i: pl.BlockSpec((HY_WIDTH, D_MODEL), lambda r: (i, 0))
    return pl.pallas_call(
        _outproj_body,
        out_shape=jax.ShapeDtypeStruct((X_ROWS, D_MODEL), F32),
        grid=(X_ROWS // OUT_TM,),
        in_specs=[pl.BlockSpec((pl.Element(OUT_TM), pl.Element(D_MODEL)), lambda r: (_x_row(r, OUT_TM), 0)),
                  pl.BlockSpec((OUT_TM, HY_WIDTH), lambda r: (r, 0)),
                  pl.BlockSpec((pl.Element(OUT_TM), pl.Element(GLA_WIDTH)),
                               lambda r: (_x_row(r, OUT_TM, BF16_SUBLANE), 0)),
                  half(0), half(1)],
        out_specs=pl.BlockSpec((OUT_TM, D_MODEL), lambda r: (r, 0)),
        compiler_params=_params(("parallel",), 48),
        name="outproj",
    )(hbuf, yh, ygbuf, wo, wo)


def _long_conv(u3, filt):
    f1, f3, mf, mi, mirror = _dft_tables()
    as_n1 = lambda a: a.reshape(2 * FFT_R1, FFT_N2, HY_WIDTH)
    by_k1 = lambda a: a.reshape(2, FFT_N1, FFT_N2, HY_WIDTH)
    zs = _lmm(f1, as_n1(filt), BF16, name="lmm_g")
    a = _lmm(f1, as_n1(u3), BF16, name="lmm_fwd")
    bm = _spec(mf, mirror, mi, by_k1(a), by_k1(zs))
    y = _lmm(f3, bm.reshape(2 * FFT_N1, FFT_N2, HY_WIDTH), F32, scale=1.0 / NFFT, name="lmm_inv")
    return y.reshape(BATCH, TFFT, HY_WIDTH)


def kernel(x, meta_tokens, ffn1_norm, ffn1_w_gate, ffn1_w_up, ffn1_w_down, mix_norm, w_in, conv_w, conv_b,
           filt_w1, filt_b1, filt_w2, filt_b2, filt_w3, filt_freq, hyena_d, hyena_norm, gk_w2, gk_b2,
           gla_norm, w_out, ffn2_norm, ffn2_w_gate, ffn2_w_up, ffn2_w_down, final_norm):
    assert x.shape == (BATCH, SEQ, D_MODEL) and ffn1_norm.shape[0] == 1

    hbuf = _ffn(x.reshape(X_ROWS, D_MODEL), ffn1_norm[0], ffn1_w_gate[0], ffn1_w_up[0], ffn1_w_down[0],
                final_norm, final=False, shared_rows=_meta_rows(meta_tokens))

    p, lg = _inproj(hbuf, _inproj_weights(mix_norm[0], w_in[0], gk_w2[0], gk_b2[0]))
    p3 = p.reshape(BATCH, TL, P_MAIN)
    lg3 = lg.reshape(BATCH, TL, 2 * GLA_KEY_WIDTH)

    filt = _filters(filt_w1[0], filt_b1[0], filt_w2[0], filt_b2[0], filt_w3[0], filt_freq[0])
    cw, cb = conv_w[0], conv_b[0].reshape(1, -1)
    y3 = _long_conv(_uconv(p3, cw, cb), filt)
    yh = _ymix(y3, p3, cw, cb, hyena_d[0], hyena_norm[0])
    yg = _gla_sweep(p3, lg3, rev=False, ob=_gla_sweep(p3, lg3, rev=True), gla_norm=gla_norm[0])

    h2 = _outproj(hbuf, yh.reshape(X_ROWS, HY_WIDTH), yg.reshape(ROWS, GLA_WIDTH), w_out[0])
    out = _ffn(h2, ffn2_norm[0], ffn2_w_gate[0], ffn2_w_up[0], ffn2_w_down[0], final_norm, final=True)
    return out.reshape(BATCH, SEQ, D_MODEL)
```

```python
import functools

import numpy as np
import jax
import jax.numpy as jnp
from jax import lax
from jax.experimental import pallas as pl
from jax.experimental.pallas import tpu as pltpu

F32 = jnp.float32
BF16 = jnp.bfloat16

D_MODEL = 2048
BATCH = 2
SEQ = 4096
N_META = 16
L_TOK = SEQ + N_META
PAD = 112
X0 = PAD + N_META
TL = PAD + L_TOK
ROWS = BATCH * TL
X_ROWS = BATCH * SEQ
HY_WIDTH = 1024
HY_GROUPS = 8
HY_GROUP = HY_WIDTH // HY_GROUPS
FILT_EMB = 33
FILT_BANDS = 16
FILT_HIDDEN = 64
GLA_WIDTH = 1024
GLA_HEADS = 4
GLA_KEY_WIDTH = 512
GLA_DK = 128
GLA_DV = 256
GATE_RANK = 16
GATE_NORMALIZER = 16.0
CHUNK = 64
SUB = 16
D_FF = 5632
P_MAIN = 3 * HY_WIDTH + 2 * GLA_KEY_WIDTH + 2 * GLA_WIDTH
EPS = 1e-6

FFT_N1 = 72
FFT_N2 = 128
NFFT = FFT_N1 * FFT_N2
FFT_R1 = 40
TFFT = FFT_R1 * FFT_N2

MIB = 1024 * 1024
SUBLANE = 8
BF16_SUBLANE = 16
LANE = 128


def _params(sem, vmem_mib):
    return pltpu.CompilerParams(dimension_semantics=sem, vmem_limit_bytes=vmem_mib * MIB)


def _rms(x):
    return x * lax.rsqrt(jnp.mean(x * x, axis=-1, keepdims=True) + EPS)


def _silu(x):
    return x * jax.nn.sigmoid(x)


LAY_TM = 528


def _x_row(i, tm, unit=SUBLANE):
    per_batch = SEQ // tm
    r = (i // per_batch) * (TL // unit) + X0 // unit + (i % per_batch) * (tm // unit)
    return pl.multiple_of(r * unit, unit)


def _lay_src_row(i):
    per_batch = TL // LAY_TM
    r = (i // per_batch) * (SEQ // SUBLANE) + jnp.maximum(
        (i % per_batch) * (LAY_TM // SUBLANE) - X0 // SUBLANE, 0)
    return pl.multiple_of(r * SUBLANE, SUBLANE)


def _meta_rows(meta_tokens):
    return jnp.concatenate([jnp.zeros((PAD, D_MODEL), F32), meta_tokens.astype(F32)], axis=0)


FFN_TM = 512
FFN_TF = 512


FFN_HEAD_TF = 256


def _ffn_begin(x_ref, shared_ref, xin_ref, gain_ref, xn_ref, acc_ref, first):
    if shared_ref is not None:
        @pl.when(first)
        def _():
            xin_ref[0:X0, :] = shared_ref[...]
            xin_ref[X0:LAY_TM, :] = x_ref[0:LAY_TM - X0, :]

        @pl.when(jnp.logical_not(first))
        def _():
            xin_ref[...] = x_ref[...]

    xn_ref[...] = (_rms(xin_ref[...]) * gain_ref[...]).astype(BF16)
    acc_ref[...] = jnp.zeros_like(acc_ref)


def _ffn_step(xn_ref, wg, wu, wd, acc_ref):
    xn = xn_ref[...]
    g = jnp.dot(xn, wg, preferred_element_type=F32)
    u = jnp.dot(xn, wu, preferred_element_type=F32)
    a = (_silu(g) * u).astype(BF16)
    acc_ref[...] += jnp.dot(a, wd, preferred_element_type=F32)


def _ffn_result(xin_ref, acc_ref, fgain_ref, final):
    h = xin_ref[...] + 0.5 * acc_ref[...]
    return _rms(h) * fgain_ref[...] if final else h


def _ffn_head_body(x_ref, gain_ref, wg_ref, wu_ref, wd_ref, fgain_ref, *rest, final, layout):
    if layout:
        shared_ref, o_ref, wg16_ref, wu16_ref, wd16_ref, xn_ref, acc_ref, xin_ref = rest
    else:
        o_ref, wg16_ref, wu16_ref, wd16_ref, xn_ref, acc_ref = rest
        shared_ref, xin_ref = None, x_ref
    j = pl.program_id(0)

    @pl.when(j == 0)
    def _():
        _ffn_begin(x_ref, shared_ref, xin_ref, gain_ref, xn_ref, acc_ref, True)

    wg, wu, wd = wg_ref[...].astype(BF16), wu_ref[...].astype(BF16), wd_ref[...].astype(BF16)
    wg16_ref[...] = wg
    wu16_ref[...] = wu
    wd16_ref[...] = wd
    _ffn_step(xn_ref, wg, wu, wd, acc_ref)

    @pl.when(j == pl.num_programs(0) - 1)
    def _():
        o_ref[...] = _ffn_result(xin_ref, acc_ref, fgain_ref, final)


def _ffn_body(x_ref, gain_ref, wg_ref, wu_ref, wd_ref, fgain_ref, head_ref, *rest, final, layout):
    if layout:
        shared_ref, o_ref, xn_ref, acc_ref, xin_ref = rest
    else:
        o_ref, xn_ref, acc_ref = rest
        shared_ref, xin_ref = None, x_ref
    i, j = pl.program_id(0), pl.program_id(1)
    last = j == pl.num_programs(1) - 1

    @pl.when(i > 0)
    def _():
        @pl.when(j == 0)
        def _():
            _ffn_begin(x_ref, shared_ref, xin_ref, gain_ref, xn_ref, acc_ref, i % (TL // LAY_TM) == 0)

        _ffn_step(xn_ref, wg_ref[...], wu_ref[...], wd_ref[...], acc_ref)

        @pl.when(last)
        def _():
            o_ref[...] = _ffn_result(xin_ref, acc_ref, fgain_ref, final)

    @pl.when(jnp.logical_and(i == 0, last))
    def _():
        o_ref[...] = head_ref[...]


CAST_STEPS = 4


def _cast_body(x_ref, o_ref):
    o_ref[...] = x_ref[...].astype(BF16)


def _to_bf16(w, cols=None):
    rows = w.shape[0]
    cols = w.shape[1] if cols is None else cols
    tr = rows // CAST_STEPS
    assert tr * CAST_STEPS == rows and tr % BF16_SUBLANE == 0 and cols % LANE == 0
    return pl.pallas_call(
        _cast_body,
        out_shape=jax.ShapeDtypeStruct((rows, cols), BF16),
        grid=(CAST_STEPS,),
        in_specs=[pl.BlockSpec((tr, cols), lambda i: (i, 0))],
        out_specs=pl.BlockSpec((tr, cols), lambda i: (i, 0)),
        compiler_params=_params(("parallel",), 48),
        name="to_bf16",
    )(w)


def _ffn(x2, gain, wg, wu, wd, fgain, final, shared_rows=None):
    layout = shared_rows is not None
    tm = LAY_TM if layout else FFN_TM
    rows = ROWS if layout else X_ROWS
    gain, fgain = gain.reshape(1, -1), fgain.reshape(1, -1)
    scratch = [pltpu.VMEM((tm, D_MODEL), BF16), pltpu.VMEM((tm, D_MODEL), F32)]
    extra = ()
    if layout:
        extra = (shared_rows,)
        scratch.append(pltpu.VMEM((tm, D_MODEL), F32))
    name = "ffn_final" if final else "ffn"

    tf = FFN_HEAD_TF
    vec1 = pl.BlockSpec((1, D_MODEL), lambda j: (0, 0))
    wide1 = pl.BlockSpec((D_MODEL, tf), lambda j: (0, j))
    tall1 = pl.BlockSpec((tf, D_MODEL), lambda j: (j, 0))
    tile1 = pl.BlockSpec((tm, D_MODEL), lambda j: (0, 0))
    head, wg16, wu16, wd16 = pl.pallas_call(
        functools.partial(_ffn_head_body, final=final, layout=layout),
        out_shape=(jax.ShapeDtypeStruct((tm, D_MODEL), F32), jax.ShapeDtypeStruct(wg.shape, BF16),
                   jax.ShapeDtypeStruct(wu.shape, BF16), jax.ShapeDtypeStruct(wd.shape, BF16)),
        grid=(D_FF // tf,),
        in_specs=[tile1, vec1, wide1, wide1, tall1, vec1] + [pl.BlockSpec((X0, D_MODEL), lambda j: (0, 0))] * layout,
        out_specs=(tile1, wide1, wide1, tall1),
        scratch_shapes=scratch,
        compiler_params=_params(("arbitrary",), 56),
        name=name + "_head",
    )(x2, gain, wg, wu, wd, fgain, *extra)

    col = lambda i, j: jnp.where(i == 0, 0, j)
    vec = pl.BlockSpec((1, D_MODEL), lambda i, j: (0, 0))
    wide = pl.BlockSpec((D_MODEL, FFN_TF), lambda i, j: (0, col(i, j)))
    tall = pl.BlockSpec((FFN_TF, D_MODEL), lambda i, j: (col(i, j), 0))
    if layout:
        x_spec = pl.BlockSpec((pl.Element(tm), pl.Element(D_MODEL)), lambda i, j: (_lay_src_row(i), 0))
    else:
        x_spec = pl.BlockSpec((tm, D_MODEL), lambda i, j: (i, 0))
    return pl.pallas_call(
        functools.partial(_ffn_body, final=final, layout=layout),
        out_shape=jax.ShapeDtypeStruct((rows, D_MODEL), F32),
        grid=(rows // tm, D_FF // FFN_TF),
        in_specs=[x_spec, vec, wide, wide, tall, vec, pl.BlockSpec((tm, D_MODEL), lambda i, j: (0, 0))]
        + [pl.BlockSpec((X0, D_MODEL), lambda i, j: (0, 0))] * layout,
        out_specs=pl.BlockSpec((tm, D_MODEL), lambda i, j: (i, 0)),
        scratch_shapes=scratch,
        compiler_params=_params(("parallel", "arbitrary"), 56),
        name=name,
    )(x2, gain, wg16, wu16, wd16, fgain, head, *extra)


INP_TM = 2 * LAY_TM
INP_TN = 1536
LR_PAD = 128


def _inproj_body(x_ref, gain_ref, w_ref, wlr_ref, w2_ref, b2_ref, p_ref, lg_ref, xn_ref, lr_ref):
    @pl.when(pl.program_id(1) == 0)
    def _():
        xn = (_rms(x_ref[...]) * gain_ref[...]).astype(BF16)
        xn_ref[...] = xn
        lr_ref[...] = jnp.dot(xn, wlr_ref[...], preferred_element_type=F32).astype(BF16)

    z = jnp.dot(lr_ref[...], w2_ref[...], preferred_element_type=F32) + b2_ref[...]
    lg_ref[...] = (jnp.minimum(z, 0.0) - jnp.log(1.0 + jnp.exp(-jnp.abs(z)))) * (1.0 / GATE_NORMALIZER)
    p_ref[...] = jnp.dot(xn_ref[...], w_ref[...], preferred_element_type=F32).astype(BF16)


SPLIT_TC = 1024


def _split_w_in_body(w_ref, wlr_ref, main_ref, lr_ref):
    main_ref[...] = w_ref[...].T.astype(BF16)

    @pl.when(pl.program_id(0) == 0)
    def _():
        rows = jnp.concatenate([wlr_ref[...], jnp.zeros((LR_PAD - 2 * GATE_RANK, D_MODEL), F32)], axis=0)
        lr_ref[...] = rows.T.astype(BF16)


def _split_w_in(w_in):
    w_t = jnp.transpose(w_in)
    return pl.pallas_call(
        _split_w_in_body,
        out_shape=(jax.ShapeDtypeStruct((D_MODEL, P_MAIN), BF16), jax.ShapeDtypeStruct((D_MODEL, LR_PAD), BF16)),
        grid=(P_MAIN // SPLIT_TC,),
        in_specs=[pl.BlockSpec((SPLIT_TC, D_MODEL), lambda i: (i, 0)),
                  pl.BlockSpec((2 * GATE_RANK, D_MODEL), lambda i: (P_MAIN // (2 * GATE_RANK), 0))],
        out_specs=(pl.BlockSpec((D_MODEL, SPLIT_TC), lambda i: (0, i)),
                   pl.BlockSpec((D_MODEL, LR_PAD), lambda i: (0, 0))),
        compiler_params=_params(("arbitrary",), 48),
        name="split_w_in",
    )(w_t, w_t)


def _inproj_weights(gain, w_in, gk_w2, gk_b2):
    w_main, w_lr = _split_w_in(w_in)
    w2 = jnp.zeros((LR_PAD, 2 * GLA_KEY_WIDTH), F32)
    w2 = w2.at[:GATE_RANK, :GLA_KEY_WIDTH].set(gk_w2[0])
    w2 = w2.at[GATE_RANK:2 * GATE_RANK, GLA_KEY_WIDTH:].set(gk_w2[1]).astype(BF16)
    return gain.reshape(1, -1), w_main, w_lr, w2, gk_b2.reshape(1, 2 * GLA_KEY_WIDTH)


def _inproj(hbuf, weights):
    tm, tn = INP_TM, INP_TN
    steps = P_MAIN // tn
    tg = 2 * GLA_KEY_WIDTH // steps
    assert steps * tn == P_MAIN and steps * tg == 2 * GLA_KEY_WIDTH and tg % LANE == 0
    return pl.pallas_call(
        _inproj_body,
        out_shape=(jax.ShapeDtypeStruct((ROWS, P_MAIN), BF16),
                   jax.ShapeDtypeStruct((ROWS, 2 * GLA_KEY_WIDTH), F32)),
        grid=(ROWS // tm, steps),
        in_specs=[
            pl.BlockSpec((tm, D_MODEL), lambda i, j: (i, 0)),
            pl.BlockSpec((1, D_MODEL), lambda i, j: (0, 0)),
            pl.BlockSpec((D_MODEL, tn), lambda i, j: (0, j)),
            pl.BlockSpec((D_MODEL, LR_PAD), lambda i, j: (0, 0)),
            pl.BlockSpec((LR_PAD, tg), lambda i, j: (0, j)),
            pl.BlockSpec((1, tg), lambda i, j: (0, j)),
        ],
        out_specs=(pl.BlockSpec((tm, tn), lambda i, j: (i, j)),
                   pl.BlockSpec((tm, tg), lambda i, j: (i, j))),
        scratch_shapes=[pltpu.VMEM((tm, D_MODEL), BF16), pltpu.VMEM((tm, LR_PAD), BF16)],
        compiler_params=_params(("parallel", "arbitrary"), 56),
        name="inproj",
    )(hbuf, *weights)


FILT_TR = 640
FEAT_PAD = 128


def _filt_tables():
    pos = np.arange(TFFT, dtype=np.float64)
    t = pos / (L_TOK - 1)
    w = (2.0 * np.pi / L_TOK) * pos
    bands = 1e-4 + np.arange(FILT_BANDS, dtype=np.float64) * ((FILT_BANDS - 1 - 1e-4) / (FILT_BANDS - 1))
    ang = w[:, None] * bands[None, :]
    feats = np.zeros((TFFT, FEAT_PAD), np.float64)
    feats[:, 0] = t
    feats[:, 1:1 + FILT_BANDS] = np.cos(ang)
    feats[:, 1 + FILT_BANDS:FILT_EMB] = -np.sin(ang)
    lo, hi = np.log(1e-2) / 1.5, np.log(1e-2) / 0.3
    deltas = np.abs(lo + np.arange(HY_WIDTH, dtype=np.float64) * ((hi - lo) / (HY_WIDTH - 1)))
    return feats.astype(np.float32), deltas.astype(np.float32).reshape(1, HY_WIDTH)


def _filt_body(feat_ref, t_ref, w1_ref, b1_ref, w2_ref, b2_ref, w3_ref, fr_ref, dl_ref, h_ref):
    start = pl.program_id(0) * FILT_TR

    @pl.when(start < L_TOK)
    def _():
        hp = lax.Precision.HIGHEST
        fr = fr_ref[...]
        z = jnp.sin(fr * (jnp.dot(w1_ref[...], feat_ref[...], precision=hp, preferred_element_type=F32)
                          + b1_ref[...]))
        z = jnp.sin(fr * (jnp.dot(w2_ref[...], z, precision=hp, preferred_element_type=F32) + b2_ref[...]))
        hh = lax.dot_general(z, w3_ref[...], (((0,), (0,)), ((), ())), precision=hp,
                             preferred_element_type=F32)
        pos = start + lax.broadcasted_iota(jnp.int32, (FILT_TR, 1), 0)
        win = jnp.exp(-t_ref[...] * dl_ref[...])
        win = jnp.where(pos < L_TOK, win, 0.0)
        hf = hh[:, :HY_WIDTH] * win
        hb = hh[:, HY_WIDTH:] * win
        h_ref[0] = jnp.where(pos == 0, hf + hb, hf)
        h_ref[1] = jnp.where(pos == 0, 0.0, hb)

    @pl.when(start >= L_TOK)
    def _():
        h_ref[...] = jnp.zeros_like(h_ref)


def _filters(w1, b1, w2, b2, w3, freq):
    feats, deltas = _filt_tables()
    w1t = jnp.pad(w1, ((0, FEAT_PAD - FILT_EMB), (0, 0))).T
    col = lambda v: v.reshape(-1, 1)
    full = lambda shape: pl.BlockSpec(shape, lambda i: (0, 0))
    return pl.pallas_call(
        _filt_body,
        out_shape=jax.ShapeDtypeStruct((2, TFFT, HY_WIDTH), F32),
        grid=(TFFT // FILT_TR,),
        in_specs=[
            pl.BlockSpec((FEAT_PAD, FILT_TR), lambda i: (0, i)),
            pl.BlockSpec((FILT_TR, 1), lambda i: (i, 0)),
            full((FILT_HIDDEN, FEAT_PAD)), full((FILT_HIDDEN, 1)),
            full((FILT_HIDDEN, FILT_HIDDEN)), full((FILT_HIDDEN, 1)),
            full((FILT_HIDDEN, 2 * HY_WIDTH)), full((FILT_HIDDEN, 1)), full((1, HY_WIDTH)),
        ],
        out_specs=pl.BlockSpec((2, FILT_TR, HY_WIDTH), lambda i: (0, i, 0)),
        compiler_params=_params(("parallel",), 40),
        name="filt",
    )(jnp.asarray(feats.T), jnp.asarray(feats[:, 0:1]), w1t, col(b1), w2.T, col(b2), w3, col(freq),
      jnp.asarray(deltas))


def _dft_tables_np():
    n1 = np.arange(FFT_N1)
    ang1 = 2.0 * np.pi * ((n1[:, None] * n1[None, :]) % FFT_N1) / FFT_N1
    c1, s1 = np.cos(ang1)[:, :FFT_R1], np.sin(ang1)[:, :FFT_R1]
    f1 = np.block([[c1, s1], [-s1, c1]])
    f3 = np.block([[c1.T, -s1.T], [s1.T, c1.T]])
    k1 = np.arange(FFT_N1)[:, None, None]
    k2 = np.arange(FFT_N2)[None, :, None]
    n2 = np.arange(FFT_N2)[None, None, :]
    ang2 = 2.0 * np.pi * ((n2 * (k1 + FFT_N1 * k2)) % NFFT) / NFFT
    c2, s2 = np.cos(ang2), np.sin(ang2)
    mf = np.concatenate([np.concatenate([c2, s2], axis=2), np.concatenate([-s2, c2], axis=2)], axis=1)
    c2t, s2t = np.swapaxes(c2, 1, 2), np.swapaxes(s2, 1, 2)
    mi = np.concatenate([np.concatenate([c2t, -s2t], axis=2), np.concatenate([s2t, c2t], axis=2)], axis=1)
    k1r = np.arange(FFT_N1)
    k2r = np.arange(FFT_N2)
    perm = np.where(k1r[:, None] == 0, (FFT_N2 - k2r[None, :]) % FFT_N2, FFT_N2 - 1 - k2r[None, :])
    rows = np.concatenate([perm, perm + FFT_N2], axis=1)
    mirror = mf[((FFT_N1 - k1r) % FFT_N1)[:, None], rows, :]
    return f1, f3, mf, mi, mirror


def _dft_tables():
    return tuple(jnp.asarray(a.astype(np.float32)).astype(BF16) for a in _dft_tables_np())


LMM_PITCH = FFT_N2 + SUBLANE
LMM_MC = 2 * FFT_N1


def _lmm_body(f_ref, x_ref, o_ref, xs_ref, os_ref, *, scale):
    m, k = f_ref.shape
    for g in range(k):
        xs_ref[pl.ds(g * LMM_PITCH, FFT_N2), :] = x_ref[pl.ds(g * FFT_N2, FFT_N2), :].astype(F32)
    for m0 in range(0, m, LMM_MC):
        f = f_ref[m0:min(m0 + LMM_MC, m), :]

        def slab(s, carry):
            x = xs_ref[pl.ds(s, k, stride=LMM_PITCH), :].astype(BF16)
            r = jnp.dot(f, x, preferred_element_type=F32)
            os_ref[pl.ds(s, f.shape[0], stride=LMM_PITCH), :] = r * scale if scale != 1.0 else r
            return carry

        lax.fori_loop(0, FFT_N2, slab, 0, unroll=8)
        for g in range(f.shape[0]):
            o_ref[pl.ds((m0 + g) * FFT_N2, FFT_N2), :] = os_ref[pl.ds(g * LMM_PITCH, FFT_N2), :].astype(
                o_ref.dtype)


def _lmm(f, x3, out_dtype, scale=1.0, name="lmm"):
    m, k = f.shape
    mc = min(m, LMM_MC)
    assert m % mc == 0
    out = pl.pallas_call(
        functools.partial(_lmm_body, scale=scale),
        out_shape=jax.ShapeDtypeStruct((m * FFT_N2, HY_WIDTH), out_dtype),
        grid=(HY_WIDTH // LANE,),
        in_specs=[pl.BlockSpec((m, k), lambda j: (0, 0)),
                  pl.BlockSpec((k * FFT_N2, LANE), lambda j: (0, j))],
        out_specs=pl.BlockSpec((m * FFT_N2, LANE), lambda j: (0, j)),
        scratch_shapes=[pltpu.VMEM((k * LMM_PITCH, LANE), F32), pltpu.VMEM((mc * LMM_PITCH, LANE), F32)],
        compiler_params=_params(("parallel",), 56),
        name=name,
    )(f, x3.reshape(k * FFT_N2, HY_WIDTH))
    return out.reshape(m, FFT_N2, HY_WIDTH)


SPEC_K1 = 4


def _spec_body(mf_ref, mr_ref, mi_ref, a_ref, z_ref, *rest):
    zm_refs, o_ref = rest[:SPEC_K1], rest[SPEC_K1]
    for t in range(SPEC_K1):
        mf = mf_ref[t]
        z = jnp.dot(mf, z_ref[:, t].reshape(2 * FFT_N2, HY_WIDTH), preferred_element_type=F32)
        zm = jnp.dot(mr_ref[t], zm_refs[t][...].reshape(2 * FFT_N2, HY_WIDTH), preferred_element_type=F32)
        a, b = z[:FFT_N2], z[FFT_N2:]
        am, bm = zm[:FFT_N2], zm[FFT_N2:]
        gr, gi = 0.5 * (a + am + b + bm), 0.5 * (b - bm + a - am)

        x = jnp.dot(mf, a_ref[:, t].reshape(2 * FFT_N2, HY_WIDTH), preferred_element_type=F32)
        xr, xi = x[:FFT_N2], x[FFT_N2:]
        y = jnp.concatenate([xr * gr - xi * gi, xr * gi + xi * gr], axis=0).astype(BF16)
        o_ref[:, t] = jnp.dot(mi_ref[t], y, preferred_element_type=F32).astype(BF16).reshape(
            2, FFT_N2, HY_WIDTH)


def _spec(mf, mirror, mi, a, zs):
    here = pl.BlockSpec((2, SPEC_K1, FFT_N2, HY_WIDTH), lambda i: (0, i, 0, 0))
    mat = pl.BlockSpec((SPEC_K1, 2 * FFT_N2, 2 * FFT_N2), lambda i: (i, 0, 0))
    mirrored = [pl.BlockSpec((2, 1, FFT_N2, HY_WIDTH),
                             lambda i, t=t: (0, (FFT_N1 - (SPEC_K1 * i + t)) % FFT_N1, 0, 0))
                for t in range(SPEC_K1)]
    return pl.pallas_call(
        _spec_body,
        out_shape=jax.ShapeDtypeStruct((2, FFT_N1, FFT_N2, HY_WIDTH), BF16),
        grid=(FFT_N1 // SPEC_K1,),
        in_specs=[mat, mat, mat, here, here] + mirrored,
        out_specs=here,
        compiler_params=_params(("parallel",), 40),
        name="spec",
    )(mf, mirror, mi, a, zs, *([zs] * SPEC_K1))


HY_CB = 256


def _short_conv(p_ref, w_ref, b_ref):
    p = p_ref[...].astype(F32)
    w = w_ref[...]
    prev = pltpu.roll(p, 1, 0)
    nxt = pltpu.roll(p, TL - 1, 0)
    return b_ref[...] + prev * w[0:1] + p * w[1:2] + nxt * w[2:3]


def _uconv_body(x1_ref, vh_ref, w1_ref, wv_ref, b1_ref, bv_ref, u_ref):
    u = _short_conv(vh_ref, wv_ref, bv_ref) * _short_conv(x1_ref, w1_ref, b1_ref)
    row = lax.broadcasted_iota(jnp.int32, (TL, 1), 0)
    u_ref[pl.ds(0, TL), :] = jnp.where(row >= PAD, u, 0.0)
    u_ref[pl.ds(TL, TFFT - TL), :] = jnp.zeros((TFFT - TL, HY_CB), F32)


def _hy_specs(first_block):
    nb = HY_WIDTH // HY_CB
    return (pl.BlockSpec((None, TL, HY_CB), lambda b, j: (b, 0, first_block * nb + j)),
            pl.BlockSpec((3, HY_CB), lambda b, j: (0, first_block * nb + j)),
            pl.BlockSpec((1, HY_CB), lambda b, j: (0, first_block * nb + j)))


def _uconv(p3, conv_w, conv_b):
    x1, w1, b1 = _hy_specs(1)
    vh, wv, bv = _hy_specs(2)
    return pl.pallas_call(
        _uconv_body,
        out_shape=jax.ShapeDtypeStruct((BATCH, TFFT, HY_WIDTH), F32),
        grid=(BATCH, HY_WIDTH // HY_CB),
        in_specs=[x1, vh, w1, wv, b1, bv],
        out_specs=pl.BlockSpec((None, TFFT, HY_CB), lambda b, j: (b, 0, j)),
        compiler_params=_params(("parallel", "parallel"), 48),
        name="uconv",
    )(p3, p3, conv_w, conv_w, conv_b, conv_b)


def _ymix_body(y_ref, x0_ref, x1_ref, vh_ref, w0_ref, w1_ref, wv_ref, b0_ref, b1_ref, bv_ref,
               d_ref, gain_ref, o_ref):
    u = _short_conv(vh_ref, wv_ref, bv_ref) * _short_conv(x1_ref, w1_ref, b1_ref)
    yy = (y_ref[...] + d_ref[...] * u) * _short_conv(x0_ref, w0_ref, b0_ref)
    gain = gain_ref[...]
    for s in range(0, HY_CB, HY_GROUP):
        o_ref[:, s:s + HY_GROUP] = (_rms(yy[X0:, s:s + HY_GROUP]) * gain[:, s:s + HY_GROUP]).astype(BF16)


def _ymix(y3, p3, conv_w, conv_b, hyena_d, hyena_norm):
    x0, w0, b0 = _hy_specs(0)
    x1, w1, b1 = _hy_specs(1)
    vh, wv, bv = _hy_specs(2)
    vec = pl.BlockSpec((1, HY_CB), lambda b, j: (0, j))
    return pl.pallas_call(
        _ymix_body,
        out_shape=jax.ShapeDtypeStruct((BATCH, SEQ, HY_WIDTH), BF16),
        grid=(BATCH, HY_WIDTH // HY_CB),
        in_specs=[pl.BlockSpec((None, TL, HY_CB), lambda b, j: (b, 0, j)),
                  x0, x1, vh, w0, w1, wv, b0, b1, bv, vec, vec],
        out_specs=pl.BlockSpec((None, SEQ, HY_CB), lambda b, j: (b, 0, j)),
        compiler_params=_params(("parallel", "parallel"), 56),
        name="ymix",
    )(y3, p3, p3, p3, conv_w, conv_w, conv_w, conv_b, conv_b, conv_b,
      hyena_d.reshape(1, -1), hyena_norm.reshape(1, -1))


N_SUB = CHUNK // SUB
SAFE_BLOCK_DECAY = -60.0


def _scores_exact(q, k, b, rev, ones, row, col):
    rsub = row % SUB
    terms = []
    for d in range(SUB):
        if d == 0:
            kr, br = k, b
        else:
            sh = CHUNK - d if rev else d
            kr, br = pltpu.roll(k, sh, 0), pltpu.roll(b, sh, 0)
        valid = (rsub + d < SUB) if rev else (rsub >= d)
        t = q * kr * jnp.exp(jnp.minimum(b - br, 0.0))
        terms.append(jnp.where(valid, t, 0.0).astype(BF16))
    sums = jnp.dot(jnp.concatenate(terms, axis=0), ones, preferred_element_type=F32)
    a = jnp.zeros((CHUNK, CHUNK), F32)
    for d in range(SUB):
        tgt = row + d if rev else row - d
        a = jnp.where(col == tgt, sums[d * CHUNK:(d + 1) * CHUNK, :CHUNK], a)

    rblk = row // SUB
    cblk = col // SUB
    for jb in (range(1, N_SUB) if rev else range(N_SUB - 1)):
        e = jb * SUB if rev else jb * SUB + SUB - 1
        ref = b[e:e + 1, :]
        qh = (q * jnp.exp(jnp.minimum(b - ref, 0.0))).astype(BF16)
        kh = (k * jnp.exp(jnp.minimum(ref - b, 0.0))).astype(BF16)
        pm = lax.dot_general(qh, kh, (((1,), (1,)), ((), ())), preferred_element_type=F32)
        side = jnp.where(cblk == jb, rblk, jb)
        a = jnp.where((side < jb) if rev else (side > jb), pm, a)

    bend = b[0:1, :] if rev else b[CHUNK - 1:CHUNK, :]
    return a, q * jnp.exp(b), k * jnp.exp(bend - b), bend


def _scores_fast(q, k, b, rev, row, col):
    order = list(range(N_SUB))[::-1] if rev else list(range(N_SUB))
    pos = {blk: p for p, blk in enumerate(order)}
    edge = lambda blk: blk * SUB if rev else blk * SUB + SUB - 1
    e = [b[edge(blk):edge(blk) + 1, :] for blk in order]
    s = [jnp.zeros((1, GLA_DK), F32)] + e[:-1]

    def by_row(vals):
        return jnp.concatenate([jnp.broadcast_to(vals[pos[blk]], (SUB, GLA_DK)) for blk in range(N_SUB)],
                               axis=0)

    srow, erow = by_row(s), by_row(e)
    qh = q * jnp.exp(b - srow)
    kh = k * jnp.exp(erow - b)
    kd = k * jnp.exp(srow - b)

    lhs = []
    for pj in range(N_SUB - 1):
        for blk in range(N_SUB):
            piece = qh[blk * SUB:(blk + 1) * SUB, :]
            p = pos[blk]
            if p <= pj:
                piece = jnp.zeros_like(piece)
            elif p > pj + 1:
                piece = piece * jnp.exp(s[p] - e[pj])
            lhs.append(piece.astype(BF16))
    contract = (((1,), (1,)), ((), ()))
    cross = lax.dot_general(jnp.concatenate(lhs, axis=0), kh.astype(BF16), contract,
                            preferred_element_type=F32)
    diag = lax.dot_general(qh.astype(BF16), kd.astype(BF16), contract, preferred_element_type=F32)

    rblk = row // SUB
    cblk = col // SUB
    a = jnp.zeros((CHUNK, CHUNK), F32)
    for pj in range(N_SUB - 1):
        a = jnp.where(cblk == order[pj], cross[pj * CHUNK:(pj + 1) * CHUNK], a)
    causal = (col >= row) if rev else (col <= row)
    a = jnp.where(cblk == rblk, jnp.where(causal, diag, 0.0), a)

    bend = e[-1]
    return a, qh * jnp.exp(srow), kh * jnp.exp(bend - erow), bend


GLA_RB = 1408


def _gla_sweep_body(*refs, rev):
    if rev:
        q_ref, k_ref, v_ref, g_ref, o_ref, st_ref, b_ref = refs
    else:
        q_ref, k_ref, v_ref, g_ref, og_ref, ob_ref, gain_ref, o_ref, st_ref, b_ref = refs

    @pl.when(pl.program_id(1) == 0)
    def _():
        st_ref[...] = jnp.zeros_like(st_ref)

    row = lax.broadcasted_iota(jnp.int32, (CHUNK, 1), 0)
    col = lax.broadcasted_iota(jnp.int32, (CHUNK, CHUNK), 1)
    rr = lax.broadcasted_iota(jnp.int32, (CHUNK, CHUNK), 0)
    tri = ((col >= rr) if rev else (col <= rr)).astype(BF16)
    ones = jnp.ones((GLA_DK, GLA_DK), BF16)
    n_chunk = GLA_RB // CHUNK

    for c in range(n_chunk):
        g = g_ref[c * CHUNK:(c + 1) * CHUNK, :]
        g_hi = g.astype(BF16)
        rest = g - g_hi.astype(F32)
        g_mid = rest.astype(BF16)
        g_lo = (rest - g_mid.astype(F32)).astype(BF16)
        b_ref[c * CHUNK:(c + 1) * CHUNK, :] = (
            jnp.dot(tri, g_hi, preferred_element_type=F32) + jnp.dot(tri, g_mid, preferred_element_type=F32)
            + jnp.dot(tri, g_lo, preferred_element_type=F32))

    def run(fast):
        def step(t, carry):
            c = n_chunk - 1 - t if rev else t
            rows = pl.ds(pl.multiple_of(c * CHUNK, CHUNK), CHUNK)
            b_all = b_ref[rows, :]
            pending = []
            for h in range(GLA_HEADS):
                kc = slice(h * GLA_DK, (h + 1) * GLA_DK)
                vc = slice(h * GLA_DV, (h + 1) * GLA_DV)
                q = q_ref[rows, kc].astype(F32) * (GLA_DK ** -0.5)
                k = k_ref[rows, kc].astype(F32)
                v = v_ref[rows, vc].astype(BF16)
                b = b_all[:, kc]
                if fast:
                    a, qt, kt, bend = _scores_fast(q, k, b, rev, row, col)
                else:
                    a, qt, kt, bend = _scores_exact(q, k, b, rev, ones, row, col)
                st = st_ref[h]
                o = lax.dot_general(qt.astype(BF16), st.astype(BF16), (((1,), (1,)), ((), ())),
                                    preferred_element_type=F32)
                st_ref[h] = st * jnp.exp(bend) + lax.dot_general(
                    v, kt.astype(BF16), (((0,), (0,)), ((), ())), preferred_element_type=F32)
                pending.append((vc, o, a, v))
            for vc, o, a, v in pending:
                o = o + jnp.dot(a.astype(BF16), v, preferred_element_type=F32)
                if rev:
                    o_ref[rows, vc] = o
                else:
                    o = _rms(o + ob_ref[rows, vc]) * gain_ref[...]
                    o_ref[rows, vc] = (o * _silu(og_ref[rows, vc].astype(F32))).astype(BF16)
            return carry

        lax.fori_loop(0, n_chunk, step, 0, unroll=2 if fast else 1)

    low = jnp.min(jnp.sum(g_ref[...].reshape(GLA_RB // SUB, SUB, GLA_KEY_WIDTH), axis=1))
    safe = low > SAFE_BLOCK_DECAY
    pl.when(safe)(lambda: run(True))
    pl.when(jnp.logical_not(safe))(lambda: run(False))


def _gla_sweep(p3, lg3, rev, ob=None, gla_norm=None):
    nb = TL // GLA_RB
    blk = (lambda i: nb - 1 - i) if rev else (lambda i: i)
    key_blocks = P_MAIN // GLA_KEY_WIDTH
    q_col = 3 * HY_WIDTH // GLA_KEY_WIDTH
    v_col = (3 * HY_WIDTH + 2 * GLA_KEY_WIDTH) // GLA_WIDTH
    assert key_blocks * GLA_KEY_WIDTH == P_MAIN
    narrow = lambda col: pl.BlockSpec((None, GLA_RB, GLA_KEY_WIDTH), lambda b, i: (b, blk(i), col))
    wide = lambda col: pl.BlockSpec((None, GLA_RB, GLA_WIDTH), lambda b, i: (b, blk(i), col))
    in_specs = [narrow(q_col), narrow(q_col + 1), wide(v_col), narrow(1 if rev else 0)]
    args = [p3, p3, p3, lg3]
    if not rev:
        in_specs += [wide(v_col + 1), wide(0), pl.BlockSpec((1, GLA_DV), lambda b, i: (0, 0))]
        args += [p3, ob, gla_norm.reshape(1, -1)]
    return pl.pallas_call(
        functools.partial(_gla_sweep_body, rev=rev),
        out_shape=jax.ShapeDtypeStruct((BATCH, TL, GLA_WIDTH), F32 if rev else BF16),
        grid=(BATCH, nb),
        in_specs=in_specs,
        out_specs=wide(0),
        scratch_shapes=[pltpu.VMEM((GLA_HEADS, GLA_DV, GLA_DK), F32),
                        pltpu.VMEM((GLA_RB, GLA_KEY_WIDTH), F32)],
        compiler_params=_params(("parallel", "arbitrary"), 52),
        name="gla_down" if rev else "gla_up",
    )(*args)


OUT_TM = 512


def _outproj_body(h_ref, yh_ref, yg_ref, wh_ref, wg_ref, o_ref):
    o_ref[...] = (h_ref[...] + jnp.dot(yh_ref[...], wh_ref[...], preferred_element_type=F32)
                  + jnp.dot(yg_ref[...], wg_ref[...], preferred_element_type=F32))


def _outproj(hbuf, yh, ygbuf, w_out):
    wo = _to_bf16(w_out)
    half = lambda i: pl.BlockSpec((HY_WIDTH, D_MODEL), lambda r: (i, 0))
    return pl.pallas_call(
        _outproj_body,
        out_shape=jax.ShapeDtypeStruct((X_ROWS, D_MODEL), F32),
        grid=(X_ROWS // OUT_TM,),
        in_specs=[pl.BlockSpec((pl.Element(OUT_TM), pl.Element(D_MODEL)), lambda r: (_x_row(r, OUT_TM), 0)),
                  pl.BlockSpec((OUT_TM, HY_WIDTH), lambda r: (r, 0)),
                  pl.BlockSpec((pl.Element(OUT_TM), pl.Element(GLA_WIDTH)),
                               lambda r: (_x_row(r, OUT_TM, BF16_SUBLANE), 0)),
                  half(0), half(1)],
        out_specs=pl.BlockSpec((OUT_TM, D_MODEL), lambda r: (r, 0)),
        compiler_params=_params(("parallel",), 48),
        name="outproj",
    )(hbuf, yh, ygbuf, wo, wo)


def _long_conv(u3, filt):
    f1, f3, mf, mi, mirror = _dft_tables()
    as_n1 = lambda a: a.reshape(2 * FFT_R1, FFT_N2, HY_WIDTH)
    by_k1 = lambda a: a.reshape(2, FFT_N1, FFT_N2, HY_WIDTH)
    zs = _lmm(f1, as_n1(filt), BF16, name="lmm_g")
    a = _lmm(f1, as_n1(u3), BF16, name="lmm_fwd")
    bm = _spec(mf, mirror, mi, by_k1(a), by_k1(zs))
    y = _lmm(f3, bm.reshape(2 * FFT_N1, FFT_N2, HY_WIDTH), F32, scale=1.0 / NFFT, name="lmm_inv")
    return y.reshape(BATCH, TFFT, HY_WIDTH)


def kernel(x, meta_tokens, ffn1_norm, ffn1_w_gate, ffn1_w_up, ffn1_w_down, mix_norm, w_in, conv_w, conv_b,
           filt_w1, filt_b1, filt_w2, filt_b2, filt_w3, filt_freq, hyena_d, hyena_norm, gk_w2, gk_b2,
           gla_norm, w_out, ffn2_norm, ffn2_w_gate, ffn2_w_up, ffn2_w_down, final_norm):
    assert x.shape == (BATCH, SEQ, D_MODEL) and ffn1_norm.shape[0] == 1

    hbuf = _ffn(x.reshape(X_ROWS, D_MODEL), ffn1_norm[0], ffn1_w_gate[0], ffn1_w_up[0], ffn1_w_down[0],
                final_norm, final=False, shared_rows=_meta_rows(meta_tokens))

    p, lg = _inproj(hbuf, _inproj_weights(mix_norm[0], w_in[0], gk_w2[0], gk_b2[0]))
    p3 = p.reshape(BATCH, TL, P_MAIN)
    lg3 = lg.reshape(BATCH, TL, 2 * GLA_KEY_WIDTH)

    filt = _filters(filt_w1[0], filt_b1[0], filt_w2[0], filt_b2[0], filt_w3[0], filt_freq[0])
    cw, cb = conv_w[0], conv_b[0].reshape(1, -1)
    y3 = _long_conv(_uconv(p3, cw, cb), filt)
    yh = _ymix(y3, p3, cw, cb, hyena_d[0], hyena_norm[0])
    yg = _gla_sweep(p3, lg3, rev=False, ob=_gla_sweep(p3, lg3, rev=True), gla_norm=gla_norm[0])

    h2 = _outproj(hbuf, yh.reshape(X_ROWS, HY_WIDTH), yg.reshape(ROWS, GLA_WIDTH), w_out[0])
    out = _ffn(h2, ffn2_norm[0], ffn2_w_gate[0], ffn2_w_up[0], ffn2_w_down[0], final_norm, final=True)
    return out.reshape(BATCH, SEQ, D_MODEL)
```

```python
import functools

import numpy as np
import jax
import jax.numpy as jnp
from jax import lax
from jax.experimental import pallas as pl
from jax.experimental.pallas import tpu as pltpu

F32 = jnp.float32
BF16 = jnp.bfloat16

D_MODEL = 2048
BATCH = 2
SEQ = 4096
N_META = 16
L_TOK = SEQ + N_META
PAD = 112
X0 = PAD + N_META
TL = PAD + L_TOK
ROWS = BATCH * TL
X_ROWS = BATCH * SEQ
HY_WIDTH = 1024
HY_GROUPS = 8
HY_GROUP = HY_WIDTH // HY_GROUPS
FILT_EMB = 33
FILT_BANDS = 16
FILT_HIDDEN = 64
GLA_WIDTH = 1024
GLA_HEADS = 4
GLA_KEY_WIDTH = 512
GLA_DK = 128
GLA_DV = 256
GATE_RANK = 16
GATE_NORMALIZER = 16.0
CHUNK = 64
SUB = 16
D_FF = 5632
P_MAIN = 3 * HY_WIDTH + 2 * GLA_KEY_WIDTH + 2 * GLA_WIDTH
EPS = 1e-6

FFT_N1 = 72
FFT_N2 = 128
NFFT = FFT_N1 * FFT_N2
FFT_R1 = 40
TFFT = FFT_R1 * FFT_N2

MIB = 1024 * 1024
SUBLANE = 8
BF16_SUBLANE = 16
LANE = 128


def _params(sem, vmem_mib):
    return pltpu.CompilerParams(dimension_semantics=sem, vmem_limit_bytes=vmem_mib * MIB)


def _rms(x):
    return x * lax.rsqrt(jnp.mean(x * x, axis=-1, keepdims=True) + EPS)


def _silu(x):
    return x * jax.nn.sigmoid(x)


LAY_TM = 528


def _x_row(i, tm, unit=SUBLANE):
    per_batch = SEQ // tm
    r = (i // per_batch) * (TL // unit) + X0 // unit + (i % per_batch) * (tm // unit)
    return pl.multiple_of(r * unit, unit)


def _lay_src_row(i):
    per_batch = TL // LAY_TM
    r = (i // per_batch) * (SEQ // SUBLANE) + jnp.maximum(
        (i % per_batch) * (LAY_TM // SUBLANE) - X0 // SUBLANE, 0)
    return pl.multiple_of(r * SUBLANE, SUBLANE)


def _meta_rows(meta_tokens):
    return jnp.concatenate([jnp.zeros((PAD, D_MODEL), F32), meta_tokens.astype(F32)], axis=0)


FFN_TM = 512
FFN_TF = 512


FFN_HEAD_TF = 256


def _ffn_begin(x_ref, shared_ref, normed_ref, xin_ref, gain_ref, xn_ref, acc_ref, first):
    if shared_ref is not None:
        @pl.when(first)
        def _():
            xin_ref[0:X0, :] = shared_ref[...]
            xin_ref[X0:LAY_TM, :] = x_ref[0:LAY_TM - X0, :]

        @pl.when(jnp.logical_not(first))
        def _():
            xin_ref[...] = x_ref[...]

    if normed_ref is not None:
        xn_ref[...] = normed_ref[...]
    else:
        xn_ref[...] = (_rms(xin_ref[...]) * gain_ref[...]).astype(BF16)
    acc_ref[...] = jnp.zeros_like(acc_ref)


def _ffn_refs(rest, layout, prenorm, n_out):
    rest = list(rest)
    shared_ref = rest.pop(0) if layout else None
    normed_ref = rest.pop(0) if prenorm else None
    outs, scratch = rest[:n_out], rest[n_out:]
    return shared_ref, normed_ref, outs, scratch


def _ffn_step(xn_ref, wg, wu, wd, acc_ref):
    xn = xn_ref[...]
    g = jnp.dot(xn, wg, preferred_element_type=F32)
    u = jnp.dot(xn, wu, preferred_element_type=F32)
    a = (_silu(g) * u).astype(BF16)
    acc_ref[...] += jnp.dot(a, wd, preferred_element_type=F32)


def _ffn_result(xin_ref, acc_ref, fgain_ref, final):
    h = xin_ref[...] + 0.5 * acc_ref[...]
    return _rms(h) * fgain_ref[...] if final else h


def _ffn_head_body(x_ref, gain_ref, wg_ref, wu_ref, wd_ref, fgain_ref, *rest, final, layout, prenorm):
    shared_ref, normed_ref, outs, scratch = _ffn_refs(rest, layout, prenorm, 4)
    o_ref, wg16_ref, wu16_ref, wd16_ref = outs
    xn_ref, acc_ref = scratch[:2]
    xin_ref = scratch[2] if layout else x_ref
    j = pl.program_id(0)

    @pl.when(j == 0)
    def _():
        _ffn_begin(x_ref, shared_ref, normed_ref, xin_ref, gain_ref, xn_ref, acc_ref, True)

    wg, wu, wd = wg_ref[...].astype(BF16), wu_ref[...].astype(BF16), wd_ref[...].astype(BF16)
    wg16_ref[...] = wg
    wu16_ref[...] = wu
    wd16_ref[...] = wd
    _ffn_step(xn_ref, wg, wu, wd, acc_ref)

    @pl.when(j == pl.num_programs(0) - 1)
    def _():
        o_ref[...] = _ffn_result(xin_ref, acc_ref, fgain_ref, final)


def _ffn_body(x_ref, gain_ref, wg_ref, wu_ref, wd_ref, fgain_ref, head_ref, *rest, final, layout, prenorm):
    shared_ref, normed_ref, (o_ref,), scratch = _ffn_refs(rest, layout, prenorm, 1)
    xn_ref, acc_ref = scratch[:2]
    xin_ref = scratch[2] if layout else x_ref
    i, j = pl.program_id(0), pl.program_id(1)
    last = j == pl.num_programs(1) - 1

    @pl.when(i > 0)
    def _():
        @pl.when(j == 0)
        def _():
            _ffn_begin(x_ref, shared_ref, normed_ref, xin_ref, gain_ref, xn_ref, acc_ref,
                       i % (TL // LAY_TM) == 0)

        _ffn_step(xn_ref, wg_ref[...], wu_ref[...], wd_ref[...], acc_ref)

        @pl.when(last)
        def _():
            o_ref[...] = _ffn_result(xin_ref, acc_ref, fgain_ref, final)

    @pl.when(jnp.logical_and(i == 0, last))
    def _():
        o_ref[...] = head_ref[...]


CAST_STEPS = 4


def _cast_body(x_ref, o_ref):
    o_ref[...] = x_ref[...].astype(BF16)


def _to_bf16(w, cols=None):
    rows = w.shape[0]
    cols = w.shape[1] if cols is None else cols
    tr = rows // CAST_STEPS
    assert tr * CAST_STEPS == rows and tr % BF16_SUBLANE == 0 and cols % LANE == 0
    return pl.pallas_call(
        _cast_body,
        out_shape=jax.ShapeDtypeStruct((rows, cols), BF16),
        grid=(CAST_STEPS,),
        in_specs=[pl.BlockSpec((tr, cols), lambda i: (i, 0))],
        out_specs=pl.BlockSpec((tr, cols), lambda i: (i, 0)),
        compiler_params=_params(("parallel",), 48),
        name="to_bf16",
    )(w)


def _ffn(x2, gain, wg, wu, wd, fgain, final, shared_rows=None, normed=None):
    layout = shared_rows is not None
    prenorm = normed is not None
    assert not (layout and prenorm)
    tm = LAY_TM if layout else FFN_TM
    rows = ROWS if layout else X_ROWS
    gain, fgain = gain.reshape(1, -1), fgain.reshape(1, -1)
    scratch = [pltpu.VMEM((tm, D_MODEL), BF16), pltpu.VMEM((tm, D_MODEL), F32)]
    extra = ()
    if layout:
        extra = (shared_rows,)
        scratch.append(pltpu.VMEM((tm, D_MODEL), F32))
    if prenorm:
        extra = (normed,)
    name = "ffn_final" if final else "ffn"
    flags = dict(final=final, layout=layout, prenorm=prenorm)

    tf = FFN_HEAD_TF
    vec1 = pl.BlockSpec((1, D_MODEL), lambda j: (0, 0))
    wide1 = pl.BlockSpec((D_MODEL, tf), lambda j: (0, j))
    tall1 = pl.BlockSpec((tf, D_MODEL), lambda j: (j, 0))
    tile1 = pl.BlockSpec((tm, D_MODEL), lambda j: (0, 0))
    head, wg16, wu16, wd16 = pl.pallas_call(
        functools.partial(_ffn_head_body, **flags),
        out_shape=(jax.ShapeDtypeStruct((tm, D_MODEL), F32), jax.ShapeDtypeStruct(wg.shape, BF16),
                   jax.ShapeDtypeStruct(wu.shape, BF16), jax.ShapeDtypeStruct(wd.shape, BF16)),
        grid=(D_FF // tf,),
        in_specs=[tile1, vec1, wide1, wide1, tall1, vec1]
        + [pl.BlockSpec((X0, D_MODEL), lambda j: (0, 0))] * layout + [tile1] * prenorm,
        out_specs=(tile1, wide1, wide1, tall1),
        scratch_shapes=scratch,
        compiler_params=_params(("arbitrary",), 56),
        name=name + "_head",
    )(x2, gain, wg, wu, wd, fgain, *extra)

    col = lambda i, j: jnp.where(i == 0, 0, j)
    vec = pl.BlockSpec((1, D_MODEL), lambda i, j: (0, 0))
    wide = pl.BlockSpec((D_MODEL, FFN_TF), lambda i, j: (0, col(i, j)))
    tall = pl.BlockSpec((FFN_TF, D_MODEL), lambda i, j: (col(i, j), 0))
    if layout:
        x_spec = pl.BlockSpec((pl.Element(tm), pl.Element(D_MODEL)), lambda i, j: (_lay_src_row(i), 0))
    else:
        x_spec = pl.BlockSpec((tm, D_MODEL), lambda i, j: (i, 0))
    return pl.pallas_call(
        functools.partial(_ffn_body, **flags),
        out_shape=jax.ShapeDtypeStruct((rows, D_MODEL), F32),
        grid=(rows // tm, D_FF // FFN_TF),
        in_specs=[x_spec, vec, wide, wide, tall, vec, pl.BlockSpec((tm, D_MODEL), lambda i, j: (0, 0))]
        + [pl.BlockSpec((X0, D_MODEL), lambda i, j: (0, 0))] * layout
        + [pl.BlockSpec((tm, D_MODEL), lambda i, j: (i, 0))] * prenorm,
        out_specs=pl.BlockSpec((tm, D_MODEL), lambda i, j: (i, 0)),
        scratch_shapes=scratch,
        compiler_params=_params(("parallel", "arbitrary"), 56),
        name=name,
    )(x2, gain, wg16, wu16, wd16, fgain, head, *extra)


INP_TM = 2 * LAY_TM
INP_TN = 1536
LR_PAD = 128


def _inproj_body(x_ref, gain_ref, w_ref, wlr_ref, w2_ref, b2_ref, p_ref, lg_ref, xn_ref, lr_ref):
    @pl.when(pl.program_id(1) == 0)
    def _():
        xn = (_rms(x_ref[...]) * gain_ref[...]).astype(BF16)
        xn_ref[...] = xn
        lr_ref[...] = jnp.dot(xn, wlr_ref[...], preferred_element_type=F32).astype(BF16)

    z = jnp.dot(lr_ref[...], w2_ref[...], preferred_element_type=F32) + b2_ref[...]
    lg_ref[...] = (jnp.minimum(z, 0.0) - jnp.log(1.0 + jnp.exp(-jnp.abs(z)))) * (1.0 / GATE_NORMALIZER)
    p_ref[...] = jnp.dot(xn_ref[...], w_ref[...], preferred_element_type=F32).astype(BF16)


SPLIT_TC = 1024


def _split_w_in_body(w_ref, wlr_ref, main_ref, lr_ref):
    main_ref[...] = w_ref[...].T.astype(BF16)

    @pl.when(pl.program_id(0) == 0)
    def _():
        rows = jnp.concatenate([wlr_ref[...], jnp.zeros((LR_PAD - 2 * GATE_RANK, D_MODEL), F32)], axis=0)
        lr_ref[...] = rows.T.astype(BF16)


def _split_w_in(w_in):
    w_t = jnp.transpose(w_in)
    return pl.pallas_call(
        _split_w_in_body,
        out_shape=(jax.ShapeDtypeStruct((D_MODEL, P_MAIN), BF16), jax.ShapeDtypeStruct((D_MODEL, LR_PAD), BF16)),
        grid=(P_MAIN // SPLIT_TC,),
        in_specs=[pl.BlockSpec((SPLIT_TC, D_MODEL), lambda i: (i, 0)),
                  pl.BlockSpec((2 * GATE_RANK, D_MODEL), lambda i: (P_MAIN // (2 * GATE_RANK), 0))],
        out_specs=(pl.BlockSpec((D_MODEL, SPLIT_TC), lambda i: (0, i)),
                   pl.BlockSpec((D_MODEL, LR_PAD), lambda i: (0, 0))),
        compiler_params=_params(("arbitrary",), 48),
        name="split_w_in",
    )(w_t, w_t)


def _inproj_weights(gain, w_in, gk_w2, gk_b2):
    w_main, w_lr = _split_w_in(w_in)
    w2 = jnp.zeros((LR_PAD, 2 * GLA_KEY_WIDTH), F32)
    w2 = w2.at[:GATE_RANK, :GLA_KEY_WIDTH].set(gk_w2[0])
    w2 = w2.at[GATE_RANK:2 * GATE_RANK, GLA_KEY_WIDTH:].set(gk_w2[1]).astype(BF16)
    return gain.reshape(1, -1), w_main, w_lr, w2, gk_b2.reshape(1, 2 * GLA_KEY_WIDTH)


def _inproj(hbuf, weights):
    tm, tn = INP_TM, INP_TN
    steps = P_MAIN // tn
    tg = 2 * GLA_KEY_WIDTH // steps
    assert steps * tn == P_MAIN and steps * tg == 2 * GLA_KEY_WIDTH and tg % LANE == 0
    return pl.pallas_call(
        _inproj_body,
        out_shape=(jax.ShapeDtypeStruct((ROWS, P_MAIN), BF16),
                   jax.ShapeDtypeStruct((ROWS, 2 * GLA_KEY_WIDTH), F32)),
        grid=(ROWS // tm, steps),
        in_specs=[
            pl.BlockSpec((tm, D_MODEL), lambda i, j: (i, 0)),
            pl.BlockSpec((1, D_MODEL), lambda i, j: (0, 0)),
            pl.BlockSpec((D_MODEL, tn), lambda i, j: (0, j)),
            pl.BlockSpec((D_MODEL, LR_PAD), lambda i, j: (0, 0)),
            pl.BlockSpec((LR_PAD, tg), lambda i, j: (0, j)),
            pl.BlockSpec((1, tg), lambda i, j: (0, j)),
        ],
        out_specs=(pl.BlockSpec((tm, tn), lambda i, j: (i, j)),
                   pl.BlockSpec((tm, tg), lambda i, j: (i, j))),
        scratch_shapes=[pltpu.VMEM((tm, D_MODEL), BF16), pltpu.VMEM((tm, LR_PAD), BF16)],
        compiler_params=_params(("parallel", "arbitrary"), 56),
        name="inproj",
    )(hbuf, *weights)


FILT_TR = 640
FEAT_PAD = 128


def _filt_tables():
    pos = np.arange(TFFT, dtype=np.float64)
    t = pos / (L_TOK - 1)
    w = (2.0 * np.pi / L_TOK) * pos
    bands = 1e-4 + np.arange(FILT_BANDS, dtype=np.float64) * ((FILT_BANDS - 1 - 1e-4) / (FILT_BANDS - 1))
    ang = w[:, None] * bands[None, :]
    feats = np.zeros((TFFT, FEAT_PAD), np.float64)
    feats[:, 0] = t
    feats[:, 1:1 + FILT_BANDS] = np.cos(ang)
    feats[:, 1 + FILT_BANDS:FILT_EMB] = -np.sin(ang)
    lo, hi = np.log(1e-2) / 1.5, np.log(1e-2) / 0.3
    deltas = np.abs(lo + np.arange(HY_WIDTH, dtype=np.float64) * ((hi - lo) / (HY_WIDTH - 1)))
    return feats.astype(np.float32), deltas.astype(np.float32).reshape(1, HY_WIDTH)


def _filt_body(feat_ref, t_ref, w1_ref, b1_ref, w2_ref, b2_ref, w3_ref, fr_ref, dl_ref, h_ref):
    start = pl.program_id(0) * FILT_TR

    @pl.when(start < L_TOK)
    def _():
        hp = lax.Precision.HIGHEST
        fr = fr_ref[...]
        z = jnp.sin(fr * (jnp.dot(w1_ref[...], feat_ref[...], precision=hp, preferred_element_type=F32)
                          + b1_ref[...]))
        z = jnp.sin(fr * (jnp.dot(w2_ref[...], z, precision=hp, preferred_element_type=F32) + b2_ref[...]))
        hh = lax.dot_general(z, w3_ref[...], (((0,), (0,)), ((), ())), precision=hp,
                             preferred_element_type=F32)
        pos = start + lax.broadcasted_iota(jnp.int32, (FILT_TR, 1), 0)
        win = jnp.exp(-t_ref[...] * dl_ref[...])
        win = jnp.where(pos < L_TOK, win, 0.0)
        hf = hh[:, :HY_WIDTH] * win
        hb = hh[:, HY_WIDTH:] * win
        h_ref[0] = jnp.where(pos == 0, hf + hb, hf)
        h_ref[1] = jnp.where(pos == 0, 0.0, hb)

    @pl.when(start >= L_TOK)
    def _():
        h_ref[...] = jnp.zeros_like(h_ref)


def _filters(w1, b1, w2, b2, w3, freq):
    feats, deltas = _filt_tables()
    w1t = jnp.pad(w1, ((0, FEAT_PAD - FILT_EMB), (0, 0))).T
    col = lambda v: v.reshape(-1, 1)
    full = lambda shape: pl.BlockSpec(shape, lambda i: (0, 0))
    return pl.pallas_call(
        _filt_body,
        out_shape=jax.ShapeDtypeStruct((2, TFFT, HY_WIDTH), F32),
        grid=(TFFT // FILT_TR,),
        in_specs=[
            pl.BlockSpec((FEAT_PAD, FILT_TR), lambda i: (0, i)),
            pl.BlockSpec((FILT_TR, 1), lambda i: (i, 0)),
            full((FILT_HIDDEN, FEAT_PAD)), full((FILT_HIDDEN, 1)),
            full((FILT_HIDDEN, FILT_HIDDEN)), full((FILT_HIDDEN, 1)),
            full((FILT_HIDDEN, 2 * HY_WIDTH)), full((FILT_HIDDEN, 1)), full((1, HY_WIDTH)),
        ],
        out_specs=pl.BlockSpec((2, FILT_TR, HY_WIDTH), lambda i: (0, i, 0)),
        compiler_params=_params(("parallel",), 40),
        name="filt",
    )(jnp.asarray(feats.T), jnp.asarray(feats[:, 0:1]), w1t, col(b1), w2.T, col(b2), w3, col(freq),
      jnp.asarray(deltas))


def _dft_tables_np():
    n1 = np.arange(FFT_N1)
    ang1 = 2.0 * np.pi * ((n1[:, None] * n1[None, :]) % FFT_N1) / FFT_N1
    c1, s1 = np.cos(ang1)[:, :FFT_R1], np.sin(ang1)[:, :FFT_R1]
    f1 = np.block([[c1, s1], [-s1, c1]])
    f3 = np.block([[c1.T, -s1.T], [s1.T, c1.T]])
    k1 = np.arange(FFT_N1)[:, None, None]
    k2 = np.arange(FFT_N2)[None, :, None]
    n2 = np.arange(FFT_N2)[None, None, :]
    ang2 = 2.0 * np.pi * ((n2 * (k1 + FFT_N1 * k2)) % NFFT) / NFFT
    c2, s2 = np.cos(ang2), np.sin(ang2)
    mf = np.concatenate([np.concatenate([c2, s2], axis=2), np.concatenate([-s2, c2], axis=2)], axis=1)
    c2t, s2t = np.swapaxes(c2, 1, 2), np.swapaxes(s2, 1, 2)
    mi = np.concatenate([np.concatenate([c2t, -s2t], axis=2), np.concatenate([s2t, c2t], axis=2)], axis=1)
    k1r = np.arange(FFT_N1)
    k2r = np.arange(FFT_N2)
    perm = np.where(k1r[:, None] == 0, (FFT_N2 - k2r[None, :]) % FFT_N2, FFT_N2 - 1 - k2r[None, :])
    rows = np.concatenate([perm, perm + FFT_N2], axis=1)
    mirror = mf[((FFT_N1 - k1r) % FFT_N1)[:, None], rows, :]
    return f1, f3, mf, mi, mirror


def _dft_tables():
    return tuple(jnp.asarray(a.astype(np.float32)).astype(BF16) for a in _dft_tables_np())


LMM_PITCH = FFT_N2 + SUBLANE
LMM_MC = 2 * FFT_N1


def _lmm_body(f_ref, x_ref, o_ref, xs_ref, os_ref, *, scale):
    m, k = f_ref.shape
    for g in range(k):
        xs_ref[pl.ds(g * LMM_PITCH, FFT_N2), :] = x_ref[pl.ds(g * FFT_N2, FFT_N2), :].astype(F32)
    for m0 in range(0, m, LMM_MC):
        f = f_ref[m0:min(m0 + LMM_MC, m), :]

        def slab(s, carry):
            x = xs_ref[pl.ds(s, k, stride=LMM_PITCH), :].astype(BF16)
            r = jnp.dot(f, x, preferred_element_type=F32)
            os_ref[pl.ds(s, f.shape[0], stride=LMM_PITCH), :] = r * scale if scale != 1.0 else r
            return carry

        lax.fori_loop(0, FFT_N2, slab, 0, unroll=8)
        for g in range(f.shape[0]):
            o_ref[pl.ds((m0 + g) * FFT_N2, FFT_N2), :] = os_ref[pl.ds(g * LMM_PITCH, FFT_N2), :].astype(
                o_ref.dtype)


def _lmm(f, x3, out_dtype, scale=1.0, name="lmm"):
    m, k = f.shape
    mc = min(m, LMM_MC)
    assert m % mc == 0
    out = pl.pallas_call(
        functools.partial(_lmm_body, scale=scale),
        out_shape=jax.ShapeDtypeStruct((m * FFT_N2, HY_WIDTH), out_dtype),
        grid=(HY_WIDTH // LANE,),
        in_specs=[pl.BlockSpec((m, k), lambda j: (0, 0)),
                  pl.BlockSpec((k * FFT_N2, LANE), lambda j: (0, j))],
        out_specs=pl.BlockSpec((m * FFT_N2, LANE), lambda j: (0, j)),
        scratch_shapes=[pltpu.VMEM((k * LMM_PITCH, LANE), F32), pltpu.VMEM((mc * LMM_PITCH, LANE), F32)],
        compiler_params=_params(("parallel",), 56),
        name=name,
    )(f, x3.reshape(k * FFT_N2, HY_WIDTH))
    return out.reshape(m, FFT_N2, HY_WIDTH)


SPEC_K1 = 4


def _spec_body(mf_ref, mr_ref, mi_ref, a_ref, z_ref, *rest):
    zm_refs, o_ref = rest[:SPEC_K1], rest[SPEC_K1]
    for t in range(SPEC_K1):
        mf = mf_ref[t]
        z = jnp.dot(mf, z_ref[:, t].reshape(2 * FFT_N2, HY_WIDTH), preferred_element_type=F32)
        zm = jnp.dot(mr_ref[t], zm_refs[t][...].reshape(2 * FFT_N2, HY_WIDTH), preferred_element_type=F32)
        a, b = z[:FFT_N2], z[FFT_N2:]
        am, bm = zm[:FFT_N2], zm[FFT_N2:]
        gr, gi = 0.5 * (a + am + b + bm), 0.5 * (b - bm + a - am)

        x = jnp.dot(mf, a_ref[:, t].reshape(2 * FFT_N2, HY_WIDTH), preferred_element_type=F32)
        xr, xi = x[:FFT_N2], x[FFT_N2:]
        y = jnp.concatenate([xr * gr - xi * gi, xr * gi + xi * gr], axis=0).astype(BF16)
        o_ref[:, t] = jnp.dot(mi_ref[t], y, preferred_element_type=F32).astype(BF16).reshape(
            2, FFT_N2, HY_WIDTH)


def _spec(mf, mirror, mi, a, zs):
    here = pl.BlockSpec((2, SPEC_K1, FFT_N2, HY_WIDTH), lambda i: (0, i, 0, 0))
    mat = pl.BlockSpec((SPEC_K1, 2 * FFT_N2, 2 * FFT_N2), lambda i: (i, 0, 0))
    mirrored = [pl.BlockSpec((2, 1, FFT_N2, HY_WIDTH),
                             lambda i, t=t: (0, (FFT_N1 - (SPEC_K1 * i + t)) % FFT_N1, 0, 0))
                for t in range(SPEC_K1)]
    return pl.pallas_call(
        _spec_body,
        out_shape=jax.ShapeDtypeStruct((2, FFT_N1, FFT_N2, HY_WIDTH), BF16),
        grid=(FFT_N1 // SPEC_K1,),
        in_specs=[mat, mat, mat, here, here] + mirrored,
        out_specs=here,
        compiler_params=_params(("parallel",), 40),
        name="spec",
    )(mf, mirror, mi, a, zs, *([zs] * SPEC_K1))


HY_CB = 256


def _short_conv(p_ref, w_ref, b_ref):
    p = p_ref[...].astype(F32)
    w = w_ref[...]
    prev = pltpu.roll(p, 1, 0)
    nxt = pltpu.roll(p, TL - 1, 0)
    return b_ref[...] + prev * w[0:1] + p * w[1:2] + nxt * w[2:3]


def _uconv_body(x1_ref, vh_ref, w1_ref, wv_ref, b1_ref, bv_ref, u_ref):
    u = _short_conv(vh_ref, wv_ref, bv_ref) * _short_conv(x1_ref, w1_ref, b1_ref)
    row = lax.broadcasted_iota(jnp.int32, (TL, 1), 0)
    u_ref[pl.ds(0, TL), :] = jnp.where(row >= PAD, u, 0.0)
    u_ref[pl.ds(TL, TFFT - TL), :] = jnp.zeros((TFFT - TL, HY_CB), F32)


def _hy_specs(first_block):
    nb = HY_WIDTH // HY_CB
    return (pl.BlockSpec((None, TL, HY_CB), lambda b, j: (b, 0, first_block * nb + j)),
            pl.BlockSpec((3, HY_CB), lambda b, j: (0, first_block * nb + j)),
            pl.BlockSpec((1, HY_CB), lambda b, j: (0, first_block * nb + j)))


def _uconv(p3, conv_w, conv_b):
    x1, w1, b1 = _hy_specs(1)
    vh, wv, bv = _hy_specs(2)
    return pl.pallas_call(
        _uconv_body,
        out_shape=jax.ShapeDtypeStruct((BATCH, TFFT, HY_WIDTH), F32),
        grid=(BATCH, HY_WIDTH // HY_CB),
        in_specs=[x1, vh, w1, wv, b1, bv],
        out_specs=pl.BlockSpec((None, TFFT, HY_CB), lambda b, j: (b, 0, j)),
        compiler_params=_params(("parallel", "parallel"), 48),
        name="uconv",
    )(p3, p3, conv_w, conv_w, conv_b, conv_b)


def _ymix_body(y_ref, x0_ref, x1_ref, vh_ref, w0_ref, w1_ref, wv_ref, b0_ref, b1_ref, bv_ref,
               d_ref, gain_ref, o_ref):
    u = _short_conv(vh_ref, wv_ref, bv_ref) * _short_conv(x1_ref, w1_ref, b1_ref)
    yy = (y_ref[...] + d_ref[...] * u) * _short_conv(x0_ref, w0_ref, b0_ref)
    gain = gain_ref[...]
    for s in range(0, HY_CB, HY_GROUP):
        o_ref[:, s:s + HY_GROUP] = (_rms(yy[X0:, s:s + HY_GROUP]) * gain[:, s:s + HY_GROUP]).astype(BF16)


def _ymix(y3, p3, conv_w, conv_b, hyena_d, hyena_norm):
    x0, w0, b0 = _hy_specs(0)
    x1, w1, b1 = _hy_specs(1)
    vh, wv, bv = _hy_specs(2)
    vec = pl.BlockSpec((1, HY_CB), lambda b, j: (0, j))
    return pl.pallas_call(
        _ymix_body,
        out_shape=jax.ShapeDtypeStruct((BATCH, SEQ, HY_WIDTH), BF16),
        grid=(BATCH, HY_WIDTH // HY_CB),
        in_specs=[pl.BlockSpec((None, TL, HY_CB), lambda b, j: (b, 0, j)),
                  x0, x1, vh, w0, w1, wv, b0, b1, bv, vec, vec],
        out_specs=pl.BlockSpec((None, SEQ, HY_CB), lambda b, j: (b, 0, j)),
        compiler_params=_params(("parallel", "parallel"), 56),
        name="ymix",
    )(y3, p3, p3, p3, conv_w, conv_w, conv_w, conv_b, conv_b, conv_b,
      hyena_d.reshape(1, -1), hyena_norm.reshape(1, -1))


N_SUB = CHUNK // SUB
SAFE_BLOCK_DECAY = -60.0


def _scores_exact(q, k, b, rev, ones, row, col):
    rsub = row % SUB
    terms = []
    for d in range(SUB):
        if d == 0:
            kr, br = k, b
        else:
            sh = CHUNK - d if rev else d
            kr, br = pltpu.roll(k, sh, 0), pltpu.roll(b, sh, 0)
        valid = (rsub + d < SUB) if rev else (rsub >= d)
        t = q * kr * jnp.exp(jnp.minimum(b - br, 0.0))
        terms.append(jnp.where(valid, t, 0.0).astype(BF16))
    sums = jnp.dot(jnp.concatenate(terms, axis=0), ones, preferred_element_type=F32)
    a = jnp.zeros((CHUNK, CHUNK), F32)
    for d in range(SUB):
        tgt = row + d if rev else row - d
        a = jnp.where(col == tgt, sums[d * CHUNK:(d + 1) * CHUNK, :CHUNK], a)

    rblk = row // SUB
    cblk = col // SUB
    for jb in (range(1, N_SUB) if rev else range(N_SUB - 1)):
        e = jb * SUB if rev else jb * SUB + SUB - 1
        ref = b[e:e + 1, :]
        qh = (q * jnp.exp(jnp.minimum(b - ref, 0.0))).astype(BF16)
        kh = (k * jnp.exp(jnp.minimum(ref - b, 0.0))).astype(BF16)
        pm = lax.dot_general(qh, kh, (((1,), (1,)), ((), ())), preferred_element_type=F32)
        side = jnp.where(cblk == jb, rblk, jb)
        a = jnp.where((side < jb) if rev else (side > jb), pm, a)

    bend = b[0:1, :] if rev else b[CHUNK - 1:CHUNK, :]
    return a, q * jnp.exp(b), k * jnp.exp(bend - b), bend


def _scores_fast(q, k, b, rev, row, col):
    order = list(range(N_SUB))[::-1] if rev else list(range(N_SUB))
    pos = {blk: p for p, blk in enumerate(order)}
    edge = lambda blk: blk * SUB if rev else blk * SUB + SUB - 1
    e = [b[edge(blk):edge(blk) + 1, :] for blk in order]
    s = [jnp.zeros((1, GLA_DK), F32)] + e[:-1]

    def by_row(vals):
        return jnp.concatenate([jnp.broadcast_to(vals[pos[blk]], (SUB, GLA_DK)) for blk in range(N_SUB)],
                               axis=0)

    srow, erow = by_row(s), by_row(e)
    qh = q * jnp.exp(b - srow)
    kh = k * jnp.exp(erow - b)
    kd = k * jnp.exp(srow - b)

    lhs = []
    for pj in range(N_SUB - 1):
        for blk in range(N_SUB):
            piece = qh[blk * SUB:(blk + 1) * SUB, :]
            p = pos[blk]
            if p <= pj:
                piece = jnp.zeros_like(piece)
            elif p > pj + 1:
                piece = piece * jnp.exp(s[p] - e[pj])
            lhs.append(piece.astype(BF16))
    contract = (((1,), (1,)), ((), ()))
    cross = lax.dot_general(jnp.concatenate(lhs, axis=0), kh.astype(BF16), contract,
                            preferred_element_type=F32)
    diag = lax.dot_general(qh.astype(BF16), kd.astype(BF16), contract, preferred_element_type=F32)

    rblk = row // SUB
    cblk = col // SUB
    a = jnp.zeros((CHUNK, CHUNK), F32)
    for pj in range(N_SUB - 1):
        a = jnp.where(cblk == order[pj], cross[pj * CHUNK:(pj + 1) * CHUNK], a)
    causal = (col >= row) if rev else (col <= row)
    a = jnp.where(cblk == rblk, jnp.where(causal, diag, 0.0), a)

    bend = e[-1]
    return a, qh * jnp.exp(srow), kh * jnp.exp(bend - erow), bend


GLA_RB = 384


def _gla_sweep_body(*refs, rev):
    if rev:
        q_ref, k_ref, v_ref, g_ref, o_ref, st_ref, b_ref = refs
    else:
        q_ref, k_ref, v_ref, g_ref, og_ref, ob_ref, gain_ref, o_ref, st_ref, b_ref = refs

    @pl.when(pl.program_id(1) == 0)
    def _():
        st_ref[...] = jnp.zeros_like(st_ref)

    row = lax.broadcasted_iota(jnp.int32, (CHUNK, 1), 0)
    col = lax.broadcasted_iota(jnp.int32, (CHUNK, CHUNK), 1)
    rr = lax.broadcasted_iota(jnp.int32, (CHUNK, CHUNK), 0)
    tri = ((col >= rr) if rev else (col <= rr)).astype(BF16)
    ones = jnp.ones((GLA_DK, GLA_DK), BF16)
    n_chunk = GLA_RB // CHUNK

    for c in range(n_chunk):
        g = g_ref[c * CHUNK:(c + 1) * CHUNK, :]
        g_hi = g.astype(BF16)
        rest = g - g_hi.astype(F32)
        g_mid = rest.astype(BF16)
        g_lo = (rest - g_mid.astype(F32)).astype(BF16)
        b_ref[c * CHUNK:(c + 1) * CHUNK, :] = (
            jnp.dot(tri, g_hi, preferred_element_type=F32) + jnp.dot(tri, g_mid, preferred_element_type=F32)
            + jnp.dot(tri, g_lo, preferred_element_type=F32))

    def run(fast):
        def step(t, carry):
            c = n_chunk - 1 - t if rev else t
            rows = pl.ds(pl.multiple_of(c * CHUNK, CHUNK), CHUNK)
            b_all = b_ref[rows, :]
            pending = []
            for h in range(GLA_HEADS):
                kc = slice(h * GLA_DK, (h + 1) * GLA_DK)
                vc = slice(h * GLA_DV, (h + 1) * GLA_DV)
                q = q_ref[rows, kc].astype(F32) * (GLA_DK ** -0.5)
                k = k_ref[rows, kc].astype(F32)
                v = v_ref[rows, vc].astype(BF16)
                b = b_all[:, kc]
                if fast:
                    a, qt, kt, bend = _scores_fast(q, k, b, rev, row, col)
                else:
                    a, qt, kt, bend = _scores_exact(q, k, b, rev, ones, row, col)
                st = st_ref[h]
                o = lax.dot_general(qt.astype(BF16), st.astype(BF16), (((1,), (1,)), ((), ())),
                                    preferred_element_type=F32)
                st_ref[h] = st * jnp.exp(bend) + lax.dot_general(
                    v, kt.astype(BF16), (((0,), (0,)), ((), ())), preferred_element_type=F32)
                pending.append((vc, o, a, v))
            for vc, o, a, v in pending:
                o = o + jnp.dot(a.astype(BF16), v, preferred_element_type=F32)
                if rev:
                    o_ref[rows, vc] = o
                else:
                    o = _rms(o + ob_ref[rows, vc]) * gain_ref[...]
                    o_ref[rows, vc] = (o * _silu(og_ref[rows, vc].astype(F32))).astype(BF16)
            return carry

        lax.fori_loop(0, n_chunk, step, 0, unroll=3 if fast else 1)

    low = jnp.min(jnp.sum(g_ref[...].reshape(GLA_RB // SUB, SUB, GLA_KEY_WIDTH), axis=1))
    safe = low > SAFE_BLOCK_DECAY
    pl.when(safe)(lambda: run(True))
    pl.when(jnp.logical_not(safe))(lambda: run(False))


def _gla_sweep(p3, lg3, rev, ob=None, gla_norm=None):
    nb = TL // GLA_RB
    blk = (lambda i: nb - 1 - i) if rev else (lambda i: i)
    key_blocks = P_MAIN // GLA_KEY_WIDTH
    q_col = 3 * HY_WIDTH // GLA_KEY_WIDTH
    v_col = (3 * HY_WIDTH + 2 * GLA_KEY_WIDTH) // GLA_WIDTH
    assert key_blocks * GLA_KEY_WIDTH == P_MAIN
    narrow = lambda col: pl.BlockSpec((None, GLA_RB, GLA_KEY_WIDTH), lambda b, i: (b, blk(i), col))
    wide = lambda col: pl.BlockSpec((None, GLA_RB, GLA_WIDTH), lambda b, i: (b, blk(i), col))
    in_specs = [narrow(q_col), narrow(q_col + 1), wide(v_col), narrow(1 if rev else 0)]
    args = [p3, p3, p3, lg3]
    if not rev:
        in_specs += [wide(v_col + 1), wide(0), pl.BlockSpec((1, GLA_DV), lambda b, i: (0, 0))]
        args += [p3, ob, gla_norm.reshape(1, -1)]
    return pl.pallas_call(
        functools.partial(_gla_sweep_body, rev=rev),
        out_shape=jax.ShapeDtypeStruct((BATCH, TL, GLA_WIDTH), F32 if rev else BF16),
        grid=(BATCH, nb),
        in_specs=in_specs,
        out_specs=wide(0),
        scratch_shapes=[pltpu.VMEM((GLA_HEADS, GLA_DV, GLA_DK), F32),
                        pltpu.VMEM((GLA_RB, GLA_KEY_WIDTH), F32)],
        compiler_params=_params(("parallel", "arbitrary"), 52),
        name="gla_down" if rev else "gla_up",
    )(*args)


OUT_TM = 512


def _outproj_body(h_ref, yh_ref, yg_ref, wh_ref, wg_ref, gain_ref, o_ref, xn_ref):
    h = (h_ref[...] + jnp.dot(yh_ref[...], wh_ref[...], preferred_element_type=F32)
         + jnp.dot(yg_ref[...], wg_ref[...], preferred_element_type=F32))
    o_ref[...] = h
    xn_ref[...] = (_rms(h) * gain_ref[...]).astype(BF16)


def _outproj(hbuf, yh, ygbuf, w_out, next_gain):
    wo = _to_bf16(w_out)
    half = lambda i: pl.BlockSpec((HY_WIDTH, D_MODEL), lambda r: (i, 0))
    tile = pl.BlockSpec((OUT_TM, D_MODEL), lambda r: (r, 0))
    return pl.pallas_call(
        _outproj_body,
        out_shape=(jax.ShapeDtypeStruct((X_ROWS, D_MODEL), F32), jax.ShapeDtypeStruct((X_ROWS, D_MODEL), BF16)),
        grid=(X_ROWS // OUT_TM,),
        in_specs=[pl.BlockSpec((pl.Element(OUT_TM), pl.Element(D_MODEL)), lambda r: (_x_row(r, OUT_TM), 0)),
                  pl.BlockSpec((OUT_TM, HY_WIDTH), lambda r: (r, 0)),
                  pl.BlockSpec((pl.Element(OUT_TM), pl.Element(GLA_WIDTH)),
                               lambda r: (_x_row(r, OUT_TM, BF16_SUBLANE), 0)),
                  half(0), half(1), pl.BlockSpec((1, D_MODEL), lambda r: (0, 0))],
        out_specs=(tile, tile),
        compiler_params=_params(("parallel",), 52),
        name="outproj",
    )(hbuf, yh, ygbuf, wo, wo, next_gain.reshape(1, -1))


def _long_conv(u3, filt):
    f1, f3, mf, mi, mirror = _dft_tables()
    as_n1 = lambda a: a.reshape(2 * FFT_R1, FFT_N2, HY_WIDTH)
    by_k1 = lambda a: a.reshape(2, FFT_N1, FFT_N2, HY_WIDTH)
    zs = _lmm(f1, as_n1(filt), BF16, name="lmm_g")
    a = _lmm(f1, as_n1(u3), BF16, name="lmm_fwd")
    bm = _spec(mf, mirror, mi, by_k1(a), by_k1(zs))
    y = _lmm(f3, bm.reshape(2 * FFT_N1, FFT_N2, HY_WIDTH), F32, scale=1.0 / NFFT, name="lmm_inv")
    return y.reshape(BATCH, TFFT, HY_WIDTH)


def kernel(x, meta_tokens, ffn1_norm, ffn1_w_gate, ffn1_w_up, ffn1_w_down, mix_norm, w_in, conv_w, conv_b,
           filt_w1, filt_b1, filt_w2, filt_b2, filt_w3, filt_freq, hyena_d, hyena_norm, gk_w2, gk_b2,
           gla_norm, w_out, ffn2_norm, ffn2_w_gate, ffn2_w_up, ffn2_w_down, final_norm):
    assert x.shape == (BATCH, SEQ, D_MODEL) and ffn1_norm.shape[0] == 1

    hbuf = _ffn(x.reshape(X_ROWS, D_MODEL), ffn1_norm[0], ffn1_w_gate[0], ffn1_w_up[0], ffn1_w_down[0],
                final_norm, final=False, shared_rows=_meta_rows(meta_tokens))

    p, lg = _inproj(hbuf, _inproj_weights(mix_norm[0], w_in[0], gk_w2[0], gk_b2[0]))
    p3 = p.reshape(BATCH, TL, P_MAIN)
    lg3 = lg.reshape(BATCH, TL, 2 * GLA_KEY_WIDTH)

    filt = _filters(filt_w1[0], filt_b1[0], filt_w2[0], filt_b2[0], filt_w3[0], filt_freq[0])
    cw, cb = conv_w[0], conv_b[0].reshape(1, -1)
    y3 = _long_conv(_uconv(p3, cw, cb), filt)
    yh = _ymix(y3, p3, cw, cb, hyena_d[0], hyena_norm[0])
    yg = _gla_sweep(p3, lg3, rev=False, ob=_gla_sweep(p3, lg3, rev=True), gla_norm=gla_norm[0])

    h2, h2n = _outproj(hbuf, yh.reshape(X_ROWS, HY_WIDTH), yg.reshape(ROWS, GLA_WIDTH), w_out[0], ffn2_norm[0])
    out = _ffn(h2, ffn2_norm[0], ffn2_w_gate[0], ffn2_w_up[0], ffn2_w_down[0], final_norm, final=True,
               normed=h2n)
    return out.reshape(BATCH, SEQ, D_MODEL)
```

```python
import functools

import numpy as np
import jax
import jax.numpy as jnp
from jax import lax
from jax.experimental import pallas as pl
from jax.experimental.pallas import tpu as pltpu

F32 = jnp.float32
BF16 = jnp.bfloat16

D_MODEL = 2048
BATCH = 2
SEQ = 4096
N_META = 16
L_TOK = SEQ + N_META
PAD = 112
X0 = PAD + N_META
TL = PAD + L_TOK
ROWS = BATCH * TL
X_ROWS = BATCH * SEQ
HY_WIDTH = 1024
HY_GROUPS = 8
HY_GROUP = HY_WIDTH // HY_GROUPS
FILT_EMB = 33
FILT_BANDS = 16
FILT_HIDDEN = 64
GLA_WIDTH = 1024
GLA_HEADS = 4
GLA_KEY_WIDTH = 512
GLA_DK = 128
GLA_DV = 256
GATE_RANK = 16
GATE_NORMALIZER = 16.0
CHUNK = 64
SUB = 16
D_FF = 5632
P_MAIN = 3 * HY_WIDTH + 2 * GLA_KEY_WIDTH + 2 * GLA_WIDTH
EPS = 1e-6

FFT_N1 = 72
FFT_N2 = 128
NFFT = FFT_N1 * FFT_N2
FFT_R1 = 40
TFFT = FFT_R1 * FFT_N2

MIB = 1024 * 1024
SUBLANE = 8
BF16_SUBLANE = 16
LANE = 128


def _params(sem, vmem_mib):
    return pltpu.CompilerParams(dimension_semantics=sem, vmem_limit_bytes=vmem_mib * MIB)


def _rms(x):
    return x * lax.rsqrt(jnp.mean(x * x, axis=-1, keepdims=True) + EPS)


def _silu(x):
    return x * jax.nn.sigmoid(x)


LAY_TM = 528


def _x_row(i, tm, unit=SUBLANE):
    per_batch = SEQ // tm
    r = (i // per_batch) * (TL // unit) + X0 // unit + (i % per_batch) * (tm // unit)
    return pl.multiple_of(r * unit, unit)


def _lay_src_row(i):
    per_batch = TL // LAY_TM
    r = (i // per_batch) * (SEQ // SUBLANE) + jnp.maximum(
        (i % per_batch) * (LAY_TM // SUBLANE) - X0 // SUBLANE, 0)
    return pl.multiple_of(r * SUBLANE, SUBLANE)


def _meta_rows(meta_tokens):
    return jnp.concatenate([jnp.zeros((PAD, D_MODEL), F32), meta_tokens.astype(F32)], axis=0)


FFN_TM = 512
FFN_TF = 512


FFN_HEAD_TF = 256


def _ffn_begin(x_ref, shared_ref, xin_ref, gain_ref, xn_ref, acc_ref, first):
    if shared_ref is not None:
        @pl.when(first)
        def _():
            xin_ref[0:X0, :] = shared_ref[...]
            xin_ref[X0:LAY_TM, :] = x_ref[0:LAY_TM - X0, :]

        @pl.when(jnp.logical_not(first))
        def _():
            xin_ref[...] = x_ref[...]

    xn_ref[...] = (_rms(xin_ref[...]) * gain_ref[...]).astype(BF16)
    acc_ref[...] = jnp.zeros_like(acc_ref)


def _ffn_step(xn_ref, wg, wu, wd, acc_ref):
    xn = xn_ref[...]
    g = jnp.dot(xn, wg, preferred_element_type=F32)
    u = jnp.dot(xn, wu, preferred_element_type=F32)
    a = (_silu(g) * u).astype(BF16)
    acc_ref[...] += jnp.dot(a, wd, preferred_element_type=F32)


def _ffn_result(xin_ref, acc_ref, fgain_ref, final):
    h = xin_ref[...] + 0.5 * acc_ref[...]
    return _rms(h) * fgain_ref[...] if final else h


def _ffn_head_body(x_ref, gain_ref, wg_ref, wu_ref, wd_ref, fgain_ref, *rest, final, layout):
    if layout:
        shared_ref, o_ref, wg16_ref, wu16_ref, wd16_ref, xn_ref, acc_ref, xin_ref = rest
    else:
        o_ref, wg16_ref, wu16_ref, wd16_ref, xn_ref, acc_ref = rest
        shared_ref, xin_ref = None, x_ref
    j = pl.program_id(0)

    @pl.when(j == 0)
    def _():
        _ffn_begin(x_ref, shared_ref, xin_ref, gain_ref, xn_ref, acc_ref, True)

    wg, wu, wd = wg_ref[...].astype(BF16), wu_ref[...].astype(BF16), wd_ref[...].astype(BF16)
    wg16_ref[...] = wg
    wu16_ref[...] = wu
    wd16_ref[...] = wd
    _ffn_step(xn_ref, wg, wu, wd, acc_ref)

    @pl.when(j == pl.num_programs(0) - 1)
    def _():
        o_ref[...] = _ffn_result(xin_ref, acc_ref, fgain_ref, final)


def _ffn_body(x_ref, gain_ref, wg_ref, wu_ref, wd_ref, fgain_ref, head_ref, *rest, final, layout):
    if layout:
        shared_ref, o_ref, xn_ref, acc_ref, xin_ref = rest
    else:
        o_ref, xn_ref, acc_ref = rest
        shared_ref, xin_ref = None, x_ref
    i, j = pl.program_id(0), pl.program_id(1)
    last = j == pl.num_programs(1) - 1

    @pl.when(i > 0)
    def _():
        @pl.when(j == 0)
        def _():
            _ffn_begin(x_ref, shared_ref, xin_ref, gain_ref, xn_ref, acc_ref, i % (TL // LAY_TM) == 0)

        _ffn_step(xn_ref, wg_ref[...], wu_ref[...], wd_ref[...], acc_ref)

        @pl.when(last)
        def _():
            o_ref[...] = _ffn_result(xin_ref, acc_ref, fgain_ref, final)

    @pl.when(jnp.logical_and(i == 0, last))
    def _():
        o_ref[...] = head_ref[...]


CAST_STEPS = 4


def _cast_body(x_ref, o_ref):
    o_ref[...] = x_ref[...].astype(BF16)


def _to_bf16(w, cols=None):
    rows = w.shape[0]
    cols = w.shape[1] if cols is None else cols
    tr = rows // CAST_STEPS
    assert tr * CAST_STEPS == rows and tr % BF16_SUBLANE == 0 and cols % LANE == 0
    return pl.pallas_call(
        _cast_body,
        out_shape=jax.ShapeDtypeStruct((rows, cols), BF16),
        grid=(CAST_STEPS,),
        in_specs=[pl.BlockSpec((tr, cols), lambda i: (i, 0))],
        out_specs=pl.BlockSpec((tr, cols), lambda i: (i, 0)),
        compiler_params=_params(("parallel",), 48),
        name="to_bf16",
    )(w)


def _ffn(x2, gain, wg, wu, wd, fgain, final, shared_rows=None):
    layout = shared_rows is not None
    tm = LAY_TM if layout else FFN_TM
    rows = ROWS if layout else X_ROWS
    gain, fgain = gain.reshape(1, -1), fgain.reshape(1, -1)
    scratch = [pltpu.VMEM((tm, D_MODEL), BF16), pltpu.VMEM((tm, D_MODEL), F32)]
    extra = ()
    if layout:
        extra = (shared_rows,)
        scratch.append(pltpu.VMEM((tm, D_MODEL), F32))
    name = "ffn_final" if final else "ffn"

    tf = FFN_HEAD_TF
    vec1 = pl.BlockSpec((1, D_MODEL), lambda j: (0, 0))
    wide1 = pl.BlockSpec((D_MODEL, tf), lambda j: (0, j))
    tall1 = pl.BlockSpec((tf, D_MODEL), lambda j: (j, 0))
    tile1 = pl.BlockSpec((tm, D_MODEL), lambda j: (0, 0))
    head, wg16, wu16, wd16 = pl.pallas_call(
        functools.partial(_ffn_head_body, final=final, layout=layout),
        out_shape=(jax.ShapeDtypeStruct((tm, D_MODEL), F32), jax.ShapeDtypeStruct(wg.shape, BF16),
                   jax.ShapeDtypeStruct(wu.shape, BF16), jax.ShapeDtypeStruct(wd.shape, BF16)),
        grid=(D_FF // tf,),
        in_specs=[tile1, vec1, wide1, wide1, tall1, vec1] + [pl.BlockSpec((X0, D_MODEL), lambda j: (0, 0))] * layout,
        out_specs=(tile1, wide1, wide1, tall1),
        scratch_shapes=scratch,
        compiler_params=_params(("arbitrary",), 56),
        name=name + "_head",
    )(x2, gain, wg, wu, wd, fgain, *extra)

    col = lambda i, j: jnp.where(i == 0, 0, j)
    vec = pl.BlockSpec((1, D_MODEL), lambda i, j: (0, 0))
    wide = pl.BlockSpec((D_MODEL, FFN_TF), lambda i, j: (0, col(i, j)))
    tall = pl.BlockSpec((FFN_TF, D_MODEL), lambda i, j: (col(i, j), 0))
    if layout:
        x_spec = pl.BlockSpec((pl.Element(tm), pl.Element(D_MODEL)), lambda i, j: (_lay_src_row(i), 0))
    else:
        x_spec = pl.BlockSpec((tm, D_MODEL), lambda i, j: (i, 0))
    return pl.pallas_call(
        functools.partial(_ffn_body, final=final, layout=layout),
        out_shape=jax.ShapeDtypeStruct((rows, D_MODEL), F32),
        grid=(rows // tm, D_FF // FFN_TF),
        in_specs=[x_spec, vec, wide, wide, tall, vec, pl.BlockSpec((tm, D_MODEL), lambda i, j: (0, 0))]
        + [pl.BlockSpec((X0, D_MODEL), lambda i, j: (0, 0))] * layout,
        out_specs=pl.BlockSpec((tm, D_MODEL), lambda i, j: (i, 0)),
        scratch_shapes=scratch,
        compiler_params=_params(("parallel", "arbitrary"), 56),
        name=name,
    )(x2, gain, wg16, wu16, wd16, fgain, head, *extra)


INP_TM = 2 * LAY_TM
INP_TN = 1536
LR_PAD = 128


def _inproj_body(x_ref, gain_ref, w_ref, wlr_ref, w2_ref, b2_ref, p_ref, lg_ref, xn_ref, lr_ref):
    @pl.when(pl.program_id(1) == 0)
    def _():
        xn = (_rms(x_ref[...]) * gain_ref[...]).astype(BF16)
        xn_ref[...] = xn
        lr_ref[...] = jnp.dot(xn, wlr_ref[...], preferred_element_type=F32).astype(BF16)

    z = jnp.dot(lr_ref[...], w2_ref[...], preferred_element_type=F32) + b2_ref[...]
    lg_ref[...] = (jnp.minimum(z, 0.0) - jnp.log(1.0 + jnp.exp(-jnp.abs(z)))) * (1.0 / GATE_NORMALIZER)
    p_ref[...] = jnp.dot(xn_ref[...], w_ref[...], preferred_element_type=F32).astype(BF16)


SPLIT_TC = 1024


def _split_w_in_body(w_ref, wlr_ref, main_ref, lr_ref):
    main_ref[...] = w_ref[...].T.astype(BF16)

    @pl.when(pl.program_id(0) == 0)
    def _():
        rows = jnp.concatenate([wlr_ref[...], jnp.zeros((LR_PAD - 2 * GATE_RANK, D_MODEL), F32)], axis=0)
        lr_ref[...] = rows.T.astype(BF16)


def _split_w_in(w_in):
    w_t = jnp.transpose(w_in)
    return pl.pallas_call(
        _split_w_in_body,
        out_shape=(jax.ShapeDtypeStruct((D_MODEL, P_MAIN), BF16), jax.ShapeDtypeStruct((D_MODEL, LR_PAD), BF16)),
        grid=(P_MAIN // SPLIT_TC,),
        in_specs=[pl.BlockSpec((SPLIT_TC, D_MODEL), lambda i: (i, 0)),
                  pl.BlockSpec((2 * GATE_RANK, D_MODEL), lambda i: (P_MAIN // (2 * GATE_RANK), 0))],
        out_specs=(pl.BlockSpec((D_MODEL, SPLIT_TC), lambda i: (0, i)),
                   pl.BlockSpec((D_MODEL, LR_PAD), lambda i: (0, 0))),
        compiler_params=_params(("arbitrary",), 48),
        name="split_w_in",
    )(w_t, w_t)


def _inproj_weights(gain, w_in, gk_w2, gk_b2):
    w_main, w_lr = _split_w_in(w_in)
    w2 = jnp.zeros((LR_PAD, 2 * GLA_KEY_WIDTH), F32)
    w2 = w2.at[:GATE_RANK, :GLA_KEY_WIDTH].set(gk_w2[0])
    w2 = w2.at[GATE_RANK:2 * GATE_RANK, GLA_KEY_WIDTH:].set(gk_w2[1]).astype(BF16)
    return gain.reshape(1, -1), w_main, w_lr, w2, gk_b2.reshape(1, 2 * GLA_KEY_WIDTH)


def _inproj(hbuf, weights):
    tm, tn = INP_TM, INP_TN
    steps = P_MAIN // tn
    tg = 2 * GLA_KEY_WIDTH // steps
    assert steps * tn == P_MAIN and steps * tg == 2 * GLA_KEY_WIDTH and tg % LANE == 0
    return pl.pallas_call(
        _inproj_body,
        out_shape=(jax.ShapeDtypeStruct((ROWS, P_MAIN), BF16),
                   jax.ShapeDtypeStruct((ROWS, 2 * GLA_KEY_WIDTH), F32)),
        grid=(ROWS // tm, steps),
        in_specs=[
            pl.BlockSpec((tm, D_MODEL), lambda i, j: (i, 0)),
            pl.BlockSpec((1, D_MODEL), lambda i, j: (0, 0)),
            pl.BlockSpec((D_MODEL, tn), lambda i, j: (0, j)),
            pl.BlockSpec((D_MODEL, LR_PAD), lambda i, j: (0, 0)),
            pl.BlockSpec((LR_PAD, tg), lambda i, j: (0, j)),
            pl.BlockSpec((1, tg), lambda i, j: (0, j)),
        ],
        out_specs=(pl.BlockSpec((tm, tn), lambda i, j: (i, j)),
                   pl.BlockSpec((tm, tg), lambda i, j: (i, j))),
        scratch_shapes=[pltpu.VMEM((tm, D_MODEL), BF16), pltpu.VMEM((tm, LR_PAD), BF16)],
        compiler_params=_params(("parallel", "arbitrary"), 56),
        name="inproj",
    )(hbuf, *weights)


FILT_TR = 640
FEAT_PAD = 128


def _filt_tables():
    pos = np.arange(TFFT, dtype=np.float64)
    t = pos / (L_TOK - 1)
    w = (2.0 * np.pi / L_TOK) * pos
    bands = 1e-4 + np.arange(FILT_BANDS, dtype=np.float64) * ((FILT_BANDS - 1 - 1e-4) / (FILT_BANDS - 1))
    ang = w[:, None] * bands[None, :]
    feats = np.zeros((TFFT, FEAT_PAD), np.float64)
    feats[:, 0] = t
    feats[:, 1:1 + FILT_BANDS] = np.cos(ang)
    feats[:, 1 + FILT_BANDS:FILT_EMB] = -np.sin(ang)
    lo, hi = np.log(1e-2) / 1.5, np.log(1e-2) / 0.3
    deltas = np.abs(lo + np.arange(HY_WIDTH, dtype=np.float64) * ((hi - lo) / (HY_WIDTH - 1)))
    return feats.astype(np.float32), deltas.astype(np.float32).reshape(1, HY_WIDTH)


def _filt_body(feat_ref, t_ref, w1_ref, b1_ref, w2_ref, b2_ref, w3_ref, fr_ref, dl_ref, h_ref):
    start = pl.program_id(0) * FILT_TR

    @pl.when(start < L_TOK)
    def _():
        hp = lax.Precision.HIGHEST
        fr = fr_ref[...]
        z = jnp.sin(fr * (jnp.dot(w1_ref[...], feat_ref[...], precision=hp, preferred_element_type=F32)
                          + b1_ref[...]))
        z = jnp.sin(fr * (jnp.dot(w2_ref[...], z, precision=hp, preferred_element_type=F32) + b2_ref[...]))
        hh = lax.dot_general(z, w3_ref[...], (((0,), (0,)), ((), ())), precision=hp,
                             preferred_element_type=F32)
        pos = start + lax.broadcasted_iota(jnp.int32, (FILT_TR, 1), 0)
        win = jnp.exp(-t_ref[...] * dl_ref[...])
        win = jnp.where(pos < L_TOK, win, 0.0)
        hf = hh[:, :HY_WIDTH] * win
        hb = hh[:, HY_WIDTH:] * win
        h_ref[0] = jnp.where(pos == 0, hf + hb, hf)
        h_ref[1] = jnp.where(pos == 0, 0.0, hb)

    @pl.when(start >= L_TOK)
    def _():
        h_ref[...] = jnp.zeros_like(h_ref)


def _filters(w1, b1, w2, b2, w3, freq):
    feats, deltas = _filt_tables()
    w1t = jnp.pad(w1, ((0, FEAT_PAD - FILT_EMB), (0, 0))).T
    col = lambda v: v.reshape(-1, 1)
    full = lambda shape: pl.BlockSpec(shape, lambda i: (0, 0))
    return pl.pallas_call(
        _filt_body,
        out_shape=jax.ShapeDtypeStruct((2, TFFT, HY_WIDTH), F32),
        grid=(TFFT // FILT_TR,),
        in_specs=[
            pl.BlockSpec((FEAT_PAD, FILT_TR), lambda i: (0, i)),
            pl.BlockSpec((FILT_TR, 1), lambda i: (i, 0)),
            full((FILT_HIDDEN, FEAT_PAD)), full((FILT_HIDDEN, 1)),
            full((FILT_HIDDEN, FILT_HIDDEN)), full((FILT_HIDDEN, 1)),
            full((FILT_HIDDEN, 2 * HY_WIDTH)), full((FILT_HIDDEN, 1)), full((1, HY_WIDTH)),
        ],
        out_specs=pl.BlockSpec((2, FILT_TR, HY_WIDTH), lambda i: (0, i, 0)),
        compiler_params=_params(("parallel",), 40),
        name="filt",
    )(jnp.asarray(feats.T), jnp.asarray(feats[:, 0:1]), w1t, col(b1), w2.T, col(b2), w3, col(freq),
      jnp.asarray(deltas))


def _dft_tables_np():
    n1 = np.arange(FFT_N1)
    ang1 = 2.0 * np.pi * ((n1[:, None] * n1[None, :]) % FFT_N1) / FFT_N1
    c1, s1 = np.cos(ang1)[:, :FFT_R1], np.sin(ang1)[:, :FFT_R1]
    f1 = np.block([[c1, s1], [-s1, c1]])
    f3 = np.block([[c1.T, -s1.T], [s1.T, c1.T]])
    k1 = np.arange(FFT_N1)[:, None, None]
    k2 = np.arange(FFT_N2)[None, :, None]
    n2 = np.arange(FFT_N2)[None, None, :]
    ang2 = 2.0 * np.pi * ((n2 * (k1 + FFT_N1 * k2)) % NFFT) / NFFT
    c2, s2 = np.cos(ang2), np.sin(ang2)
    mf = np.concatenate([np.concatenate([c2, s2], axis=2), np.concatenate([-s2, c2], axis=2)], axis=1)
    c2t, s2t = np.swapaxes(c2, 1, 2), np.swapaxes(s2, 1, 2)
    mi = np.concatenate([np.concatenate([c2t, -s2t], axis=2), np.concatenate([s2t, c2t], axis=2)], axis=1)
    k1r = np.arange(FFT_N1)
    k2r = np.arange(FFT_N2)
    perm = np.where(k1r[:, None] == 0, (FFT_N2 - k2r[None, :]) % FFT_N2, FFT_N2 - 1 - k2r[None, :])
    rows = np.concatenate([perm, perm + FFT_N2], axis=1)
    mirror = mf[((FFT_N1 - k1r) % FFT_N1)[:, None], rows, :]
    return f1, f3, mf, mi, mirror


def _dft_tables():
    return tuple(jnp.asarray(a.astype(np.float32)).astype(BF16) for a in _dft_tables_np())


LMM_PITCH = FFT_N2 + SUBLANE
LMM_MC = 2 * FFT_N1


def _lmm_body(f_ref, x_ref, o_ref, xs_ref, os_ref, *, scale):
    m, k = f_ref.shape
    for g in range(k):
        xs_ref[pl.ds(g * LMM_PITCH, FFT_N2), :] = x_ref[pl.ds(g * FFT_N2, FFT_N2), :].astype(F32)
    for m0 in range(0, m, LMM_MC):
        f = f_ref[m0:min(m0 + LMM_MC, m), :]

        def slab(s, carry):
            x = xs_ref[pl.ds(s, k, stride=LMM_PITCH), :].astype(BF16)
            r = jnp.dot(f, x, preferred_element_type=F32)
            os_ref[pl.ds(s, f.shape[0], stride=LMM_PITCH), :] = r * scale if scale != 1.0 else r
            return carry

        lax.fori_loop(0, FFT_N2, slab, 0, unroll=32)
        for g in range(f.shape[0]):
            o_ref[pl.ds((m0 + g) * FFT_N2, FFT_N2), :] = os_ref[pl.ds(g * LMM_PITCH, FFT_N2), :].astype(
                o_ref.dtype)


def _lmm(f, x3, out_dtype, scale=1.0, name="lmm"):
    m, k = f.shape
    mc = min(m, LMM_MC)
    assert m % mc == 0
    out = pl.pallas_call(
        functools.partial(_lmm_body, scale=scale),
        out_shape=jax.ShapeDtypeStruct((m * FFT_N2, HY_WIDTH), out_dtype),
        grid=(HY_WIDTH // LANE,),
        in_specs=[pl.BlockSpec((m, k), lambda j: (0, 0)),
                  pl.BlockSpec((k * FFT_N2, LANE), lambda j: (0, j))],
        out_specs=pl.BlockSpec((m * FFT_N2, LANE), lambda j: (0, j)),
        scratch_shapes=[pltpu.VMEM((k * LMM_PITCH, LANE), F32), pltpu.VMEM((mc * LMM_PITCH, LANE), F32)],
        compiler_params=_params(("parallel",), 56),
        name=name,
    )(f, x3.reshape(k * FFT_N2, HY_WIDTH))
    return out.reshape(m, FFT_N2, HY_WIDTH)


SPEC_K1 = 4


def _spec_body(mf_ref, mr_ref, mi_ref, a_ref, z_ref, *rest):
    zm_refs, o_ref = rest[:SPEC_K1], rest[SPEC_K1]
    for t in range(SPEC_K1):
        mf = mf_ref[t]
        z = jnp.dot(mf, z_ref[:, t].reshape(2 * FFT_N2, HY_WIDTH), preferred_element_type=F32)
        zm = jnp.dot(mr_ref[t], zm_refs[t][...].reshape(2 * FFT_N2, HY_WIDTH), preferred_element_type=F32)
        a, b = z[:FFT_N2], z[FFT_N2:]
        am, bm = zm[:FFT_N2], zm[FFT_N2:]
        gr, gi = 0.5 * (a + am + b + bm), 0.5 * (b - bm + a - am)

        x = jnp.dot(mf, a_ref[:, t].reshape(2 * FFT_N2, HY_WIDTH), preferred_element_type=F32)
        xr, xi = x[:FFT_N2], x[FFT_N2:]
        y = jnp.concatenate([xr * gr - xi * gi, xr * gi + xi * gr], axis=0).astype(BF16)
        o_ref[:, t] = jnp.dot(mi_ref[t], y, preferred_element_type=F32).astype(BF16).reshape(
            2, FFT_N2, HY_WIDTH)


def _spec(mf, mirror, mi, a, zs):
    here = pl.BlockSpec((2, SPEC_K1, FFT_N2, HY_WIDTH), lambda i: (0, i, 0, 0))
    mat = pl.BlockSpec((SPEC_K1, 2 * FFT_N2, 2 * FFT_N2), lambda i: (i, 0, 0))
    mirrored = [pl.BlockSpec((2, 1, FFT_N2, HY_WIDTH),
                             lambda i, t=t: (0, (FFT_N1 - (SPEC_K1 * i + t)) % FFT_N1, 0, 0))
                for t in range(SPEC_K1)]
    return pl.pallas_call(
        _spec_body,
        out_shape=jax.ShapeDtypeStruct((2, FFT_N1, FFT_N2, HY_WIDTH), BF16),
        grid=(FFT_N1 // SPEC_K1,),
        in_specs=[mat, mat, mat, here, here] + mirrored,
        out_specs=here,
        compiler_params=_params(("parallel",), 40),
        name="spec",
    )(mf, mirror, mi, a, zs, *([zs] * SPEC_K1))


HY_CB = 256


def _short_conv(p_ref, w_ref, b_ref):
    p = p_ref[...].astype(F32)
    w = w_ref[...]
    prev = pltpu.roll(p, 1, 0)
    nxt = pltpu.roll(p, TL - 1, 0)
    return b_ref[...] + prev * w[0:1] + p * w[1:2] + nxt * w[2:3]


def _uconv_body(x1_ref, vh_ref, w1_ref, wv_ref, b1_ref, bv_ref, u_ref):
    u = _short_conv(vh_ref, wv_ref, bv_ref) * _short_conv(x1_ref, w1_ref, b1_ref)
    row = lax.broadcasted_iota(jnp.int32, (TL, 1), 0)
    u_ref[pl.ds(0, TL), :] = jnp.where(row >= PAD, u, 0.0)
    u_ref[pl.ds(TL, TFFT - TL), :] = jnp.zeros((TFFT - TL, HY_CB), F32)


def _hy_specs(first_block):
    nb = HY_WIDTH // HY_CB
    return (pl.BlockSpec((None, TL, HY_CB), lambda b, j: (b, 0, first_block * nb + j)),
            pl.BlockSpec((3, HY_CB), lambda b, j: (0, first_block * nb + j)),
            pl.BlockSpec((1, HY_CB), lambda b, j: (0, first_block * nb + j)))


def _uconv(p3, conv_w, conv_b):
    x1, w1, b1 = _hy_specs(1)
    vh, wv, bv = _hy_specs(2)
    return pl.pallas_call(
        _uconv_body,
        out_shape=jax.ShapeDtypeStruct((BATCH, TFFT, HY_WIDTH), F32),
        grid=(BATCH, HY_WIDTH // HY_CB),
        in_specs=[x1, vh, w1, wv, b1, bv],
        out_specs=pl.BlockSpec((None, TFFT, HY_CB), lambda b, j: (b, 0, j)),
        compiler_params=_params(("parallel", "parallel"), 48),
        name="uconv",
    )(p3, p3, conv_w, conv_w, conv_b, conv_b)


def _ymix_body(y_ref, x0_ref, x1_ref, vh_ref, w0_ref, w1_ref, wv_ref, b0_ref, b1_ref, bv_ref,
               d_ref, gain_ref, o_ref):
    u = _short_conv(vh_ref, wv_ref, bv_ref) * _short_conv(x1_ref, w1_ref, b1_ref)
    yy = (y_ref[...] + d_ref[...] * u) * _short_conv(x0_ref, w0_ref, b0_ref)
    gain = gain_ref[...]
    for s in range(0, HY_CB, HY_GROUP):
        o_ref[:, s:s + HY_GROUP] = (_rms(yy[X0:, s:s + HY_GROUP]) * gain[:, s:s + HY_GROUP]).astype(BF16)


def _ymix(y3, p3, conv_w, conv_b, hyena_d, hyena_norm):
    x0, w0, b0 = _hy_specs(0)
    x1, w1, b1 = _hy_specs(1)
    vh, wv, bv = _hy_specs(2)
    vec = pl.BlockSpec((1, HY_CB), lambda b, j: (0, j))
    return pl.pallas_call(
        _ymix_body,
        out_shape=jax.ShapeDtypeStruct((BATCH, SEQ, HY_WIDTH), BF16),
        grid=(BATCH, HY_WIDTH // HY_CB),
        in_specs=[pl.BlockSpec((None, TL, HY_CB), lambda b, j: (b, 0, j)),
                  x0, x1, vh, w0, w1, wv, b0, b1, bv, vec, vec],
        out_specs=pl.BlockSpec((None, SEQ, HY_CB), lambda b, j: (b, 0, j)),
        compiler_params=_params(("parallel", "parallel"), 56),
        name="ymix",
    )(y3, p3, p3, p3, conv_w, conv_w, conv_w, conv_b, conv_b, conv_b,
      hyena_d.reshape(1, -1), hyena_norm.reshape(1, -1))


N_SUB = CHUNK // SUB
SAFE_BLOCK_DECAY = -60.0


def _scores_exact(q, k, b, rev, ones, row, col):
    rsub = row % SUB
    terms = []
    for d in range(SUB):
        if d == 0:
            kr, br = k, b
        else:
            sh = CHUNK - d if rev else d
            kr, br = pltpu.roll(k, sh, 0), pltpu.roll(b, sh, 0)
        valid = (rsub + d < SUB) if rev else (rsub >= d)
        t = q * kr * jnp.exp(jnp.minimum(b - br, 0.0))
        terms.append(jnp.where(valid, t, 0.0).astype(BF16))
    sums = jnp.dot(jnp.concatenate(terms, axis=0), ones, preferred_element_type=F32)
    a = jnp.zeros((CHUNK, CHUNK), F32)
    for d in range(SUB):
        tgt = row + d if rev else row - d
        a = jnp.where(col == tgt, sums[d * CHUNK:(d + 1) * CHUNK, :CHUNK], a)

    rblk = row // SUB
    cblk = col // SUB
    for jb in (range(1, N_SUB) if rev else range(N_SUB - 1)):
        e = jb * SUB if rev else jb * SUB + SUB - 1
        ref = b[e:e + 1, :]
        qh = (q * jnp.exp(jnp.minimum(b - ref, 0.0))).astype(BF16)
        kh = (k * jnp.exp(jnp.minimum(ref - b, 0.0))).astype(BF16)
        pm = lax.dot_general(qh, kh, (((1,), (1,)), ((), ())), preferred_element_type=F32)
        side = jnp.where(cblk == jb, rblk, jb)
        a = jnp.where((side < jb) if rev else (side > jb), pm, a)

    bend = b[0:1, :] if rev else b[CHUNK - 1:CHUNK, :]
    return a, q * jnp.exp(b), k * jnp.exp(bend - b), bend


def _scores_fast(q, k, b, rev, row, col):
    order = list(range(N_SUB))[::-1] if rev else list(range(N_SUB))
    pos = {blk: p for p, blk in enumerate(order)}
    edge = lambda blk: blk * SUB if rev else blk * SUB + SUB - 1
    e = [b[edge(blk):edge(blk) + 1, :] for blk in order]
    s = [jnp.zeros((1, GLA_DK), F32)] + e[:-1]

    def by_row(vals):
        return jnp.concatenate([jnp.broadcast_to(vals[pos[blk]], (SUB, GLA_DK)) for blk in range(N_SUB)],
                               axis=0)

    srow, erow = by_row(s), by_row(e)
    qh = q * jnp.exp(b - srow)
    kh = k * jnp.exp(erow - b)
    kd = k * jnp.exp(srow - b)

    lhs = []
    for pj in range(N_SUB - 1):
        for blk in range(N_SUB):
            piece = qh[blk * SUB:(blk + 1) * SUB, :]
            p = pos[blk]
            if p <= pj:
                piece = jnp.zeros_like(piece)
            elif p > pj + 1:
                piece = piece * jnp.exp(s[p] - e[pj])
            lhs.append(piece.astype(BF16))
    contract = (((1,), (1,)), ((), ()))
    cross = lax.dot_general(jnp.concatenate(lhs, axis=0), kh.astype(BF16), contract,
                            preferred_element_type=F32)
    diag = lax.dot_general(qh.astype(BF16), kd.astype(BF16), contract, preferred_element_type=F32)

    rblk = row // SUB
    cblk = col // SUB
    a = jnp.zeros((CHUNK, CHUNK), F32)
    for pj in range(N_SUB - 1):
        a = jnp.where(cblk == order[pj], cross[pj * CHUNK:(pj + 1) * CHUNK], a)
    causal = (col >= row) if rev else (col <= row)
    a = jnp.where(cblk == rblk, jnp.where(causal, diag, 0.0), a)

    bend = e[-1]
    return a, qh * jnp.exp(srow), kh * jnp.exp(bend - erow), bend


GLA_RB = 1408


def _gla_sweep_body(*refs, rev):
    if rev:
        q_ref, k_ref, v_ref, g_ref, o_ref, st_ref, b_ref = refs
    else:
        q_ref, k_ref, v_ref, g_ref, og_ref, ob_ref, gain_ref, o_ref, st_ref, b_ref = refs

    @pl.when(pl.program_id(1) == 0)
    def _():
        st_ref[...] = jnp.zeros_like(st_ref)

    row = lax.broadcasted_iota(jnp.int32, (CHUNK, 1), 0)
    col = lax.broadcasted_iota(jnp.int32, (CHUNK, CHUNK), 1)
    rr = lax.broadcasted_iota(jnp.int32, (CHUNK, CHUNK), 0)
    tri = ((col >= rr) if rev else (col <= rr)).astype(BF16)
    ones = jnp.ones((GLA_DK, GLA_DK), BF16)
    n_chunk = GLA_RB // CHUNK

    for c in range(n_chunk):
        g = g_ref[c * CHUNK:(c + 1) * CHUNK, :]
        g_hi = g.astype(BF16)
        rest = g - g_hi.astype(F32)
        g_mid = rest.astype(BF16)
        g_lo = (rest - g_mid.astype(F32)).astype(BF16)
        b_ref[c * CHUNK:(c + 1) * CHUNK, :] = (
            jnp.dot(tri, g_hi, preferred_element_type=F32) + jnp.dot(tri, g_mid, preferred_element_type=F32)
            + jnp.dot(tri, g_lo, preferred_element_type=F32))

    def run(fast):
        def step(t, carry):
            c = n_chunk - 1 - t if rev else t
            rows = pl.ds(pl.multiple_of(c * CHUNK, CHUNK), CHUNK)
            b_all = b_ref[rows, :]
            pending = []
            for h in range(GLA_HEADS):
                kc = slice(h * GLA_DK, (h + 1) * GLA_DK)
                vc = slice(h * GLA_DV, (h + 1) * GLA_DV)
                q = q_ref[rows, kc].astype(F32) * (GLA_DK ** -0.5)
                k = k_ref[rows, kc].astype(F32)
                v = v_ref[rows, vc].astype(BF16)
                b = b_all[:, kc]
                if fast:
                    a, qt, kt, bend = _scores_fast(q, k, b, rev, row, col)
                else:
                    a, qt, kt, bend = _scores_exact(q, k, b, rev, ones, row, col)
                st = st_ref[h]
                o = lax.dot_general(qt.astype(BF16), st.astype(BF16), (((1,), (1,)), ((), ())),
                                    preferred_element_type=F32)
                st_ref[h] = st * jnp.exp(bend) + lax.dot_general(
                    v, kt.astype(BF16), (((0,), (0,)), ((), ())), preferred_element_type=F32)
                pending.append((vc, o, a, v))
            for vc, o, a, v in pending:
                o = o + jnp.dot(a.astype(BF16), v, preferred_element_type=F32)
                if rev:
                    o_ref[rows, vc] = o
                else:
                    o = _rms(o + ob_ref[rows, vc]) * gain_ref[...]
                    o_ref[rows, vc] = (o * _silu(og_ref[rows, vc].astype(F32))).astype(BF16)
            return carry

        lax.fori_loop(0, n_chunk, step, 0, unroll=2 if fast else 1)

    low = jnp.min(jnp.sum(g_ref[...].reshape(GLA_RB // SUB, SUB, GLA_KEY_WIDTH), axis=1))
    safe = low > SAFE_BLOCK_DECAY
    pl.when(safe)(lambda: run(True))
    pl.when(jnp.logical_not(safe))(lambda: run(False))


def _gla_sweep(p3, lg3, rev, ob=None, gla_norm=None):
    nb = TL // GLA_RB
    blk = (lambda i: nb - 1 - i) if rev else (lambda i: i)
    key_blocks = P_MAIN // GLA_KEY_WIDTH
    q_col = 3 * HY_WIDTH // GLA_KEY_WIDTH
    v_col = (3 * HY_WIDTH + 2 * GLA_KEY_WIDTH) // GLA_WIDTH
    assert key_blocks * GLA_KEY_WIDTH == P_MAIN
    narrow = lambda col: pl.BlockSpec((None, GLA_RB, GLA_KEY_WIDTH), lambda b, i: (b, blk(i), col))
    wide = lambda col: pl.BlockSpec((None, GLA_RB, GLA_WIDTH), lambda b, i: (b, blk(i), col))
    in_specs = [narrow(q_col), narrow(q_col + 1), wide(v_col), narrow(1 if rev else 0)]
    args = [p3, p3, p3, lg3]
    if not rev:
        in_specs += [wide(v_col + 1), wide(0), pl.BlockSpec((1, GLA_DV), lambda b, i: (0, 0))]
        args += [p3, ob, gla_norm.reshape(1, -1)]
    return pl.pallas_call(
        functools.partial(_gla_sweep_body, rev=rev),
        out_shape=jax.ShapeDtypeStruct((BATCH, TL, GLA_WIDTH), F32 if rev else BF16),
        grid=(BATCH, nb),
        in_specs=in_specs,
        out_specs=wide(0),
        scratch_shapes=[pltpu.VMEM((GLA_HEADS, GLA_DV, GLA_DK), F32),
                        pltpu.VMEM((GLA_RB, GLA_KEY_WIDTH), F32)],
        compiler_params=_params(("parallel", "arbitrary"), 52),
        name="gla_down" if rev else "gla_up",
    )(*args)


OUT_TM = 512


def _outproj_body(h_ref, yh_ref, yg_ref, wh_ref, wg_ref, o_ref):
    o_ref[...] = (h_ref[...] + jnp.dot(yh_ref[...], wh_ref[...], preferred_element_type=F32)
                  + jnp.dot(yg_ref[...], wg_ref[...], preferred_element_type=F32))


def _outproj(hbuf, yh, ygbuf, w_out):
    wo = _to_bf16(w_out)
    half = lambda i: pl.BlockSpec((HY_WIDTH, D_MODEL), lambda r: (i, 0))
    return pl.pallas_call(
        _outproj_body,
        out_shape=jax.ShapeDtypeStruct((X_ROWS, D_MODEL), F32),
        grid=(X_ROWS // OUT_TM,),
        in_specs=[pl.BlockSpec((pl.Element(OUT_TM), pl.Element(D_MODEL)), lambda r: (_x_row(r, OUT_TM), 0)),
                  pl.BlockSpec((OUT_TM, HY_WIDTH), lambda r: (r, 0)),
                  pl.BlockSpec((pl.Element(OUT_TM), pl.Element(GLA_WIDTH)),
                               lambda r: (_x_row(r, OUT_TM, BF16_SUBLANE), 0)),
                  half(0), half(1)],
        out_specs=pl.BlockSpec((OUT_TM, D_MODEL), lambda r: (r, 0)),
        compiler_params=_params(("parallel",), 48),
        name="outproj",
    )(hbuf, yh, ygbuf, wo, wo)


def _long_conv(u3, filt):
    f1, f3, mf, mi, mirror = _dft_tables()
    as_n1 = lambda a: a.reshape(2 * FFT_R1, FFT_N2, HY_WIDTH)
    by_k1 = lambda a: a.reshape(2, FFT_N1, FFT_N2, HY_WIDTH)
    zs = _lmm(f1, as_n1(filt), BF16, name="lmm_g")
    a = _lmm(f1, as_n1(u3), BF16, name="lmm_fwd")
    bm = _spec(mf, mirror, mi, by_k1(a), by_k1(zs))
    y = _lmm(f3, bm.reshape(2 * FFT_N1, FFT_N2, HY_WIDTH), F32, scale=1.0 / NFFT, name="lmm_inv")
    return y.reshape(BATCH, TFFT, HY_WIDTH)


def kernel(x, meta_tokens, ffn1_norm, ffn1_w_gate, ffn1_w_up, ffn1_w_down, mix_norm, w_in, conv_w, conv_b,
           filt_w1, filt_b1, filt_w2, filt_b2, filt_w3, filt_freq, hyena_d, hyena_norm, gk_w2, gk_b2,
           gla_norm, w_out, ffn2_norm, ffn2_w_gate, ffn2_w_up, ffn2_w_down, final_norm):
    assert x.shape == (BATCH, SEQ, D_MODEL) and ffn1_norm.shape[0] == 1

    hbuf = _ffn(x.reshape(X_ROWS, D_MODEL), ffn1_norm[0], ffn1_w_gate[0], ffn1_w_up[0], ffn1_w_down[0],
                final_norm, final=False, shared_rows=_meta_rows(meta_tokens))

    p, lg = _inproj(hbuf, _inproj_weights(mix_norm[0], w_in[0], gk_w2[0], gk_b2[0]))
    p3 = p.reshape(BATCH, TL, P_MAIN)
    lg3 = lg.reshape(BATCH, TL, 2 * GLA_KEY_WIDTH)

    filt = _filters(filt_w1[0], filt_b1[0], filt_w2[0], filt_b2[0], filt_w3[0], filt_freq[0])
    cw, cb = conv_w[0], conv_b[0].reshape(1, -1)
    y3 = _long_conv(_uconv(p3, cw, cb), filt)
    yh = _ymix(y3, p3, cw, cb, hyena_d[0], hyena_norm[0])
    yg = _gla_sweep(p3, lg3, rev=False, ob=_gla_sweep(p3, lg3, rev=True), gla_norm=gla_norm[0])

    h2 = _outproj(hbuf, yh.reshape(X_ROWS, HY_WIDTH), yg.reshape(ROWS, GLA_WIDTH), w_out[0])
    out = _ffn(h2, ffn2_norm[0], ffn2_w_gate[0], ffn2_w_up[0], ffn2_w_down[0], final_norm, final=True)
    return out.reshape(BATCH, SEQ, D_MODEL)
```

```python
import functools

import numpy as np
import jax
import jax.numpy as jnp
from jax import lax
from jax.experimental import pallas as pl
from jax.experimental.pallas import tpu as pltpu

F32 = jnp.float32
BF16 = jnp.bfloat16

D_MODEL = 2048
BATCH = 2
SEQ = 4096
N_META = 16
L_TOK = SEQ + N_META
PAD = 112
X0 = PAD + N_META
TL = PAD + L_TOK
ROWS = BATCH * TL
X_ROWS = BATCH * SEQ
HY_WIDTH = 1024
HY_GROUPS = 8
HY_GROUP = HY_WIDTH // HY_GROUPS
FILT_EMB = 33
FILT_BANDS = 16
FILT_HIDDEN = 64
GLA_WIDTH = 1024
GLA_HEADS = 4
GLA_KEY_WIDTH = 512
GLA_DK = 128
GLA_DV = 256
GATE_RANK = 16
GATE_NORMALIZER = 16.0
CHUNK = 64
SUB = 16
D_FF = 5632
P_MAIN = 3 * HY_WIDTH + 2 * GLA_KEY_WIDTH + 2 * GLA_WIDTH
EPS = 1e-6

FFT_N1 = 72
FFT_N2 = 128
NFFT = FFT_N1 * FFT_N2
FFT_R1 = 40
TFFT = FFT_R1 * FFT_N2

MIB = 1024 * 1024
SUBLANE = 8
BF16_SUBLANE = 16
LANE = 128


def _params(sem, vmem_mib):
    return pltpu.CompilerParams(dimension_semantics=sem, vmem_limit_bytes=vmem_mib * MIB)


def _rms(x):
    return x * lax.rsqrt(jnp.mean(x * x, axis=-1, keepdims=True) + EPS)


def _silu(x):
    return x * jax.nn.sigmoid(x)


LAY_TM = 528


def _x_row(i, tm, unit=SUBLANE):
    per_batch = SEQ // tm
    r = (i // per_batch) * (TL // unit) + X0 // unit + (i % per_batch) * (tm // unit)
    return pl.multiple_of(r * unit, unit)


def _lay_src_row(i):
    per_batch = TL // LAY_TM
    r = (i // per_batch) * (SEQ // SUBLANE) + jnp.maximum(
        (i % per_batch) * (LAY_TM // SUBLANE) - X0 // SUBLANE, 0)
    return pl.multiple_of(r * SUBLANE, SUBLANE)


def _meta_rows(meta_tokens):
    return jnp.concatenate([jnp.zeros((PAD, D_MODEL), F32), meta_tokens.astype(F32)], axis=0)


FFN_TM = 512
FFN_TF = 512


FFN_HEAD_TF = 256


def _ffn_begin(x_ref, shared_ref, xin_ref, gain_ref, xn_ref, acc_ref, first):
    if shared_ref is not None:
        @pl.when(first)
        def _():
            xin_ref[0:X0, :] = shared_ref[...]
            xin_ref[X0:LAY_TM, :] = x_ref[0:LAY_TM - X0, :]

        @pl.when(jnp.logical_not(first))
        def _():
            xin_ref[...] = x_ref[...]

    xn_ref[...] = (_rms(xin_ref[...]) * gain_ref[...]).astype(BF16)
    acc_ref[...] = jnp.zeros_like(acc_ref)


def _ffn_step(xn_ref, wg, wu, wd, acc_ref):
    xn = xn_ref[...]
    g = jnp.dot(xn, wg, preferred_element_type=F32)
    u = jnp.dot(xn, wu, preferred_element_type=F32)
    a = (_silu(g) * u).astype(BF16)
    acc_ref[...] += jnp.dot(a, wd, preferred_element_type=F32)


def _ffn_result(xin_ref, acc_ref, fgain_ref, final):
    h = xin_ref[...] + 0.5 * acc_ref[...]
    return _rms(h) * fgain_ref[...] if final else h


def _ffn_head_body(x_ref, gain_ref, wg_ref, wu_ref, wd_ref, fgain_ref, *rest, final, layout):
    if layout:
        shared_ref, o_ref, wgu16_ref, wd16_ref, xn_ref, acc_ref, xin_ref = rest
    else:
        o_ref, wgu16_ref, wd16_ref, xn_ref, acc_ref = rest
        shared_ref, xin_ref = None, x_ref
    j = pl.program_id(0)

    @pl.when(j == 0)
    def _():
        _ffn_begin(x_ref, shared_ref, xin_ref, gain_ref, xn_ref, acc_ref, True)

    wg, wu, wd = wg_ref[...].astype(BF16), wu_ref[...].astype(BF16), wd_ref[...].astype(BF16)
    tf = FFN_HEAD_TF
    for part in range(FFN_TF // tf):
        @pl.when(j % (FFN_TF // tf) == part)
        def _(part=part):
            wgu16_ref[:, part * tf:(part + 1) * tf] = wg
            wgu16_ref[:, FFN_TF + part * tf:FFN_TF + (part + 1) * tf] = wu

    wd16_ref[...] = wd
    _ffn_step(xn_ref, wg, wu, wd, acc_ref)

    @pl.when(j == pl.num_programs(0) - 1)
    def _():
        o_ref[...] = _ffn_result(xin_ref, acc_ref, fgain_ref, final)


def _ffn_body(x_ref, gain_ref, wgu_ref, wd_ref, fgain_ref, head_ref, *rest, final, layout):
    if layout:
        shared_ref, o_ref, xn_ref, acc_ref, xin_ref = rest
    else:
        o_ref, xn_ref, acc_ref = rest
        shared_ref, xin_ref = None, x_ref
    i, j = pl.program_id(0), pl.program_id(1)
    last = j == pl.num_programs(1) - 1

    @pl.when(i > 0)
    def _():
        @pl.when(j == 0)
        def _():
            _ffn_begin(x_ref, shared_ref, xin_ref, gain_ref, xn_ref, acc_ref, i % (TL // LAY_TM) == 0)

        gu = jnp.dot(xn_ref[...], wgu_ref[...], preferred_element_type=F32)
        a = (_silu(gu[:, :FFN_TF]) * gu[:, FFN_TF:]).astype(BF16)
        acc_ref[...] += jnp.dot(a, wd_ref[...], preferred_element_type=F32)

        @pl.when(last)
        def _():
            o_ref[...] = _ffn_result(xin_ref, acc_ref, fgain_ref, final)

    @pl.when(jnp.logical_and(i == 0, last))
    def _():
        o_ref[...] = head_ref[...]


CAST_STEPS = 4


def _cast_body(x_ref, o_ref):
    o_ref[...] = x_ref[...].astype(BF16)


def _to_bf16(w, cols=None):
    rows = w.shape[0]
    cols = w.shape[1] if cols is None else cols
    tr = rows // CAST_STEPS
    assert tr * CAST_STEPS == rows and tr % BF16_SUBLANE == 0 and cols % LANE == 0
    return pl.pallas_call(
        _cast_body,
        out_shape=jax.ShapeDtypeStruct((rows, cols), BF16),
        grid=(CAST_STEPS,),
        in_specs=[pl.BlockSpec((tr, cols), lambda i: (i, 0))],
        out_specs=pl.BlockSpec((tr, cols), lambda i: (i, 0)),
        compiler_params=_params(("parallel",), 48),
        name="to_bf16",
    )(w)


def _ffn(x2, gain, wg, wu, wd, fgain, final, shared_rows=None):
    layout = shared_rows is not None
    tm = LAY_TM if layout else FFN_TM
    rows = ROWS if layout else X_ROWS
    gain, fgain = gain.reshape(1, -1), fgain.reshape(1, -1)
    scratch = [pltpu.VMEM((tm, D_MODEL), BF16), pltpu.VMEM((tm, D_MODEL), F32)]
    extra = ()
    if layout:
        extra = (shared_rows,)
        scratch.append(pltpu.VMEM((tm, D_MODEL), F32))
    name = "ffn_final" if final else "ffn"

    tf = FFN_HEAD_TF
    vec1 = pl.BlockSpec((1, D_MODEL), lambda j: (0, 0))
    wide1 = pl.BlockSpec((D_MODEL, tf), lambda j: (0, j))
    tall1 = pl.BlockSpec((tf, D_MODEL), lambda j: (j, 0))
    tile1 = pl.BlockSpec((tm, D_MODEL), lambda j: (0, 0))
    per = FFN_TF // tf
    head, wgu16, wd16 = pl.pallas_call(
        functools.partial(_ffn_head_body, final=final, layout=layout),
        out_shape=(jax.ShapeDtypeStruct((tm, D_MODEL), F32), jax.ShapeDtypeStruct((D_MODEL, 2 * D_FF), BF16),
                   jax.ShapeDtypeStruct(wd.shape, BF16)),
        grid=(D_FF // tf,),
        in_specs=[tile1, vec1, wide1, wide1, tall1, vec1] + [pl.BlockSpec((X0, D_MODEL), lambda j: (0, 0))] * layout,
        out_specs=(tile1, pl.BlockSpec((D_MODEL, 2 * FFN_TF), lambda j: (0, j // per)), tall1),
        scratch_shapes=scratch,
        compiler_params=_params(("arbitrary",), 56),
        name=name + "_head",
    )(x2, gain, wg, wu, wd, fgain, *extra)

    col = lambda i, j: jnp.where(i == 0, 0, j)
    vec = pl.BlockSpec((1, D_MODEL), lambda i, j: (0, 0))
    wide = pl.BlockSpec((D_MODEL, 2 * FFN_TF), lambda i, j: (0, col(i, j)))
    tall = pl.BlockSpec((FFN_TF, D_MODEL), lambda i, j: (col(i, j), 0))
    if layout:
        x_spec = pl.BlockSpec((pl.Element(tm), pl.Element(D_MODEL)), lambda i, j: (_lay_src_row(i), 0))
    else:
        x_spec = pl.BlockSpec((tm, D_MODEL), lambda i, j: (i, 0))
    return pl.pallas_call(
        functools.partial(_ffn_body, final=final, layout=layout),
        out_shape=jax.ShapeDtypeStruct((rows, D_MODEL), F32),
        grid=(rows // tm, D_FF // FFN_TF),
        in_specs=[x_spec, vec, wide, tall, vec, pl.BlockSpec((tm, D_MODEL), lambda i, j: (0, 0))]
        + [pl.BlockSpec((X0, D_MODEL), lambda i, j: (0, 0))] * layout,
        out_specs=pl.BlockSpec((tm, D_MODEL), lambda i, j: (i, 0)),
        scratch_shapes=scratch,
        compiler_params=_params(("parallel", "arbitrary"), 56),
        name=name,
    )(x2, gain, wgu16, wd16, fgain, head, *extra)


INP_TM = 2 * LAY_TM
INP_TN = 1536
LR_PAD = 128


def _inproj_body(x_ref, gain_ref, w_ref, wlr_ref, w2_ref, b2_ref, p_ref, lg_ref, xn_ref, lr_ref):
    @pl.when(pl.program_id(1) == 0)
    def _():
        xn = (_rms(x_ref[...]) * gain_ref[...]).astype(BF16)
        xn_ref[...] = xn
        lr_ref[...] = jnp.dot(xn, wlr_ref[...], preferred_element_type=F32).astype(BF16)

    z = jnp.dot(lr_ref[...], w2_ref[...], preferred_element_type=F32) + b2_ref[...]
    lg_ref[...] = (jnp.minimum(z, 0.0) - jnp.log(1.0 + jnp.exp(-jnp.abs(z)))) * (1.0 / GATE_NORMALIZER)
    p_ref[...] = jnp.dot(xn_ref[...], w_ref[...], preferred_element_type=F32).astype(BF16)


SPLIT_TC = 1024


def _split_w_in_body(w_ref, wlr_ref, main_ref, lr_ref):
    main_ref[...] = w_ref[...].T.astype(BF16)

    @pl.when(pl.program_id(0) == 0)
    def _():
        rows = jnp.concatenate([wlr_ref[...], jnp.zeros((LR_PAD - 2 * GATE_RANK, D_MODEL), F32)], axis=0)
        lr_ref[...] = rows.T.astype(BF16)


def _split_w_in(w_in):
    w_t = jnp.transpose(w_in)
    return pl.pallas_call(
        _split_w_in_body,
        out_shape=(jax.ShapeDtypeStruct((D_MODEL, P_MAIN), BF16), jax.ShapeDtypeStruct((D_MODEL, LR_PAD), BF16)),
        grid=(P_MAIN // SPLIT_TC,),
        in_specs=[pl.BlockSpec((SPLIT_TC, D_MODEL), lambda i: (i, 0)),
                  pl.BlockSpec((2 * GATE_RANK, D_MODEL), lambda i: (P_MAIN // (2 * GATE_RANK), 0))],
        out_specs=(pl.BlockSpec((D_MODEL, SPLIT_TC), lambda i: (0, i)),
                   pl.BlockSpec((D_MODEL, LR_PAD), lambda i: (0, 0))),
        compiler_params=_params(("arbitrary",), 48),
        name="split_w_in",
    )(w_t, w_t)


def _inproj_weights(gain, w_in, gk_w2, gk_b2):
    w_main, w_lr = _split_w_in(w_in)
    w2 = jnp.zeros((LR_PAD, 2 * GLA_KEY_WIDTH), F32)
    w2 = w2.at[:GATE_RANK, :GLA_KEY_WIDTH].set(gk_w2[0])
    w2 = w2.at[GATE_RANK:2 * GATE_RANK, GLA_KEY_WIDTH:].set(gk_w2[1]).astype(BF16)
    return gain.reshape(1, -1), w_main, w_lr, w2, gk_b2.reshape(1, 2 * GLA_KEY_WIDTH)


def _inproj(hbuf, weights):
    tm, tn = INP_TM, INP_TN
    steps = P_MAIN // tn
    tg = 2 * GLA_KEY_WIDTH // steps
    assert steps * tn == P_MAIN and steps * tg == 2 * GLA_KEY_WIDTH and tg % LANE == 0
    return pl.pallas_call(
        _inproj_body,
        out_shape=(jax.ShapeDtypeStruct((ROWS, P_MAIN), BF16),
                   jax.ShapeDtypeStruct((ROWS, 2 * GLA_KEY_WIDTH), F32)),
        grid=(ROWS // tm, steps),
        in_specs=[
            pl.BlockSpec((tm, D_MODEL), lambda i, j: (i, 0)),
            pl.BlockSpec((1, D_MODEL), lambda i, j: (0, 0)),
            pl.BlockSpec((D_MODEL, tn), lambda i, j: (0, j)),
            pl.BlockSpec((D_MODEL, LR_PAD), lambda i, j: (0, 0)),
            pl.BlockSpec((LR_PAD, tg), lambda i, j: (0, j)),
            pl.BlockSpec((1, tg), lambda i, j: (0, j)),
        ],
        out_specs=(pl.BlockSpec((tm, tn), lambda i, j: (i, j)),
                   pl.BlockSpec((tm, tg), lambda i, j: (i, j))),
        scratch_shapes=[pltpu.VMEM((tm, D_MODEL), BF16), pltpu.VMEM((tm, LR_PAD), BF16)],
        compiler_params=_params(("parallel", "arbitrary"), 56),
        name="inproj",
    )(hbuf, *weights)


FILT_TR = 640
FEAT_PAD = 128


def _filt_tables():
    pos = np.arange(TFFT, dtype=np.float64)
    t = pos / (L_TOK - 1)
    w = (2.0 * np.pi / L_TOK) * pos
    bands = 1e-4 + np.arange(FILT_BANDS, dtype=np.float64) * ((FILT_BANDS - 1 - 1e-4) / (FILT_BANDS - 1))
    ang = w[:, None] * bands[None, :]
    feats = np.zeros((TFFT, FEAT_PAD), np.float64)
    feats[:, 0] = t
    feats[:, 1:1 + FILT_BANDS] = np.cos(ang)
    feats[:, 1 + FILT_BANDS:FILT_EMB] = -np.sin(ang)
    lo, hi = np.log(1e-2) / 1.5, np.log(1e-2) / 0.3
    deltas = np.abs(lo + np.arange(HY_WIDTH, dtype=np.float64) * ((hi - lo) / (HY_WIDTH - 1)))
    return feats.astype(np.float32), deltas.astype(np.float32).reshape(1, HY_WIDTH)


def _filt_body(feat_ref, t_ref, w1_ref, b1_ref, w2_ref, b2_ref, w3_ref, fr_ref, dl_ref, h_ref):
    start = pl.program_id(0) * FILT_TR

    @pl.when(start < L_TOK)
    def _():
        hp = lax.Precision.HIGHEST
        fr = fr_ref[...]
        z = jnp.sin(fr * (jnp.dot(w1_ref[...], feat_ref[...], precision=hp, preferred_element_type=F32)
                          + b1_ref[...]))
        z = jnp.sin(fr * (jnp.dot(w2_ref[...], z, precision=hp, preferred_element_type=F32) + b2_ref[...]))
        hh = lax.dot_general(z, w3_ref[...], (((0,), (0,)), ((), ())), precision=hp,
                             preferred_element_type=F32)
        pos = start + lax.broadcasted_iota(jnp.int32, (FILT_TR, 1), 0)
        win = jnp.exp(-t_ref[...] * dl_ref[...])
        win = jnp.where(pos < L_TOK, win, 0.0)
        hf = hh[:, :HY_WIDTH] * win
        hb = hh[:, HY_WIDTH:] * win
        h_ref[0] = jnp.where(pos == 0, hf + hb, hf)
        h_ref[1] = jnp.where(pos == 0, 0.0, hb)

    @pl.when(start >= L_TOK)
    def _():
        h_ref[...] = jnp.zeros_like(h_ref)


def _filters(w1, b1, w2, b2, w3, freq):
    feats, deltas = _filt_tables()
    w1t = jnp.pad(w1, ((0, FEAT_PAD - FILT_EMB), (0, 0))).T
    col = lambda v: v.reshape(-1, 1)
    full = lambda shape: pl.BlockSpec(shape, lambda i: (0, 0))
    return pl.pallas_call(
        _filt_body,
        out_shape=jax.ShapeDtypeStruct((2, TFFT, HY_WIDTH), F32),
        grid=(TFFT // FILT_TR,),
        in_specs=[
            pl.BlockSpec((FEAT_PAD, FILT_TR), lambda i: (0, i)),
            pl.BlockSpec((FILT_TR, 1), lambda i: (i, 0)),
            full((FILT_HIDDEN, FEAT_PAD)), full((FILT_HIDDEN, 1)),
            full((FILT_HIDDEN, FILT_HIDDEN)), full((FILT_HIDDEN, 1)),
            full((FILT_HIDDEN, 2 * HY_WIDTH)), full((FILT_HIDDEN, 1)), full((1, HY_WIDTH)),
        ],
        out_specs=pl.BlockSpec((2, FILT_TR, HY_WIDTH), lambda i: (0, i, 0)),
        compiler_params=_params(("parallel",), 40),
        name="filt",
    )(jnp.asarray(feats.T), jnp.asarray(feats[:, 0:1]), w1t, col(b1), w2.T, col(b2), w3, col(freq),
      jnp.asarray(deltas))


def _dft_tables_np():
    n1 = np.arange(FFT_N1)
    ang1 = 2.0 * np.pi * ((n1[:, None] * n1[None, :]) % FFT_N1) / FFT_N1
    c1, s1 = np.cos(ang1)[:, :FFT_R1], np.sin(ang1)[:, :FFT_R1]
    f1 = np.block([[c1, s1], [-s1, c1]])
    f3 = np.block([[c1.T, -s1.T], [s1.T, c1.T]])
    k1 = np.arange(FFT_N1)[:, None, None]
    k2 = np.arange(FFT_N2)[None, :, None]
    n2 = np.arange(FFT_N2)[None, None, :]
    ang2 = 2.0 * np.pi * ((n2 * (k1 + FFT_N1 * k2)) % NFFT) / NFFT
    c2, s2 = np.cos(ang2), np.sin(ang2)
    mf = np.concatenate([np.concatenate([c2, s2], axis=2), np.concatenate([-s2, c2], axis=2)], axis=1)
    c2t, s2t = np.swapaxes(c2, 1, 2), np.swapaxes(s2, 1, 2)
    mi = np.concatenate([np.concatenate([c2t, -s2t], axis=2), np.concatenate([s2t, c2t], axis=2)], axis=1)
    k1r = np.arange(FFT_N1)
    k2r = np.arange(FFT_N2)
    perm = np.where(k1r[:, None] == 0, (FFT_N2 - k2r[None, :]) % FFT_N2, FFT_N2 - 1 - k2r[None, :])
    rows = np.concatenate([perm, perm + FFT_N2], axis=1)
    mirror = mf[((FFT_N1 - k1r) % FFT_N1)[:, None], rows, :]
    return f1, f3, mf, mi, mirror


def _dft_tables():
    return tuple(jnp.asarray(a.astype(np.float32)).astype(BF16) for a in _dft_tables_np())


LMM_PITCH = FFT_N2 + SUBLANE
LMM_MC = 2 * FFT_N1


def _lmm_body(f_ref, x_ref, o_ref, xs_ref, os_ref, *, scale):
    m, k = f_ref.shape
    for g in range(k):
        xs_ref[pl.ds(g * LMM_PITCH, FFT_N2), :] = x_ref[pl.ds(g * FFT_N2, FFT_N2), :].astype(F32)
    for m0 in range(0, m, LMM_MC):
        f = f_ref[m0:min(m0 + LMM_MC, m), :]

        def slab(s, carry):
            x = xs_ref[pl.ds(s, k, stride=LMM_PITCH), :].astype(BF16)
            r = jnp.dot(f, x, preferred_element_type=F32)
            os_ref[pl.ds(s, f.shape[0], stride=LMM_PITCH), :] = r * scale if scale != 1.0 else r
            return carry

        lax.fori_loop(0, FFT_N2, slab, 0, unroll=32)
        for g in range(f.shape[0]):
            o_ref[pl.ds((m0 + g) * FFT_N2, FFT_N2), :] = os_ref[pl.ds(g * LMM_PITCH, FFT_N2), :].astype(
                o_ref.dtype)


def _lmm(f, x3, out_dtype, scale=1.0, name="lmm"):
    m, k = f.shape
    mc = min(m, LMM_MC)
    assert m % mc == 0
    out = pl.pallas_call(
        functools.partial(_lmm_body, scale=scale),
        out_shape=jax.ShapeDtypeStruct((m * FFT_N2, HY_WIDTH), out_dtype),
        grid=(HY_WIDTH // LANE,),
        in_specs=[pl.BlockSpec((m, k), lambda j: (0, 0)),
                  pl.BlockSpec((k * FFT_N2, LANE), lambda j: (0, j))],
        out_specs=pl.BlockSpec((m * FFT_N2, LANE), lambda j: (0, j)),
        scratch_shapes=[pltpu.VMEM((k * LMM_PITCH, LANE), F32), pltpu.VMEM((mc * LMM_PITCH, LANE), F32)],
        compiler_params=_params(("parallel",), 56),
        name=name,
    )(f, x3.reshape(k * FFT_N2, HY_WIDTH))
    return out.reshape(m, FFT_N2, HY_WIDTH)


SPEC_K1 = 4


def _spec_body(mf_ref, mr_ref, mi_ref, a_ref, z_ref, *rest):
    zm_refs, o_ref = rest[:SPEC_K1], rest[SPEC_K1]
    for t in range(SPEC_K1):
        mf = mf_ref[t]
        z = jnp.dot(mf, z_ref[:, t].reshape(2 * FFT_N2, HY_WIDTH), preferred_element_type=F32)
        zm = jnp.dot(mr_ref[t], zm_refs[t][...].reshape(2 * FFT_N2, HY_WIDTH), preferred_element_type=F32)
        a, b = z[:FFT_N2], z[FFT_N2:]
        am, bm = zm[:FFT_N2], zm[FFT_N2:]
        gr, gi = 0.5 * (a + am + b + bm), 0.5 * (b - bm + a - am)

        x = jnp.dot(mf, a_ref[:, t].reshape(2 * FFT_N2, HY_WIDTH), preferred_element_type=F32)
        xr, xi = x[:FFT_N2], x[FFT_N2:]
        y = jnp.concatenate([xr * gr - xi * gi, xr * gi + xi * gr], axis=0).astype(BF16)
        o_ref[:, t] = jnp.dot(mi_ref[t], y, preferred_element_type=F32).astype(BF16).reshape(
            2, FFT_N2, HY_WIDTH)


def _spec(mf, mirror, mi, a, zs):
    here = pl.BlockSpec((2, SPEC_K1, FFT_N2, HY_WIDTH), lambda i: (0, i, 0, 0))
    mat = pl.BlockSpec((SPEC_K1, 2 * FFT_N2, 2 * FFT_N2), lambda i: (i, 0, 0))
    mirrored = [pl.BlockSpec((2, 1, FFT_N2, HY_WIDTH),
                             lambda i, t=t: (0, (FFT_N1 - (SPEC_K1 * i + t)) % FFT_N1, 0, 0))
                for t in range(SPEC_K1)]
    return pl.pallas_call(
        _spec_body,
        out_shape=jax.ShapeDtypeStruct((2, FFT_N1, FFT_N2, HY_WIDTH), BF16),
        grid=(FFT_N1 // SPEC_K1,),
        in_specs=[mat, mat, mat, here, here] + mirrored,
        out_specs=here,
        compiler_params=_params(("parallel",), 40),
        name="spec",
    )(mf, mirror, mi, a, zs, *([zs] * SPEC_K1))


HY_CB = 256


def _short_conv(p_ref, w_ref, b_ref):
    p = p_ref[...].astype(F32)
    w = w_ref[...]
    prev = pltpu.roll(p, 1, 0)
    nxt = pltpu.roll(p, TL - 1, 0)
    return b_ref[...] + prev * w[0:1] + p * w[1:2] + nxt * w[2:3]


def _uconv_body(x1_ref, vh_ref, w1_ref, wv_ref, b1_ref, bv_ref, u_ref):
    u = _short_conv(vh_ref, wv_ref, bv_ref) * _short_conv(x1_ref, w1_ref, b1_ref)
    row = lax.broadcasted_iota(jnp.int32, (TL, 1), 0)
    u_ref[pl.ds(0, TL), :] = jnp.where(row >= PAD, u, 0.0)
    u_ref[pl.ds(TL, TFFT - TL), :] = jnp.zeros((TFFT - TL, HY_CB), F32)


def _hy_specs(first_block):
    nb = HY_WIDTH // HY_CB
    return (pl.BlockSpec((None, TL, HY_CB), lambda b, j: (b, 0, first_block * nb + j)),
            pl.BlockSpec((3, HY_CB), lambda b, j: (0, first_block * nb + j)),
            pl.BlockSpec((1, HY_CB), lambda b, j: (0, first_block * nb + j)))


def _uconv(p3, conv_w, conv_b):
    x1, w1, b1 = _hy_specs(1)
    vh, wv, bv = _hy_specs(2)
    return pl.pallas_call(
        _uconv_body,
        out_shape=jax.ShapeDtypeStruct((BATCH, TFFT, HY_WIDTH), F32),
        grid=(BATCH, HY_WIDTH // HY_CB),
        in_specs=[x1, vh, w1, wv, b1, bv],
        out_specs=pl.BlockSpec((None, TFFT, HY_CB), lambda b, j: (b, 0, j)),
        compiler_params=_params(("parallel", "parallel"), 48),
        name="uconv",
    )(p3, p3, conv_w, conv_w, conv_b, conv_b)


def _ymix_body(y_ref, x0_ref, x1_ref, vh_ref, w0_ref, w1_ref, wv_ref, b0_ref, b1_ref, bv_ref,
               d_ref, gain_ref, o_ref):
    u = _short_conv(vh_ref, wv_ref, bv_ref) * _short_conv(x1_ref, w1_ref, b1_ref)
    yy = (y_ref[...] + d_ref[...] * u) * _short_conv(x0_ref, w0_ref, b0_ref)
    gain = gain_ref[...]
    for s in range(0, HY_CB, HY_GROUP):
        o_ref[:, s:s + HY_GROUP] = (_rms(yy[X0:, s:s + HY_GROUP]) * gain[:, s:s + HY_GROUP]).astype(BF16)


def _ymix(y3, p3, conv_w, conv_b, hyena_d, hyena_norm):
    x0, w0, b0 = _hy_specs(0)
    x1, w1, b1 = _hy_specs(1)
    vh, wv, bv = _hy_specs(2)
    vec = pl.BlockSpec((1, HY_CB), lambda b, j: (0, j))
    return pl.pallas_call(
        _ymix_body,
        out_shape=jax.ShapeDtypeStruct((BATCH, SEQ, HY_WIDTH), BF16),
        grid=(BATCH, HY_WIDTH // HY_CB),
        in_specs=[pl.BlockSpec((None, TL, HY_CB), lambda b, j: (b, 0, j)),
                  x0, x1, vh, w0, w1, wv, b0, b1, bv, vec, vec],
        out_specs=pl.BlockSpec((None, SEQ, HY_CB), lambda b, j: (b, 0, j)),
        compiler_params=_params(("parallel", "parallel"), 56),
        name="ymix",
    )(y3, p3, p3, p3, conv_w, conv_w, conv_w, conv_b, conv_b, conv_b,
      hyena_d.reshape(1, -1), hyena_norm.reshape(1, -1))


N_SUB = CHUNK // SUB
SAFE_BLOCK_DECAY = -60.0


def _scores_exact(q, k, b, rev, ones, row, col):
    rsub = row % SUB
    terms = []
    for d in range(SUB):
        if d == 0:
            kr, br = k, b
        else:
            sh = CHUNK - d if rev else d
            kr, br = pltpu.roll(k, sh, 0), pltpu.roll(b, sh, 0)
        valid = (rsub + d < SUB) if rev else (rsub >= d)
        t = q * kr * jnp.exp(jnp.minimum(b - br, 0.0))
        terms.append(jnp.where(valid, t, 0.0).astype(BF16))
    sums = jnp.dot(jnp.concatenate(terms, axis=0), ones, preferred_element_type=F32)
    a = jnp.zeros((CHUNK, CHUNK), F32)
    for d in range(SUB):
        tgt = row + d if rev else row - d
        a = jnp.where(col == tgt, sums[d * CHUNK:(d + 1) * CHUNK, :CHUNK], a)

    rblk = row // SUB
    cblk = col // SUB
    for jb in (range(1, N_SUB) if rev else range(N_SUB - 1)):
        e = jb * SUB if rev else jb * SUB + SUB - 1
        ref = b[e:e + 1, :]
        qh = (q * jnp.exp(jnp.minimum(b - ref, 0.0))).astype(BF16)
        kh = (k * jnp.exp(jnp.minimum(ref - b, 0.0))).astype(BF16)
        pm = lax.dot_general(qh, kh, (((1,), (1,)), ((), ())), preferred_element_type=F32)
        side = jnp.where(cblk == jb, rblk, jb)
        a = jnp.where((side < jb) if rev else (side > jb), pm, a)

    bend = b[0:1, :] if rev else b[CHUNK - 1:CHUNK, :]
    return a, q * jnp.exp(b), k * jnp.exp(bend - b), bend


def _scores_fast(q, k, b, rev, row, col):
    order = list(range(N_SUB))[::-1] if rev else list(range(N_SUB))
    pos = {blk: p for p, blk in enumerate(order)}
    edge = lambda blk: blk * SUB if rev else blk * SUB + SUB - 1
    e = [b[edge(blk):edge(blk) + 1, :] for blk in order]
    s = [jnp.zeros((1, GLA_DK), F32)] + e[:-1]

    def by_row(vals):
        return jnp.concatenate([jnp.broadcast_to(vals[pos[blk]], (SUB, GLA_DK)) for blk in range(N_SUB)],
                               axis=0)

    srow, erow = by_row(s), by_row(e)
    qh = q * jnp.exp(b - srow)
    kh = k * jnp.exp(erow - b)
    kd = k * jnp.exp(srow - b)

    lhs = []
    for pj in range(N_SUB - 1):
        for blk in range(N_SUB):
            piece = qh[blk * SUB:(blk + 1) * SUB, :]
            p = pos[blk]
            if p <= pj:
                piece = jnp.zeros_like(piece)
            elif p > pj + 1:
                piece = piece * jnp.exp(s[p] - e[pj])
            lhs.append(piece.astype(BF16))
    contract = (((1,), (1,)), ((), ()))
    cross = lax.dot_general(jnp.concatenate(lhs, axis=0), kh.astype(BF16), contract,
                            preferred_element_type=F32)
    diag = lax.dot_general(qh.astype(BF16), kd.astype(BF16), contract, preferred_element_type=F32)

    rblk = row // SUB
    cblk = col // SUB
    a = jnp.zeros((CHUNK, CHUNK), F32)
    for pj in range(N_SUB - 1):
        a = jnp.where(cblk == order[pj], cross[pj * CHUNK:(pj + 1) * CHUNK], a)
    causal = (col >= row) if rev else (col <= row)
    a = jnp.where(cblk == rblk, jnp.where(causal, diag, 0.0), a)

    bend = e[-1]
    return a, qh * jnp.exp(srow), kh * jnp.exp(bend - erow), bend


GLA_RB = 1408


def _gla_sweep_body(*refs, rev):
    if rev:
        q_ref, k_ref, v_ref, g_ref, o_ref, st_ref, b_ref = refs
    else:
        q_ref, k_ref, v_ref, g_ref, og_ref, ob_ref, gain_ref, o_ref, st_ref, b_ref = refs

    @pl.when(pl.program_id(1) == 0)
    def _():
        st_ref[...] = jnp.zeros_like(st_ref)

    row = lax.broadcasted_iota(jnp.int32, (CHUNK, 1), 0)
    col = lax.broadcasted_iota(jnp.int32, (CHUNK, CHUNK), 1)
    rr = lax.broadcasted_iota(jnp.int32, (CHUNK, CHUNK), 0)
    tri = ((col >= rr) if rev else (col <= rr)).astype(BF16)
    ones = jnp.ones((GLA_DK, GLA_DK), BF16)
    n_chunk = GLA_RB // CHUNK

    for c in range(n_chunk):
        g = g_ref[c * CHUNK:(c + 1) * CHUNK, :]
        g_hi = g.astype(BF16)
        rest = g - g_hi.astype(F32)
        g_mid = rest.astype(BF16)
        g_lo = (rest - g_mid.astype(F32)).astype(BF16)
        b_ref[c * CHUNK:(c + 1) * CHUNK, :] = (
            jnp.dot(tri, g_hi, preferred_element_type=F32) + jnp.dot(tri, g_mid, preferred_element_type=F32)
            + jnp.dot(tri, g_lo, preferred_element_type=F32))

    def run(fast):
        def step(t, carry):
            c = n_chunk - 1 - t if rev else t
            rows = pl.ds(pl.multiple_of(c * CHUNK, CHUNK), CHUNK)
            b_all = b_ref[rows, :]
            pending = []
            for h in range(GLA_HEADS):
                kc = slice(h * GLA_DK, (h + 1) * GLA_DK)
                vc = slice(h * GLA_DV, (h + 1) * GLA_DV)
                q = q_ref[rows, kc].astype(F32) * (GLA_DK ** -0.5)
                k = k_ref[rows, kc].astype(F32)
                v = v_ref[rows, vc].astype(BF16)
                b = b_all[:, kc]
                if fast:
                    a, qt, kt, bend = _scores_fast(q, k, b, rev, row, col)
                else:
                    a, qt, kt, bend = _scores_exact(q, k, b, rev, ones, row, col)
                st = st_ref[h]
                o = lax.dot_general(qt.astype(BF16), st.astype(BF16), (((1,), (1,)), ((), ())),
                                    preferred_element_type=F32)
                st_ref[h] = st * jnp.exp(bend) + lax.dot_general(
                    v, kt.astype(BF16), (((0,), (0,)), ((), ())), preferred_element_type=F32)
                pending.append((vc, o, a, v))
            for vc, o, a, v in pending:
                o = o + jnp.dot(a.astype(BF16), v, preferred_element_type=F32)
                if rev:
                    o_ref[rows, vc] = o
                else:
                    o = _rms(o + ob_ref[rows, vc]) * gain_ref[...]
                    o_ref[rows, vc] = (o * _silu(og_ref[rows, vc].astype(F32))).astype(BF16)
            return carry

        lax.fori_loop(0, n_chunk, step, 0, unroll=2 if fast else 1)

    low = jnp.min(jnp.sum(g_ref[...].reshape(GLA_RB // SUB, SUB, GLA_KEY_WIDTH), axis=1))
    safe = low > SAFE_BLOCK_DECAY
    pl.when(safe)(lambda: run(True))
    pl.when(jnp.logical_not(safe))(lambda: run(False))


def _gla_sweep(p3, lg3, rev, ob=None, gla_norm=None):
    nb = TL // GLA_RB
    blk = (lambda i: nb - 1 - i) if rev else (lambda i: i)
    key_blocks = P_MAIN // GLA_KEY_WIDTH
    q_col = 3 * HY_WIDTH // GLA_KEY_WIDTH
    v_col = (3 * HY_WIDTH + 2 * GLA_KEY_WIDTH) // GLA_WIDTH
    assert key_blocks * GLA_KEY_WIDTH == P_MAIN
    narrow = lambda col: pl.BlockSpec((None, GLA_RB, GLA_KEY_WIDTH), lambda b, i: (b, blk(i), col))
    wide = lambda col: pl.BlockSpec((None, GLA_RB, GLA_WIDTH), lambda b, i: (b, blk(i), col))
    in_specs = [narrow(q_col), narrow(q_col + 1), wide(v_col), narrow(1 if rev else 0)]
    args = [p3, p3, p3, lg3]
    if not rev:
        in_specs += [wide(v_col + 1), wide(0), pl.BlockSpec((1, GLA_DV), lambda b, i: (0, 0))]
        args += [p3, ob, gla_norm.reshape(1, -1)]
    return pl.pallas_call(
        functools.partial(_gla_sweep_body, rev=rev),
        out_shape=jax.ShapeDtypeStruct((BATCH, TL, GLA_WIDTH), F32 if rev else BF16),
        grid=(BATCH, nb),
        in_specs=in_specs,
        out_specs=wide(0),
        scratch_shapes=[pltpu.VMEM((GLA_HEADS, GLA_DV, GLA_DK), F32),
                        pltpu.VMEM((GLA_RB, GLA_KEY_WIDTH), F32)],
        compiler_params=_params(("parallel", "arbitrary"), 52),
        name="gla_down" if rev else "gla_up",
    )(*args)


OUT_TM = 512


def _outproj_body(h_ref, yh_ref, yg_ref, wh_ref, wg_ref, o_ref):
    o_ref[...] = (h_ref[...] + jnp.dot(yh_ref[...], wh_ref[...], preferred_element_type=F32)
                  + jnp.dot(yg_ref[...], wg_ref[...], preferred_element_type=F32))


def _outproj(hbuf, yh, ygbuf, w_out):
    wo = _to_bf16(w_out)
    half = lambda i: pl.BlockSpec((HY_WIDTH, D_MODEL), lambda r: (i, 0))
    return pl.pallas_call(
        _outproj_body,
        out_shape=jax.ShapeDtypeStruct((X_ROWS, D_MODEL), F32),
        grid=(X_ROWS // OUT_TM,),
        in_specs=[pl.BlockSpec((pl.Element(OUT_TM), pl.Element(D_MODEL)), lambda r: (_x_row(r, OUT_TM), 0)),
                  pl.BlockSpec((OUT_TM, HY_WIDTH), lambda r: (r, 0)),
                  pl.BlockSpec((pl.Element(OUT_TM), pl.Element(GLA_WIDTH)),
                               lambda r: (_x_row(r, OUT_TM, BF16_SUBLANE), 0)),
                  half(0), half(1)],
        out_specs=pl.BlockSpec((OUT_TM, D_MODEL), lambda r: (r, 0)),
        compiler_params=_params(("parallel",), 48),
        name="outproj",
    )(hbuf, yh, ygbuf, wo, wo)


def _long_conv(u3, filt):
    f1, f3, mf, mi, mirror = _dft_tables()
    as_n1 = lambda a: a.reshape(2 * FFT_R1, FFT_N2, HY_WIDTH)
    by_k1 = lambda a: a.reshape(2, FFT_N1, FFT_N2, HY_WIDTH)
    zs = _lmm(f1, as_n1(filt), BF16, name="lmm_g")
    a = _lmm(f1, as_n1(u3), BF16, name="lmm_fwd")
    bm = _spec(mf, mirror, mi, by_k1(a), by_k1(zs))
    y = _lmm(f3, bm.reshape(2 * FFT_N1, FFT_N2, HY_WIDTH), F32, scale=1.0 / NFFT, name="lmm_inv")
    return y.reshape(BATCH, TFFT, HY_WIDTH)


def kernel(x, meta_tokens, ffn1_norm, ffn1_w_gate, ffn1_w_up, ffn1_w_down, mix_norm, w_in, conv_w, conv_b,
           filt_w1, filt_b1, filt_w2, filt_b2, filt_w3, filt_freq, hyena_d, hyena_norm, gk_w2, gk_b2,
           gla_norm, w_out, ffn2_norm, ffn2_w_gate, ffn2_w_up, ffn2_w_down, final_norm):
    assert x.shape == (BATCH, SEQ, D_MODEL) and ffn1_norm.shape[0] == 1

    hbuf = _ffn(x.reshape(X_ROWS, D_MODEL), ffn1_norm[0], ffn1_w_gate[0], ffn1_w_up[0], ffn1_w_down[0],
                final_norm, final=False, shared_rows=_meta_rows(meta_tokens))

    p, lg = _inproj(hbuf, _inproj_weights(mix_norm[0], w_in[0], gk_w2[0], gk_b2[0]))
    p3 = p.reshape(BATCH, TL, P_MAIN)
    lg3 = lg.reshape(BATCH, TL, 2 * GLA_KEY_WIDTH)

    filt = _filters(filt_w1[0], filt_b1[0], filt_w2[0], filt_b2[0], filt_w3[0], filt_freq[0])
    cw, cb = conv_w[0], conv_b[0].reshape(1, -1)
    y3 = _long_conv(_uconv(p3, cw, cb), filt)
    yh = _ymix(y3, p3, cw, cb, hyena_d[0], hyena_norm[0])
    yg = _gla_sweep(p3, lg3, rev=False, ob=_gla_sweep(p3, lg3, rev=True), gla_norm=gla_norm[0])

    h2 = _outproj(hbuf, yh.reshape(X_ROWS, HY_WIDTH), yg.reshape(ROWS, GLA_WIDTH), w_out[0])
    out = _ffn(h2, ffn2_norm[0], ffn2_w_gate[0], ffn2_w_up[0], ffn2_w_down[0], final_norm, final=True)
    return out.reshape(BATCH, SEQ, D_MODEL)
```

```python
import functools

import numpy as np
import jax
import jax.numpy as jnp
from jax import lax
from jax.experimental import pallas as pl
from jax.experimental.pallas import tpu as pltpu

F32 = jnp.float32
BF16 = jnp.bfloat16

D_MODEL = 2048
BATCH = 2
SEQ = 4096
N_META = 16
L_TOK = SEQ + N_META
PAD = 112
X0 = PAD + N_META
TL = PAD + L_TOK
ROWS = BATCH * TL
X_ROWS = BATCH * SEQ
HY_WIDTH = 1024
HY_GROUPS = 8
HY_GROUP = HY_WIDTH // HY_GROUPS
FILT_EMB = 33
FILT_BANDS = 16
FILT_HIDDEN = 64
GLA_WIDTH = 1024
GLA_HEADS = 4
GLA_KEY_WIDTH = 512
GLA_DK = 128
GLA_DV = 256
GATE_RANK = 16
GATE_NORMALIZER = 16.0
CHUNK = 64
SUB = 16
D_FF = 5632
P_MAIN = 3 * HY_WIDTH + 2 * GLA_KEY_WIDTH + 2 * GLA_WIDTH
EPS = 1e-6

FFT_N1 = 72
FFT_N2 = 128
NFFT = FFT_N1 * FFT_N2
FFT_R1 = 40
TFFT = FFT_R1 * FFT_N2

MIB = 1024 * 1024
SUBLANE = 8
BF16_SUBLANE = 16
LANE = 128


def _params(sem, vmem_mib):
    return pltpu.CompilerParams(dimension_semantics=sem, vmem_limit_bytes=vmem_mib * MIB)


def _rms(x):
    return x * lax.rsqrt(jnp.mean(x * x, axis=-1, keepdims=True) + EPS)


def _silu(x):
    return x * jax.nn.sigmoid(x)


LAY_TM = 528


def _x_row(i, tm, unit=SUBLANE):
    per_batch = SEQ // tm
    r = (i // per_batch) * (TL // unit) + X0 // unit + (i % per_batch) * (tm // unit)
    return pl.multiple_of(r * unit, unit)


def _lay_src_row(i):
    per_batch = TL // LAY_TM
    r = (i // per_batch) * (SEQ // SUBLANE) + jnp.maximum(
        (i % per_batch) * (LAY_TM // SUBLANE) - X0 // SUBLANE, 0)
    return pl.multiple_of(r * SUBLANE, SUBLANE)


def _meta_rows(meta_tokens):
    return jnp.concatenate([jnp.zeros((PAD, D_MODEL), F32), meta_tokens.astype(F32)], axis=0)


FFN_TM = 512
FFN_TF = 512


FFN_HEAD_TF = 256


def _ffn_begin(x_ref, shared_ref, xin_ref, gain_ref, xn_ref, acc_ref, first):
    if shared_ref is not None:
        @pl.when(first)
        def _():
            xin_ref[0:X0, :] = shared_ref[...]
            xin_ref[X0:LAY_TM, :] = x_ref[0:LAY_TM - X0, :]

        @pl.when(jnp.logical_not(first))
        def _():
            xin_ref[...] = x_ref[...]

    xn_ref[...] = (_rms(xin_ref[...]) * gain_ref[...]).astype(BF16)
    acc_ref[...] = jnp.zeros_like(acc_ref)


def _ffn_step(xn_ref, wg, wu, wd, acc_ref):
    xn = xn_ref[...]
    g = jnp.dot(xn, wg, preferred_element_type=F32)
    u = jnp.dot(xn, wu, preferred_element_type=F32)
    a = (_silu(g) * u).astype(BF16)
    acc_ref[...] += jnp.dot(a, wd, preferred_element_type=F32)


def _ffn_result(xin_ref, acc_ref, fgain_ref, final):
    h = xin_ref[...] + 0.5 * acc_ref[...]
    return _rms(h) * fgain_ref[...] if final else h


def _ffn_head_body(x_ref, gain_ref, wg_ref, wu_ref, wd_ref, fgain_ref, *rest, final, layout):
    if layout:
        shared_ref, o_ref, wg16_ref, wu16_ref, wd16_ref, xn_ref, acc_ref, xin_ref = rest
    else:
        o_ref, wg16_ref, wu16_ref, wd16_ref, xn_ref, acc_ref = rest
        shared_ref, xin_ref = None, x_ref
    j = pl.program_id(0)

    @pl.when(j == 0)
    def _():
        _ffn_begin(x_ref, shared_ref, xin_ref, gain_ref, xn_ref, acc_ref, True)

    wg, wu, wd = wg_ref[...].astype(BF16), wu_ref[...].astype(BF16), wd_ref[...].astype(BF16)
    wg16_ref[...] = wg
    wu16_ref[...] = wu
    wd16_ref[...] = wd
    _ffn_step(xn_ref, wg, wu, wd, acc_ref)

    @pl.when(j == pl.num_programs(0) - 1)
    def _():
        o_ref[...] = _ffn_result(xin_ref, acc_ref, fgain_ref, final)


def _ffn_body(x_ref, gain_ref, wg_ref, wu_ref, wd_ref, fgain_ref, head_ref, *rest, final, layout):
    if layout:
        shared_ref, o_ref, xn_ref, acc_ref, xin_ref = rest
    else:
        o_ref, xn_ref, acc_ref = rest
        shared_ref, xin_ref = None, x_ref
    i, j = pl.program_id(0), pl.program_id(1)
    last = j == pl.num_programs(1) - 1

    @pl.when(i > 0)
    def _():
        @pl.when(j == 0)
        def _():
            _ffn_begin(x_ref, shared_ref, xin_ref, gain_ref, xn_ref, acc_ref, i % (TL // LAY_TM) == 0)

        _ffn_step(xn_ref, wg_ref[...], wu_ref[...], wd_ref[...], acc_ref)

        @pl.when(last)
        def _():
            o_ref[...] = _ffn_result(xin_ref, acc_ref, fgain_ref, final)

    @pl.when(jnp.logical_and(i == 0, last))
    def _():
        o_ref[...] = head_ref[...]


CAST_STEPS = 4


def _cast_body(x_ref, o_ref):
    o_ref[...] = x_ref[...].astype(BF16)


def _to_bf16(w, cols=None):
    rows = w.shape[0]
    cols = w.shape[1] if cols is None else cols
    tr = rows // CAST_STEPS
    assert tr * CAST_STEPS == rows and tr % BF16_SUBLANE == 0 and cols % LANE == 0
    return pl.pallas_call(
        _cast_body,
        out_shape=jax.ShapeDtypeStruct((rows, cols), BF16),
        grid=(CAST_STEPS,),
        in_specs=[pl.BlockSpec((tr, cols), lambda i: (i, 0))],
        out_specs=pl.BlockSpec((tr, cols), lambda i: (i, 0)),
        compiler_params=_params(("parallel",), 48),
        name="to_bf16",
    )(w)


def _ffn(x2, gain, wg, wu, wd, fgain, final, shared_rows=None):
    layout = shared_rows is not None
    tm = LAY_TM if layout else FFN_TM
    rows = ROWS if layout else X_ROWS
    gain, fgain = gain.reshape(1, -1), fgain.reshape(1, -1)
    scratch = [pltpu.VMEM((tm, D_MODEL), BF16), pltpu.VMEM((tm, D_MODEL), F32)]
    extra = ()
    if layout:
        extra = (shared_rows,)
        scratch.append(pltpu.VMEM((tm, D_MODEL), F32))
    name = "ffn_final" if final else "ffn"

    tf = FFN_HEAD_TF
    vec1 = pl.BlockSpec((1, D_MODEL), lambda j: (0, 0))
    wide1 = pl.BlockSpec((D_MODEL, tf), lambda j: (0, j))
    tall1 = pl.BlockSpec((tf, D_MODEL), lambda j: (j, 0))
    tile1 = pl.BlockSpec((tm, D_MODEL), lambda j: (0, 0))
    head, wg16, wu16, wd16 = pl.pallas_call(
        functools.partial(_ffn_head_body, final=final, layout=layout),
        out_shape=(jax.ShapeDtypeStruct((tm, D_MODEL), F32), jax.ShapeDtypeStruct(wg.shape, BF16),
                   jax.ShapeDtypeStruct(wu.shape, BF16), jax.ShapeDtypeStruct(wd.shape, BF16)),
        grid=(D_FF // tf,),
        in_specs=[tile1, vec1, wide1, wide1, tall1, vec1] + [pl.BlockSpec((X0, D_MODEL), lambda j: (0, 0))] * layout,
        out_specs=(tile1, wide1, wide1, tall1),
        scratch_shapes=scratch,
        compiler_params=_params(("arbitrary",), 56),
        name=name + "_head",
    )(x2, gain, wg, wu, wd, fgain, *extra)

    col = lambda i, j: jnp.where(i == 0, 0, j)
    vec = pl.BlockSpec((1, D_MODEL), lambda i, j: (0, 0))
    wide = pl.BlockSpec((D_MODEL, FFN_TF), lambda i, j: (0, col(i, j)))
    tall = pl.BlockSpec((FFN_TF, D_MODEL), lambda i, j: (col(i, j), 0))
    if layout:
        x_spec = pl.BlockSpec((pl.Element(tm), pl.Element(D_MODEL)), lambda i, j: (_lay_src_row(i), 0))
    else:
        x_spec = pl.BlockSpec((tm, D_MODEL), lambda i, j: (i, 0))
    return pl.pallas_call(
        functools.partial(_ffn_body, final=final, layout=layout),
        out_shape=jax.ShapeDtypeStruct((rows, D_MODEL), F32),
        grid=(rows // tm, D_FF // FFN_TF),
        in_specs=[x_spec, vec, wide, wide, tall, vec, pl.BlockSpec((tm, D_MODEL), lambda i, j: (0, 0))]
        + [pl.BlockSpec((X0, D_MODEL), lambda i, j: (0, 0))] * layout,
        out_specs=pl.BlockSpec((tm, D_MODEL), lambda i, j: (i, 0)),
        scratch_shapes=scratch,
        compiler_params=_params(("parallel", "arbitrary"), 56),
        name=name,
    )(x2, gain, wg16, wu16, wd16, fgain, head, *extra)


INP_TM = 2 * LAY_TM
INP_TN = 1536
LR_PAD = 128


def _inproj_body(x_ref, gain_ref, w_ref, wlr_ref, w2_ref, b2_ref, p_ref, lg_ref, xn_ref, lr_ref):
    @pl.when(pl.program_id(1) == 0)
    def _():
        xn = (_rms(x_ref[...]) * gain_ref[...]).astype(BF16)
        xn_ref[...] = xn
        lr_ref[...] = jnp.dot(xn, wlr_ref[...], preferred_element_type=F32).astype(BF16)

    z = jnp.dot(lr_ref[...], w2_ref[...], preferred_element_type=F32) + b2_ref[...]
    lg_ref[...] = (jnp.minimum(z, 0.0) - jnp.log(1.0 + jnp.exp(-jnp.abs(z)))) * (1.0 / GATE_NORMALIZER)
    p_ref[...] = jnp.dot(xn_ref[...], w_ref[...], preferred_element_type=F32).astype(BF16)


SPLIT_TC = 1024


def _split_w_in_body(w_ref, wlr_ref, main_ref, lr_ref):
    main_ref[...] = w_ref[...].T.astype(BF16)

    @pl.when(pl.program_id(0) == 0)
    def _():
        rows = jnp.concatenate([wlr_ref[...], jnp.zeros((LR_PAD - 2 * GATE_RANK, D_MODEL), F32)], axis=0)
        lr_ref[...] = rows.T.astype(BF16)


def _split_w_in(w_in):
    w_t = jnp.transpose(w_in)
    return pl.pallas_call(
        _split_w_in_body,
        out_shape=(jax.ShapeDtypeStruct((D_MODEL, P_MAIN), BF16), jax.ShapeDtypeStruct((D_MODEL, LR_PAD), BF16)),
        grid=(P_MAIN // SPLIT_TC,),
        in_specs=[pl.BlockSpec((SPLIT_TC, D_MODEL), lambda i: (i, 0)),
                  pl.BlockSpec((2 * GATE_RANK, D_MODEL), lambda i: (P_MAIN // (2 * GATE_RANK), 0))],
        out_specs=(pl.BlockSpec((D_MODEL, SPLIT_TC), lambda i: (0, i)),
                   pl.BlockSpec((D_MODEL, LR_PAD), lambda i: (0, 0))),
        compiler_params=_params(("arbitrary",), 48),
        name="split_w_in",
    )(w_t, w_t)


def _inproj_weights(gain, w_in, gk_w2, gk_b2):
    w_main, w_lr = _split_w_in(w_in)
    w2 = jnp.zeros((LR_PAD, 2 * GLA_KEY_WIDTH), F32)
    w2 = w2.at[:GATE_RANK, :GLA_KEY_WIDTH].set(gk_w2[0])
    w2 = w2.at[GATE_RANK:2 * GATE_RANK, GLA_KEY_WIDTH:].set(gk_w2[1]).astype(BF16)
    return gain.reshape(1, -1), w_main, w_lr, w2, gk_b2.reshape(1, 2 * GLA_KEY_WIDTH)


def _inproj(hbuf, weights):
    tm, tn = INP_TM, INP_TN
    steps = P_MAIN // tn
    tg = 2 * GLA_KEY_WIDTH // steps
    assert steps * tn == P_MAIN and steps * tg == 2 * GLA_KEY_WIDTH and tg % LANE == 0
    return pl.pallas_call(
        _inproj_body,
        out_shape=(jax.ShapeDtypeStruct((ROWS, P_MAIN), BF16),
                   jax.ShapeDtypeStruct((ROWS, 2 * GLA_KEY_WIDTH), F32)),
        grid=(ROWS // tm, steps),
        in_specs=[
            pl.BlockSpec((tm, D_MODEL), lambda i, j: (i, 0)),
            pl.BlockSpec((1, D_MODEL), lambda i, j: (0, 0)),
            pl.BlockSpec((D_MODEL, tn), lambda i, j: (0, j)),
            pl.BlockSpec((D_MODEL, LR_PAD), lambda i, j: (0, 0)),
            pl.BlockSpec((LR_PAD, tg), lambda i, j: (0, j)),
            pl.BlockSpec((1, tg), lambda i, j: (0, j)),
        ],
        out_specs=(pl.BlockSpec((tm, tn), lambda i, j: (i, j)),
                   pl.BlockSpec((tm, tg), lambda i, j: (i, j))),
        scratch_shapes=[pltpu.VMEM((tm, D_MODEL), BF16), pltpu.VMEM((tm, LR_PAD), BF16)],
        compiler_params=_params(("parallel", "arbitrary"), 56),
        name="inproj",
    )(hbuf, *weights)


FILT_TR = 640
FEAT_PAD = 128


def _filt_tables():
    pos = np.arange(TFFT, dtype=np.float64)
    t = pos / (L_TOK - 1)
    w = (2.0 * np.pi / L_TOK) * pos
    bands = 1e-4 + np.arange(FILT_BANDS, dtype=np.float64) * ((FILT_BANDS - 1 - 1e-4) / (FILT_BANDS - 1))
    ang = w[:, None] * bands[None, :]
    feats = np.zeros((TFFT, FEAT_PAD), np.float64)
    feats[:, 0] = t
    feats[:, 1:1 + FILT_BANDS] = np.cos(ang)
    feats[:, 1 + FILT_BANDS:FILT_EMB] = -np.sin(ang)
    lo, hi = np.log(1e-2) / 1.5, np.log(1e-2) / 0.3
    deltas = np.abs(lo + np.arange(HY_WIDTH, dtype=np.float64) * ((hi - lo) / (HY_WIDTH - 1)))
    return feats.astype(np.float32), deltas.astype(np.float32).reshape(1, HY_WIDTH)


def _filt_body(feat_ref, t_ref, w1_ref, b1_ref, w2_ref, b2_ref, w3_ref, fr_ref, dl_ref, h_ref):
    start = pl.program_id(0) * FILT_TR

    @pl.when(start < L_TOK)
    def _():
        hp = lax.Precision.HIGHEST
        fr = fr_ref[...]
        z = jnp.sin(fr * (jnp.dot(w1_ref[...], feat_ref[...], precision=hp, preferred_element_type=F32)
                          + b1_ref[...]))
        z = jnp.sin(fr * (jnp.dot(w2_ref[...], z, precision=hp, preferred_element_type=F32) + b2_ref[...]))
        hh = lax.dot_general(z, w3_ref[...], (((0,), (0,)), ((), ())), precision=hp,
                             preferred_element_type=F32)
        pos = start + lax.broadcasted_iota(jnp.int32, (FILT_TR, 1), 0)
        win = jnp.exp(-t_ref[...] * dl_ref[...])
        win = jnp.where(pos < L_TOK, win, 0.0)
        hf = hh[:, :HY_WIDTH] * win
        hb = hh[:, HY_WIDTH:] * win
        h_ref[0] = jnp.where(pos == 0, hf + hb, hf)
        h_ref[1] = jnp.where(pos == 0, 0.0, hb)

    @pl.when(start >= L_TOK)
    def _():
        h_ref[...] = jnp.zeros_like(h_ref)


def _filters(w1, b1, w2, b2, w3, freq):
    feats, deltas = _filt_tables()
    w1t = jnp.pad(w1, ((0, FEAT_PAD - FILT_EMB), (0, 0))).T
    col = lambda v: v.reshape(-1, 1)
    full = lambda shape: pl.BlockSpec(shape, lambda i: (0, 0))
    return pl.pallas_call(
        _filt_body,
        out_shape=jax.ShapeDtypeStruct((2, TFFT, HY_WIDTH), F32),
        grid=(TFFT // FILT_TR,),
        in_specs=[
            pl.BlockSpec((FEAT_PAD, FILT_TR), lambda i: (0, i)),
            pl.BlockSpec((FILT_TR, 1), lambda i: (i, 0)),
            full((FILT_HIDDEN, FEAT_PAD)), full((FILT_HIDDEN, 1)),
            full((FILT_HIDDEN, FILT_HIDDEN)), full((FILT_HIDDEN, 1)),
            full((FILT_HIDDEN, 2 * HY_WIDTH)), full((FILT_HIDDEN, 1)), full((1, HY_WIDTH)),
        ],
        out_specs=pl.BlockSpec((2, FILT_TR, HY_WIDTH), lambda i: (0, i, 0)),
        compiler_params=_params(("parallel",), 40),
        name="filt",
    )(jnp.asarray(feats.T), jnp.asarray(feats[:, 0:1]), w1t, col(b1), w2.T, col(b2), w3, col(freq),
      jnp.asarray(deltas))


def _dft_tables_np():
    n1 = np.arange(FFT_N1)
    ang1 = 2.0 * np.pi * ((n1[:, None] * n1[None, :]) % FFT_N1) / FFT_N1
    c1, s1 = np.cos(ang1)[:, :FFT_R1], np.sin(ang1)[:, :FFT_R1]
    f1 = np.block([[c1, s1], [-s1, c1]])
    f3 = np.block([[c1.T, -s1.T], [s1.T, c1.T]])
    k1 = np.arange(FFT_N1)[:, None, None]
    k2 = np.arange(FFT_N2)[None, :, None]
    n2 = np.arange(FFT_N2)[None, None, :]
    ang2 = 2.0 * np.pi * ((n2 * (k1 + FFT_N1 * k2)) % NFFT) / NFFT
    c2, s2 = np.cos(ang2), np.sin(ang2)
    mf = np.concatenate([np.concatenate([c2, s2], axis=2), np.concatenate([-s2, c2], axis=2)], axis=1)
    c2t, s2t = np.swapaxes(c2, 1, 2), np.swapaxes(s2, 1, 2)
    mi = np.concatenate([np.concatenate([c2t, -s2t], axis=2), np.concatenate([s2t, c2t], axis=2)], axis=1)
    k1r = np.arange(FFT_N1)
    k2r = np.arange(FFT_N2)
    perm = np.where(k1r[:, None] == 0, (FFT_N2 - k2r[None, :]) % FFT_N2, FFT_N2 - 1 - k2r[None, :])
    rows = np.concatenate([perm, perm + FFT_N2], axis=1)
    mirror = mf[((FFT_N1 - k1r) % FFT_N1)[:, None], rows, :]
    return f1, f3, mf, mi, mirror


def _dft_tables():
    return tuple(jnp.asarray(a.astype(np.float32)).astype(BF16) for a in _dft_tables_np())


LMM_PITCH = FFT_N2 + SUBLANE
LMM_MC = 2 * FFT_N1


def _lmm_body(f_ref, x_ref, o_ref, xs_ref, os_ref, *, scale):
    m, k = f_ref.shape
    for g in range(k):
        xs_ref[pl.ds(g * LMM_PITCH, FFT_N2), :] = x_ref[pl.ds(g * FFT_N2, FFT_N2), :].astype(F32)
    for m0 in range(0, m, LMM_MC):
        f = f_ref[m0:min(m0 + LMM_MC, m), :]

        def slab(s, carry):
            x = xs_ref[pl.ds(s, k, stride=LMM_PITCH), :].astype(BF16)
            r = jnp.dot(f, x, preferred_element_type=F32)
            os_ref[pl.ds(s, f.shape[0], stride=LMM_PITCH), :] = r * scale if scale != 1.0 else r
            return carry

        lax.fori_loop(0, FFT_N2, slab, 0, unroll=32)
        for g in range(f.shape[0]):
            o_ref[pl.ds((m0 + g) * FFT_N2, FFT_N2), :] = os_ref[pl.ds(g * LMM_PITCH, FFT_N2), :].astype(
                o_ref.dtype)


def _lmm(f, x3, out_dtype, scale=1.0, name="lmm"):
    m, k = f.shape
    mc = min(m, LMM_MC)
    assert m % mc == 0
    out = pl.pallas_call(
        functools.partial(_lmm_body, scale=scale),
        out_shape=jax.ShapeDtypeStruct((m * FFT_N2, HY_WIDTH), out_dtype),
        grid=(HY_WIDTH // LANE,),
        in_specs=[pl.BlockSpec((m, k), lambda j: (0, 0)),
                  pl.BlockSpec((k * FFT_N2, LANE), lambda j: (0, j))],
        out_specs=pl.BlockSpec((m * FFT_N2, LANE), lambda j: (0, j)),
        scratch_shapes=[pltpu.VMEM((k * LMM_PITCH, LANE), F32), pltpu.VMEM((mc * LMM_PITCH, LANE), F32)],
        compiler_params=_params(("parallel",), 56),
        name=name,
    )(f, x3.reshape(k * FFT_N2, HY_WIDTH))
    return out.reshape(m, FFT_N2, HY_WIDTH)


SPEC_K1 = 4


def _spec_body(mf_ref, mr_ref, mi_ref, a_ref, z_ref, *rest):
    zm_refs, o_ref = rest[:SPEC_K1], rest[SPEC_K1]
    for t in range(SPEC_K1):
        mf = mf_ref[t]
        z = jnp.dot(mf, z_ref[:, t].reshape(2 * FFT_N2, HY_WIDTH), preferred_element_type=F32)
        zm = jnp.dot(mr_ref[t], zm_refs[t][...].reshape(2 * FFT_N2, HY_WIDTH), preferred_element_type=F32)
        a, b = z[:FFT_N2], z[FFT_N2:]
        am, bm = zm[:FFT_N2], zm[FFT_N2:]
        gr, gi = 0.5 * (a + am + b + bm), 0.5 * (b - bm + a - am)

        x = jnp.dot(mf, a_ref[:, t].reshape(2 * FFT_N2, HY_WIDTH), preferred_element_type=F32)
        xr, xi = x[:FFT_N2], x[FFT_N2:]
        y = jnp.concatenate([xr * gr - xi * gi, xr * gi + xi * gr], axis=0).astype(BF16)
        o_ref[:, t] = jnp.dot(mi_ref[t], y, preferred_element_type=F32).astype(BF16).reshape(
            2, FFT_N2, HY_WIDTH)


def _spec(mf, mirror, mi, a, zs):
    here = pl.BlockSpec((2, SPEC_K1, FFT_N2, HY_WIDTH), lambda i: (0, i, 0, 0))
    mat = pl.BlockSpec((SPEC_K1, 2 * FFT_N2, 2 * FFT_N2), lambda i: (i, 0, 0))
    mirrored = [pl.BlockSpec((2, 1, FFT_N2, HY_WIDTH),
                             lambda i, t=t: (0, (FFT_N1 - (SPEC_K1 * i + t)) % FFT_N1, 0, 0))
                for t in range(SPEC_K1)]
    return pl.pallas_call(
        _spec_body,
        out_shape=jax.ShapeDtypeStruct((2, FFT_N1, FFT_N2, HY_WIDTH), BF16),
        grid=(FFT_N1 // SPEC_K1,),
        in_specs=[mat, mat, mat, here, here] + mirrored,
        out_specs=here,
        compiler_params=_params(("parallel",), 40),
        name="spec",
    )(mf, mirror, mi, a, zs, *([zs] * SPEC_K1))


HY_CB = 256


def _short_conv(p_ref, w_ref, b_ref):
    p = p_ref[...].astype(F32)
    w = w_ref[...]
    prev = pltpu.roll(p, 1, 0)
    nxt = pltpu.roll(p, TL - 1, 0)
    return b_ref[...] + prev * w[0:1] + p * w[1:2] + nxt * w[2:3]


def _uconv_body(x1_ref, vh_ref, w1_ref, wv_ref, b1_ref, bv_ref, u_ref):
    u = _short_conv(vh_ref, wv_ref, bv_ref) * _short_conv(x1_ref, w1_ref, b1_ref)
    row = lax.broadcasted_iota(jnp.int32, (TL, 1), 0)
    u_ref[pl.ds(0, TL), :] = jnp.where(row >= PAD, u, 0.0)
    u_ref[pl.ds(TL, TFFT - TL), :] = jnp.zeros((TFFT - TL, HY_CB), F32)


def _hy_specs(first_block):
    nb = HY_WIDTH // HY_CB
    return (pl.BlockSpec((None, TL, HY_CB), lambda b, j: (b, 0, first_block * nb + j)),
            pl.BlockSpec((3, HY_CB), lambda b, j: (0, first_block * nb + j)),
            pl.BlockSpec((1, HY_CB), lambda b, j: (0, first_block * nb + j)))


def _uconv(p3, conv_w, conv_b):
    x1, w1, b1 = _hy_specs(1)
    vh, wv, bv = _hy_specs(2)
    return pl.pallas_call(
        _uconv_body,
        out_shape=jax.ShapeDtypeStruct((BATCH, TFFT, HY_WIDTH), F32),
        grid=(BATCH, HY_WIDTH // HY_CB),
        in_specs=[x1, vh, w1, wv, b1, bv],
        out_specs=pl.BlockSpec((None, TFFT, HY_CB), lambda b, j: (b, 0, j)),
        compiler_params=_params(("parallel", "parallel"), 48),
        name="uconv",
    )(p3, p3, conv_w, conv_w, conv_b, conv_b)


def _ymix_body(y_ref, u_ref, x0_ref, w0_ref, b0_ref, d_ref, gain_ref, o_ref):
    yy = (y_ref[...] + d_ref[...] * u_ref[...]) * _short_conv(x0_ref, w0_ref, b0_ref)
    gain = gain_ref[...]
    for s in range(0, HY_CB, HY_GROUP):
        o_ref[:, s:s + HY_GROUP] = (_rms(yy[X0:, s:s + HY_GROUP]) * gain[:, s:s + HY_GROUP]).astype(BF16)


def _ymix(y3, u3, p3, conv_w, conv_b, hyena_d, hyena_norm):
    x0, w0, b0 = _hy_specs(0)
    vec = pl.BlockSpec((1, HY_CB), lambda b, j: (0, j))
    rows = pl.BlockSpec((None, TL, HY_CB), lambda b, j: (b, 0, j))
    return pl.pallas_call(
        _ymix_body,
        out_shape=jax.ShapeDtypeStruct((BATCH, SEQ, HY_WIDTH), BF16),
        grid=(BATCH, HY_WIDTH // HY_CB),
        in_specs=[rows, rows, x0, w0, b0, vec, vec],
        out_specs=pl.BlockSpec((None, SEQ, HY_CB), lambda b, j: (b, 0, j)),
        compiler_params=_params(("parallel", "parallel"), 48),
        name="ymix",
    )(y3, u3, p3, conv_w, conv_b, hyena_d.reshape(1, -1), hyena_norm.reshape(1, -1))


N_SUB = CHUNK // SUB
SAFE_BLOCK_DECAY = -60.0


def _scores_exact(q, k, b, rev, ones, row, col):
    rsub = row % SUB
    terms = []
    for d in range(SUB):
        if d == 0:
            kr, br = k, b
        else:
            sh = CHUNK - d if rev else d
            kr, br = pltpu.roll(k, sh, 0), pltpu.roll(b, sh, 0)
        valid = (rsub + d < SUB) if rev else (rsub >= d)
        t = q * kr * jnp.exp(jnp.minimum(b - br, 0.0))
        terms.append(jnp.where(valid, t, 0.0).astype(BF16))
    sums = jnp.dot(jnp.concatenate(terms, axis=0), ones, preferred_element_type=F32)
    a = jnp.zeros((CHUNK, CHUNK), F32)
    for d in range(SUB):
        tgt = row + d if rev else row - d
        a = jnp.where(col == tgt, sums[d * CHUNK:(d + 1) * CHUNK, :CHUNK], a)

    rblk = row // SUB
    cblk = col // SUB
    for jb in (range(1, N_SUB) if rev else range(N_SUB - 1)):
        e = jb * SUB if rev else jb * SUB + SUB - 1
        ref = b[e:e + 1, :]
        qh = (q * jnp.exp(jnp.minimum(b - ref, 0.0))).astype(BF16)
        kh = (k * jnp.exp(jnp.minimum(ref - b, 0.0))).astype(BF16)
        pm = lax.dot_general(qh, kh, (((1,), (1,)), ((), ())), preferred_element_type=F32)
        side = jnp.where(cblk == jb, rblk, jb)
        a = jnp.where((side < jb) if rev else (side > jb), pm, a)

    bend = b[0:1, :] if rev else b[CHUNK - 1:CHUNK, :]
    return a, q * jnp.exp(b), k * jnp.exp(bend - b), bend


def _scores_fast(q, k, b, rev, row, col):
    order = list(range(N_SUB))[::-1] if rev else list(range(N_SUB))
    pos = {blk: p for p, blk in enumerate(order)}
    edge = lambda blk: blk * SUB if rev else blk * SUB + SUB - 1
    e = [b[edge(blk):edge(blk) + 1, :] for blk in order]
    s = [jnp.zeros((1, GLA_DK), F32)] + e[:-1]

    def by_row(vals):
        return jnp.concatenate([jnp.broadcast_to(vals[pos[blk]], (SUB, GLA_DK)) for blk in range(N_SUB)],
                               axis=0)

    srow, erow = by_row(s), by_row(e)
    qh = q * jnp.exp(b - srow)
    kh = k * jnp.exp(erow - b)
    kd = k * jnp.exp(srow - b)

    lhs = []
    for pj in range(N_SUB - 1):
        for blk in range(N_SUB):
            piece = qh[blk * SUB:(blk + 1) * SUB, :]
            p = pos[blk]
            if p <= pj:
                piece = jnp.zeros_like(piece)
            elif p > pj + 1:
                piece = piece * jnp.exp(s[p] - e[pj])
            lhs.append(piece.astype(BF16))
    contract = (((1,), (1,)), ((), ()))
    cross = lax.dot_general(jnp.concatenate(lhs, axis=0), kh.astype(BF16), contract,
                            preferred_element_type=F32)
    diag = lax.dot_general(qh.astype(BF16), kd.astype(BF16), contract, preferred_element_type=F32)

    rblk = row // SUB
    cblk = col // SUB
    a = jnp.zeros((CHUNK, CHUNK), F32)
    for pj in range(N_SUB - 1):
        a = jnp.where(cblk == order[pj], cross[pj * CHUNK:(pj + 1) * CHUNK], a)
    causal = (col >= row) if rev else (col <= row)
    a = jnp.where(cblk == rblk, jnp.where(causal, diag, 0.0), a)

    bend = e[-1]
    return a, qh * jnp.exp(srow), kh * jnp.exp(bend - erow), bend


GLA_RB = 1408


def _gla_sweep_body(*refs, rev):
    if rev:
        q_ref, k_ref, v_ref, g_ref, o_ref, st_ref, b_ref = refs
    else:
        q_ref, k_ref, v_ref, g_ref, og_ref, ob_ref, gain_ref, o_ref, st_ref, b_ref = refs

    @pl.when(pl.program_id(1) == 0)
    def _():
        st_ref[...] = jnp.zeros_like(st_ref)

    row = lax.broadcasted_iota(jnp.int32, (CHUNK, 1), 0)
    col = lax.broadcasted_iota(jnp.int32, (CHUNK, CHUNK), 1)
    rr = lax.broadcasted_iota(jnp.int32, (CHUNK, CHUNK), 0)
    tri = ((col >= rr) if rev else (col <= rr)).astype(BF16)
    ones = jnp.ones((GLA_DK, GLA_DK), BF16)
    n_chunk = GLA_RB // CHUNK

    for c in range(n_chunk):
        g = g_ref[c * CHUNK:(c + 1) * CHUNK, :]
        g_hi = g.astype(BF16)
        rest = g - g_hi.astype(F32)
        g_mid = rest.astype(BF16)
        g_lo = (rest - g_mid.astype(F32)).astype(BF16)
        b_ref[c * CHUNK:(c + 1) * CHUNK, :] = (
            jnp.dot(tri, g_hi, preferred_element_type=F32) + jnp.dot(tri, g_mid, preferred_element_type=F32)
            + jnp.dot(tri, g_lo, preferred_element_type=F32))

    def run(fast):
        def step(t, carry):
            c = n_chunk - 1 - t if rev else t
            rows = pl.ds(pl.multiple_of(c * CHUNK, CHUNK), CHUNK)
            b_all = b_ref[rows, :]
            pending = []
            for h in range(GLA_HEADS):
                kc = slice(h * GLA_DK, (h + 1) * GLA_DK)
                vc = slice(h * GLA_DV, (h + 1) * GLA_DV)
                q = q_ref[rows, kc].astype(F32) * (GLA_DK ** -0.5)
                k = k_ref[rows, kc].astype(F32)
                v = v_ref[rows, vc].astype(BF16)
                b = b_all[:, kc]
                if fast:
                    a, qt, kt, bend = _scores_fast(q, k, b, rev, row, col)
                else:
                    a, qt, kt, bend = _scores_exact(q, k, b, rev, ones, row, col)
                st = st_ref[h]
                o = lax.dot_general(qt.astype(BF16), st.astype(BF16), (((1,), (1,)), ((), ())),
                                    preferred_element_type=F32)
                st_ref[h] = st * jnp.exp(bend) + lax.dot_general(
                    v, kt.astype(BF16), (((0,), (0,)), ((), ())), preferred_element_type=F32)
                pending.append((vc, o, a, v))
            for vc, o, a, v in pending:
                o = o + jnp.dot(a.astype(BF16), v, preferred_element_type=F32)
                if rev:
                    o_ref[rows, vc] = o
                else:
                    o = _rms(o + ob_ref[rows, vc]) * gain_ref[...]
                    o_ref[rows, vc] = (o * _silu(og_ref[rows, vc].astype(F32))).astype(BF16)
            return carry

        lax.fori_loop(0, n_chunk, step, 0, unroll=2 if fast else 1)

    low = jnp.min(jnp.sum(g_ref[...].reshape(GLA_RB // SUB, SUB, GLA_KEY_WIDTH), axis=1))
    safe = low > SAFE_BLOCK_DECAY
    pl.when(safe)(lambda: run(True))
    pl.when(jnp.logical_not(safe))(lambda: run(False))


def _gla_sweep(p3, lg3, rev, ob=None, gla_norm=None):
    nb = TL // GLA_RB
    blk = (lambda i: nb - 1 - i) if rev else (lambda i: i)
    key_blocks = P_MAIN // GLA_KEY_WIDTH
    q_col = 3 * HY_WIDTH // GLA_KEY_WIDTH
    v_col = (3 * HY_WIDTH + 2 * GLA_KEY_WIDTH) // GLA_WIDTH
    assert key_blocks * GLA_KEY_WIDTH == P_MAIN
    narrow = lambda col: pl.BlockSpec((None, GLA_RB, GLA_KEY_WIDTH), lambda b, i: (b, blk(i), col))
    wide = lambda col: pl.BlockSpec((None, GLA_RB, GLA_WIDTH), lambda b, i: (b, blk(i), col))
    in_specs = [narrow(q_col), narrow(q_col + 1), wide(v_col), narrow(1 if rev else 0)]
    args = [p3, p3, p3, lg3]
    if not rev:
        in_specs += [wide(v_col + 1), wide(0), pl.BlockSpec((1, GLA_DV), lambda b, i: (0, 0))]
        args += [p3, ob, gla_norm.reshape(1, -1)]
    return pl.pallas_call(
        functools.partial(_gla_sweep_body, rev=rev),
        out_shape=jax.ShapeDtypeStruct((BATCH, TL, GLA_WIDTH), F32 if rev else BF16),
        grid=(BATCH, nb),
        in_specs=in_specs,
        out_specs=wide(0),
        scratch_shapes=[pltpu.VMEM((GLA_HEADS, GLA_DV, GLA_DK), F32),
                        pltpu.VMEM((GLA_RB, GLA_KEY_WIDTH), F32)],
        compiler_params=_params(("parallel", "arbitrary"), 52),
        name="gla_down" if rev else "gla_up",
    )(*args)


OUT_TM = 512


def _outproj_body(h_ref, yh_ref, yg_ref, wh_ref, wg_ref, o_ref):
    o_ref[...] = (h_ref[...] + jnp.dot(yh_ref[...], wh_ref[...], preferred_element_type=F32)
                  + jnp.dot(yg_ref[...], wg_ref[...], preferred_element_type=F32))


def _outproj(hbuf, yh, ygbuf, w_out):
    wo = _to_bf16(w_out)
    half = lambda i: pl.BlockSpec((HY_WIDTH, D_MODEL), lambda r: (i, 0))
    return pl.pallas_call(
        _outproj_body,
        out_shape=jax.ShapeDtypeStruct((X_ROWS, D_MODEL), F32),
        grid=(X_ROWS // OUT_TM,),
        in_specs=[pl.BlockSpec((pl.Element(OUT_TM), pl.Element(D_MODEL)), lambda r: (_x_row(r, OUT_TM), 0)),
                  pl.BlockSpec((OUT_TM, HY_WIDTH), lambda r: (r, 0)),
                  pl.BlockSpec((pl.Element(OUT_TM), pl.Element(GLA_WIDTH)),
                               lambda r: (_x_row(r, OUT_TM, BF16_SUBLANE), 0)),
                  half(0), half(1)],
        out_specs=pl.BlockSpec((OUT_TM, D_MODEL), lambda r: (r, 0)),
        compiler_params=_params(("parallel",), 48),
        name="outproj",
    )(hbuf, yh, ygbuf, wo, wo)


def _long_conv(u3, filt):
    f1, f3, mf, mi, mirror = _dft_tables()
    as_n1 = lambda a: a.reshape(2 * FFT_R1, FFT_N2, HY_WIDTH)
    by_k1 = lambda a: a.reshape(2, FFT_N1, FFT_N2, HY_WIDTH)
    zs = _lmm(f1, as_n1(filt), BF16, name="lmm_g")
    a = _lmm(f1, as_n1(u3), BF16, name="lmm_fwd")
    bm = _spec(mf, mirror, mi, by_k1(a), by_k1(zs))
    y = _lmm(f3, bm.reshape(2 * FFT_N1, FFT_N2, HY_WIDTH), F32, scale=1.0 / NFFT, name="lmm_inv")
    return y.reshape(BATCH, TFFT, HY_WIDTH)


def kernel(x, meta_tokens, ffn1_norm, ffn1_w_gate, ffn1_w_up, ffn1_w_down, mix_norm, w_in, conv_w, conv_b,
           filt_w1, filt_b1, filt_w2, filt_b2, filt_w3, filt_freq, hyena_d, hyena_norm, gk_w2, gk_b2,
           gla_norm, w_out, ffn2_norm, ffn2_w_gate, ffn2_w_up, ffn2_w_down, final_norm):
    assert x.shape == (BATCH, SEQ, D_MODEL) and ffn1_norm.shape[0] == 1

    hbuf = _ffn(x.reshape(X_ROWS, D_MODEL), ffn1_norm[0], ffn1_w_gate[0], ffn1_w_up[0], ffn1_w_down[0],
                final_norm, final=False, shared_rows=_meta_rows(meta_tokens))

    p, lg = _inproj(hbuf, _inproj_weights(mix_norm[0], w_in[0], gk_w2[0], gk_b2[0]))
    p3 = p.reshape(BATCH, TL, P_MAIN)
    lg3 = lg.reshape(BATCH, TL, 2 * GLA_KEY_WIDTH)

    filt = _filters(filt_w1[0], filt_b1[0], filt_w2[0], filt_b2[0], filt_w3[0], filt_freq[0])
    cw, cb = conv_w[0], conv_b[0].reshape(1, -1)
    u3 = _uconv(p3, cw, cb)
    yh = _ymix(_long_conv(u3, filt), u3, p3, cw, cb, hyena_d[0], hyena_norm[0])
    yg = _gla_sweep(p3, lg3, rev=False, ob=_gla_sweep(p3, lg3, rev=True), gla_norm=gla_norm[0])

    h2 = _outproj(hbuf, yh.reshape(X_ROWS, HY_WIDTH), yg.reshape(ROWS, GLA_WIDTH), w_out[0])
    out = _ffn(h2, ffn2_norm[0], ffn2_w_gate[0], ffn2_w_up[0], ffn2_w_down[0], final_norm, final=True)
    return out.reshape(BATCH, SEQ, D_MODEL)
```
